```python
import math
import jax, jax.numpy as jnp
from jax import lax
import numpy as np

D_MODEL = 1024
BATCH = 8
SEQ = 2048
DEPTH = 1

MEM_TOKENS = 256
RET_HEADS = 4
RET_DK = 64
RET_DV = 128
RET_CHUNK = 128
DSA_HEADS = 8
DSA_DH = 64
IDX_HEADS = 8
IDX_DIM = 64
TOPK_MAX = 256
Q_BLOCK = 128
MEM_HEADS = 4
MEM_DH = 128
T5_BUCKETS = 32
T5_MAX_DIST = 128
D_FF = 2816
ROPE_BASE = 10000.0
LN_EPS = 1e-5
NEG_INF = -1e30

RET_QK_W = RET_HEADS * RET_DK
RET_V_W = RET_HEADS * RET_DV
DSA_W = DSA_HEADS * DSA_DH
IDX_Q_W = IDX_HEADS * IDX_DIM
MEM_W = MEM_HEADS * MEM_DH
N_BRANCH = 3
W_IN_COLS = 2 * RET_QK_W + 2 * RET_V_W + 3 * DSA_W + IDX_Q_W + IDX_DIM + IDX_HEADS + MEM_W + N_BRANCH * D_MODEL
DEEPNORM_ALPHA = (2.0 * DEPTH) ** 0.25
DEEPNORM_BETA = (8.0 * DEPTH) ** -0.25

kernel_name = "hybrid_retention_dsa_memory_macaron_deepnorm"


def _split_points():
    widths = [RET_QK_W, RET_QK_W, RET_V_W, RET_V_W,
              DSA_W, DSA_W, DSA_W,
              IDX_Q_W, IDX_DIM, IDX_HEADS,
              MEM_W,
              N_BRANCH * D_MODEL]
    pts, acc = [], 0
    for w in widths[:-1]:
        acc += w
        pts.append(acc)
    return pts


def layer_norm(x, g, b):
    xf = x.astype(jnp.float32)
    mu = jnp.mean(xf, axis=-1, keepdims=True)
    var = jnp.mean(jnp.square(xf - mu), axis=-1, keepdims=True)
    return ((xf - mu) * lax.rsqrt(var + LN_EPS) * g + b).astype(x.dtype)


def swiglu_ffn(x, w_in, w_out):
    a, u = jnp.split(x @ w_in, 2, axis=-1)
    return (jax.nn.silu(a) * u) @ w_out


def rope(x, pos):
    half = x.shape[-1] // 2
    freqs = ROPE_BASE ** (-jnp.arange(half, dtype=jnp.float32) / half)
    ang = pos.astype(jnp.float32)[:, None] * freqs[None, :]
    cos = jnp.cos(ang)[None, :, None, :].astype(x.dtype)
    sin = jnp.sin(ang)[None, :, None, :].astype(x.dtype)
    x1, x2 = x[..., :half], x[..., half:]
    return jnp.concatenate([x1 * cos - x2 * sin, x1 * sin + x2 * cos], axis=-1)


def t5_bucket(n):
    n = jnp.maximum(n, 0)
    max_exact = T5_BUCKETS // 2
    nf = jnp.maximum(n, 1).astype(jnp.float32)
    large = max_exact + (jnp.log(nf / max_exact) / math.log(T5_MAX_DIST / max_exact)
                         * (T5_BUCKETS - max_exact)).astype(jnp.int32)
    large = jnp.minimum(large, T5_BUCKETS - 1)
    return jnp.where(n < max_exact, n, large)


def retention(q, k, v, g, gn_g, gn_b):
    B, L = q.shape[0], q.shape[1]
    C = RET_CHUNK
    NC = L // C
    dt = q.dtype
    gamma = 1.0 - 2.0 ** (-5.0 - jnp.arange(RET_HEADS, dtype=jnp.float32))
    lg = jnp.log(gamma)
    i = jnp.arange(C)
    diff = i[:, None] - i[None, :]
    decay_in = jnp.where(diff[None] >= 0, jnp.exp(jnp.maximum(diff, 0)[None] * lg[:, None, None]), 0.0).astype(dt)
    k_dec = jnp.exp((C - 1 - i)[None, :] * lg[:, None]).astype(dt)
    q_dec = jnp.exp((i + 1)[None, :] * lg[:, None]).astype(dt)
    chunk_dec = jnp.exp(C * lg).astype(dt)

    qc = q.reshape(B, NC, C, RET_HEADS, RET_DK)
    kc = k.reshape(B, NC, C, RET_HEADS, RET_DK)
    vc = v.reshape(B, NC, C, RET_HEADS, RET_DV)

    scores = jnp.einsum('bnihd,bnjhd->bnhij', qc, kc) * decay_in[None, None]
    intra = jnp.einsum('bnhij,bnjhe->bnihe', scores, vc)

    kv = jnp.einsum('bnjhd,hj,bnjhe->bnhde', kc, k_dec, vc)

    def step(state, kv_n):
        return chunk_dec[None, :, None, None] * state + kv_n, state

    init = jnp.zeros((B, RET_HEADS, RET_DK, RET_DV), dt)
    _, r_prev = lax.scan(step, init, jnp.moveaxis(kv, 1, 0))
    r_prev = jnp.moveaxis(r_prev, 0, 1)
    cross = jnp.einsum('bnihd,hi,bnhde->bnihe', qc, q_dec, r_prev)

    o = (intra + cross).reshape(B, L, RET_HEADS, RET_DV).astype(jnp.float32)
    mu = jnp.mean(o, axis=-1, keepdims=True)
    var = jnp.mean(jnp.square(o - mu), axis=-1, keepdims=True)
    o = ((o - mu) * lax.rsqrt(var + LN_EPS)).reshape(B, L, RET_V_W) * gn_g + gn_b
    return jax.nn.silu(g) * o.astype(dt)


def dsa_attention(q, k, v, qi, ki, wi, t5_table):
    B, L = q.shape[0], q.shape[1]
    topk = min(TOPK_MAX, L // 4)
    NB = L // Q_BLOCK
    key_pos = jnp.arange(L)
    scale = DSA_DH ** -0.5
    gather = jax.vmap(lambda a, ix: a[ix])

    def to_blocks(a):
        return jnp.moveaxis(a.reshape(B, NB, Q_BLOCK, *a.shape[2:]), 1, 0)

    def block(args):
        n, qb, qib, wib = args
        t = n * Q_BLOCK + jnp.arange(Q_BLOCK)
        rel = jax.nn.relu(jnp.einsum('bqhd,bsd->bqhs', qib, ki).astype(jnp.float32))
        score = jnp.einsum('bqhs,bqh->bqs', rel, wib.astype(jnp.float32))
        causal = key_pos[None, :] <= t[:, None]
        score = jnp.where(causal[None], score, NEG_INF)
        _, idx = lax.top_k(score, topk)
        k_sel = gather(k, idx)
        v_sel = gather(v, idx)
        logits = jnp.einsum('bqhd,bqkhd->bhqk', qb, k_sel).astype(jnp.float32) * scale
        bias = jnp.transpose(t5_table[t5_bucket(t[None, :, None] - idx)], (0, 3, 1, 2))
        valid = (idx <= t[None, :, None])[:, None]
        logits = jnp.where(valid, logits + bias, NEG_INF)
        p = jax.nn.softmax(logits, axis=-1).astype(v.dtype)
        return jnp.einsum('bhqk,bqkhd->bqhd', p, v_sel)

    out = lax.map(block, (jnp.arange(NB), to_blocks(q), to_blocks(qi), to_blocks(wi)))
    return jnp.moveaxis(out, 0, 1).reshape(B, L, DSA_W)


def memory_attention(q, mem, w_kv):
    B, M = mem.shape[0], mem.shape[1]
    mk, mv = jnp.split(mem @ w_kv, 2, axis=-1)
    mk = mk.reshape(B, M, MEM_HEADS, MEM_DH)
    mv = mv.reshape(B, M, MEM_HEADS, MEM_DH)
    logits = jnp.einsum('bqhd,bmhd->bhqm', q, mk).astype(jnp.float32) * MEM_DH ** -0.5
    p = jax.nn.softmax(logits, axis=-1).astype(mv.dtype)
    out = jnp.einsum('bhqm,bmhd->bqhd', p, mv)
    return out.reshape(B, q.shape[1], MEM_W)


def hybrid_mixer(h, mem, w_in, t5_table, ret_gn_g, ret_gn_b, w_mem_kv,
                 w_br_ret, w_br_dsa, w_br_mem, w_out):
    B, L, _ = h.shape
    pos = jnp.arange(L)
    parts = jnp.split(h @ w_in, _split_points(), axis=-1)
    rq, rk, rv, rg, dq, dk, dv, iq, ik, iw, mq, gates = parts

    rq = rope(rq.reshape(B, L, RET_HEADS, RET_DK), pos)
    rk = rope(rk.reshape(B, L, RET_HEADS, RET_DK), pos) * (RET_DK ** -0.5)
    o_ret = retention(rq, rk, rv, rg, ret_gn_g, ret_gn_b)

    iw = iw * (IDX_HEADS ** -0.5 * IDX_DIM ** -0.5)
    o_dsa = dsa_attention(dq.reshape(B, L, DSA_HEADS, DSA_DH),
                          dk.reshape(B, L, DSA_HEADS, DSA_DH),
                          dv.reshape(B, L, DSA_HEADS, DSA_DH),
                          iq.reshape(B, L, IDX_HEADS, IDX_DIM), ik, iw, t5_table)

    o_mem = memory_attention(mq.reshape(B, L, MEM_HEADS, MEM_DH), mem, w_mem_kv)

    g_ret, g_dsa, g_mem = jnp.split(jax.nn.sigmoid(gates), N_BRANCH, axis=-1)
    merged = g_ret * (o_ret @ w_br_ret) + g_dsa * (o_dsa @ w_br_dsa) + g_mem * (o_mem @ w_br_mem)
    return merged @ w_out


def setup_inputs(seed: int = 0) -> dict:
    key = jax.random.key(seed)
    ks = jax.random.split(key, 26)
    f32 = jnp.float32

    def w(k, shape, fan_in, scale=1.0):
        return jax.random.normal(k, shape, f32) * (fan_in ** -0.5) * scale

    def gain(k, shape):
        return 1.0 + 0.02 * jax.random.normal(k, shape, f32)

    def bias(k, shape):
        return 0.02 * jax.random.normal(k, shape, f32)

    return {
        "x": jax.random.normal(ks[0], (BATCH, SEQ, D_MODEL), f32),
        "mem": jax.random.normal(ks[1], (BATCH, MEM_TOKENS, D_MODEL), f32),
        "ffn1_w_in": w(ks[2], (DEPTH, D_MODEL, 2 * D_FF), D_MODEL),
        "ffn1_w_out": w(ks[3], (DEPTH, D_FF, D_MODEL), D_FF, DEEPNORM_BETA),
        "ln1_g": gain(ks[4], (DEPTH, D_MODEL)),
        "ln1_b": bias(ks[5], (DEPTH, D_MODEL)),
        "w_in": w(ks[6], (DEPTH, D_MODEL, W_IN_COLS), D_MODEL),
        "t5_table": 0.5 * jax.random.normal(ks[7], (T5_BUCKETS, DSA_HEADS), f32),
        "ret_gn_g": gain(ks[8], (DEPTH, RET_V_W)),
        "ret_gn_b": bias(ks[9], (DEPTH, RET_V_W)),
        "w_mem_kv": w(ks[10], (DEPTH, D_MODEL, 2 * MEM_W), D_MODEL),
        "w_br_ret": w(ks[11], (DEPTH, RET_V_W, D_MODEL), RET_V_W, DEEPNORM_BETA),
        "w_br_dsa": w(ks[12], (DEPTH, DSA_W, D_MODEL), DSA_W, DEEPNORM_BETA),
        "w_br_mem": w(ks[13], (DEPTH, MEM_W, D_MODEL), MEM_W, DEEPNORM_BETA),
        "w_out": w(ks[14], (DEPTH, D_MODEL, D_MODEL), D_MODEL, DEEPNORM_BETA),
        "ln2_g": gain(ks[15], (DEPTH, D_MODEL)),
        "ln2_b": bias(ks[16], (DEPTH, D_MODEL)),
        "ffn2_w_in": w(ks[17], (DEPTH, D_MODEL, 2 * D_FF), D_MODEL),
        "ffn2_w_out": w(ks[18], (DEPTH, D_FF, D_MODEL), D_FF, DEEPNORM_BETA),
        "ln3_g": gain(ks[19], (DEPTH, D_MODEL)),
        "ln3_b": bias(ks[20], (DEPTH, D_MODEL)),
    }


def reference(x, mem, ffn1_w_in, ffn1_w_out, ln1_g, ln1_b, w_in, t5_table, ret_gn_g, ret_gn_b,
              w_mem_kv, w_br_ret, w_br_dsa, w_br_mem, w_out, ln2_g, ln2_b,
              ffn2_w_in, ffn2_w_out, ln3_g, ln3_b):
    for l in range(DEPTH):
        x = layer_norm(DEEPNORM_ALPHA * x + 0.5 * swiglu_ffn(x, ffn1_w_in[l], ffn1_w_out[l]),
                       ln1_g[l], ln1_b[l])
        mix = hybrid_mixer(x, mem, w_in[l], t5_table, ret_gn_g[l], ret_gn_b[l], w_mem_kv[l],
                           w_br_ret[l], w_br_dsa[l], w_br_mem[l], w_out[l])
        x = layer_norm(DEEPNORM_ALPHA * x + mix, ln2_g[l], ln2_b[l])
        x = layer_norm(DEEPNORM_ALPHA * x + 0.5 * swiglu_ffn(x, ffn2_w_in[l], ffn2_w_out[l]),
                       ln3_g[l], ln3_b[l])
    return x
```

```python
import functools
import math

import jax
import jax.numpy as jnp
from jax import lax
from jax.experimental import pallas as pl
from jax.experimental.pallas import tpu as pltpu

F32 = jnp.float32
BF16 = jnp.bfloat16

D_MODEL = 1024
BATCH = 8
SEQ = 2048
MEM_TOKENS = 256
RET_HEADS, RET_DK, RET_DV, RET_CHUNK = 4, 64, 128, 128
DSA_HEADS, DSA_DH = 8, 64
IDX_HEADS, IDX_DIM = 8, 64
TOPK = min(256, SEQ // 4)
MEM_HEADS, MEM_DH = 4, 128
T5_BUCKETS, T5_MAX_DIST = 32, 128
D_FF = 2816
ROPE_BASE = 10000.0
LN_EPS = 1e-5
NEG_INF = -1e30
DEPTH = 1
ALPHA = (2.0 * DEPTH) ** 0.25

RET_QK_W = RET_HEADS * RET_DK
RET_V_W = RET_HEADS * RET_DV
DSA_W = DSA_HEADS * DSA_DH
IDX_Q_W = IDX_HEADS * IDX_DIM
MEM_W = MEM_HEADS * MEM_DH
N_TOK = BATCH * SEQ

V7X_VMEM_BYTES = 64 * 1024 * 1024
VMEM_LIMIT = V7X_VMEM_BYTES - 8 * 1024 * 1024
LANES = 128

FFN_ROWS = 512
FFN_CHUNK = 1408
PROJ_ROWS = 512
MERGE_ROWS = 512
MEMKV_ROWS = 512
MEMATT_ROWS = 512
DSA_T = 256
BISECT_ROWS = 128
BISECT_STEPS_PER_CHECK = 4

P_RQ, P_RK, P_RV, P_RG = 0, 256, 512, 1024
P_DQ, P_DK, P_DV, P_IQ, P_MQ, P_IKW = 1536, 2048, 2560, 3072, 3584, 4096
P_COLS = 4224
W_IK0 = 2 * RET_QK_W + 2 * RET_V_W + 3 * DSA_W + IDX_Q_W
W_IW0 = W_IK0 + IDX_DIM
W_MQ0 = W_IW0 + IDX_HEADS
W_G0 = W_MQ0 + MEM_W
IW_SCALE = IDX_HEADS ** -0.5 * IDX_DIM ** -0.5


def _cparams(sem):
    return pltpu.CompilerParams(dimension_semantics=sem, vmem_limit_bytes=VMEM_LIMIT)


def _const_spec(shape):
    nd = len(shape)
    return pl.BlockSpec(shape, lambda *_: (0,) * nd, pipeline_mode=pl.Buffered(1))


def _layer_norm(y, g, b):
    mu = jnp.mean(y, axis=-1, keepdims=True)
    yc = y - mu
    var = jnp.mean(yc * yc, axis=-1, keepdims=True)
    return yc * lax.rsqrt(var + LN_EPS) * g + b


def _dot(a, b):
    return jnp.dot(a, b, preferred_element_type=F32)


def _dot_nt(a, b):
    return lax.dot_general(a, b, (((1,), (1,)), ((), ())), preferred_element_type=F32)


def _dot_tn(a, b):
    return lax.dot_general(a, b, (((0,), (0,)), ((), ())), preferred_element_type=F32)


def _ffn_ln_kernel(x_ref, wi_ref, wo_ref, g_ref, b_ref, o_ref):
    x = x_ref[...]
    xb = x.astype(BF16)
    acc = None
    for c in range(D_FF // FFN_CHUNK):
        lo = c * FFN_CHUNK
        a = _dot(xb, wi_ref[:, lo:lo + FFN_CHUNK])
        u = _dot(xb, wi_ref[:, D_FF + lo:D_FF + lo + FFN_CHUNK])
        act = (jax.nn.silu(a) * u).astype(BF16)
        part = _dot(act, wo_ref[lo:lo + FFN_CHUNK, :])
        acc = part if acc is None else acc + part
    o_ref[...] = _layer_norm(ALPHA * x + 0.5 * acc, g_ref[...], b_ref[...])


def _ffn_ln(x, w_in_bf, w_out_bf, g, b, name):
    return pl.pallas_call(
        _ffn_ln_kernel,
        out_shape=jax.ShapeDtypeStruct((N_TOK, D_MODEL), F32),
        grid=(N_TOK // FFN_ROWS,),
        in_specs=[
            pl.BlockSpec((FFN_ROWS, D_MODEL), lambda i: (i, 0)),
            _const_spec((D_MODEL, 2 * D_FF)),
            _const_spec((D_FF, D_MODEL)),
            _const_spec((1, D_MODEL)),
            _const_spec((1, D_MODEL)),
        ],
        out_specs=pl.BlockSpec((FFN_ROWS, D_MODEL), lambda i: (i, 0)),
        compiler_params=_cparams(("parallel",)),
        name=name,
    )(x, w_in_bf, w_out_bf, g, b)


def _rope(x, cos, sin_signed):
    width = x.shape[-1]
    lane = lax.broadcasted_iota(jnp.int32, x.shape, 1)
    first_half = (lane % RET_DK) < (RET_DK // 2)
    swapped = jnp.where(first_half,
                        pltpu.roll(x, width - RET_DK // 2, 1),
                        pltpu.roll(x, RET_DK // 2, 1))
    return x * cos + swapped * sin_signed


def _proj_kernel(h_ref, w_ref, cos_ref, sin_ref,
                 rq_ref, rk_ref, rv_ref, rg_ref, dq_ref, dk_ref, dv_ref, iq_ref, mq_ref, ikw_ref):
    hb = h_ref[...].astype(BF16)

    def proj(lo, width):
        return _dot(hb, w_ref[:, lo:lo + width])

    cos = cos_ref[...]
    sin = sin_ref[...]
    rq_ref[...] = _rope(proj(P_RQ, RET_QK_W), cos, sin)
    rk_ref[...] = _rope(proj(P_RK, RET_QK_W), cos, sin) * (RET_DK ** -0.5)
    rv_ref[...] = proj(P_RV, RET_V_W).astype(BF16)
    rg_ref[...] = proj(P_RG, RET_V_W)
    dq_ref[...] = proj(P_DQ, DSA_W).astype(BF16)
    dk_ref[...] = proj(P_DK, DSA_W).astype(BF16)
    dv_ref[...] = proj(P_DV, DSA_W).astype(BF16)
    iq_ref[...] = proj(P_IQ, IDX_Q_W).astype(BF16)
    mq_ref[...] = proj(P_MQ, MEM_W).astype(BF16)
    ikw_ref[...] = proj(P_IKW, LANES)


def _proj(h, w_proj_bf, cos_t, sin_t):
    rows = lambda w: pl.BlockSpec((PROJ_ROWS, w), lambda i: (i, 0))
    seq_tiles = SEQ // PROJ_ROWS
    pos = lambda w: pl.BlockSpec((PROJ_ROWS, w), lambda i: (i % seq_tiles, 0))
    sd = jax.ShapeDtypeStruct
    return pl.pallas_call(
        _proj_kernel,
        out_shape=(
            sd((N_TOK, RET_QK_W), F32), sd((N_TOK, RET_QK_W), F32),
            sd((N_TOK, RET_V_W), BF16), sd((N_TOK, RET_V_W), F32),
            sd((N_TOK, DSA_W), BF16), sd((N_TOK, DSA_W), BF16), sd((N_TOK, DSA_W), BF16),
            sd((N_TOK, IDX_Q_W), BF16), sd((N_TOK, MEM_W), BF16), sd((N_TOK, LANES), F32),
        ),
        grid=(N_TOK // PROJ_ROWS,),
        in_specs=[rows(D_MODEL), _const_spec((D_MODEL, P_COLS)), pos(RET_QK_W), pos(RET_QK_W)],
        out_specs=(rows(RET_QK_W), rows(RET_QK_W), rows(RET_V_W), rows(RET_V_W),
                   rows(DSA_W), rows(DSA_W), rows(DSA_W), rows(IDX_Q_W), rows(MEM_W), rows(LANES)),
        compiler_params=_cparams(("parallel",)),
        name="mixer_proj",
    )(h, w_proj_bf, cos_t, sin_t)


def _retention_kernel(q_ref, k_ref, v_ref, g_ref, din_ref, kdec_ref, qdec_ref, cdec_ref,
                      gng_ref, gnb_ref, o_ref, state_ref):
    @pl.when(pl.program_id(1) == 0)
    def _():
        state_ref[...] = jnp.zeros_like(state_ref)

    q = q_ref[...]
    k = k_ref[...]
    kd = k * kdec_ref[...]
    v = v_ref[...]
    qdec = qdec_ref[...]
    gate = g_ref[...]
    gn_g = gng_ref[...]
    gn_b = gnb_ref[...]
    for h in range(RET_HEADS):
        ks = slice(h * RET_DK, (h + 1) * RET_DK)
        vs = slice(h * RET_DV, (h + 1) * RET_DV)
        qh = q[:, ks].astype(BF16)
        kh = k[:, ks].astype(BF16)
        vh = v[:, vs]
        state = state_ref[h]
        scores = _dot_nt(qh, kh) * din_ref[h]
        intra = _dot(scores.astype(BF16), vh)
        cross = _dot(qh, state.astype(BF16)) * qdec[:, vs]
        kv = _dot_tn(kd[:, ks].astype(BF16), vh)
        state_ref[h] = cdec_ref[h] * state + kv
        o = intra + cross
        mu = jnp.mean(o, axis=-1, keepdims=True)
        oc = o - mu
        var = jnp.mean(oc * oc, axis=-1, keepdims=True)
        o = oc * lax.rsqrt(var + LN_EPS) * gn_g[:, vs] + gn_b[:, vs]
        o_ref[:, vs] = (jax.nn.silu(gate[:, vs]) * o).astype(BF16)


def _retention(rq, rk, rv, rg, din, kdec, qdec, cdec, gn_g, gn_b):
    nc = SEQ // RET_CHUNK
    rows = lambda w: pl.BlockSpec((RET_CHUNK, w), lambda b, n: (b * nc + n, 0))
    return pl.pallas_call(
        _retention_kernel,
        out_shape=jax.ShapeDtypeStruct((N_TOK, RET_V_W), BF16),
        grid=(BATCH, nc),
        in_specs=[rows(RET_QK_W), rows(RET_QK_W), rows(RET_V_W), rows(RET_V_W),
                  _const_spec((RET_HEADS, RET_CHUNK, RET_CHUNK)),
                  _const_spec((RET_CHUNK, RET_QK_W)),
                  _const_spec((RET_CHUNK, RET_V_W)),
                  _const_spec((RET_HEADS, RET_DK, RET_DV)),
                  _const_spec((1, RET_V_W)), _const_spec((1, RET_V_W))],
        out_specs=rows(RET_V_W),
        scratch_shapes=[pltpu.VMEM((RET_HEADS, RET_DK, RET_DV), F32)],
        compiler_params=_cparams(("parallel", "arbitrary")),
        name="retention",
    )(rq, rk, rv, rg, din, kdec, qdec, cdec, gn_g, gn_b)


def _memkv_kernel(m_ref, w_ref, o_ref):
    o_ref[...] = _dot(m_ref[...].astype(BF16), w_ref[...]).astype(BF16)


def _memkv(mem2d, w_bf):
    n = mem2d.shape[0]
    return pl.pallas_call(
        _memkv_kernel,
        out_shape=jax.ShapeDtypeStruct((n, 2 * MEM_W), BF16),
        grid=(n // MEMKV_ROWS,),
        in_specs=[pl.BlockSpec((MEMKV_ROWS, D_MODEL), lambda i: (i, 0)),
                  _const_spec((D_MODEL, 2 * MEM_W))],
        out_specs=pl.BlockSpec((MEMKV_ROWS, 2 * MEM_W), lambda i: (i, 0)),
        compiler_params=_cparams(("parallel",)),
        name="mem_kv",
    )(mem2d, w_bf)


def _mematt_kernel(q_ref, k_ref, v_ref, o_ref):
    q = q_ref[...]
    k = k_ref[...]
    v = v_ref[...]
    for h in range(MEM_HEADS):
        hs = slice(h * MEM_DH, (h + 1) * MEM_DH)
        logits = _dot_nt(q[:, hs], k[:, hs]) * (MEM_DH ** -0.5)
        m = jnp.max(logits, axis=-1, keepdims=True)
        p = jnp.exp(logits - m)
        denom = jnp.sum(p, axis=-1, keepdims=True)
        o_ref[:, hs] = (_dot(p.astype(BF16), v[:, hs]) / denom).astype(BF16)


def _mematt(mq, mkv):
    tiles = SEQ // MEMATT_ROWS
    return pl.pallas_call(
        _mematt_kernel,
        out_shape=jax.ShapeDtypeStruct((N_TOK, MEM_W), BF16),
        grid=(BATCH, tiles),
        in_specs=[pl.BlockSpec((MEMATT_ROWS, MEM_W), lambda b, i: (b * tiles + i, 0)),
                  pl.BlockSpec((MEM_TOKENS, MEM_W), lambda b, i: (b, 0)),
                  pl.BlockSpec((MEM_TOKENS, MEM_W), lambda b, i: (b, 1))],
        out_specs=pl.BlockSpec((MEMATT_ROWS, MEM_W), lambda b, i: (b * tiles + i, 0)),
        compiler_params=_cparams(("parallel", "parallel")),
        name="mem_attention",
    )(mq, mkv, mkv)


def _t5_bias_kernel(table_ref, o_ref):
    r = lax.broadcasted_iota(jnp.int32, (DSA_T, 2 * DSA_T), 0)
    j = lax.broadcasted_iota(jnp.int32, (DSA_T, 2 * DSA_T), 1)
    n = jnp.maximum(r + DSA_T - j, 0)
    max_exact = T5_BUCKETS // 2
    nf = jnp.maximum(n, 1).astype(F32)
    large = max_exact + (jnp.log(nf / max_exact) / math.log(T5_MAX_DIST / max_exact)
                         * (T5_BUCKETS - max_exact)).astype(jnp.int32)
    large = jnp.minimum(large, T5_BUCKETS - 1)
    bucket = jnp.where(n < max_exact, n, large)
    for h in range(DSA_HEADS):
        acc = jnp.zeros((DSA_T, 2 * DSA_T), F32)
        for b in range(T5_BUCKETS):
            acc = jnp.where(bucket == b, table_ref[b, h], acc)
        o_ref[h] = acc


def _t5_bias(t5_table):
    return pl.pallas_call(
        _t5_bias_kernel,
        out_shape=jax.ShapeDtypeStruct((DSA_HEADS, DSA_T, 2 * DSA_T), F32),
        in_specs=[pl.BlockSpec(memory_space=pltpu.SMEM)],
        out_specs=pl.BlockSpec(memory_space=pltpu.VMEM),
        compiler_params=pltpu.CompilerParams(vmem_limit_bytes=VMEM_LIMIT),
        name="t5_bias",
    )(t5_table)


def _dsa_kernel(table_ref, bias_ref, iq_ref, dq_ref, ikw_q_ref, ikw_k_ref, dk_ref, dv_ref,
                o_ref, score_ref, logit_ref):
    qb = pl.program_id(1)
    q0 = qb * DSA_T
    n_chunks = qb + 1
    halves = DSA_T // LANES

    def chunk_start(c):
        return pl.multiple_of(c * DSA_T, DSA_T)

    iw = ikw_q_ref[:, IDX_DIM:IDX_DIM + IDX_HEADS] * IW_SCALE
    iq = iq_ref[...]

    def index_scores(c):
        ik = ikw_k_ref[pl.ds(chunk_start(c), DSA_T), 0:IDX_DIM].astype(BF16)
        s = jnp.zeros((DSA_T, DSA_T), F32)
        for h in range(IDX_HEADS):
            rel = _dot_nt(iq[:, h * IDX_DIM:(h + 1) * IDX_DIM], ik)
            s = s + jnp.maximum(rel, 0.0) * iw[:, h:h + 1]
        return s

    def score_body(c, carry):
        score_ref[c] = index_scores(c)
        return carry

    lax.fori_loop(0, qb, score_body, 0)
    row = lax.broadcasted_iota(jnp.int32, (DSA_T, DSA_T), 0)
    col = lax.broadcasted_iota(jnp.int32, (DSA_T, DSA_T), 1)
    score_ref[qb] = jnp.where(col <= row, index_scores(qb), NEG_INF)

    def count_ge(rows, thr_b, n_ch):
        def body(c, acc):
            for j in range(halves):
                part = score_ref[c, rows, j * LANES:(j + 1) * LANES]
                acc = acc + jnp.where(part >= thr_b, 1.0, 0.0)
            return acc
        acc = lax.fori_loop(0, n_ch, body, jnp.zeros((BISECT_ROWS, LANES), F32))
        return jnp.sum(acc, axis=-1, keepdims=True)

    kf = float(TOPK)
    for rb in range(DSA_T // BISECT_ROWS):
        rows = slice(rb * BISECT_ROWS, (rb + 1) * BISECT_ROWS)

        def minmax_body(c, carry):
            mn, mx = carry
            for j in range(halves):
                part = score_ref[c, rows, j * LANES:(j + 1) * LANES]
                mx = jnp.maximum(mx, part)
                mn = jnp.minimum(mn, jnp.where(part > 0.5 * NEG_INF, part, -NEG_INF))
            return mn, mx

        mn, mx = lax.fori_loop(
            0, n_chunks, minmax_body,
            (jnp.full((BISECT_ROWS, LANES), -NEG_INF, F32), jnp.full((BISECT_ROWS, LANES), NEG_INF, F32)))
        row_min = jnp.min(mn, axis=-1, keepdims=True)
        row_max = jnp.max(mx, axis=-1, keepdims=True)
        t_row = q0 + rb * BISECT_ROWS + lax.broadcasted_iota(jnp.int32, (BISECT_ROWS, 1), 0)

        lo0 = row_min
        hi0 = row_max + jnp.maximum(jnp.abs(row_max), 1.0) * 1e-6
        cnt_lo0 = (t_row + 1).astype(F32)
        cnt_hi0 = jnp.zeros((BISECT_ROWS, 1), F32)

        def unresolved(lo, hi, cnt_lo):
            mid = 0.5 * (lo + hi)
            return (cnt_lo > kf) & (mid > lo) & (mid < hi)

        def bisect_step(carry):
            lo, hi, cnt_lo, cnt_hi = carry
            mid = 0.5 * (lo + hi)
            cnt = count_ge(rows, jnp.broadcast_to(mid, (BISECT_ROWS, LANES)), n_chunks)
            ge = cnt >= kf
            return (jnp.where(ge, mid, lo), jnp.where(ge, hi, mid),
                    jnp.where(ge, cnt, cnt_lo), jnp.where(ge, cnt_hi, cnt))

        def pending_of(carry):
            lo, hi, cnt_lo, _ = carry
            return jnp.max(jnp.where(unresolved(lo, hi, cnt_lo), 1.0, 0.0))

        def while_body(state):
            carry, _ = state
            for _i in range(BISECT_STEPS_PER_CHECK):
                carry = bisect_step(carry)
            return carry, pending_of(carry)

        init = (lo0, hi0, cnt_lo0, cnt_hi0)
        (lo, hi, cnt_lo, cnt_hi), _ = lax.while_loop(
            lambda state: state[1] > 0.5, while_body, (init, pending_of(init)))

        tied = cnt_lo > kf
        need = kf - cnt_hi
        lo_b = jnp.broadcast_to(lo, (BISECT_ROWS, LANES))
        lane_idx = lax.broadcasted_iota(jnp.int32, (BISECT_ROWS, LANES), 1)

        def count_tied_upto(cut_b):
            def body(c, acc):
                for j in range(halves):
                    part = score_ref[c, rows, j * LANES:(j + 1) * LANES]
                    idx = lane_idx + (c * DSA_T + j * LANES)
                    acc = acc + jnp.where((part == lo_b) & (idx <= cut_b), 1.0, 0.0)
                return acc
            acc = lax.fori_loop(0, n_chunks, body, jnp.zeros((BISECT_ROWS, LANES), F32))
            return jnp.sum(acc, axis=-1, keepdims=True)

        def tie_search(_):
            def body(_i, carry):
                lo_i, hi_i = carry
                mid_i = lax.shift_right_arithmetic(lo_i + hi_i, 1)
                enough = count_tied_upto(jnp.broadcast_to(mid_i, (BISECT_ROWS, LANES))) >= need
                return jnp.where(enough, lo_i, mid_i), jnp.where(enough, mid_i, hi_i)
            lo_i = jnp.full((BISECT_ROWS, 1), -1, jnp.int32)
            hi_i = jnp.full((BISECT_ROWS, 1), SEQ - 1, jnp.int32)
            _, cut = lax.fori_loop(0, int(math.log2(SEQ)) + 1, body, (lo_i, hi_i))
            return jnp.where(tied, cut, SEQ)

        cut = lax.cond(jnp.max(tied.astype(jnp.int32)) > 0, tie_search,
                       lambda _: jnp.full((BISECT_ROWS, 1), SEQ, jnp.int32), 0)
        cut_b = jnp.broadcast_to(cut, (BISECT_ROWS, LANES))

        def mask_body(c, carry):
            for j in range(halves):
                cols = slice(j * LANES, (j + 1) * LANES)
                part = score_ref[c, rows, cols]
                idx = lane_idx + (c * DSA_T + j * LANES)
                keep = (part > lo_b) | ((part == lo_b) & (idx <= cut_b))
                score_ref[c, rows, cols] = jnp.where(keep, 0.0, NEG_INF)
            return carry

        lax.fori_loop(0, n_chunks, mask_body, 0)

    dq = dq_ref[...]
    scale = DSA_DH ** -0.5
    for h in range(DSA_HEADS):
        hs = slice(h * DSA_DH, (h + 1) * DSA_DH)
        qh = dq[:, hs]
        far_bias = table_ref[T5_BUCKETS - 1, h]

        def logits_of(c):
            kh = dk_ref[pl.ds(chunk_start(c), DSA_T), hs]
            return _dot_nt(qh, kh) * scale + score_ref[c]

        def run_max(m, lg):
            for j in range(halves):
                m = jnp.maximum(m, lg[:, j * LANES:(j + 1) * LANES])
            return m

        def far_body(c, m):
            lg = logits_of(c) + far_bias
            logit_ref[c] = lg
            return run_max(m, lg)

        m = lax.fori_loop(0, jnp.maximum(qb - 1, 0), far_body,
                          jnp.full((DSA_T, LANES), NEG_INF, F32))

        def near(c, window_half, m):
            lg = logits_of(c) + bias_ref[h, :, window_half * DSA_T:(window_half + 1) * DSA_T]
            logit_ref[c] = lg
            return run_max(m, lg)

        m = lax.cond(qb > 0, lambda m: near(qb - 1, 0, m), lambda m: m, m)
        m = near(qb, 1, m)
        row_m = jnp.max(m, axis=-1, keepdims=True)

        def pv_body(c, carry):
            acc, den = carry
            p = jnp.exp(logit_ref[c] - row_m)
            for j in range(halves):
                den = den + p[:, j * LANES:(j + 1) * LANES]
            vh = dv_ref[pl.ds(chunk_start(c), DSA_T), hs]
            return acc + _dot(p.astype(BF16), vh), den

        acc, den = lax.fori_loop(
            0, n_chunks, pv_body,
            (jnp.zeros((DSA_T, DSA_DH), F32), jnp.zeros((DSA_T, LANES), F32)))
        o_ref[:, hs] = (acc / jnp.sum(den, axis=-1, keepdims=True)).astype(BF16)


def _dsa(t5_table, bias, iq, dq, ikw, dk, dv):
    nqb = SEQ // DSA_T
    qrows = lambda w: pl.BlockSpec((DSA_T, w), lambda b, i: (b * nqb + i, 0))
    krows = lambda w: pl.BlockSpec((SEQ, w), lambda b, i: (b, 0))
    return pl.pallas_call(
        _dsa_kernel,
        out_shape=jax.ShapeDtypeStruct((N_TOK, DSA_W), BF16),
        grid=(BATCH, nqb),
        in_specs=[pl.BlockSpec(memory_space=pltpu.SMEM),
                  _const_spec((DSA_HEADS, DSA_T, 2 * DSA_T)),
                  qrows(IDX_Q_W), qrows(DSA_W), qrows(LANES),
                  krows(LANES), krows(DSA_W), krows(DSA_W)],
        out_specs=qrows(DSA_W),
        scratch_shapes=[pltpu.VMEM((SEQ // DSA_T, DSA_T, DSA_T), F32),
                        pltpu.VMEM((SEQ // DSA_T, DSA_T, DSA_T), F32)],
        compiler_params=_cparams(("parallel", "arbitrary")),
        name="dsa_attention",
    )(t5_table, bias, iq, dq, ikw, ikw, dk, dv)


def _merge_kernel(h_ref, oret_ref, odsa_ref, omem_ref, wg_ref, wr_ref, wd_ref, wm_ref, wo_ref,
                  g_ref, b_ref, o_ref):
    h = h_ref[...]
    hb = h.astype(BF16)
    merged = None
    for i, (src, w) in enumerate(((oret_ref, wr_ref), (odsa_ref, wd_ref), (omem_ref, wm_ref))):
        gate = jax.nn.sigmoid(_dot(hb, wg_ref[:, i * D_MODEL:(i + 1) * D_MODEL]))
        term = gate * _dot(src[...], w[...])
        merged = term if merged is None else merged + term
    mix = _dot(merged.astype(BF16), wo_ref[...])
    o_ref[...] = _layer_norm(ALPHA * h + mix, g_ref[...], b_ref[...])


def _merge(h, o_ret, o_dsa, o_mem, wg, wr, wd, wm, wo, g, b):
    rows = lambda w: pl.BlockSpec((MERGE_ROWS, w), lambda i: (i, 0))
    return pl.pallas_call(
        _merge_kernel,
        out_shape=jax.ShapeDtypeStruct((N_TOK, D_MODEL), F32),
        grid=(N_TOK // MERGE_ROWS,),
        in_specs=[rows(D_MODEL), rows(RET_V_W), rows(DSA_W), rows(MEM_W),
                  _const_spec((D_MODEL, 3 * D_MODEL)),
                  _const_spec((RET_V_W, D_MODEL)), _const_spec((DSA_W, D_MODEL)),
                  _const_spec((MEM_W, D_MODEL)), _const_spec((D_MODEL, D_MODEL)),
                  _const_spec((1, D_MODEL)), _const_spec((1, D_MODEL))],
        out_specs=rows(D_MODEL),
        compiler_params=_cparams(("parallel",)),
        name="merge_ln2",
    )(h, o_ret, o_dsa, o_mem, wg, wr, wd, wm, wo, g, b)


def _rope_tables():
    half = RET_DK // 2
    freqs = ROPE_BASE ** (-jnp.arange(half, dtype=F32) / half)
    ang = jnp.arange(SEQ).astype(F32)[:, None] * freqs[None, :]
    cos, sin = jnp.cos(ang), jnp.sin(ang)
    cos_t = jnp.tile(jnp.concatenate([cos, cos], axis=-1), (1, RET_HEADS))
    sin_t = jnp.tile(jnp.concatenate([-sin, sin], axis=-1), (1, RET_HEADS))
    return cos_t, sin_t


def _decay_tables():
    c = RET_CHUNK
    gamma = 1.0 - 2.0 ** (-5.0 - jnp.arange(RET_HEADS, dtype=F32))
    lg = jnp.log(gamma)
    i = jnp.arange(c)
    diff = i[:, None] - i[None, :]
    decay_in = jnp.where(diff[None] >= 0, jnp.exp(jnp.maximum(diff, 0)[None] * lg[:, None, None]), 0.0).astype(F32)
    k_dec = jnp.exp((c - 1 - i)[None, :] * lg[:, None]).astype(F32)
    q_dec = jnp.exp((i + 1)[None, :] * lg[:, None]).astype(F32)
    chunk_dec = jnp.exp(c * lg).astype(F32)
    kdec_t = jnp.repeat(k_dec.T, RET_DK, axis=1)
    qdec_t = jnp.repeat(q_dec.T, RET_DV, axis=1)
    cdec_t = jnp.broadcast_to(chunk_dec[:, None, None], (RET_HEADS, RET_DK, RET_DV))
    return decay_in, kdec_t, qdec_t, cdec_t


def _layer(x, mem2d, ffn1_w_in, ffn1_w_out, ln1_g, ln1_b, w_in, t5_table, ret_gn_g, ret_gn_b,
           w_mem_kv, w_br_ret, w_br_dsa, w_br_mem, w_out, ln2_g, ln2_b,
           ffn2_w_in, ffn2_w_out, ln3_g, ln3_b, tables):
    cos_t, sin_t, din, kdec, qdec, cdec = tables
    row = lambda v: v.reshape(1, -1)
    bf = lambda w: w.astype(BF16)

    w_proj = jnp.concatenate(
        [w_in[:, :W_IK0], w_in[:, W_MQ0:W_G0], w_in[:, W_IK0:W_MQ0],
         jnp.zeros((D_MODEL, P_COLS - P_IKW - IDX_DIM - IDX_HEADS), F32)], axis=1)

    h = _ffn_ln(x, bf(ffn1_w_in), bf(ffn1_w_out), row(ln1_g), row(ln1_b), "ffn1_ln1")
    rq, rk, rv, rg, dq, dk, dv, iq, mq, ikw = _proj(h, bf(w_proj), cos_t, sin_t)
    o_ret = _retention(rq, rk, rv, rg, din, kdec, qdec, cdec, row(ret_gn_g), row(ret_gn_b))
    o_mem = _mematt(mq, _memkv(mem2d, bf(w_mem_kv)))
    o_dsa = _dsa(t5_table, _t5_bias(t5_table), iq, dq, ikw, dk, dv)
    x2 = _merge(h, o_ret, o_dsa, o_mem, bf(w_in[:, W_G0:]), bf(w_br_ret), bf(w_br_dsa), bf(w_br_mem),
                bf(w_out), row(ln2_g), row(ln2_b))
    return _ffn_ln(x2, bf(ffn2_w_in), bf(ffn2_w_out), row(ln3_g), row(ln3_b), "ffn2_ln3")


def kernel(x, mem, ffn1_w_in, ffn1_w_out, ln1_g, ln1_b, w_in, t5_table, ret_gn_g, ret_gn_b,
           w_mem_kv, w_br_ret, w_br_dsa, w_br_mem, w_out, ln2_g, ln2_b,
           ffn2_w_in, ffn2_w_out, ln3_g, ln3_b):
    assert x.shape == (BATCH, SEQ, D_MODEL) and mem.shape == (BATCH, MEM_TOKENS, D_MODEL)
    tables = _rope_tables() + _decay_tables()
    y = x.reshape(N_TOK, D_MODEL)
    mem2d = mem.reshape(BATCH * MEM_TOKENS, D_MODEL)
    for l in range(DEPTH):
        y = _layer(y, mem2d, ffn1_w_in[l], ffn1_w_out[l], ln1_g[l], ln1_b[l], w_in[l], t5_table,
                   ret_gn_g[l], ret_gn_b[l], w_mem_kv[l], w_br_ret[l], w_br_dsa[l], w_br_mem[l],
                   w_out[l], ln2_g[l], ln2_b[l], ffn2_w_in[l], ffn2_w_out[l], ln3_g[l], ln3_b[l],
                   tables)
    return y.reshape(BATCH, SEQ, D_MODEL)
```

```python
import math

import jax
import jax.numpy as jnp
from jax import lax
from jax.experimental import pallas as pl
from jax.experimental.pallas import tpu as pltpu

F32 = jnp.float32
BF16 = jnp.bfloat16

D_MODEL = 1024
BATCH = 8
SEQ = 2048
MEM_TOKENS = 256
RET_HEADS, RET_DK, RET_DV, RET_CHUNK = 4, 64, 128, 128
DSA_HEADS, DSA_DH = 8, 64
IDX_HEADS, IDX_DIM = 8, 64
TOPK = min(256, SEQ // 4)
MEM_HEADS, MEM_DH = 4, 128
T5_BUCKETS, T5_MAX_DIST = 32, 128
D_FF = 2816
ROPE_BASE = 10000.0
LN_EPS = 1e-5
NEG_INF = -1e30
DEPTH = 1
ALPHA = (2.0 * DEPTH) ** 0.25

RET_QK_W = RET_HEADS * RET_DK
RET_V_W = RET_HEADS * RET_DV
DSA_W = DSA_HEADS * DSA_DH
IDX_Q_W = IDX_HEADS * IDX_DIM
MEM_W = MEM_HEADS * MEM_DH
N_TOK = BATCH * SEQ

V7X_VMEM_BYTES = 64 * 1024 * 1024
VMEM_LIMIT = V7X_VMEM_BYTES - 8 * 1024 * 1024
LANES = 128
SUBLANES = 8

FFN_ROWS = 512
FFN_CHUNK = 1408
PROJ_ROWS = 512
MERGE_ROWS = 512
MEMKV_ROWS = 512
MEMATT_ROWS = 512
DSA_T = 256
HEAD_PAD = LANES
BISECT_STEPS_PER_CHECK = 4
COUNT_ACCUMULATORS = 4

P_RQ, P_RK, P_RV, P_RG = 0, 256, 512, 1024
P_DQ, P_DK, P_IQ, P_MQ, P_IKW = 1536, 2048, 2560, 3072, 3584
P_COLS = 3712
W_DV0 = 2 * RET_QK_W + 2 * RET_V_W + 2 * DSA_W
W_IQ0 = W_DV0 + DSA_W
W_IK0 = W_IQ0 + IDX_Q_W
W_IW0 = W_IK0 + IDX_DIM
W_MQ0 = W_IW0 + IDX_HEADS
W_G0 = W_MQ0 + MEM_W
IW_SCALE = IDX_HEADS ** -0.5 * IDX_DIM ** -0.5


def _cparams(sem):
    return pltpu.CompilerParams(dimension_semantics=sem, vmem_limit_bytes=VMEM_LIMIT)


def _const_spec(shape):
    nd = len(shape)
    return pl.BlockSpec(shape, lambda *_: (0,) * nd, pipeline_mode=pl.Buffered(1))


def _layer_norm(y, g, b):
    mu = jnp.mean(y, axis=-1, keepdims=True)
    yc = y - mu
    var = jnp.mean(yc * yc, axis=-1, keepdims=True)
    return yc * lax.rsqrt(var + LN_EPS) * g + b


def _dot(a, b):
    return jnp.dot(a, b, preferred_element_type=F32)


def _dot_nt(a, b):
    return lax.dot_general(a, b, (((1,), (1,)), ((), ())), preferred_element_type=F32)


def _dot_tn(a, b):
    return lax.dot_general(a, b, (((0,), (0,)), ((), ())), preferred_element_type=F32)


def _ffn_ln_kernel(x_ref, wi_ref, wo_ref, g_ref, b_ref, o_ref):
    x = x_ref[...]
    xb = x.astype(BF16)
    acc = None
    for c in range(D_FF // FFN_CHUNK):
        lo = c * FFN_CHUNK
        a = _dot(xb, wi_ref[:, lo:lo + FFN_CHUNK])
        u = _dot(xb, wi_ref[:, D_FF + lo:D_FF + lo + FFN_CHUNK])
        act = (jax.nn.silu(a) * u).astype(BF16)
        part = _dot(act, wo_ref[lo:lo + FFN_CHUNK, :])
        acc = part if acc is None else acc + part
    o_ref[...] = _layer_norm(ALPHA * x + 0.5 * acc, g_ref[...], b_ref[...])


def _ffn_ln(x, w_in_bf, w_out_bf, g, b, name):
    return pl.pallas_call(
        _ffn_ln_kernel,
        out_shape=jax.ShapeDtypeStruct((N_TOK, D_MODEL), F32),
        grid=(N_TOK // FFN_ROWS,),
        in_specs=[
            pl.BlockSpec((FFN_ROWS, D_MODEL), lambda i: (i, 0)),
            _const_spec((D_MODEL, 2 * D_FF)),
            _const_spec((D_FF, D_MODEL)),
            _const_spec((1, D_MODEL)),
            _const_spec((1, D_MODEL)),
        ],
        out_specs=pl.BlockSpec((FFN_ROWS, D_MODEL), lambda i: (i, 0)),
        compiler_params=_cparams(("parallel",)),
        name=name,
    )(x, w_in_bf, w_out_bf, g, b)


def _rope(x, cos, sin_signed):
    width = x.shape[-1]
    lane = lax.broadcasted_iota(jnp.int32, x.shape, 1)
    first_half = (lane % RET_DK) < (RET_DK // 2)
    swapped = jnp.where(first_half,
                        pltpu.roll(x, width - RET_DK // 2, 1),
                        pltpu.roll(x, RET_DK // 2, 1))
    return x * cos + swapped * sin_signed


def _store_padded_heads(dst_ref, val):
    low = lax.broadcasted_iota(jnp.int32, (val.shape[0], LANES), 1) < DSA_DH
    for j in range(val.shape[1] // LANES):
        pair = val[:, j * LANES:(j + 1) * LANES]
        dst_ref[:, (2 * j) * LANES:(2 * j + 1) * LANES] = jnp.where(low, pair, 0.0).astype(dst_ref.dtype)
        dst_ref[:, (2 * j + 1) * LANES:(2 * j + 2) * LANES] = jnp.where(
            low, pltpu.roll(pair, DSA_DH, 1), 0.0).astype(dst_ref.dtype)


def _proj_kernel(h_ref, w_ref, wdvt_ref, cos_ref, sin_ref,
                 rq_ref, rk_ref, rv_ref, rg_ref, dq_ref, dk_ref, dvt_ref, iq_ref, mq_ref,
                 ikw_ref, ikb_ref):
    hb = h_ref[...].astype(BF16)

    def proj(lo, width):
        return _dot(hb, w_ref[:, lo:lo + width])

    cos = cos_ref[...]
    sin = sin_ref[...]
    rq_ref[...] = _rope(proj(P_RQ, RET_QK_W), cos, sin)
    rk_ref[...] = _rope(proj(P_RK, RET_QK_W), cos, sin) * (RET_DK ** -0.5)
    rv_ref[...] = proj(P_RV, RET_V_W).astype(BF16)
    rg_ref[...] = proj(P_RG, RET_V_W)
    _store_padded_heads(dq_ref, proj(P_DQ, DSA_W) * (DSA_DH ** -0.5))
    _store_padded_heads(dk_ref, proj(P_DK, DSA_W))
    _store_padded_heads(iq_ref, proj(P_IQ, IDX_Q_W))
    mq_ref[...] = proj(P_MQ, MEM_W).astype(BF16)
    ikw = proj(P_IKW, LANES)
    ikw_ref[...] = ikw
    ikb_ref[...] = ikw.astype(BF16)
    vt = _dot_nt(wdvt_ref[...], hb).astype(BF16)
    for j in range(PROJ_ROWS // DSA_T):
        dvt_ref[j] = vt[:, j * DSA_T:(j + 1) * DSA_T]


def _proj(h, w_proj_bf, w_dvt_bf, cos_t, sin_t):
    rows = lambda w: pl.BlockSpec((PROJ_ROWS, w), lambda i: (i, 0))
    seq_tiles = SEQ // PROJ_ROWS
    pos = lambda w: pl.BlockSpec((PROJ_ROWS, w), lambda i: (i % seq_tiles, 0))
    sd = jax.ShapeDtypeStruct
    padded = DSA_HEADS * HEAD_PAD
    slabs = PROJ_ROWS // DSA_T
    return pl.pallas_call(
        _proj_kernel,
        out_shape=(
            sd((N_TOK, RET_QK_W), F32), sd((N_TOK, RET_QK_W), F32),
            sd((N_TOK, RET_V_W), BF16), sd((N_TOK, RET_V_W), F32),
            sd((N_TOK, padded), BF16), sd((N_TOK, padded), BF16),
            sd((N_TOK // DSA_T, DSA_W, DSA_T), BF16),
            sd((N_TOK, padded), BF16), sd((N_TOK, MEM_W), BF16),
            sd((N_TOK, LANES), F32), sd((N_TOK, LANES), BF16),
        ),
        grid=(N_TOK // PROJ_ROWS,),
        in_specs=[rows(D_MODEL), _const_spec((D_MODEL, P_COLS)), _const_spec((DSA_W, D_MODEL)),
                  pos(RET_QK_W), pos(RET_QK_W)],
        out_specs=(rows(RET_QK_W), rows(RET_QK_W), rows(RET_V_W), rows(RET_V_W),
                   rows(padded), rows(padded),
                   pl.BlockSpec((slabs, DSA_W, DSA_T), lambda i: (i, 0, 0)),
                   rows(padded), rows(MEM_W), rows(LANES), rows(LANES)),
        compiler_params=_cparams(("parallel",)),
        name="mixer_proj",
    )(h, w_proj_bf, w_dvt_bf, cos_t, sin_t)


def _retention_kernel(q_ref, k_ref, v_ref, g_ref, din_ref, kdec_ref, qdec_ref, cdec_ref,
                      gng_ref, gnb_ref, o_ref, state_ref):
    @pl.when(pl.program_id(1) == 0)
    def _():
        state_ref[...] = jnp.zeros_like(state_ref)

    q = q_ref[...]
    k = k_ref[...]
    kd = k * kdec_ref[...]
    v = v_ref[...]
    qdec = qdec_ref[...]
    gate = g_ref[...]
    gn_g = gng_ref[...]
    gn_b = gnb_ref[...]
    for h in range(RET_HEADS):
        ks = slice(h * RET_DK, (h + 1) * RET_DK)
        vs = slice(h * RET_DV, (h + 1) * RET_DV)
        qh = q[:, ks].astype(BF16)
        kh = k[:, ks].astype(BF16)
        vh = v[:, vs]
        state = state_ref[h]
        scores = _dot_nt(qh, kh) * din_ref[h]
        intra = _dot(scores.astype(BF16), vh)
        cross = _dot(qh, state.astype(BF16)) * qdec[:, vs]
        kv = _dot_tn(kd[:, ks].astype(BF16), vh)
        state_ref[h] = cdec_ref[h] * state + kv
        o = intra + cross
        mu = jnp.mean(o, axis=-1, keepdims=True)
        oc = o - mu
        var = jnp.mean(oc * oc, axis=-1, keepdims=True)
        o = oc * lax.rsqrt(var + LN_EPS) * gn_g[:, vs] + gn_b[:, vs]
        o_ref[:, vs] = (jax.nn.silu(gate[:, vs]) * o).astype(BF16)


def _retention(rq, rk, rv, rg, din, kdec, qdec, cdec, gn_g, gn_b):
    nc = SEQ // RET_CHUNK
    rows = lambda w: pl.BlockSpec((RET_CHUNK, w), lambda b, n: (b * nc + n, 0))
    return pl.pallas_call(
        _retention_kernel,
        out_shape=jax.ShapeDtypeStruct((N_TOK, RET_V_W), BF16),
        grid=(BATCH, nc),
        in_specs=[rows(RET_QK_W), rows(RET_QK_W), rows(RET_V_W), rows(RET_V_W),
                  _const_spec((RET_HEADS, RET_CHUNK, RET_CHUNK)),
                  _const_spec((RET_CHUNK, RET_QK_W)),
                  _const_spec((RET_CHUNK, RET_V_W)),
                  _const_spec((RET_HEADS, RET_DK, RET_DV)),
                  _const_spec((1, RET_V_W)), _const_spec((1, RET_V_W))],
        out_specs=rows(RET_V_W),
        scratch_shapes=[pltpu.VMEM((RET_HEADS, RET_DK, RET_DV), F32)],
        compiler_params=_cparams(("parallel", "arbitrary")),
        name="retention",
    )(rq, rk, rv, rg, din, kdec, qdec, cdec, gn_g, gn_b)


def _memkv_kernel(m_ref, w_ref, o_ref):
    o_ref[...] = _dot(m_ref[...].astype(BF16), w_ref[...]).astype(BF16)


def _memkv(mem2d, w_bf):
    n = mem2d.shape[0]
    return pl.pallas_call(
        _memkv_kernel,
        out_shape=jax.ShapeDtypeStruct((n, 2 * MEM_W), BF16),
        grid=(n // MEMKV_ROWS,),
        in_specs=[pl.BlockSpec((MEMKV_ROWS, D_MODEL), lambda i: (i, 0)),
                  _const_spec((D_MODEL, 2 * MEM_W))],
        out_specs=pl.BlockSpec((MEMKV_ROWS, 2 * MEM_W), lambda i: (i, 0)),
        compiler_params=_cparams(("parallel",)),
        name="mem_kv",
    )(mem2d, w_bf)


def _mematt_kernel(q_ref, k_ref, v_ref, o_ref):
    q = q_ref[...]
    k = k_ref[...]
    v = v_ref[...]
    for h in range(MEM_HEADS):
        hs = slice(h * MEM_DH, (h + 1) * MEM_DH)
        logits = _dot_nt(q[:, hs], k[:, hs]) * (MEM_DH ** -0.5)
        m = jnp.max(logits, axis=-1, keepdims=True)
        p = jnp.exp(logits - m)
        denom = jnp.sum(p, axis=-1, keepdims=True)
        o_ref[:, hs] = (_dot(p.astype(BF16), v[:, hs]) / denom).astype(BF16)


def _mematt(mq, mkv):
    tiles = SEQ // MEMATT_ROWS
    return pl.pallas_call(
        _mematt_kernel,
        out_shape=jax.ShapeDtypeStruct((N_TOK, MEM_W), BF16),
        grid=(BATCH, tiles),
        in_specs=[pl.BlockSpec((MEMATT_ROWS, MEM_W), lambda b, i: (b * tiles + i, 0)),
                  pl.BlockSpec((MEM_TOKENS, MEM_W), lambda b, i: (b, 0)),
                  pl.BlockSpec((MEM_TOKENS, MEM_W), lambda b, i: (b, 1))],
        out_specs=pl.BlockSpec((MEMATT_ROWS, MEM_W), lambda b, i: (b * tiles + i, 0)),
        compiler_params=_cparams(("parallel", "parallel")),
        name="mem_attention",
    )(mq, mkv, mkv)


def _t5_bias_kernel(table_ref, o_ref):
    j = lax.broadcasted_iota(jnp.int32, (DSA_T, DSA_T), 0)
    r = lax.broadcasted_iota(jnp.int32, (DSA_T, DSA_T), 1)
    max_exact = T5_BUCKETS // 2
    for w in range(2):
        n = jnp.maximum(r + (1 - w) * DSA_T - j, 0)
        nf = jnp.maximum(n, 1).astype(F32)
        large = max_exact + (jnp.log(nf / max_exact) / math.log(T5_MAX_DIST / max_exact)
                             * (T5_BUCKETS - max_exact)).astype(jnp.int32)
        large = jnp.minimum(large, T5_BUCKETS - 1)
        bucket = jnp.where(n < max_exact, n, large)
        for h in range(DSA_HEADS):
            acc = jnp.zeros((DSA_T, DSA_T), F32)
            for b in range(T5_BUCKETS):
                acc = jnp.where(bucket == b, table_ref[b, h], acc)
            o_ref[w, h] = acc


def _t5_bias(t5_table):
    return pl.pallas_call(
        _t5_bias_kernel,
        out_shape=jax.ShapeDtypeStruct((2, DSA_HEADS, DSA_T, DSA_T), F32),
        in_specs=[pl.BlockSpec(memory_space=pltpu.SMEM)],
        out_specs=pl.BlockSpec(memory_space=pltpu.VMEM),
        compiler_params=pltpu.CompilerParams(vmem_limit_bytes=VMEM_LIMIT),
        name="t5_bias",
    )(t5_table)


def _fold_groups(x, op):
    return op(x.reshape(DSA_T // SUBLANES, SUBLANES, DSA_T), axis=0)


def _dsa_kernel(table_ref, bias_ref, iq_ref, dq_ref, ikw_q_ref, ikb_ref, dk_ref, dvt_ref,
                o_ref, score_ref, logit_ref, acc_ref):
    qb = pl.program_id(1)
    q0 = qb * DSA_T
    n_chunks = qb + 1
    groups = DSA_T // SUBLANES
    key_in_chunk = lax.broadcasted_iota(jnp.int32, (DSA_T, DSA_T), 0)
    query_in_step = lax.broadcasted_iota(jnp.int32, (DSA_T, DSA_T), 1)

    def key_rows(c):
        return pl.ds(pl.multiple_of(c * DSA_T, DSA_T), DSA_T)

    def head_cols(h):
        return slice(h * HEAD_PAD, (h + 1) * HEAD_PAD)

    iw_t = ikw_q_ref[...].T[IDX_DIM:IDX_DIM + IDX_HEADS, :] * IW_SCALE

    def index_scores(c):
        ik = ikb_ref[key_rows(c), :]
        s = None
        for h in range(IDX_HEADS):
            rel = _dot_nt(ik, iq_ref[:, head_cols(h)])
            term = jnp.maximum(rel, 0.0) * iw_t[h:h + 1, :]
            s = term if s is None else s + term
        return s

    def score_body(c, carry):
        score_ref[c] = index_scores(c)
        return carry

    lax.fori_loop(0, qb, score_body, 0)
    score_ref[qb] = jnp.where(key_in_chunk <= query_in_step, index_scores(qb), NEG_INF)

    def count_where(pred):
        def body(c, acc):
            parts = [acc] + [jnp.zeros((SUBLANES, DSA_T), F32)] * (COUNT_ACCUMULATORS - 1)
            for g in range(groups):
                blk = score_ref[c, g * SUBLANES:(g + 1) * SUBLANES, :]
                a = g % COUNT_ACCUMULATORS
                parts[a] = parts[a] + jnp.where(pred(c, g * SUBLANES, blk), 1.0, 0.0)
            while len(parts) > 1:
                parts = [parts[i] + parts[i + 1] for i in range(0, len(parts), 2)]
            return parts[0]
        acc = lax.fori_loop(0, n_chunks, body, jnp.zeros((SUBLANES, DSA_T), F32))
        return jnp.sum(acc, axis=0, keepdims=True)

    def count_ge(thr):
        thr_b = jnp.broadcast_to(thr, (SUBLANES, DSA_T))
        return count_where(lambda c, r0, blk: blk >= thr_b)

    def minmax_body(c, carry):
        mn, mx = carry
        blk = score_ref[c]
        mx = jnp.maximum(mx, _fold_groups(blk, jnp.max))
        mn = jnp.minimum(mn, _fold_groups(jnp.where(blk > 0.5 * NEG_INF, blk, -NEG_INF), jnp.min))
        return mn, mx

    mn, mx = lax.fori_loop(
        0, n_chunks, minmax_body,
        (jnp.full((SUBLANES, DSA_T), -NEG_INF, F32), jnp.full((SUBLANES, DSA_T), NEG_INF, F32)))
    q_min = jnp.min(mn, axis=0, keepdims=True)
    q_max = jnp.max(mx, axis=0, keepdims=True)
    t_q = q0 + lax.broadcasted_iota(jnp.int32, (1, DSA_T), 1)

    kf = float(TOPK)
    init = (q_min, q_max + jnp.maximum(jnp.abs(q_max), 1.0) * 1e-6,
            (t_q + 1).astype(F32), jnp.zeros((1, DSA_T), F32))

    def pending_of(carry):
        lo, hi, cnt_lo, _ = carry
        mid = 0.5 * (lo + hi)
        unresolved = (cnt_lo > kf) & (mid > lo) & (mid < hi)
        return jnp.max(jnp.where(unresolved, 1.0, 0.0))

    def bisect_step(carry):
        lo, hi, cnt_lo, cnt_hi = carry
        mid = 0.5 * (lo + hi)
        cnt = count_ge(mid)
        ge = cnt >= kf
        return (jnp.where(ge, mid, lo), jnp.where(ge, hi, mid),
                jnp.where(ge, cnt, cnt_lo), jnp.where(ge, cnt_hi, cnt))

    def while_body(state):
        carry, _ = state
        for _i in range(BISECT_STEPS_PER_CHECK):
            carry = bisect_step(carry)
        return carry, pending_of(carry)

    (lo, hi, cnt_lo, cnt_hi), _ = lax.while_loop(
        lambda state: state[1] > 0.5, while_body, (init, pending_of(init)))

    tied = cnt_lo > kf
    need = kf - cnt_hi
    lo_b = jnp.broadcast_to(lo, (SUBLANES, DSA_T))
    key_in_group = lax.broadcasted_iota(jnp.int32, (SUBLANES, DSA_T), 0)

    def tie_search(_):
        def body(_i, carry):
            lo_i, hi_i = carry
            mid_i = lax.shift_right_arithmetic(lo_i + hi_i, 1)
            mid_b = jnp.broadcast_to(mid_i, (SUBLANES, DSA_T))
            cnt = count_where(lambda c, r0, blk: (blk == lo_b) & (key_in_group + (c * DSA_T + r0) <= mid_b))
            enough = cnt >= need
            return jnp.where(enough, lo_i, mid_i), jnp.where(enough, mid_i, hi_i)
        lo_i = jnp.full((1, DSA_T), -1, jnp.int32)
        hi_i = jnp.full((1, DSA_T), SEQ - 1, jnp.int32)
        _, cut = lax.fori_loop(0, int(math.log2(SEQ)) + 1, body, (lo_i, hi_i))
        return jnp.where(tied, cut, SEQ)

    cut = lax.cond(jnp.max(jnp.where(tied, 1.0, 0.0)) > 0.5, tie_search,
                   lambda _: jnp.full((1, DSA_T), SEQ, jnp.int32), 0)

    def mask_body(c, carry):
        blk = score_ref[c]
        keep = (blk > lo) | ((blk == lo) & (key_in_chunk + c * DSA_T <= cut))
        score_ref[c] = jnp.where(keep, 0.0, NEG_INF)
        return carry

    lax.fori_loop(0, n_chunks, mask_body, 0)

    def qk_logits(c, h):
        kh = dk_ref[key_rows(c), head_cols(h)]
        return _dot_nt(kh, dq_ref[:, head_cols(h)]) + score_ref[c]

    def far_body(c, ms):
        out = []
        for h in range(DSA_HEADS):
            lg = qk_logits(c, h) + table_ref[T5_BUCKETS - 1, h]
            logit_ref[h, c] = lg
            out.append(jnp.maximum(ms[h], _fold_groups(lg, jnp.max)))
        return tuple(out)

    def near_body(c, ms):
        w = c - qb + 1
        out = []
        for h in range(DSA_HEADS):
            lg = qk_logits(c, h) + bias_ref[w, h]
            logit_ref[h, c] = lg
            out.append(jnp.maximum(ms[h], _fold_groups(lg, jnp.max)))
        return tuple(out)

    n_far = jnp.maximum(qb - 1, 0)
    ms = tuple(jnp.full((SUBLANES, DSA_T), NEG_INF, F32) for _ in range(DSA_HEADS))
    ms = lax.fori_loop(0, n_far, far_body, ms)
    ms = lax.fori_loop(n_far, n_chunks, near_body, ms)
    q_m = [jnp.max(m, axis=0, keepdims=True) for m in ms]

    acc_ref[...] = jnp.zeros_like(acc_ref)

    def pv_body(c, dens):
        out = []
        for h in range(DSA_HEADS):
            p = jnp.exp(logit_ref[h, c] - q_m[h])
            out.append(dens[h] + _fold_groups(p, jnp.sum))
            rows = slice(h * DSA_DH, (h + 1) * DSA_DH)
            acc_ref[rows, :] += _dot(dvt_ref[c, rows, :], p.astype(BF16))
        return tuple(out)

    dens = lax.fori_loop(0, n_chunks, pv_body,
                         tuple(jnp.zeros((SUBLANES, DSA_T), F32) for _ in range(DSA_HEADS)))
    for h in range(DSA_HEADS):
        rows = slice(h * DSA_DH, (h + 1) * DSA_DH)
        acc_ref[rows, :] = acc_ref[rows, :] / jnp.sum(dens[h], axis=0, keepdims=True)
    o_ref[...] = acc_ref[...].T.astype(BF16)


def _dsa(t5_table, bias, iq, dq, ikw, ikb, dk, dvt):
    nqb = SEQ // DSA_T
    padded = DSA_HEADS * HEAD_PAD
    qrows = lambda w: pl.BlockSpec((DSA_T, w), lambda b, i: (b * nqb + i, 0))
    krows = lambda w: pl.BlockSpec((SEQ, w), lambda b, i: (b, 0))
    return pl.pallas_call(
        _dsa_kernel,
        out_shape=jax.ShapeDtypeStruct((N_TOK, DSA_W), BF16),
        grid=(BATCH, nqb),
        in_specs=[pl.BlockSpec(memory_space=pltpu.SMEM),
                  _const_spec((2, DSA_HEADS, DSA_T, DSA_T)),
                  qrows(padded), qrows(padded), qrows(LANES),
                  krows(LANES), krows(padded),
                  pl.BlockSpec((nqb, DSA_W, DSA_T), lambda b, i: (b, 0, 0))],
        out_specs=qrows(DSA_W),
        scratch_shapes=[pltpu.VMEM((nqb, DSA_T, DSA_T), F32),
                        pltpu.VMEM((DSA_HEADS, nqb, DSA_T, DSA_T), F32),
                        pltpu.VMEM((DSA_W, DSA_T), F32)],
        compiler_params=_cparams(("parallel", "arbitrary")),
        name="dsa_attention",
    )(t5_table, bias, iq, dq, ikw, ikb, dk, dvt)


def _merge_kernel(h_ref, oret_ref, odsa_ref, omem_ref, wg_ref, wr_ref, wd_ref, wm_ref, wo_ref,
                  g_ref, b_ref, o_ref):
    h = h_ref[...]
    hb = h.astype(BF16)
    merged = None
    for i, (src, w) in enumerate(((oret_ref, wr_ref), (odsa_ref, wd_ref), (omem_ref, wm_ref))):
        gate = jax.nn.sigmoid(_dot(hb, wg_ref[:, i * D_MODEL:(i + 1) * D_MODEL]))
        term = gate * _dot(src[...], w[...])
        merged = term if merged is None else merged + term
    mix = _dot(merged.astype(BF16), wo_ref[...])
    o_ref[...] = _layer_norm(ALPHA * h + mix, g_ref[...], b_ref[...])


def _merge(h, o_ret, o_dsa, o_mem, wg, wr, wd, wm, wo, g, b):
    rows = lambda w: pl.BlockSpec((MERGE_ROWS, w), lambda i: (i, 0))
    return pl.pallas_call(
        _merge_kernel,
        out_shape=jax.ShapeDtypeStruct((N_TOK, D_MODEL), F32),
        grid=(N_TOK // MERGE_ROWS,),
        in_specs=[rows(D_MODEL), rows(RET_V_W), rows(DSA_W), rows(MEM_W),
                  _const_spec((D_MODEL, 3 * D_MODEL)),
                  _const_spec((RET_V_W, D_MODEL)), _const_spec((DSA_W, D_MODEL)),
                  _const_spec((MEM_W, D_MODEL)), _const_spec((D_MODEL, D_MODEL)),
                  _const_spec((1, D_MODEL)), _const_spec((1, D_MODEL))],
        out_specs=rows(D_MODEL),
        compiler_params=_cparams(("parallel",)),
        name="merge_ln2",
    )(h, o_ret, o_dsa, o_mem, wg, wr, wd, wm, wo, g, b)


def _rope_tables():
    half = RET_DK // 2
    freqs = ROPE_BASE ** (-jnp.arange(half, dtype=F32) / half)
    ang = jnp.arange(SEQ).astype(F32)[:, None] * freqs[None, :]
    cos, sin = jnp.cos(ang), jnp.sin(ang)
    cos_t = jnp.tile(jnp.concatenate([cos, cos], axis=-1), (1, RET_HEADS))
    sin_t = jnp.tile(jnp.concatenate([-sin, sin], axis=-1), (1, RET_HEADS))
    return cos_t, sin_t


def _decay_tables():
    c = RET_CHUNK
    gamma = 1.0 - 2.0 ** (-5.0 - jnp.arange(RET_HEADS, dtype=F32))
    lg = jnp.log(gamma)
    i = jnp.arange(c)
    diff = i[:, None] - i[None, :]
    decay_in = jnp.where(diff[None] >= 0, jnp.exp(jnp.maximum(diff, 0)[None] * lg[:, None, None]), 0.0).astype(F32)
    k_dec = jnp.exp((c - 1 - i)[None, :] * lg[:, None]).astype(F32)
    q_dec = jnp.exp((i + 1)[None, :] * lg[:, None]).astype(F32)
    chunk_dec = jnp.exp(c * lg).astype(F32)
    kdec_t = jnp.repeat(k_dec.T, RET_DK, axis=1)
    qdec_t = jnp.repeat(q_dec.T, RET_DV, axis=1)
    cdec_t = jnp.broadcast_to(chunk_dec[:, None, None], (RET_HEADS, RET_DK, RET_DV))
    return decay_in, kdec_t, qdec_t, cdec_t


def _layer(x, mem2d, ffn1_w_in, ffn1_w_out, ln1_g, ln1_b, w_in, t5_table, ret_gn_g, ret_gn_b,
           w_mem_kv, w_br_ret, w_br_dsa, w_br_mem, w_out, ln2_g, ln2_b,
           ffn2_w_in, ffn2_w_out, ln3_g, ln3_b, tables):
    cos_t, sin_t, din, kdec, qdec, cdec = tables
    row = lambda v: v.reshape(1, -1)
    bf = lambda w: w.astype(BF16)

    w_proj = jnp.concatenate(
        [w_in[:, :W_DV0], w_in[:, W_IQ0:W_IK0], w_in[:, W_MQ0:W_G0], w_in[:, W_IK0:W_MQ0],
         jnp.zeros((D_MODEL, LANES - IDX_DIM - IDX_HEADS), F32)], axis=1)
    w_dvt = w_in[:, W_DV0:W_IQ0].T

    h = _ffn_ln(x, bf(ffn1_w_in), bf(ffn1_w_out), row(ln1_g), row(ln1_b), "ffn1_ln1")
    rq, rk, rv, rg, dq, dk, dvt, iq, mq, ikw, ikb = _proj(h, bf(w_proj), bf(w_dvt), cos_t, sin_t)
    o_ret = _retention(rq, rk, rv, rg, din, kdec, qdec, cdec, row(ret_gn_g), row(ret_gn_b))
    o_mem = _mematt(mq, _memkv(mem2d, bf(w_mem_kv)))
    o_dsa = _dsa(t5_table, _t5_bias(t5_table), iq, dq, ikw, ikb, dk, dvt)
    x2 = _merge(h, o_ret, o_dsa, o_mem, bf(w_in[:, W_G0:]), bf(w_br_ret), bf(w_br_dsa), bf(w_br_mem),
                bf(w_out), row(ln2_g), row(ln2_b))
    return _ffn_ln(x2, bf(ffn2_w_in), bf(ffn2_w_out), row(ln3_g), row(ln3_b), "ffn2_ln3")


def kernel(x, mem, ffn1_w_in, ffn1_w_out, ln1_g, ln1_b, w_in, t5_table, ret_gn_g, ret_gn_b,
           w_mem_kv, w_br_ret, w_br_dsa, w_br_mem, w_out, ln2_g, ln2_b,
           ffn2_w_in, ffn2_w_out, ln3_g, ln3_b):
    assert x.shape == (BATCH, SEQ, D_MODEL) and mem.shape == (BATCH, MEM_TOKENS, D_MODEL)
    tables = _rope_tables() + _decay_tables()
    y = x.reshape(N_TOK, D_MODEL)
    mem2d = mem.reshape(BATCH * MEM_TOKENS, D_MODEL)
    for l in range(DEPTH):
        y = _layer(y, mem2d, ffn1_w_in[l], ffn1_w_out[l], ln1_g[l], ln1_b[l], w_in[l], t5_table,
                   ret_gn_g[l], ret_gn_b[l], w_mem_kv[l], w_br_ret[l], w_br_dsa[l], w_br_mem[l],
                   w_out[l], ln2_g[l], ln2_b[l], ffn2_w_in[l], ffn2_w_out[l], ln3_g[l], ln3_b[l],
                   tables)
    return y.reshape(BATCH, SEQ, D_MODEL)
```

```python
import math

import jax
import jax.numpy as jnp
from jax import lax
from jax.experimental import pallas as pl
from jax.experimental.pallas import tpu as pltpu

F32 = jnp.float32
BF16 = jnp.bfloat16

D_MODEL = 1024
BATCH = 8
SEQ = 2048
MEM_TOKENS = 256
RET_HEADS, RET_DK, RET_DV, RET_CHUNK = 4, 64, 128, 128
DSA_HEADS, DSA_DH = 8, 64
IDX_HEADS, IDX_DIM = 8, 64
TOPK = min(256, SEQ // 4)
MEM_HEADS, MEM_DH = 4, 128
T5_BUCKETS, T5_MAX_DIST = 32, 128
D_FF = 2816
ROPE_BASE = 10000.0
LN_EPS = 1e-5
NEG_INF = -1e30
DEPTH = 1
ALPHA = (2.0 * DEPTH) ** 0.25

RET_QK_W = RET_HEADS * RET_DK
RET_V_W = RET_HEADS * RET_DV
DSA_W = DSA_HEADS * DSA_DH
IDX_Q_W = IDX_HEADS * IDX_DIM
MEM_W = MEM_HEADS * MEM_DH
N_TOK = BATCH * SEQ

V7X_VMEM_BYTES = 64 * 1024 * 1024
VMEM_LIMIT = V7X_VMEM_BYTES - 8 * 1024 * 1024
LANES = 128
SUBLANES = 8

FFN_ROWS = 512
FFN_CHUNK = 1408
PROJ_ROWS = 512
MERGE_ROWS = 512
MEMKV_ROWS = 512
MEMATT_ROWS = 512
DSA_T = 256
HEAD_PAD = LANES
BISECT_PLAIN_STEPS = 14
COUNT_ACCUMULATORS = 4
LOG2E = math.log2(math.e)

P_RQ, P_RK, P_RV, P_RG = 0, 256, 512, 1024
P_DQ, P_DK, P_IQ, P_MQ, P_IKW = 1536, 2048, 2560, 3072, 3584
P_COLS = 3712
W_DV0 = 2 * RET_QK_W + 2 * RET_V_W + 2 * DSA_W
W_IQ0 = W_DV0 + DSA_W
W_IK0 = W_IQ0 + IDX_Q_W
W_IW0 = W_IK0 + IDX_DIM
W_MQ0 = W_IW0 + IDX_HEADS
W_G0 = W_MQ0 + MEM_W
IW_SCALE = IDX_HEADS ** -0.5 * IDX_DIM ** -0.5


def _cparams(sem):
    return pltpu.CompilerParams(dimension_semantics=sem, vmem_limit_bytes=VMEM_LIMIT)


def _const_spec(shape):
    nd = len(shape)
    return pl.BlockSpec(shape, lambda *_: (0,) * nd, pipeline_mode=pl.Buffered(1))


def _layer_norm(y, g, b):
    mu = jnp.mean(y, axis=-1, keepdims=True)
    yc = y - mu
    var = jnp.mean(yc * yc, axis=-1, keepdims=True)
    return yc * lax.rsqrt(var + LN_EPS) * g + b


def _dot(a, b):
    return jnp.dot(a, b, preferred_element_type=F32)


def _dot_nt(a, b):
    return lax.dot_general(a, b, (((1,), (1,)), ((), ())), preferred_element_type=F32)


def _dot_tn(a, b):
    return lax.dot_general(a, b, (((0,), (0,)), ((), ())), preferred_element_type=F32)


def _ffn_ln_kernel(x_ref, wi_ref, wo_ref, g_ref, b_ref, o_ref):
    x = x_ref[...]
    xb = x.astype(BF16)
    acc = None
    for c in range(D_FF // FFN_CHUNK):
        lo = c * FFN_CHUNK
        a = _dot(xb, wi_ref[:, lo:lo + FFN_CHUNK])
        u = _dot(xb, wi_ref[:, D_FF + lo:D_FF + lo + FFN_CHUNK])
        act = (jax.nn.silu(a) * u).astype(BF16)
        part = _dot(act, wo_ref[lo:lo + FFN_CHUNK, :])
        acc = part if acc is None else acc + part
    o_ref[...] = _layer_norm(ALPHA * x + 0.5 * acc, g_ref[...], b_ref[...])


def _ffn_ln(x, w_in_bf, w_out_bf, g, b, name):
    return pl.pallas_call(
        _ffn_ln_kernel,
        out_shape=jax.ShapeDtypeStruct((N_TOK, D_MODEL), F32),
        grid=(N_TOK // FFN_ROWS,),
        in_specs=[
            pl.BlockSpec((FFN_ROWS, D_MODEL), lambda i: (i, 0)),
            _const_spec((D_MODEL, 2 * D_FF)),
            _const_spec((D_FF, D_MODEL)),
            _const_spec((1, D_MODEL)),
            _const_spec((1, D_MODEL)),
        ],
        out_specs=pl.BlockSpec((FFN_ROWS, D_MODEL), lambda i: (i, 0)),
        compiler_params=_cparams(("parallel",)),
        name=name,
    )(x, w_in_bf, w_out_bf, g, b)


def _rope(x, cos, sin_signed):
    width = x.shape[-1]
    lane = lax.broadcasted_iota(jnp.int32, x.shape, 1)
    first_half = (lane % RET_DK) < (RET_DK // 2)
    swapped = jnp.where(first_half,
                        pltpu.roll(x, width - RET_DK // 2, 1),
                        pltpu.roll(x, RET_DK // 2, 1))
    return x * cos + swapped * sin_signed


def _store_padded_heads(dst_ref, val):
    low = lax.broadcasted_iota(jnp.int32, (val.shape[0], LANES), 1) < DSA_DH
    for j in range(val.shape[1] // LANES):
        pair = val[:, j * LANES:(j + 1) * LANES]
        dst_ref[:, (2 * j) * LANES:(2 * j + 1) * LANES] = jnp.where(low, pair, 0.0).astype(dst_ref.dtype)
        dst_ref[:, (2 * j + 1) * LANES:(2 * j + 2) * LANES] = jnp.where(
            low, pltpu.roll(pair, DSA_DH, 1), 0.0).astype(dst_ref.dtype)


def _proj_kernel(h_ref, w_ref, wdvt_ref, cos_ref, sin_ref,
                 rq_ref, rk_ref, rv_ref, rg_ref, dq_ref, dk_ref, dvt_ref, iq_ref, mq_ref,
                 ikw_ref, ikb_ref):
    hb = h_ref[...].astype(BF16)

    def proj(lo, width):
        return _dot(hb, w_ref[:, lo:lo + width])

    cos = cos_ref[...]
    sin = sin_ref[...]
    rq_ref[...] = _rope(proj(P_RQ, RET_QK_W), cos, sin)
    rk_ref[...] = _rope(proj(P_RK, RET_QK_W), cos, sin) * (RET_DK ** -0.5)
    rv_ref[...] = proj(P_RV, RET_V_W).astype(BF16)
    rg_ref[...] = proj(P_RG, RET_V_W)
    _store_padded_heads(dq_ref, proj(P_DQ, DSA_W) * (DSA_DH ** -0.5 * LOG2E))
    _store_padded_heads(dk_ref, proj(P_DK, DSA_W))
    _store_padded_heads(iq_ref, proj(P_IQ, IDX_Q_W))
    mq_ref[...] = proj(P_MQ, MEM_W).astype(BF16)
    ikw = proj(P_IKW, LANES)
    ikw_ref[...] = ikw
    ikb_ref[...] = ikw.astype(BF16)
    vt = _dot_nt(wdvt_ref[...], hb).astype(BF16)
    for j in range(PROJ_ROWS // DSA_T):
        dvt_ref[j] = vt[:, j * DSA_T:(j + 1) * DSA_T]


def _proj(h, w_proj_bf, w_dvt_bf, cos_t, sin_t):
    rows = lambda w: pl.BlockSpec((PROJ_ROWS, w), lambda i: (i, 0))
    seq_tiles = SEQ // PROJ_ROWS
    pos = lambda w: pl.BlockSpec((PROJ_ROWS, w), lambda i: (i % seq_tiles, 0))
    sd = jax.ShapeDtypeStruct
    padded = DSA_HEADS * HEAD_PAD
    slabs = PROJ_ROWS // DSA_T
    return pl.pallas_call(
        _proj_kernel,
        out_shape=(
            sd((N_TOK, RET_QK_W), F32), sd((N_TOK, RET_QK_W), F32),
            sd((N_TOK, RET_V_W), BF16), sd((N_TOK, RET_V_W), F32),
            sd((N_TOK, padded), BF16), sd((N_TOK, padded), BF16),
            sd((N_TOK // DSA_T, DSA_W, DSA_T), BF16),
            sd((N_TOK, padded), BF16), sd((N_TOK, MEM_W), BF16),
            sd((N_TOK, LANES), F32), sd((N_TOK, LANES), BF16),
        ),
        grid=(N_TOK // PROJ_ROWS,),
        in_specs=[rows(D_MODEL), _const_spec((D_MODEL, P_COLS)), _const_spec((DSA_W, D_MODEL)),
                  pos(RET_QK_W), pos(RET_QK_W)],
        out_specs=(rows(RET_QK_W), rows(RET_QK_W), rows(RET_V_W), rows(RET_V_W),
                   rows(padded), rows(padded),
                   pl.BlockSpec((slabs, DSA_W, DSA_T), lambda i: (i, 0, 0)),
                   rows(padded), rows(MEM_W), rows(LANES), rows(LANES)),
        compiler_params=_cparams(("parallel",)),
        name="mixer_proj",
    )(h, w_proj_bf, w_dvt_bf, cos_t, sin_t)


def _retention_kernel(q_ref, k_ref, v_ref, g_ref, din_ref, kdec_ref, qdec_ref, cdec_ref,
                      gng_ref, gnb_ref, o_ref, state_ref):
    @pl.when(pl.program_id(1) == 0)
    def _():
        state_ref[...] = jnp.zeros_like(state_ref)

    q = q_ref[...]
    k = k_ref[...]
    kd = k * kdec_ref[...]
    v = v_ref[...]
    qdec = qdec_ref[...]
    gate = g_ref[...]
    gn_g = gng_ref[...]
    gn_b = gnb_ref[...]
    for h in range(RET_HEADS):
        ks = slice(h * RET_DK, (h + 1) * RET_DK)
        vs = slice(h * RET_DV, (h + 1) * RET_DV)
        qh = q[:, ks].astype(BF16)
        kh = k[:, ks].astype(BF16)
        vh = v[:, vs]
        state = state_ref[h]
        scores = _dot_nt(qh, kh) * din_ref[h]
        intra = _dot(scores.astype(BF16), vh)
        cross = _dot(qh, state.astype(BF16)) * qdec[:, vs]
        kv = _dot_tn(kd[:, ks].astype(BF16), vh)
        state_ref[h] = cdec_ref[h] * state + kv
        o = intra + cross
        mu = jnp.mean(o, axis=-1, keepdims=True)
        oc = o - mu
        var = jnp.mean(oc * oc, axis=-1, keepdims=True)
        o = oc * lax.rsqrt(var + LN_EPS) * gn_g[:, vs] + gn_b[:, vs]
        o_ref[:, vs] = (jax.nn.silu(gate[:, vs]) * o).astype(BF16)


def _retention(rq, rk, rv, rg, din, kdec, qdec, cdec, gn_g, gn_b):
    nc = SEQ // RET_CHUNK
    rows = lambda w: pl.BlockSpec((RET_CHUNK, w), lambda b, n: (b * nc + n, 0))
    return pl.pallas_call(
        _retention_kernel,
        out_shape=jax.ShapeDtypeStruct((N_TOK, RET_V_W), BF16),
        grid=(BATCH, nc),
        in_specs=[rows(RET_QK_W), rows(RET_QK_W), rows(RET_V_W), rows(RET_V_W),
                  _const_spec((RET_HEADS, RET_CHUNK, RET_CHUNK)),
                  _const_spec((RET_CHUNK, RET_QK_W)),
                  _const_spec((RET_CHUNK, RET_V_W)),
                  _const_spec((RET_HEADS, RET_DK, RET_DV)),
                  _const_spec((1, RET_V_W)), _const_spec((1, RET_V_W))],
        out_specs=rows(RET_V_W),
        scratch_shapes=[pltpu.VMEM((RET_HEADS, RET_DK, RET_DV), F32)],
        compiler_params=_cparams(("parallel", "arbitrary")),
        name="retention",
    )(rq, rk, rv, rg, din, kdec, qdec, cdec, gn_g, gn_b)


def _memkv_kernel(m_ref, w_ref, o_ref):
    o_ref[...] = _dot(m_ref[...].astype(BF16), w_ref[...]).astype(BF16)


def _memkv(mem2d, w_bf):
    n = mem2d.shape[0]
    return pl.pallas_call(
        _memkv_kernel,
        out_shape=jax.ShapeDtypeStruct((n, 2 * MEM_W), BF16),
        grid=(n // MEMKV_ROWS,),
        in_specs=[pl.BlockSpec((MEMKV_ROWS, D_MODEL), lambda i: (i, 0)),
                  _const_spec((D_MODEL, 2 * MEM_W))],
        out_specs=pl.BlockSpec((MEMKV_ROWS, 2 * MEM_W), lambda i: (i, 0)),
        compiler_params=_cparams(("parallel",)),
        name="mem_kv",
    )(mem2d, w_bf)


def _mematt_kernel(q_ref, k_ref, v_ref, o_ref):
    q = q_ref[...]
    k = k_ref[...]
    v = v_ref[...]
    for h in range(MEM_HEADS):
        hs = slice(h * MEM_DH, (h + 1) * MEM_DH)
        logits = _dot_nt(q[:, hs], k[:, hs]) * (MEM_DH ** -0.5)
        m = jnp.max(logits, axis=-1, keepdims=True)
        p = jnp.exp(logits - m)
        denom = jnp.sum(p, axis=-1, keepdims=True)
        o_ref[:, hs] = (_dot(p.astype(BF16), v[:, hs]) / denom).astype(BF16)


def _mematt(mq, mkv):
    tiles = SEQ // MEMATT_ROWS
    return pl.pallas_call(
        _mematt_kernel,
        out_shape=jax.ShapeDtypeStruct((N_TOK, MEM_W), BF16),
        grid=(BATCH, tiles),
        in_specs=[pl.BlockSpec((MEMATT_ROWS, MEM_W), lambda b, i: (b * tiles + i, 0)),
                  pl.BlockSpec((MEM_TOKENS, MEM_W), lambda b, i: (b, 0)),
                  pl.BlockSpec((MEM_TOKENS, MEM_W), lambda b, i: (b, 1))],
        out_specs=pl.BlockSpec((MEMATT_ROWS, MEM_W), lambda b, i: (b * tiles + i, 0)),
        compiler_params=_cparams(("parallel", "parallel")),
        name="mem_attention",
    )(mq, mkv, mkv)


def _t5_bias_kernel(table_ref, o_ref):
    j = lax.broadcasted_iota(jnp.int32, (DSA_T, DSA_T), 0)
    r = lax.broadcasted_iota(jnp.int32, (DSA_T, DSA_T), 1)
    max_exact = T5_BUCKETS // 2
    for w in range(2):
        n = jnp.maximum(r + (1 - w) * DSA_T - j, 0)
        nf = jnp.maximum(n, 1).astype(F32)
        large = max_exact + (jnp.log(nf / max_exact) / math.log(T5_MAX_DIST / max_exact)
                             * (T5_BUCKETS - max_exact)).astype(jnp.int32)
        large = jnp.minimum(large, T5_BUCKETS - 1)
        bucket = jnp.where(n < max_exact, n, large)
        for h in range(DSA_HEADS):
            acc = jnp.zeros((DSA_T, DSA_T), F32)
            for b in range(T5_BUCKETS):
                acc = jnp.where(bucket == b, table_ref[b, h] * LOG2E, acc)
            o_ref[w, h] = acc


def _t5_bias(t5_table):
    return pl.pallas_call(
        _t5_bias_kernel,
        out_shape=jax.ShapeDtypeStruct((2, DSA_HEADS, DSA_T, DSA_T), F32),
        in_specs=[pl.BlockSpec(memory_space=pltpu.SMEM)],
        out_specs=pl.BlockSpec(memory_space=pltpu.VMEM),
        compiler_params=pltpu.CompilerParams(vmem_limit_bytes=VMEM_LIMIT),
        name="t5_bias",
    )(t5_table)


def _fold_groups(x, op):
    return op(x.reshape(DSA_T // SUBLANES, SUBLANES, DSA_T), axis=0)


def _dsa_kernel(table_ref, bias_ref, iq_ref, dq_ref, ikw_q_ref, ikb_ref, dk_ref, dvt_ref,
                o_ref, score_ref, logit_ref, acc_ref):
    qb = pl.program_id(1)
    q0 = qb * DSA_T
    n_chunks = qb + 1
    groups = DSA_T // SUBLANES
    key_in_chunk = lax.broadcasted_iota(jnp.int32, (DSA_T, DSA_T), 0)
    query_in_step = lax.broadcasted_iota(jnp.int32, (DSA_T, DSA_T), 1)

    def key_rows(c):
        return pl.ds(pl.multiple_of(c * DSA_T, DSA_T), DSA_T)

    def head_cols(h):
        return slice(h * HEAD_PAD, (h + 1) * HEAD_PAD)

    iw_t = ikw_q_ref[...].T[IDX_DIM:IDX_DIM + IDX_HEADS, :] * IW_SCALE

    def index_scores(c):
        ik = ikb_ref[key_rows(c), :]
        s = None
        for h in range(IDX_HEADS):
            rel = _dot_nt(ik, iq_ref[:, head_cols(h)])
            term = jnp.maximum(rel, 0.0) * iw_t[h:h + 1, :]
            s = term if s is None else s + term
        return s

    def score_body(c, carry):
        score_ref[c] = index_scores(c)
        return carry

    lax.fori_loop(0, qb, score_body, 0)
    score_ref[qb] = jnp.where(key_in_chunk <= query_in_step, index_scores(qb), NEG_INF)

    def count_where(pred):
        def body(c, acc):
            parts = [acc] + [jnp.zeros((SUBLANES, DSA_T), F32)] * (COUNT_ACCUMULATORS - 1)
            for g in range(groups):
                blk = score_ref[c, g * SUBLANES:(g + 1) * SUBLANES, :]
                a = g % COUNT_ACCUMULATORS
                parts[a] = parts[a] + jnp.where(pred(c, g * SUBLANES, blk), 1.0, 0.0)
            while len(parts) > 1:
                parts = [parts[i] + parts[i + 1] for i in range(0, len(parts), 2)]
            return parts[0]
        acc = lax.fori_loop(0, n_chunks, body, jnp.zeros((SUBLANES, DSA_T), F32))
        return jnp.sum(acc, axis=0, keepdims=True)

    def count_ge(thr):
        thr_b = jnp.broadcast_to(thr, (SUBLANES, DSA_T))
        return count_where(lambda c, r0, blk: blk >= thr_b)

    def minmax_body(c, carry):
        mn, mx = carry
        blk = score_ref[c]
        mx = jnp.maximum(mx, _fold_groups(blk, jnp.max))
        mn = jnp.minimum(mn, _fold_groups(jnp.where(blk > 0.5 * NEG_INF, blk, -NEG_INF), jnp.min))
        return mn, mx

    mn, mx = lax.fori_loop(
        0, n_chunks, minmax_body,
        (jnp.full((SUBLANES, DSA_T), -NEG_INF, F32), jnp.full((SUBLANES, DSA_T), NEG_INF, F32)))
    q_min = jnp.min(mn, axis=0, keepdims=True)
    q_max = jnp.max(mx, axis=0, keepdims=True)
    t_q = q0 + lax.broadcasted_iota(jnp.int32, (1, DSA_T), 1)

    kf = float(TOPK)
    init = (q_min, q_max + jnp.maximum(jnp.abs(q_max), 1.0) * 1e-6,
            (t_q + 1).astype(F32), jnp.zeros((1, DSA_T), F32))

    def bisect_step(_i, carry):
        lo, hi, cnt_lo, cnt_hi = carry
        mid = 0.5 * (lo + hi)
        cnt = count_ge(mid)
        ge = cnt >= kf
        return (jnp.where(ge, mid, lo), jnp.where(ge, hi, mid),
                jnp.where(ge, cnt, cnt_lo), jnp.where(ge, cnt_hi, cnt))

    lo, hi, cnt_lo, cnt_hi = lax.fori_loop(0, BISECT_PLAIN_STEPS, bisect_step, init)

    big = -NEG_INF

    def snap_body(c, carry):
        mn, mx = carry
        blk = score_ref[c]
        mn = jnp.minimum(mn, _fold_groups(jnp.where(blk >= lo, blk, big), jnp.min))
        mx = jnp.maximum(mx, _fold_groups(jnp.where(blk < hi, blk, -big), jnp.max))
        return mn, mx

    mn, mx = lax.fori_loop(
        0, n_chunks, snap_body,
        (jnp.full((SUBLANES, DSA_T), big, F32), jnp.full((SUBLANES, DSA_T), -big, F32)))
    lo = jnp.min(mn, axis=0, keepdims=True)
    top = jnp.max(mx, axis=0, keepdims=True)

    def pending_of(carry):
        lo, top, cnt_lo, _ = carry
        return jnp.max(jnp.where((cnt_lo > kf) & (lo < top), 1.0, 0.0))

    def value_step(state):
        (lo, top, cnt_lo, cnt_hi), _ = state
        unresolved = (cnt_lo > kf) & (lo < top)
        mid = lo + (top - lo) * 0.5
        mid = jnp.minimum(jnp.where(mid > lo, mid, top), top)

        def body(c, carry):
            cnt, mn, mx = carry
            blk = score_ref[c]
            is_ge = blk >= mid
            cnt = cnt + _fold_groups(jnp.where(is_ge, 1.0, 0.0), jnp.sum)
            mn = jnp.minimum(mn, _fold_groups(jnp.where(is_ge, blk, big), jnp.min))
            mx = jnp.maximum(mx, _fold_groups(jnp.where(is_ge, -big, blk), jnp.max))
            return cnt, mn, mx

        cnt, mn, mx = lax.fori_loop(
            0, n_chunks, body,
            (jnp.zeros((SUBLANES, DSA_T), F32), jnp.full((SUBLANES, DSA_T), big, F32),
             jnp.full((SUBLANES, DSA_T), -big, F32)))
        cnt = jnp.sum(cnt, axis=0, keepdims=True)
        ge = cnt >= kf
        raise_lo = unresolved & ge
        lower_top = unresolved & jnp.logical_not(ge)
        carry = (jnp.where(raise_lo, jnp.min(mn, axis=0, keepdims=True), lo),
                 jnp.where(lower_top, jnp.max(mx, axis=0, keepdims=True), top),
                 jnp.where(raise_lo, cnt, cnt_lo), jnp.where(lower_top, cnt, cnt_hi))
        return carry, pending_of(carry)

    start = (lo, top, cnt_lo, cnt_hi)
    (lo, top, cnt_lo, cnt_hi), _ = lax.while_loop(
        lambda state: state[1] > 0.5, value_step, (start, pending_of(start)))

    tied = cnt_lo > kf
    need = kf - cnt_hi
    lo_b = jnp.broadcast_to(lo, (SUBLANES, DSA_T))
    key_in_group = lax.broadcasted_iota(jnp.int32, (SUBLANES, DSA_T), 0)

    def tie_search(_):
        def body(_i, carry):
            lo_i, hi_i = carry
            mid_i = lax.shift_right_arithmetic(lo_i + hi_i, 1)
            mid_b = jnp.broadcast_to(mid_i, (SUBLANES, DSA_T))
            cnt = count_where(lambda c, r0, blk: (blk == lo_b) & (key_in_group + (c * DSA_T + r0) <= mid_b))
            enough = cnt >= need
            return jnp.where(enough, lo_i, mid_i), jnp.where(enough, mid_i, hi_i)
        lo_i = jnp.full((1, DSA_T), -1, jnp.int32)
        hi_i = jnp.full((1, DSA_T), SEQ - 1, jnp.int32)
        _, cut = lax.fori_loop(0, int(math.log2(SEQ)) + 1, body, (lo_i, hi_i))
        return jnp.where(tied, cut, SEQ)

    cut = lax.cond(jnp.max(jnp.where(tied, 1.0, 0.0)) > 0.5, tie_search,
                   lambda _: jnp.full((1, DSA_T), SEQ, jnp.int32), 0)

    def mask_body(c, carry):
        blk = score_ref[c]
        keep = (blk > lo) | ((blk == lo) & (key_in_chunk + c * DSA_T <= cut))
        score_ref[c] = jnp.where(keep, 0.0, NEG_INF)
        return carry

    lax.fori_loop(0, n_chunks, mask_body, 0)

    def qk_logits(c, h):
        kh = dk_ref[key_rows(c), head_cols(h)]
        return _dot_nt(kh, dq_ref[:, head_cols(h)]) + score_ref[c]

    def far_body(c, ms):
        out = []
        for h in range(DSA_HEADS):
            lg = qk_logits(c, h) + table_ref[T5_BUCKETS - 1, h] * LOG2E
            logit_ref[h, c] = lg
            out.append(jnp.maximum(ms[h], _fold_groups(lg, jnp.max)))
        return tuple(out)

    def near_body(c, ms):
        w = c - qb + 1
        out = []
        for h in range(DSA_HEADS):
            lg = qk_logits(c, h) + bias_ref[w, h]
            logit_ref[h, c] = lg
            out.append(jnp.maximum(ms[h], _fold_groups(lg, jnp.max)))
        return tuple(out)

    n_far = jnp.maximum(qb - 1, 0)
    ms = tuple(jnp.full((SUBLANES, DSA_T), NEG_INF, F32) for _ in range(DSA_HEADS))
    ms = lax.fori_loop(0, n_far, far_body, ms)
    ms = lax.fori_loop(n_far, n_chunks, near_body, ms)
    q_m = [jnp.max(m, axis=0, keepdims=True) for m in ms]

    acc_ref[...] = jnp.zeros_like(acc_ref)

    def pv_body(c, dens):
        out = []
        for h in range(DSA_HEADS):
            p = jnp.exp2(logit_ref[h, c] - q_m[h])
            out.append(dens[h] + _fold_groups(p, jnp.sum))
            rows = slice(h * DSA_DH, (h + 1) * DSA_DH)
            acc_ref[rows, :] += _dot(dvt_ref[c, rows, :], p.astype(BF16))
        return tuple(out)

    dens = lax.fori_loop(0, n_chunks, pv_body,
                         tuple(jnp.zeros((SUBLANES, DSA_T), F32) for _ in range(DSA_HEADS)))
    for h in range(DSA_HEADS):
        rows = slice(h * DSA_DH, (h + 1) * DSA_DH)
        acc_ref[rows, :] = acc_ref[rows, :] / jnp.sum(dens[h], axis=0, keepdims=True)
    o_ref[...] = acc_ref[...].T.astype(BF16)


def _dsa(t5_table, bias, iq, dq, ikw, ikb, dk, dvt):
    nqb = SEQ // DSA_T
    padded = DSA_HEADS * HEAD_PAD
    qrows = lambda w: pl.BlockSpec((DSA_T, w), lambda b, i: (b * nqb + i, 0))
    krows = lambda w: pl.BlockSpec((SEQ, w), lambda b, i: (b, 0))
    return pl.pallas_call(
        _dsa_kernel,
        out_shape=jax.ShapeDtypeStruct((N_TOK, DSA_W), BF16),
        grid=(BATCH, nqb),
        in_specs=[pl.BlockSpec(memory_space=pltpu.SMEM),
                  _const_spec((2, DSA_HEADS, DSA_T, DSA_T)),
                  qrows(padded), qrows(padded), qrows(LANES),
                  krows(LANES), krows(padded),
                  pl.BlockSpec((nqb, DSA_W, DSA_T), lambda b, i: (b, 0, 0))],
        out_specs=qrows(DSA_W),
        scratch_shapes=[pltpu.VMEM((nqb, DSA_T, DSA_T), F32),
                        pltpu.VMEM((DSA_HEADS, nqb, DSA_T, DSA_T), F32),
                        pltpu.VMEM((DSA_W, DSA_T), F32)],
        compiler_params=_cparams(("parallel", "arbitrary")),
        name="dsa_attention",
    )(t5_table, bias, iq, dq, ikw, ikb, dk, dvt)


def _merge_kernel(h_ref, oret_ref, odsa_ref, omem_ref, wg_ref, wr_ref, wd_ref, wm_ref, wo_ref,
                  g_ref, b_ref, o_ref):
    h = h_ref[...]
    hb = h.astype(BF16)
    merged = None
    for i, (src, w) in enumerate(((oret_ref, wr_ref), (odsa_ref, wd_ref), (omem_ref, wm_ref))):
        gate = jax.nn.sigmoid(_dot(hb, wg_ref[:, i * D_MODEL:(i + 1) * D_MODEL]))
        term = gate * _dot(src[...], w[...])
        merged = term if merged is None else merged + term
    mix = _dot(merged.astype(BF16), wo_ref[...])
    o_ref[...] = _layer_norm(ALPHA * h + mix, g_ref[...], b_ref[...])


def _merge(h, o_ret, o_dsa, o_mem, wg, wr, wd, wm, wo, g, b):
    rows = lambda w: pl.BlockSpec((MERGE_ROWS, w), lambda i: (i, 0))
    return pl.pallas_call(
        _merge_kernel,
        out_shape=jax.ShapeDtypeStruct((N_TOK, D_MODEL), F32),
        grid=(N_TOK // MERGE_ROWS,),
        in_specs=[rows(D_MODEL), rows(RET_V_W), rows(DSA_W), rows(MEM_W),
                  _const_spec((D_MODEL, 3 * D_MODEL)),
                  _const_spec((RET_V_W, D_MODEL)), _const_spec((DSA_W, D_MODEL)),
                  _const_spec((MEM_W, D_MODEL)), _const_spec((D_MODEL, D_MODEL)),
                  _const_spec((1, D_MODEL)), _const_spec((1, D_MODEL))],
        out_specs=rows(D_MODEL),
        compiler_params=_cparams(("parallel",)),
        name="merge_ln2",
    )(h, o_ret, o_dsa, o_mem, wg, wr, wd, wm, wo, g, b)


def _rope_tables():
    half = RET_DK // 2
    freqs = ROPE_BASE ** (-jnp.arange(half, dtype=F32) / half)
    ang = jnp.arange(SEQ).astype(F32)[:, None] * freqs[None, :]
    cos, sin = jnp.cos(ang), jnp.sin(ang)
    cos_t = jnp.tile(jnp.concatenate([cos, cos], axis=-1), (1, RET_HEADS))
    sin_t = jnp.tile(jnp.concatenate([-sin, sin], axis=-1), (1, RET_HEADS))
    return cos_t, sin_t


def _decay_tables():
    c = RET_CHUNK
    gamma = 1.0 - 2.0 ** (-5.0 - jnp.arange(RET_HEADS, dtype=F32))
    lg = jnp.log(gamma)
    i = jnp.arange(c)
    diff = i[:, None] - i[None, :]
    decay_in = jnp.where(diff[None] >= 0, jnp.exp(jnp.maximum(diff, 0)[None] * lg[:, None, None]), 0.0).astype(F32)
    k_dec = jnp.exp((c - 1 - i)[None, :] * lg[:, None]).astype(F32)
    q_dec = jnp.exp((i + 1)[None, :] * lg[:, None]).astype(F32)
    chunk_dec = jnp.exp(c * lg).astype(F32)
    kdec_t = jnp.repeat(k_dec.T, RET_DK, axis=1)
    qdec_t = jnp.repeat(q_dec.T, RET_DV, axis=1)
    cdec_t = jnp.broadcast_to(chunk_dec[:, None, None], (RET_HEADS, RET_DK, RET_DV))
    return decay_in, kdec_t, qdec_t, cdec_t


def _layer(x, mem2d, ffn1_w_in, ffn1_w_out, ln1_g, ln1_b, w_in, t5_table, ret_gn_g, ret_gn_b,
           w_mem_kv, w_br_ret, w_br_dsa, w_br_mem, w_out, ln2_g, ln2_b,
           ffn2_w_in, ffn2_w_out, ln3_g, ln3_b, tables):
    cos_t, sin_t, din, kdec, qdec, cdec = tables
    row = lambda v: v.reshape(1, -1)
    bf = lambda w: w.astype(BF16)

    w_proj = jnp.concatenate(
        [w_in[:, :W_DV0], w_in[:, W_IQ0:W_IK0], w_in[:, W_MQ0:W_G0], w_in[:, W_IK0:W_MQ0],
         jnp.zeros((D_MODEL, LANES - IDX_DIM - IDX_HEADS), F32)], axis=1)
    w_dvt = w_in[:, W_DV0:W_IQ0].T

    h = _ffn_ln(x, bf(ffn1_w_in), bf(ffn1_w_out), row(ln1_g), row(ln1_b), "ffn1_ln1")
    rq, rk, rv, rg, dq, dk, dvt, iq, mq, ikw, ikb = _proj(h, bf(w_proj), bf(w_dvt), cos_t, sin_t)
    o_ret = _retention(rq, rk, rv, rg, din, kdec, qdec, cdec, row(ret_gn_g), row(ret_gn_b))
    o_mem = _mematt(mq, _memkv(mem2d, bf(w_mem_kv)))
    o_dsa = _dsa(t5_table, _t5_bias(t5_table), iq, dq, ikw, ikb, dk, dvt)
    x2 = _merge(h, o_ret, o_dsa, o_mem, bf(w_in[:, W_G0:]), bf(w_br_ret), bf(w_br_dsa), bf(w_br_mem),
                bf(w_out), row(ln2_g), row(ln2_b))
    return _ffn_ln(x2, bf(ffn2_w_in), bf(ffn2_w_out), row(ln3_g), row(ln3_b), "ffn2_ln3")


def kernel(x, mem, ffn1_w_in, ffn1_w_out, ln1_g, ln1_b, w_in, t5_table, ret_gn_g, ret_gn_b,
           w_mem_kv, w_br_ret, w_br_dsa, w_br_mem, w_out, ln2_g, ln2_b,
           ffn2_w_in, ffn2_w_out, ln3_g, ln3_b):
    assert x.shape == (BATCH, SEQ, D_MODEL) and mem.shape == (BATCH, MEM_TOKENS, D_MODEL)
    tables = _rope_tables() + _decay_tables()
    y = x.reshape(N_TOK, D_MODEL)
    mem2d = mem.reshape(BATCH * MEM_TOKENS, D_MODEL)
    for l in range(DEPTH):
        y = _layer(y, mem2d, ffn1_w_in[l], ffn1_w_out[l], ln1_g[l], ln1_b[l], w_in[l], t5_table,
                   ret_gn_g[l], ret_gn_b[l], w_mem_kv[l], w_br_ret[l], w_br_dsa[l], w_br_mem[l],
                   w_out[l], ln2_g[l], ln2_b[l], ffn2_w_in[l], ffn2_w_out[l], ln3_g[l], ln3_b[l],
                   tables)
    return y.reshape(BATCH, SEQ, D_MODEL)
```

```python
import math

import jax
import jax.numpy as jnp
from jax import lax
from jax.experimental import pallas as pl
from jax.experimental.pallas import tpu as pltpu

F32 = jnp.float32
BF16 = jnp.bfloat16

D_MODEL = 1024
BATCH = 8
SEQ = 2048
MEM_TOKENS = 256
RET_HEADS, RET_DK, RET_DV, RET_CHUNK = 4, 64, 128, 128
DSA_HEADS, DSA_DH = 8, 64
IDX_HEADS, IDX_DIM = 8, 64
TOPK = min(256, SEQ // 4)
MEM_HEADS, MEM_DH = 4, 128
T5_BUCKETS, T5_MAX_DIST = 32, 128
D_FF = 2816
ROPE_BASE = 10000.0
LN_EPS = 1e-5
NEG_INF = -1e30
DEPTH = 1
ALPHA = (2.0 * DEPTH) ** 0.25

RET_QK_W = RET_HEADS * RET_DK
RET_V_W = RET_HEADS * RET_DV
DSA_W = DSA_HEADS * DSA_DH
IDX_Q_W = IDX_HEADS * IDX_DIM
MEM_W = MEM_HEADS * MEM_DH
N_TOK = BATCH * SEQ

V7X_VMEM_BYTES = 64 * 1024 * 1024
VMEM_LIMIT = V7X_VMEM_BYTES - 8 * 1024 * 1024
LANES = 128
SUBLANES = 8

FFN_ROWS = 1024
FFN_CHUNK = 1408
PROJ_ROWS = 512
MERGE_ROWS = 512
RET_ROWS = 512
MEMKV_ROWS = 512
MEMATT_ROWS = 512
DSA_T = 256
HEAD_PAD = LANES
BISECT_PLAIN_STEPS = 14
COUNT_ACCUMULATORS = 4
LOG2E = math.log2(math.e)

P_RQ, P_RK, P_RV, P_RG = 0, 256, 512, 1024
P_DQ, P_DK, P_IQ, P_MQ, P_IKW = 1536, 2048, 2560, 3072, 3584
P_COLS = 3712
W_DV0 = 2 * RET_QK_W + 2 * RET_V_W + 2 * DSA_W
W_IQ0 = W_DV0 + DSA_W
W_IK0 = W_IQ0 + IDX_Q_W
W_IW0 = W_IK0 + IDX_DIM
W_MQ0 = W_IW0 + IDX_HEADS
W_G0 = W_MQ0 + MEM_W
IW_SCALE = IDX_HEADS ** -0.5 * IDX_DIM ** -0.5


def _cparams(sem):
    return pltpu.CompilerParams(dimension_semantics=sem, vmem_limit_bytes=VMEM_LIMIT)


def _const_spec(shape):
    nd = len(shape)
    return pl.BlockSpec(shape, lambda *_: (0,) * nd, pipeline_mode=pl.Buffered(1))


def _layer_norm(y, g, b):
    mu = jnp.mean(y, axis=-1, keepdims=True)
    yc = y - mu
    var = jnp.mean(yc * yc, axis=-1, keepdims=True)
    return yc * lax.rsqrt(var + LN_EPS) * g + b


def _dot(a, b):
    return jnp.dot(a, b, preferred_element_type=F32)


def _dot_nt(a, b):
    return lax.dot_general(a, b, (((1,), (1,)), ((), ())), preferred_element_type=F32)


def _dot_tn(a, b):
    return lax.dot_general(a, b, (((0,), (0,)), ((), ())), preferred_element_type=F32)


def _ffn_ln_kernel(x_ref, wi_ref, wo_ref, g_ref, b_ref, o_ref):
    x = x_ref[...]
    xb = x.astype(BF16)
    acc = None
    for c in range(D_FF // FFN_CHUNK):
        lo = c * FFN_CHUNK
        a = _dot(xb, wi_ref[:, lo:lo + FFN_CHUNK])
        u = _dot(xb, wi_ref[:, D_FF + lo:D_FF + lo + FFN_CHUNK])
        act = (jax.nn.silu(a) * u).astype(BF16)
        part = _dot(act, wo_ref[lo:lo + FFN_CHUNK, :])
        acc = part if acc is None else acc + part
    o_ref[...] = _layer_norm(ALPHA * x + 0.5 * acc, g_ref[...], b_ref[...])


def _ffn_ln(x, w_in_bf, w_out_bf, g, b, name):
    return pl.pallas_call(
        _ffn_ln_kernel,
        out_shape=jax.ShapeDtypeStruct((N_TOK, D_MODEL), F32),
        grid=(N_TOK // FFN_ROWS,),
        in_specs=[
            pl.BlockSpec((FFN_ROWS, D_MODEL), lambda i: (i, 0)),
            _const_spec((D_MODEL, 2 * D_FF)),
            _const_spec((D_FF, D_MODEL)),
            _const_spec((1, D_MODEL)),
            _const_spec((1, D_MODEL)),
        ],
        out_specs=pl.BlockSpec((FFN_ROWS, D_MODEL), lambda i: (i, 0)),
        compiler_params=_cparams(("parallel",)),
        name=name,
    )(x, w_in_bf, w_out_bf, g, b)


def _rope(x, cos, sin_signed):
    width = x.shape[-1]
    lane = lax.broadcasted_iota(jnp.int32, x.shape, 1)
    first_half = (lane % RET_DK) < (RET_DK // 2)
    swapped = jnp.where(first_half,
                        pltpu.roll(x, width - RET_DK // 2, 1),
                        pltpu.roll(x, RET_DK // 2, 1))
    return x * cos + swapped * sin_signed


def _store_padded_heads(dst_ref, val):
    low = lax.broadcasted_iota(jnp.int32, (val.shape[0], LANES), 1) < DSA_DH
    for j in range(val.shape[1] // LANES):
        pair = val[:, j * LANES:(j + 1) * LANES]
        dst_ref[:, (2 * j) * LANES:(2 * j + 1) * LANES] = jnp.where(low, pair, 0.0).astype(dst_ref.dtype)
        dst_ref[:, (2 * j + 1) * LANES:(2 * j + 2) * LANES] = jnp.where(
            low, pltpu.roll(pair, DSA_DH, 1), 0.0).astype(dst_ref.dtype)


def _proj_kernel(h_ref, w_ref, wdvt_ref, cos_ref, sin_ref,
                 rq_ref, rk_ref, rv_ref, rg_ref, dq_ref, dk_ref, dvt_ref, iq_ref, mq_ref,
                 ikw_ref, ikb_ref):
    hb = h_ref[...].astype(BF16)

    def proj(lo, width):
        return _dot(hb, w_ref[:, lo:lo + width])

    cos = cos_ref[...]
    sin = sin_ref[...]
    rq_ref[...] = _rope(proj(P_RQ, RET_QK_W), cos, sin)
    rk_ref[...] = _rope(proj(P_RK, RET_QK_W), cos, sin) * (RET_DK ** -0.5)
    rv_ref[...] = proj(P_RV, RET_V_W).astype(BF16)
    rg_ref[...] = proj(P_RG, RET_V_W)
    _store_padded_heads(dq_ref, proj(P_DQ, DSA_W) * (DSA_DH ** -0.5 * LOG2E))
    _store_padded_heads(dk_ref, proj(P_DK, DSA_W))
    _store_padded_heads(iq_ref, proj(P_IQ, IDX_Q_W))
    mq_ref[...] = proj(P_MQ, MEM_W).astype(BF16)
    ikw = proj(P_IKW, LANES)
    ikw_ref[...] = ikw
    ikb_ref[...] = ikw.astype(BF16)
    vt = _dot_nt(wdvt_ref[...], hb).astype(BF16)
    for j in range(PROJ_ROWS // DSA_T):
        dvt_ref[j] = vt[:, j * DSA_T:(j + 1) * DSA_T]


def _proj(h, w_proj_bf, w_dvt_bf, cos_t, sin_t):
    rows = lambda w: pl.BlockSpec((PROJ_ROWS, w), lambda i: (i, 0))
    seq_tiles = SEQ // PROJ_ROWS
    pos = lambda w: pl.BlockSpec((PROJ_ROWS, w), lambda i: (i % seq_tiles, 0))
    sd = jax.ShapeDtypeStruct
    padded = DSA_HEADS * HEAD_PAD
    slabs = PROJ_ROWS // DSA_T
    return pl.pallas_call(
        _proj_kernel,
        out_shape=(
            sd((N_TOK, RET_QK_W), F32), sd((N_TOK, RET_QK_W), F32),
            sd((N_TOK, RET_V_W), BF16), sd((N_TOK, RET_V_W), F32),
            sd((N_TOK, padded), BF16), sd((N_TOK, padded), BF16),
            sd((N_TOK // DSA_T, DSA_W, DSA_T), BF16),
            sd((N_TOK, padded), BF16), sd((N_TOK, MEM_W), BF16),
            sd((N_TOK, LANES), F32), sd((N_TOK, LANES), BF16),
        ),
        grid=(N_TOK // PROJ_ROWS,),
        in_specs=[rows(D_MODEL), _const_spec((D_MODEL, P_COLS)), _const_spec((DSA_W, D_MODEL)),
                  pos(RET_QK_W), pos(RET_QK_W)],
        out_specs=(rows(RET_QK_W), rows(RET_QK_W), rows(RET_V_W), rows(RET_V_W),
                   rows(padded), rows(padded),
                   pl.BlockSpec((slabs, DSA_W, DSA_T), lambda i: (i, 0, 0)),
                   rows(padded), rows(MEM_W), rows(LANES), rows(LANES)),
        compiler_params=_cparams(("parallel",)),
        name="mixer_proj",
    )(h, w_proj_bf, w_dvt_bf, cos_t, sin_t)


def _retention_kernel(q_ref, k_ref, v_ref, g_ref, din_ref, kdec_ref, qdec_ref, cdec_ref,
                      gng_ref, gnb_ref, o_ref, state_ref):
    @pl.when(pl.program_id(1) == 0)
    def _():
        state_ref[...] = jnp.zeros_like(state_ref)

    kdec = kdec_ref[...]
    qdec = qdec_ref[...]
    gn_g = gng_ref[...]
    gn_b = gnb_ref[...]
    states = [state_ref[h] for h in range(RET_HEADS)]
    for j in range(RET_ROWS // RET_CHUNK):
        rows = slice(j * RET_CHUNK, (j + 1) * RET_CHUNK)
        q = q_ref[rows, :]
        k = k_ref[rows, :]
        kd = k * kdec
        v = v_ref[rows, :]
        gate = g_ref[rows, :]
        for h in range(RET_HEADS):
            ks = slice(h * RET_DK, (h + 1) * RET_DK)
            vs = slice(h * RET_DV, (h + 1) * RET_DV)
            qh = q[:, ks].astype(BF16)
            kh = k[:, ks].astype(BF16)
            vh = v[:, vs]
            scores = _dot_nt(qh, kh) * din_ref[h]
            intra = _dot(scores.astype(BF16), vh)
            cross = _dot(qh, states[h].astype(BF16)) * qdec[:, vs]
            kv = _dot_tn(kd[:, ks].astype(BF16), vh)
            states[h] = cdec_ref[h] * states[h] + kv
            o = intra + cross
            mu = jnp.mean(o, axis=-1, keepdims=True)
            oc = o - mu
            var = jnp.mean(oc * oc, axis=-1, keepdims=True)
            o = oc * lax.rsqrt(var + LN_EPS) * gn_g[:, vs] + gn_b[:, vs]
            o_ref[rows, vs] = (jax.nn.silu(gate[:, vs]) * o).astype(BF16)
    for h in range(RET_HEADS):
        state_ref[h] = states[h]


def _retention(rq, rk, rv, rg, din, kdec, qdec, cdec, gn_g, gn_b):
    nc = SEQ // RET_ROWS
    rows = lambda w: pl.BlockSpec((RET_ROWS, w), lambda b, n: (b * nc + n, 0))
    return pl.pallas_call(
        _retention_kernel,
        out_shape=jax.ShapeDtypeStruct((N_TOK, RET_V_W), BF16),
        grid=(BATCH, nc),
        in_specs=[rows(RET_QK_W), rows(RET_QK_W), rows(RET_V_W), rows(RET_V_W),
                  _const_spec((RET_HEADS, RET_CHUNK, RET_CHUNK)),
                  _const_spec((RET_CHUNK, RET_QK_W)),
                  _const_spec((RET_CHUNK, RET_V_W)),
                  _const_spec((RET_HEADS, RET_DK, RET_DV)),
                  _const_spec((1, RET_V_W)), _const_spec((1, RET_V_W))],
        out_specs=rows(RET_V_W),
        scratch_shapes=[pltpu.VMEM((RET_HEADS, RET_DK, RET_DV), F32)],
        compiler_params=_cparams(("parallel", "arbitrary")),
        name="retention",
    )(rq, rk, rv, rg, din, kdec, qdec, cdec, gn_g, gn_b)


def _memkv_kernel(m_ref, w_ref, o_ref):
    o_ref[...] = _dot(m_ref[...].astype(BF16), w_ref[...]).astype(BF16)


def _memkv(mem2d, w_bf):
    n = mem2d.shape[0]
    return pl.pallas_call(
        _memkv_kernel,
        out_shape=jax.ShapeDtypeStruct((n, 2 * MEM_W), BF16),
        grid=(n // MEMKV_ROWS,),
        in_specs=[pl.BlockSpec((MEMKV_ROWS, D_MODEL), lambda i: (i, 0)),
                  _const_spec((D_MODEL, 2 * MEM_W))],
        out_specs=pl.BlockSpec((MEMKV_ROWS, 2 * MEM_W), lambda i: (i, 0)),
        compiler_params=_cparams(("parallel",)),
        name="mem_kv",
    )(mem2d, w_bf)


def _mematt_kernel(q_ref, k_ref, v_ref, o_ref):
    q = q_ref[...]
    k = k_ref[...]
    v = v_ref[...]
    for h in range(MEM_HEADS):
        hs = slice(h * MEM_DH, (h + 1) * MEM_DH)
        logits = _dot_nt(q[:, hs], k[:, hs]) * (MEM_DH ** -0.5)
        m = jnp.max(logits, axis=-1, keepdims=True)
        p = jnp.exp(logits - m)
        denom = jnp.sum(p, axis=-1, keepdims=True)
        o_ref[:, hs] = (_dot(p.astype(BF16), v[:, hs]) / denom).astype(BF16)


def _mematt(mq, mkv):
    tiles = SEQ // MEMATT_ROWS
    return pl.pallas_call(
        _mematt_kernel,
        out_shape=jax.ShapeDtypeStruct((N_TOK, MEM_W), BF16),
        grid=(BATCH, tiles),
        in_specs=[pl.BlockSpec((MEMATT_ROWS, MEM_W), lambda b, i: (b * tiles + i, 0)),
                  pl.BlockSpec((MEM_TOKENS, MEM_W), lambda b, i: (b, 0)),
                  pl.BlockSpec((MEM_TOKENS, MEM_W), lambda b, i: (b, 1))],
        out_specs=pl.BlockSpec((MEMATT_ROWS, MEM_W), lambda b, i: (b * tiles + i, 0)),
        compiler_params=_cparams(("parallel", "parallel")),
        name="mem_attention",
    )(mq, mkv, mkv)


def _t5_bias_kernel(table_ref, o_ref):
    j = lax.broadcasted_iota(jnp.int32, (DSA_T, DSA_T), 0)
    r = lax.broadcasted_iota(jnp.int32, (DSA_T, DSA_T), 1)
    max_exact = T5_BUCKETS // 2
    for w in range(2):
        n = jnp.maximum(r + (1 - w) * DSA_T - j, 0)
        nf = jnp.maximum(n, 1).astype(F32)
        large = max_exact + (jnp.log(nf / max_exact) / math.log(T5_MAX_DIST / max_exact)
                             * (T5_BUCKETS - max_exact)).astype(jnp.int32)
        large = jnp.minimum(large, T5_BUCKETS - 1)
        bucket = jnp.where(n < max_exact, n, large)
        for h in range(DSA_HEADS):
            acc = jnp.zeros((DSA_T, DSA_T), F32)
            for b in range(T5_BUCKETS):
                acc = jnp.where(bucket == b, table_ref[b, h] * LOG2E, acc)
            o_ref[w, h] = acc


def _t5_bias(t5_table):
    return pl.pallas_call(
        _t5_bias_kernel,
        out_shape=jax.ShapeDtypeStruct((2, DSA_HEADS, DSA_T, DSA_T), F32),
        in_specs=[pl.BlockSpec(memory_space=pltpu.SMEM)],
        out_specs=pl.BlockSpec(memory_space=pltpu.VMEM),
        compiler_params=pltpu.CompilerParams(vmem_limit_bytes=VMEM_LIMIT),
        name="t5_bias",
    )(t5_table)


def _fold_groups(x, op):
    return op(x.reshape(DSA_T // SUBLANES, SUBLANES, DSA_T), axis=0)


def _dsa_kernel(table_ref, bias_ref, iq_ref, dq_ref, ikw_q_ref, ikb_ref, dk_ref, dvt_ref,
                o_ref, score_ref, logit_ref, acc_ref):
    qb = pl.program_id(1)
    q0 = qb * DSA_T
    n_chunks = qb + 1
    groups = DSA_T // SUBLANES
    key_in_chunk = lax.broadcasted_iota(jnp.int32, (DSA_T, DSA_T), 0)
    query_in_step = lax.broadcasted_iota(jnp.int32, (DSA_T, DSA_T), 1)

    def key_rows(c):
        return pl.ds(pl.multiple_of(c * DSA_T, DSA_T), DSA_T)

    def head_cols(h):
        return slice(h * HEAD_PAD, (h + 1) * HEAD_PAD)

    iw_t = ikw_q_ref[...].T[IDX_DIM:IDX_DIM + IDX_HEADS, :] * IW_SCALE

    def index_scores(c):
        ik = ikb_ref[key_rows(c), :]
        s = None
        for h in range(IDX_HEADS):
            rel = _dot_nt(ik, iq_ref[:, head_cols(h)])
            term = jnp.maximum(rel, 0.0) * iw_t[h:h + 1, :]
            s = term if s is None else s + term
        return s

    def score_body(c, carry):
        score_ref[c] = index_scores(c)
        return carry

    lax.fori_loop(0, qb, score_body, 0)
    score_ref[qb] = jnp.where(key_in_chunk <= query_in_step, index_scores(qb), NEG_INF)

    def count_where(pred):
        def body(c, acc):
            parts = [acc] + [jnp.zeros((SUBLANES, DSA_T), F32)] * (COUNT_ACCUMULATORS - 1)
            for g in range(groups):
                blk = score_ref[c, g * SUBLANES:(g + 1) * SUBLANES, :]
                a = g % COUNT_ACCUMULATORS
                parts[a] = parts[a] + jnp.where(pred(c, g * SUBLANES, blk), 1.0, 0.0)
            while len(parts) > 1:
                parts = [parts[i] + parts[i + 1] for i in range(0, len(parts), 2)]
            return parts[0]
        acc = lax.fori_loop(0, n_chunks, body, jnp.zeros((SUBLANES, DSA_T), F32))
        return jnp.sum(acc, axis=0, keepdims=True)

    def count_ge(thr):
        thr_b = jnp.broadcast_to(thr, (SUBLANES, DSA_T))
        return count_where(lambda c, r0, blk: blk >= thr_b)

    def minmax_body(c, carry):
        mn, mx = carry
        blk = score_ref[c]
        mx = jnp.maximum(mx, _fold_groups(blk, jnp.max))
        mn = jnp.minimum(mn, _fold_groups(jnp.where(blk > 0.5 * NEG_INF, blk, -NEG_INF), jnp.min))
        return mn, mx

    mn, mx = lax.fori_loop(
        0, n_chunks, minmax_body,
        (jnp.full((SUBLANES, DSA_T), -NEG_INF, F32), jnp.full((SUBLANES, DSA_T), NEG_INF, F32)))
    q_min = jnp.min(mn, axis=0, keepdims=True)
    q_max = jnp.max(mx, axis=0, keepdims=True)
    t_q = q0 + lax.broadcasted_iota(jnp.int32, (1, DSA_T), 1)

    kf = float(TOPK)
    init = (q_min, q_max + jnp.maximum(jnp.abs(q_max), 1.0) * 1e-6,
            (t_q + 1).astype(F32), jnp.zeros((1, DSA_T), F32))

    def bisect_step(_i, carry):
        lo, hi, cnt_lo, cnt_hi = carry
        mid = 0.5 * (lo + hi)
        cnt = count_ge(mid)
        ge = cnt >= kf
        return (jnp.where(ge, mid, lo), jnp.where(ge, hi, mid),
                jnp.where(ge, cnt, cnt_lo), jnp.where(ge, cnt_hi, cnt))

    lo, hi, cnt_lo, cnt_hi = lax.fori_loop(0, BISECT_PLAIN_STEPS, bisect_step, init)

    big = -NEG_INF

    def snap_body(c, carry):
        mn, mx = carry
        blk = score_ref[c]
        mn = jnp.minimum(mn, _fold_groups(jnp.where(blk >= lo, blk, big), jnp.min))
        mx = jnp.maximum(mx, _fold_groups(jnp.where(blk < hi, blk, -big), jnp.max))
        return mn, mx

    mn, mx = lax.fori_loop(
        0, n_chunks, snap_body,
        (jnp.full((SUBLANES, DSA_T), big, F32), jnp.full((SUBLANES, DSA_T), -big, F32)))
    lo = jnp.min(mn, axis=0, keepdims=True)
    top = jnp.max(mx, axis=0, keepdims=True)

    def pending_of(carry):
        lo, top, cnt_lo, _ = carry
        return jnp.max(jnp.where((cnt_lo > kf) & (lo < top), 1.0, 0.0))

    def value_step(state):
        (lo, top, cnt_lo, cnt_hi), _ = state
        unresolved = (cnt_lo > kf) & (lo < top)
        mid = lo + (top - lo) * 0.5
        mid = jnp.minimum(jnp.where(mid > lo, mid, top), top)

        def body(c, carry):
            cnt, mn, mx = carry
            blk = score_ref[c]
            is_ge = blk >= mid
            cnt = cnt + _fold_groups(jnp.where(is_ge, 1.0, 0.0), jnp.sum)
            mn = jnp.minimum(mn, _fold_groups(jnp.where(is_ge, blk, big), jnp.min))
            mx = jnp.maximum(mx, _fold_groups(jnp.where(is_ge, -big, blk), jnp.max))
            return cnt, mn, mx

        cnt, mn, mx = lax.fori_loop(
            0, n_chunks, body,
            (jnp.zeros((SUBLANES, DSA_T), F32), jnp.full((SUBLANES, DSA_T), big, F32),
             jnp.full((SUBLANES, DSA_T), -big, F32)))
        cnt = jnp.sum(cnt, axis=0, keepdims=True)
        ge = cnt >= kf
        raise_lo = unresolved & ge
        lower_top = unresolved & jnp.logical_not(ge)
        carry = (jnp.where(raise_lo, jnp.min(mn, axis=0, keepdims=True), lo),
                 jnp.where(lower_top, jnp.max(mx, axis=0, keepdims=True), top),
                 jnp.where(raise_lo, cnt, cnt_lo), jnp.where(lower_top, cnt, cnt_hi))
        return carry, pending_of(carry)

    start = (lo, top, cnt_lo, cnt_hi)
    (lo, top, cnt_lo, cnt_hi), _ = lax.while_loop(
        lambda state: state[1] > 0.5, value_step, (start, pending_of(start)))

    tied = cnt_lo > kf
    need = kf - cnt_hi
    lo_b = jnp.broadcast_to(lo, (SUBLANES, DSA_T))
    key_in_group = lax.broadcasted_iota(jnp.int32, (SUBLANES, DSA_T), 0)

    def tie_search(_):
        def body(_i, carry):
            lo_i, hi_i = carry
            mid_i = lax.shift_right_arithmetic(lo_i + hi_i, 1)
            mid_b = jnp.broadcast_to(mid_i, (SUBLANES, DSA_T))
            cnt = count_where(lambda c, r0, blk: (blk == lo_b) & (key_in_group + (c * DSA_T + r0) <= mid_b))
            enough = cnt >= need
            return jnp.where(enough, lo_i, mid_i), jnp.where(enough, mid_i, hi_i)
        lo_i = jnp.full((1, DSA_T), -1, jnp.int32)
        hi_i = jnp.full((1, DSA_T), SEQ - 1, jnp.int32)
        _, cut = lax.fori_loop(0, int(math.log2(SEQ)) + 1, body, (lo_i, hi_i))
        return jnp.where(tied, cut, SEQ)

    cut = lax.cond(jnp.max(jnp.where(tied, 1.0, 0.0)) > 0.5, tie_search,
                   lambda _: jnp.full((1, DSA_T), SEQ, jnp.int32), 0)

    def mask_body(c, carry):
        blk = score_ref[c]
        keep = (blk > lo) | ((blk == lo) & (key_in_chunk + c * DSA_T <= cut))
        score_ref[c] = jnp.where(keep, 0.0, NEG_INF)
        return carry

    lax.fori_loop(0, n_chunks, mask_body, 0)

    def qk_logits(c, h):
        kh = dk_ref[key_rows(c), head_cols(h)]
        return _dot_nt(kh, dq_ref[:, head_cols(h)]) + score_ref[c]

    def far_body(c, ms):
        out = []
        for h in range(DSA_HEADS):
            lg = qk_logits(c, h) + table_ref[T5_BUCKETS - 1, h] * LOG2E
            logit_ref[h, c] = lg
            out.append(jnp.maximum(ms[h], _fold_groups(lg, jnp.max)))
        return tuple(out)

    def near_body(c, ms):
        w = c - qb + 1
        out = []
        for h in range(DSA_HEADS):
            lg = qk_logits(c, h) + bias_ref[w, h]
            logit_ref[h, c] = lg
            out.append(jnp.maximum(ms[h], _fold_groups(lg, jnp.max)))
        return tuple(out)

    n_far = jnp.maximum(qb - 1, 0)
    ms = tuple(jnp.full((SUBLANES, DSA_T), NEG_INF, F32) for _ in range(DSA_HEADS))
    ms = lax.fori_loop(0, n_far, far_body, ms)
    ms = lax.fori_loop(n_far, n_chunks, near_body, ms)
    q_m = [jnp.max(m, axis=0, keepdims=True) for m in ms]

    acc_ref[...] = jnp.zeros_like(acc_ref)

    def pv_body(c, dens):
        out = []
        for h in range(DSA_HEADS):
            p = jnp.exp2(logit_ref[h, c] - q_m[h])
            out.append(dens[h] + _fold_groups(p, jnp.sum))
            rows = slice(h * DSA_DH, (h + 1) * DSA_DH)
            acc_ref[rows, :] += _dot(dvt_ref[c, rows, :], p.astype(BF16))
        return tuple(out)

    dens = lax.fori_loop(0, n_chunks, pv_body,
                         tuple(jnp.zeros((SUBLANES, DSA_T), F32) for _ in range(DSA_HEADS)))
    for h in range(DSA_HEADS):
        rows = slice(h * DSA_DH, (h + 1) * DSA_DH)
        acc_ref[rows, :] = acc_ref[rows, :] / jnp.sum(dens[h], axis=0, keepdims=True)
    o_ref[...] = acc_ref[...].T.astype(BF16)


def _dsa(t5_table, bias, iq, dq, ikw, ikb, dk, dvt):
    nqb = SEQ // DSA_T
    padded = DSA_HEADS * HEAD_PAD
    qrows = lambda w: pl.BlockSpec((DSA_T, w), lambda b, i: (b * nqb + i, 0))
    krows = lambda w: pl.BlockSpec((SEQ, w), lambda b, i: (b, 0))
    return pl.pallas_call(
        _dsa_kernel,
        out_shape=jax.ShapeDtypeStruct((N_TOK, DSA_W), BF16),
        grid=(BATCH, nqb),
        in_specs=[pl.BlockSpec(memory_space=pltpu.SMEM),
                  _const_spec((2, DSA_HEADS, DSA_T, DSA_T)),
                  qrows(padded), qrows(padded), qrows(LANES),
                  krows(LANES), krows(padded),
                  pl.BlockSpec((nqb, DSA_W, DSA_T), lambda b, i: (b, 0, 0))],
        out_specs=qrows(DSA_W),
        scratch_shapes=[pltpu.VMEM((nqb, DSA_T, DSA_T), F32),
                        pltpu.VMEM((DSA_HEADS, nqb, DSA_T, DSA_T), F32),
                        pltpu.VMEM((DSA_W, DSA_T), F32)],
        compiler_params=_cparams(("parallel", "arbitrary")),
        name="dsa_attention",
    )(t5_table, bias, iq, dq, ikw, ikb, dk, dvt)


def _merge_kernel(h_ref, oret_ref, odsa_ref, omem_ref, wg_ref, wr_ref, wd_ref, wm_ref, wo_ref,
                  g_ref, b_ref, o_ref):
    h = h_ref[...]
    hb = h.astype(BF16)
    merged = None
    for i, (src, w) in enumerate(((oret_ref, wr_ref), (odsa_ref, wd_ref), (omem_ref, wm_ref))):
        gate = jax.nn.sigmoid(_dot(hb, wg_ref[:, i * D_MODEL:(i + 1) * D_MODEL]))
        term = gate * _dot(src[...], w[...])
        merged = term if merged is None else merged + term
    mix = _dot(merged.astype(BF16), wo_ref[...])
    o_ref[...] = _layer_norm(ALPHA * h + mix, g_ref[...], b_ref[...])


def _merge(h, o_ret, o_dsa, o_mem, wg, wr, wd, wm, wo, g, b):
    rows = lambda w: pl.BlockSpec((MERGE_ROWS, w), lambda i: (i, 0))
    return pl.pallas_call(
        _merge_kernel,
        out_shape=jax.ShapeDtypeStruct((N_TOK, D_MODEL), F32),
        grid=(N_TOK // MERGE_ROWS,),
        in_specs=[rows(D_MODEL), rows(RET_V_W), rows(DSA_W), rows(MEM_W),
                  _const_spec((D_MODEL, 3 * D_MODEL)),
                  _const_spec((RET_V_W, D_MODEL)), _const_spec((DSA_W, D_MODEL)),
                  _const_spec((MEM_W, D_MODEL)), _const_spec((D_MODEL, D_MODEL)),
                  _const_spec((1, D_MODEL)), _const_spec((1, D_MODEL))],
        out_specs=rows(D_MODEL),
        compiler_params=_cparams(("parallel",)),
        name="merge_ln2",
    )(h, o_ret, o_dsa, o_mem, wg, wr, wd, wm, wo, g, b)


def _rope_tables():
    half = RET_DK // 2
    freqs = ROPE_BASE ** (-jnp.arange(half, dtype=F32) / half)
    ang = jnp.arange(SEQ).astype(F32)[:, None] * freqs[None, :]
    cos, sin = jnp.cos(ang), jnp.sin(ang)
    cos_t = jnp.tile(jnp.concatenate([cos, cos], axis=-1), (1, RET_HEADS))
    sin_t = jnp.tile(jnp.concatenate([-sin, sin], axis=-1), (1, RET_HEADS))
    return cos_t, sin_t


def _decay_tables():
    c = RET_CHUNK
    gamma = 1.0 - 2.0 ** (-5.0 - jnp.arange(RET_HEADS, dtype=F32))
    lg = jnp.log(gamma)
    i = jnp.arange(c)
    diff = i[:, None] - i[None, :]
    decay_in = jnp.where(diff[None] >= 0, jnp.exp(jnp.maximum(diff, 0)[None] * lg[:, None, None]), 0.0).astype(F32)
    k_dec = jnp.exp((c - 1 - i)[None, :] * lg[:, None]).astype(F32)
    q_dec = jnp.exp((i + 1)[None, :] * lg[:, None]).astype(F32)
    chunk_dec = jnp.exp(c * lg).astype(F32)
    kdec_t = jnp.repeat(k_dec.T, RET_DK, axis=1)
    qdec_t = jnp.repeat(q_dec.T, RET_DV, axis=1)
    cdec_t = jnp.broadcast_to(chunk_dec[:, None, None], (RET_HEADS, RET_DK, RET_DV))
    return decay_in, kdec_t, qdec_t, cdec_t


def _layer(x, mem2d, ffn1_w_in, ffn1_w_out, ln1_g, ln1_b, w_in, t5_table, ret_gn_g, ret_gn_b,
           w_mem_kv, w_br_ret, w_br_dsa, w_br_mem, w_out, ln2_g, ln2_b,
           ffn2_w_in, ffn2_w_out, ln3_g, ln3_b, tables):
    cos_t, sin_t, din, kdec, qdec, cdec = tables
    row = lambda v: v.reshape(1, -1)
    bf = lambda w: w.astype(BF16)

    w_proj = jnp.concatenate(
        [w_in[:, :W_DV0], w_in[:, W_IQ0:W_IK0], w_in[:, W_MQ0:W_G0], w_in[:, W_IK0:W_MQ0],
         jnp.zeros((D_MODEL, LANES - IDX_DIM - IDX_HEADS), F32)], axis=1)
    w_dvt = w_in[:, W_DV0:W_IQ0].T

    h = _ffn_ln(x, bf(ffn1_w_in), bf(ffn1_w_out), row(ln1_g), row(ln1_b), "ffn1_ln1")
    rq, rk, rv, rg, dq, dk, dvt, iq, mq, ikw, ikb = _proj(h, bf(w_proj), bf(w_dvt), cos_t, sin_t)
    o_ret = _retention(rq, rk, rv, rg, din, kdec, qdec, cdec, row(ret_gn_g), row(ret_gn_b))
    o_mem = _mematt(mq, _memkv(mem2d, bf(w_mem_kv)))
    o_dsa = _dsa(t5_table, _t5_bias(t5_table), iq, dq, ikw, ikb, dk, dvt)
    x2 = _merge(h, o_ret, o_dsa, o_mem, bf(w_in[:, W_G0:]), bf(w_br_ret), bf(w_br_dsa), bf(w_br_mem),
                bf(w_out), row(ln2_g), row(ln2_b))
    return _ffn_ln(x2, bf(ffn2_w_in), bf(ffn2_w_out), row(ln3_g), row(ln3_b), "ffn2_ln3")


def kernel(x, mem, ffn1_w_in, ffn1_w_out, ln1_g, ln1_b, w_in, t5_table, ret_gn_g, ret_gn_b,
           w_mem_kv, w_br_ret, w_br_dsa, w_br_mem, w_out, ln2_g, ln2_b,
           ffn2_w_in, ffn2_w_out, ln3_g, ln3_b):
    assert x.shape == (BATCH, SEQ, D_MODEL) and mem.shape == (BATCH, MEM_TOKENS, D_MODEL)
    tables = _rope_tables() + _decay_tables()
    y = x.reshape(N_TOK, D_MODEL)
    mem2d = mem.reshape(BATCH * MEM_TOKENS, D_MODEL)
    for l in range(DEPTH):
        y = _layer(y, mem2d, ffn1_w_in[l], ffn1_w_out[l], ln1_g[l], ln1_b[l], w_in[l], t5_table,
                   ret_gn_g[l], ret_gn_b[l], w_mem_kv[l], w_br_ret[l], w_br_dsa[l], w_br_mem[l],
                   w_out[l], ln2_g[l], ln2_b[l], ffn2_w_in[l], ffn2_w_out[l], ln3_g[l], ln3_b[l],
                   tables)
    return y.reshape(BATCH, SEQ, D_MODEL)
```

```python
import math

import jax
import jax.numpy as jnp
from jax import lax
from jax.experimental import pallas as pl
from jax.experimental.pallas import tpu as pltpu

F32 = jnp.float32
BF16 = jnp.bfloat16

D_MODEL = 1024
BATCH = 8
SEQ = 2048
MEM_TOKENS = 256
RET_HEADS, RET_DK, RET_DV, RET_CHUNK = 4, 64, 128, 128
DSA_HEADS, DSA_DH = 8, 64
IDX_HEADS, IDX_DIM = 8, 64
TOPK = min(256, SEQ // 4)
MEM_HEADS, MEM_DH = 4, 128
T5_BUCKETS, T5_MAX_DIST = 32, 128
D_FF = 2816
ROPE_BASE = 10000.0
LN_EPS = 1e-5
NEG_INF = -1e30
DEPTH = 1
ALPHA = (2.0 * DEPTH) ** 0.25

RET_QK_W = RET_HEADS * RET_DK
RET_V_W = RET_HEADS * RET_DV
DSA_W = DSA_HEADS * DSA_DH
IDX_Q_W = IDX_HEADS * IDX_DIM
MEM_W = MEM_HEADS * MEM_DH
N_TOK = BATCH * SEQ

V7X_VMEM_BYTES = 64 * 1024 * 1024
VMEM_LIMIT = V7X_VMEM_BYTES - 8 * 1024 * 1024
LANES = 128
SUBLANES = 8

FFN_ROWS = 1024
MXU_TILE = 256
FFN_CHUNK_EDGES = (0, 6 * MXU_TILE, D_FF)
PROJ_ROWS = 512
MERGE_ROWS = 512
RET_ROWS = 512
MEMKV_ROWS = 512
MEMATT_ROWS = 512
DSA_T = 256
HEAD_PAD = LANES
DVT_HEAD_ROWS = DSA_DH + 16
DVT_ROWS = DSA_HEADS * DVT_HEAD_ROWS
BISECT_PLAIN_STEPS = 14
COUNT_ACCUMULATORS = 4
LOG2E = math.log2(math.e)
DEN_FLOOR = 2.0 ** -100

P_RQ, P_RK, P_RV, P_RG = 0, 256, 512, 1024
P_DQ, P_DK, P_IQ, P_MQ, P_IKW = 1536, 2048, 2560, 3072, 3584
P_COLS = 3712
W_DV0 = 2 * RET_QK_W + 2 * RET_V_W + 2 * DSA_W
W_IQ0 = W_DV0 + DSA_W
W_IK0 = W_IQ0 + IDX_Q_W
W_IW0 = W_IK0 + IDX_DIM
W_MQ0 = W_IW0 + IDX_HEADS
W_G0 = W_MQ0 + MEM_W
IW_SCALE = IDX_HEADS ** -0.5 * IDX_DIM ** -0.5


def _cparams(sem):
    return pltpu.CompilerParams(dimension_semantics=sem, vmem_limit_bytes=VMEM_LIMIT)


def _const_spec(shape):
    nd = len(shape)
    return pl.BlockSpec(shape, lambda *_: (0,) * nd, pipeline_mode=pl.Buffered(1))


def _layer_norm(y, g, b):
    mu = jnp.mean(y, axis=-1, keepdims=True)
    yc = y - mu
    var = jnp.mean(yc * yc, axis=-1, keepdims=True)
    return yc * lax.rsqrt(var + LN_EPS) * g + b


def _dot(a, b):
    return jnp.dot(a, b, preferred_element_type=F32)


def _dot_nt(a, b):
    return lax.dot_general(a, b, (((1,), (1,)), ((), ())), preferred_element_type=F32)


def _dot_tn(a, b):
    return lax.dot_general(a, b, (((0,), (0,)), ((), ())), preferred_element_type=F32)


def _ffn_ln_kernel(x_ref, wi_ref, wo_ref, g_ref, b_ref, o_ref):
    x = x_ref[...]
    xb = x.astype(BF16)
    acc = None
    for lo, hi in zip(FFN_CHUNK_EDGES[:-1], FFN_CHUNK_EDGES[1:]):
        a = _dot(xb, wi_ref[:, lo:hi])
        u = _dot(xb, wi_ref[:, D_FF + lo:D_FF + hi])
        act = (jax.nn.silu(a) * u).astype(BF16)
        part = _dot(act, wo_ref[lo:hi, :])
        acc = part if acc is None else acc + part
    o_ref[...] = _layer_norm(ALPHA * x + 0.5 * acc, g_ref[...], b_ref[...])


def _ffn_ln(x, w_in_bf, w_out_bf, g, b, name):
    return pl.pallas_call(
        _ffn_ln_kernel,
        out_shape=jax.ShapeDtypeStruct((N_TOK, D_MODEL), F32),
        grid=(N_TOK // FFN_ROWS,),
        in_specs=[
            pl.BlockSpec((FFN_ROWS, D_MODEL), lambda i: (i, 0)),
            _const_spec((D_MODEL, 2 * D_FF)),
            _const_spec((D_FF, D_MODEL)),
            _const_spec((1, D_MODEL)),
            _const_spec((1, D_MODEL)),
        ],
        out_specs=pl.BlockSpec((FFN_ROWS, D_MODEL), lambda i: (i, 0)),
        compiler_params=_cparams(("parallel",)),
        name=name,
    )(x, w_in_bf, w_out_bf, g, b)


def _rope(x, cos, sin_signed):
    width = x.shape[-1]
    lane = lax.broadcasted_iota(jnp.int32, x.shape, 1)
    first_half = (lane % RET_DK) < (RET_DK // 2)
    swapped = jnp.where(first_half,
                        pltpu.roll(x, width - RET_DK // 2, 1),
                        pltpu.roll(x, RET_DK // 2, 1))
    return x * cos + swapped * sin_signed


def _store_split_heads(dst_ref, val):
    low = lax.broadcasted_iota(jnp.int32, (val.shape[0], LANES), 1) < DSA_DH
    for j in range(val.shape[1] // LANES):
        pair = val[:, j * LANES:(j + 1) * LANES]
        dst_ref[:, (2 * j) * LANES:(2 * j + 1) * LANES] = jnp.where(low, pair, 0.0).astype(dst_ref.dtype)
        dst_ref[:, (2 * j + 1) * LANES:(2 * j + 2) * LANES] = jnp.where(low, 0.0, pair).astype(dst_ref.dtype)


def _proj_kernel(h_ref, w_ref, wdvt_ref, cos_ref, sin_ref,
                 rq_ref, rk_ref, rv_ref, rg_ref, dq_ref, dk_ref, dvt_ref, iq_ref, mq_ref,
                 ikw_ref, ikb_ref):
    hb = h_ref[...].astype(BF16)

    def proj(lo, width):
        return _dot(hb, w_ref[:, lo:lo + width])

    cos = cos_ref[...]
    sin = sin_ref[...]
    rq_ref[...] = _rope(proj(P_RQ, RET_QK_W), cos, sin)
    rk_ref[...] = _rope(proj(P_RK, RET_QK_W), cos, sin) * (RET_DK ** -0.5)
    rv_ref[...] = proj(P_RV, RET_V_W).astype(BF16)
    rg_ref[...] = proj(P_RG, RET_V_W)
    dq_ref[...] = (proj(P_DQ, DSA_W) * (DSA_DH ** -0.5 * LOG2E)).astype(BF16)
    _store_split_heads(dk_ref, proj(P_DK, DSA_W))
    iq_ref[...] = proj(P_IQ, IDX_Q_W).astype(BF16)
    mq_ref[...] = proj(P_MQ, MEM_W).astype(BF16)
    ikw = proj(P_IKW, LANES)
    ikw_ref[...] = ikw
    low = lax.broadcasted_iota(jnp.int32, ikw.shape, 1) < IDX_DIM
    ik_low = jnp.where(low, ikw, 0.0)
    ikb_ref[:, 0:LANES] = ik_low.astype(BF16)
    ikb_ref[:, LANES:2 * LANES] = pltpu.roll(ik_low, IDX_DIM, 1).astype(BF16)
    vt = _dot_nt(wdvt_ref[...], hb).astype(BF16)
    ones = jnp.ones((DVT_HEAD_ROWS - DSA_DH, DSA_T), BF16)
    for j in range(PROJ_ROWS // DSA_T):
        for h in range(DSA_HEADS):
            r0 = h * DVT_HEAD_ROWS
            dvt_ref[j, r0:r0 + DSA_DH, :] = vt[h * DSA_DH:(h + 1) * DSA_DH, j * DSA_T:(j + 1) * DSA_T]
            dvt_ref[j, r0 + DSA_DH:r0 + DVT_HEAD_ROWS, :] = ones


def _proj(h, w_proj_bf, w_dvt_bf, cos_t, sin_t):
    rows = lambda w: pl.BlockSpec((PROJ_ROWS, w), lambda i: (i, 0))
    seq_tiles = SEQ // PROJ_ROWS
    pos = lambda w: pl.BlockSpec((PROJ_ROWS, w), lambda i: (i % seq_tiles, 0))
    sd = jax.ShapeDtypeStruct
    padded = DSA_HEADS * HEAD_PAD
    slabs = PROJ_ROWS // DSA_T
    return pl.pallas_call(
        _proj_kernel,
        out_shape=(
            sd((N_TOK, RET_QK_W), F32), sd((N_TOK, RET_QK_W), F32),
            sd((N_TOK, RET_V_W), BF16), sd((N_TOK, RET_V_W), F32),
            sd((N_TOK, DSA_W), BF16), sd((N_TOK, padded), BF16),
            sd((N_TOK // DSA_T, DVT_ROWS, DSA_T), BF16),
            sd((N_TOK, IDX_Q_W), BF16), sd((N_TOK, MEM_W), BF16),
            sd((N_TOK, LANES), F32), sd((N_TOK, 2 * LANES), BF16),
        ),
        grid=(N_TOK // PROJ_ROWS,),
        in_specs=[rows(D_MODEL), _const_spec((D_MODEL, P_COLS)), _const_spec((DSA_W, D_MODEL)),
                  pos(RET_QK_W), pos(RET_QK_W)],
        out_specs=(rows(RET_QK_W), rows(RET_QK_W), rows(RET_V_W), rows(RET_V_W),
                   rows(DSA_W), rows(padded),
                   pl.BlockSpec((slabs, DVT_ROWS, DSA_T), lambda i: (i, 0, 0)),
                   rows(IDX_Q_W), rows(MEM_W), rows(LANES), rows(2 * LANES)),
        compiler_params=_cparams(("parallel",)),
        name="mixer_proj",
    )(h, w_proj_bf, w_dvt_bf, cos_t, sin_t)


def _retention_kernel(q_ref, k_ref, v_ref, g_ref, din_ref, kdec_ref, qdec_ref, cdec_ref,
                      gng_ref, gnb_ref, o_ref, state_ref):
    @pl.when(pl.program_id(1) == 0)
    def _():
        state_ref[...] = jnp.zeros_like(state_ref)

    kdec = kdec_ref[...]
    qdec = qdec_ref[...]
    gn_g = gng_ref[...]
    gn_b = gnb_ref[...]
    states = [state_ref[h] for h in range(RET_HEADS)]
    for j in range(RET_ROWS // RET_CHUNK):
        rows = slice(j * RET_CHUNK, (j + 1) * RET_CHUNK)
        q = q_ref[rows, :]
        k = k_ref[rows, :]
        kd = k * kdec
        v = v_ref[rows, :]
        gate = g_ref[rows, :]
        for h in range(RET_HEADS):
            ks = slice(h * RET_DK, (h + 1) * RET_DK)
            vs = slice(h * RET_DV, (h + 1) * RET_DV)
            qh = q[:, ks].astype(BF16)
            kh = k[:, ks].astype(BF16)
            vh = v[:, vs]
            scores = _dot_nt(qh, kh) * din_ref[h]
            intra = _dot(scores.astype(BF16), vh)
            cross = _dot(qh, states[h].astype(BF16)) * qdec[:, vs]
            kv = _dot_tn(kd[:, ks].astype(BF16), vh)
            states[h] = cdec_ref[h] * states[h] + kv
            o = intra + cross
            mu = jnp.mean(o, axis=-1, keepdims=True)
            oc = o - mu
            var = jnp.mean(oc * oc, axis=-1, keepdims=True)
            o = oc * lax.rsqrt(var + LN_EPS) * gn_g[:, vs] + gn_b[:, vs]
            o_ref[rows, vs] = (jax.nn.silu(gate[:, vs]) * o).astype(BF16)
    for h in range(RET_HEADS):
        state_ref[h] = states[h]


def _retention(rq, rk, rv, rg, din, kdec, qdec, cdec, gn_g, gn_b):
    nc = SEQ // RET_ROWS
    rows = lambda w: pl.BlockSpec((RET_ROWS, w), lambda b, n: (b * nc + n, 0))
    return pl.pallas_call(
        _retention_kernel,
        out_shape=jax.ShapeDtypeStruct((N_TOK, RET_V_W), BF16),
        grid=(BATCH, nc),
        in_specs=[rows(RET_QK_W), rows(RET_QK_W), rows(RET_V_W), rows(RET_V_W),
                  _const_spec((RET_HEADS, RET_CHUNK, RET_CHUNK)),
                  _const_spec((RET_CHUNK, RET_QK_W)),
                  _const_spec((RET_CHUNK, RET_V_W)),
                  _const_spec((RET_HEADS, RET_DK, RET_DV)),
                  _const_spec((1, RET_V_W)), _const_spec((1, RET_V_W))],
        out_specs=rows(RET_V_W),
        scratch_shapes=[pltpu.VMEM((RET_HEADS, RET_DK, RET_DV), F32)],
        compiler_params=_cparams(("parallel", "arbitrary")),
        name="retention",
    )(rq, rk, rv, rg, din, kdec, qdec, cdec, gn_g, gn_b)


def _memkv_kernel(m_ref, w_ref, o_ref):
    o_ref[...] = _dot(m_ref[...].astype(BF16), w_ref[...]).astype(BF16)


def _memkv(mem2d, w_bf):
    n = mem2d.shape[0]
    return pl.pallas_call(
        _memkv_kernel,
        out_shape=jax.ShapeDtypeStruct((n, 2 * MEM_W), BF16),
        grid=(n // MEMKV_ROWS,),
        in_specs=[pl.BlockSpec((MEMKV_ROWS, D_MODEL), lambda i: (i, 0)),
                  _const_spec((D_MODEL, 2 * MEM_W))],
        out_specs=pl.BlockSpec((MEMKV_ROWS, 2 * MEM_W), lambda i: (i, 0)),
        compiler_params=_cparams(("parallel",)),
        name="mem_kv",
    )(mem2d, w_bf)


def _mematt_kernel(q_ref, k_ref, v_ref, o_ref):
    q = q_ref[...]
    k = k_ref[...]
    v = v_ref[...]
    for h in range(MEM_HEADS):
        hs = slice(h * MEM_DH, (h + 1) * MEM_DH)
        logits = _dot_nt(q[:, hs], k[:, hs]) * (MEM_DH ** -0.5)
        m = jnp.max(logits, axis=-1, keepdims=True)
        p = jnp.exp(logits - m)
        denom = jnp.sum(p, axis=-1, keepdims=True)
        o_ref[:, hs] = (_dot(p.astype(BF16), v[:, hs]) / denom).astype(BF16)


def _mematt(mq, mkv):
    tiles = SEQ // MEMATT_ROWS
    return pl.pallas_call(
        _mematt_kernel,
        out_shape=jax.ShapeDtypeStruct((N_TOK, MEM_W), BF16),
        grid=(BATCH, tiles),
        in_specs=[pl.BlockSpec((MEMATT_ROWS, MEM_W), lambda b, i: (b * tiles + i, 0)),
                  pl.BlockSpec((MEM_TOKENS, MEM_W), lambda b, i: (b, 0)),
                  pl.BlockSpec((MEM_TOKENS, MEM_W), lambda b, i: (b, 1))],
        out_specs=pl.BlockSpec((MEMATT_ROWS, MEM_W), lambda b, i: (b * tiles + i, 0)),
        compiler_params=_cparams(("parallel", "parallel")),
        name="mem_attention",
    )(mq, mkv, mkv)


def _t5_bias_kernel(table_ref, o_ref, bmax_ref):
    j = lax.broadcasted_iota(jnp.int32, (DSA_T, DSA_T), 0)
    r = lax.broadcasted_iota(jnp.int32, (DSA_T, DSA_T), 1)
    max_exact = T5_BUCKETS // 2
    head_max = [None] * DSA_HEADS
    for w in range(2):
        n = jnp.maximum(r + (1 - w) * DSA_T - j, 0)
        nf = jnp.maximum(n, 1).astype(F32)
        large = max_exact + (jnp.log(nf / max_exact) / math.log(T5_MAX_DIST / max_exact)
                             * (T5_BUCKETS - max_exact)).astype(jnp.int32)
        large = jnp.minimum(large, T5_BUCKETS - 1)
        bucket = jnp.where(n < max_exact, n, large)
        for h in range(DSA_HEADS):
            acc = jnp.zeros((DSA_T, DSA_T), F32)
            for b in range(T5_BUCKETS):
                acc = jnp.where(bucket == b, table_ref[b, h] * LOG2E, acc)
            o_ref[w, h] = acc
            top = jnp.max(jnp.max(acc, axis=0, keepdims=True), axis=1, keepdims=True)
            head_max[h] = top if head_max[h] is None else jnp.maximum(head_max[h], top)
    for h in range(DSA_HEADS):
        bmax_ref[h:h + 1, :] = jnp.broadcast_to(head_max[h], (1, LANES))


def _t5_bias(t5_table):
    return pl.pallas_call(
        _t5_bias_kernel,
        out_shape=(jax.ShapeDtypeStruct((2, DSA_HEADS, DSA_T, DSA_T), F32),
                   jax.ShapeDtypeStruct((DSA_HEADS, LANES), F32)),
        in_specs=[pl.BlockSpec(memory_space=pltpu.SMEM)],
        out_specs=(pl.BlockSpec(memory_space=pltpu.VMEM), pl.BlockSpec(memory_space=pltpu.VMEM)),
        compiler_params=pltpu.CompilerParams(vmem_limit_bytes=VMEM_LIMIT),
        name="t5_bias",
    )(t5_table)


def _fold_groups(x, op):
    return op(x.reshape(DSA_T // SUBLANES, SUBLANES, DSA_T), axis=0)


def _dsa_kernel(table_ref, bias_ref, bmax_ref, iq_ref, dq_ref, ikw_q_ref, ikb_ref, dk_ref, dvt_ref,
                o_ref, score_ref, logit_ref, acc_ref):
    qb = pl.program_id(1)
    q0 = qb * DSA_T
    n_chunks = qb + 1
    groups = DSA_T // SUBLANES
    key_in_chunk = lax.broadcasted_iota(jnp.int32, (DSA_T, DSA_T), 0)
    query_in_step = lax.broadcasted_iota(jnp.int32, (DSA_T, DSA_T), 1)

    def key_rows(c):
        return pl.ds(pl.multiple_of(c * DSA_T, DSA_T), DSA_T)

    def pair_cols(j):
        return slice(j * LANES, (j + 1) * LANES)

    def pair_products(keys_even, keys_odd, queries):
        prod = _dot_nt(jnp.concatenate([keys_even, keys_odd], axis=0), queries)
        return prod[:DSA_T], prod[DSA_T:]

    iw_t = ikw_q_ref[...].T[IDX_DIM:IDX_DIM + IDX_HEADS, :] * IW_SCALE

    def scores_and_logits(c, raw_max, causal):
        ik_low = ikb_ref[key_rows(c), 0:LANES]
        ik_high = ikb_ref[key_rows(c), LANES:2 * LANES]
        last = IDX_HEADS // 2 - 1
        for j in range(IDX_HEADS // 2):
            rel_even, rel_odd = pair_products(ik_low, ik_high, iq_ref[:, pair_cols(j)])
            term = (jnp.maximum(rel_even, 0.0) * iw_t[2 * j:2 * j + 1, :]
                    + jnp.maximum(rel_odd, 0.0) * iw_t[2 * j + 1:2 * j + 2, :])
            total = term if j == 0 else score_ref[c] + term
            score_ref[c] = jnp.where(causal, total, NEG_INF) if (j == last and causal is not None) else total
        out = list(raw_max)
        for j in range(DSA_HEADS // 2):
            keys = [dk_ref[key_rows(c), (2 * j + i) * HEAD_PAD:(2 * j + i + 1) * HEAD_PAD] for i in range(2)]
            for h, qk in zip((2 * j, 2 * j + 1), pair_products(keys[0], keys[1], dq_ref[:, pair_cols(j)])):
                logit_ref[h, c] = qk
                out[h] = jnp.maximum(out[h], _fold_groups(qk, jnp.max))
        return tuple(out)

    def paired_loop(start, stop, body, carry):
        n = stop - start
        carry = lax.fori_loop(
            0, lax.shift_right_logical(n, 1),
            lambda i, cr: body(start + 2 * i + 1, body(start + 2 * i, cr)), carry)
        return lax.cond(jnp.bitwise_and(n, 1) == 1, lambda cr: body(stop - 1, cr), lambda cr: cr, carry)

    raw_max = tuple(jnp.full((SUBLANES, DSA_T), NEG_INF, F32) for _ in range(DSA_HEADS))
    raw_max = paired_loop(0, qb, lambda c, mx: scores_and_logits(c, mx, None), raw_max)
    raw_max = scores_and_logits(qb, raw_max, key_in_chunk <= query_in_step)

    def count_where(pred):
        def body(c, acc):
            parts = [acc] + [jnp.zeros((SUBLANES, DSA_T), F32)] * (COUNT_ACCUMULATORS - 1)
            for g in range(groups):
                blk = score_ref[c, g * SUBLANES:(g + 1) * SUBLANES, :]
                a = g % COUNT_ACCUMULATORS
                parts[a] = parts[a] + jnp.where(pred(c, g * SUBLANES, blk), 1.0, 0.0)
            while len(parts) > 1:
                parts = [parts[i] + parts[i + 1] for i in range(0, len(parts), 2)]
            return parts[0]
        acc = lax.fori_loop(0, n_chunks, body, jnp.zeros((SUBLANES, DSA_T), F32))
        return jnp.sum(acc, axis=0, keepdims=True)

    def count_ge(thr):
        thr_b = jnp.broadcast_to(thr, (SUBLANES, DSA_T))
        return count_where(lambda c, r0, blk: blk >= thr_b)

    def minmax_body(c, carry):
        mn, mx = carry
        blk = score_ref[c]
        mx = jnp.maximum(mx, _fold_groups(blk, jnp.max))
        mn = jnp.minimum(mn, _fold_groups(jnp.where(blk > 0.5 * NEG_INF, blk, -NEG_INF), jnp.min))
        return mn, mx

    mn, mx = lax.fori_loop(
        0, n_chunks, minmax_body,
        (jnp.full((SUBLANES, DSA_T), -NEG_INF, F32), jnp.full((SUBLANES, DSA_T), NEG_INF, F32)))
    q_min = jnp.min(mn, axis=0, keepdims=True)
    q_max = jnp.max(mx, axis=0, keepdims=True)
    t_q = q0 + lax.broadcasted_iota(jnp.int32, (1, DSA_T), 1)

    kf = float(TOPK)
    init = (q_min, q_max + jnp.maximum(jnp.abs(q_max), 1.0) * 1e-6,
            (t_q + 1).astype(F32), jnp.zeros((1, DSA_T), F32))

    def bisect_step(_i, carry):
        lo, hi, cnt_lo, cnt_hi = carry
        mid = 0.5 * (lo + hi)
        cnt = count_ge(mid)
        ge = cnt >= kf
        return (jnp.where(ge, mid, lo), jnp.where(ge, hi, mid),
                jnp.where(ge, cnt, cnt_lo), jnp.where(ge, cnt_hi, cnt))

    lo, hi, cnt_lo, cnt_hi = lax.fori_loop(0, BISECT_PLAIN_STEPS, bisect_step, init)

    big = -NEG_INF

    def snap_body(c, carry):
        mn, mx = carry
        blk = score_ref[c]
        mn = jnp.minimum(mn, _fold_groups(jnp.where(blk >= lo, blk, big), jnp.min))
        mx = jnp.maximum(mx, _fold_groups(jnp.where(blk < hi, blk, -big), jnp.max))
        return mn, mx

    mn, mx = lax.fori_loop(
        0, n_chunks, snap_body,
        (jnp.full((SUBLANES, DSA_T), big, F32), jnp.full((SUBLANES, DSA_T), -big, F32)))
    lo = jnp.min(mn, axis=0, keepdims=True)
    top = jnp.max(mx, axis=0, keepdims=True)

    def pending_of(carry):
        lo, top, cnt_lo, _ = carry
        return jnp.max(jnp.where((cnt_lo > kf) & (lo < top), 1.0, 0.0))

    def value_step(state):
        (lo, top, cnt_lo, cnt_hi), _ = state
        unresolved = (cnt_lo > kf) & (lo < top)
        mid = lo + (top - lo) * 0.5
        mid = jnp.minimum(jnp.where(mid > lo, mid, top), top)

        def body(c, carry):
            cnt, mn, mx = carry
            blk = score_ref[c]
            is_ge = blk >= mid
            cnt = cnt + _fold_groups(jnp.where(is_ge, 1.0, 0.0), jnp.sum)
            mn = jnp.minimum(mn, _fold_groups(jnp.where(is_ge, blk, big), jnp.min))
            mx = jnp.maximum(mx, _fold_groups(jnp.where(is_ge, -big, blk), jnp.max))
            return cnt, mn, mx

        cnt, mn, mx = lax.fori_loop(
            0, n_chunks, body,
            (jnp.zeros((SUBLANES, DSA_T), F32), jnp.full((SUBLANES, DSA_T), big, F32),
             jnp.full((SUBLANES, DSA_T), -big, F32)))
        cnt = jnp.sum(cnt, axis=0, keepdims=True)
        ge = cnt >= kf
        raise_lo = unresolved & ge
        lower_top = unresolved & jnp.logical_not(ge)
        carry = (jnp.where(raise_lo, jnp.min(mn, axis=0, keepdims=True), lo),
                 jnp.where(lower_top, jnp.max(mx, axis=0, keepdims=True), top),
                 jnp.where(raise_lo, cnt, cnt_lo), jnp.where(lower_top, cnt, cnt_hi))
        return carry, pending_of(carry)

    start = (lo, top, cnt_lo, cnt_hi)
    (lo, top, cnt_lo, cnt_hi), _ = lax.while_loop(
        lambda state: state[1] > 0.5, value_step, (start, pending_of(start)))

    tied = cnt_lo > kf
    need = kf - cnt_hi
    lo_b = jnp.broadcast_to(lo, (SUBLANES, DSA_T))
    key_in_group = lax.broadcasted_iota(jnp.int32, (SUBLANES, DSA_T), 0)

    def tie_search(_):
        def body(_i, carry):
            lo_i, hi_i = carry
            mid_i = lax.shift_right_arithmetic(lo_i + hi_i, 1)
            mid_b = jnp.broadcast_to(mid_i, (SUBLANES, DSA_T))
            cnt = count_where(lambda c, r0, blk: (blk == lo_b) & (key_in_group + (c * DSA_T + r0) <= mid_b))
            enough = cnt >= need
            return jnp.where(enough, lo_i, mid_i), jnp.where(enough, mid_i, hi_i)
        lo_i = jnp.full((1, DSA_T), -1, jnp.int32)
        hi_i = jnp.full((1, DSA_T), SEQ - 1, jnp.int32)
        _, cut = lax.fori_loop(0, int(math.log2(SEQ)) + 1, body, (lo_i, hi_i))
        return jnp.where(tied, cut, SEQ)

    cut = lax.cond(jnp.max(jnp.where(tied, 1.0, 0.0)) > 0.5, tie_search,
                   lambda _: jnp.full((1, DSA_T), SEQ, jnp.int32), 0)

    def mask_body(c, carry):
        blk = score_ref[c]
        keep = (blk > lo) | ((blk == lo) & (key_in_chunk + c * DSA_T <= cut))
        score_ref[c] = jnp.where(keep, 0.0, NEG_INF)
        return carry

    lax.fori_loop(0, n_chunks, mask_body, 0)

    n_far = jnp.maximum(qb - 1, 0)

    def masked_logits(c, h, far):
        lg = logit_ref[h, c] + score_ref[c]
        return lg if far else lg + bias_ref[c - qb + 1, h]

    def far_bias(h):
        return table_ref[T5_BUCKETS - 1, h] * LOG2E

    def den_row(h):
        return h * DVT_HEAD_ROWS + DSA_DH

    def exp_and_pv(shifts):
        acc_ref[...] = jnp.zeros_like(acc_ref)

        def body(c, carry, far):
            for h in range(DSA_HEADS):
                shift = shifts[h] - far_bias(h) if far else shifts[h]
                p = jnp.exp2(masked_logits(c, h, far) - shift).astype(BF16)
                rows = slice(h * DVT_HEAD_ROWS, (h + 1) * DVT_HEAD_ROWS)
                acc_ref[rows, :] += _dot(dvt_ref[c, rows, :], p)
            return carry

        paired_loop(0, n_far, lambda c, carry: body(c, carry, True), 0)
        paired_loop(n_far, n_chunks, lambda c, carry: body(c, carry, False), 0)

    upper = [jnp.max(raw_max[h], axis=0, keepdims=True) + bmax_ref[h:h + 1, 0:1] for h in range(DSA_HEADS)]
    exp_and_pv(upper)

    den_min = acc_ref[den_row(0):den_row(0) + 1, :]
    for h in range(1, DSA_HEADS):
        den_min = jnp.minimum(den_min, acc_ref[den_row(h):den_row(h) + 1, :])

    @pl.when(jnp.min(den_min) < DEN_FLOOR)
    def _():
        def max_body(c, ms, far):
            return tuple(jnp.maximum(ms[h], _fold_groups(masked_logits(c, h, far), jnp.max) + (far_bias(h) if far else 0.0))
                         for h in range(DSA_HEADS))
        ms = tuple(jnp.full((SUBLANES, DSA_T), NEG_INF, F32) for _ in range(DSA_HEADS))
        ms = lax.fori_loop(0, n_far, lambda c, ms: max_body(c, ms, True), ms)
        ms = lax.fori_loop(n_far, n_chunks, lambda c, ms: max_body(c, ms, False), ms)
        exp_and_pv([jnp.max(m, axis=0, keepdims=True) for m in ms])

    heads = []
    for h in range(DSA_HEADS):
        r0 = h * DVT_HEAD_ROWS
        heads.append(acc_ref[r0:r0 + DSA_DH, :] / acc_ref[den_row(h):den_row(h) + 1, :])
    o_ref[...] = jnp.concatenate(heads, axis=0).T.astype(BF16)


def _dsa(t5_table, bias, bias_max, iq, dq, ikw, ikb, dk, dvt):
    nqb = SEQ // DSA_T
    padded = DSA_HEADS * HEAD_PAD
    qrows = lambda w: pl.BlockSpec((DSA_T, w), lambda b, i: (b * nqb + i, 0))
    krows = lambda w: pl.BlockSpec((SEQ, w), lambda b, i: (b, 0))
    return pl.pallas_call(
        _dsa_kernel,
        out_shape=jax.ShapeDtypeStruct((N_TOK, DSA_W), BF16),
        grid=(BATCH, nqb),
        in_specs=[pl.BlockSpec(memory_space=pltpu.SMEM),
                  _const_spec((2, DSA_HEADS, DSA_T, DSA_T)), _const_spec((DSA_HEADS, LANES)),
                  qrows(IDX_Q_W), qrows(DSA_W), qrows(LANES),
                  krows(2 * LANES), krows(padded),
                  pl.BlockSpec((nqb, DVT_ROWS, DSA_T), lambda b, i: (b, 0, 0))],
        out_specs=qrows(DSA_W),
        scratch_shapes=[pltpu.VMEM((nqb, DSA_T, DSA_T), F32),
                        pltpu.VMEM((DSA_HEADS, nqb, DSA_T, DSA_T), F32),
                        pltpu.VMEM((DVT_ROWS, DSA_T), F32)],
        compiler_params=_cparams(("parallel", "arbitrary")),
        name="dsa_attention",
    )(t5_table, bias, bias_max, iq, dq, ikw, ikb, dk, dvt)


def _merge_kernel(h_ref, oret_ref, odsa_ref, omem_ref, wg_ref, wr_ref, wd_ref, wm_ref, wo_ref,
                  g_ref, b_ref, o_ref):
    h = h_ref[...]
    hb = h.astype(BF16)
    merged = None
    for i, (src, w) in enumerate(((oret_ref, wr_ref), (odsa_ref, wd_ref), (omem_ref, wm_ref))):
        gate = jax.nn.sigmoid(_dot(hb, wg_ref[:, i * D_MODEL:(i + 1) * D_MODEL]))
        term = gate * _dot(src[...], w[...])
        merged = term if merged is None else merged + term
    mix = _dot(merged.astype(BF16), wo_ref[...])
    o_ref[...] = _layer_norm(ALPHA * h + mix, g_ref[...], b_ref[...])


def _merge(h, o_ret, o_dsa, o_mem, wg, wr, wd, wm, wo, g, b):
    rows = lambda w: pl.BlockSpec((MERGE_ROWS, w), lambda i: (i, 0))
    return pl.pallas_call(
        _merge_kernel,
        out_shape=jax.ShapeDtypeStruct((N_TOK, D_MODEL), F32),
        grid=(N_TOK // MERGE_ROWS,),
        in_specs=[rows(D_MODEL), rows(RET_V_W), rows(DSA_W), rows(MEM_W),
                  _const_spec((D_MODEL, 3 * D_MODEL)),
                  _const_spec((RET_V_W, D_MODEL)), _const_spec((DSA_W, D_MODEL)),
                  _const_spec((MEM_W, D_MODEL)), _const_spec((D_MODEL, D_MODEL)),
                  _const_spec((1, D_MODEL)), _const_spec((1, D_MODEL))],
        out_specs=rows(D_MODEL),
        compiler_params=_cparams(("parallel",)),
        name="merge_ln2",
    )(h, o_ret, o_dsa, o_mem, wg, wr, wd, wm, wo, g, b)


def _rope_tables():
    half = RET_DK // 2
    freqs = ROPE_BASE ** (-jnp.arange(half, dtype=F32) / half)
    ang = jnp.arange(SEQ).astype(F32)[:, None] * freqs[None, :]
    cos, sin = jnp.cos(ang), jnp.sin(ang)
    cos_t = jnp.tile(jnp.concatenate([cos, cos], axis=-1), (1, RET_HEADS))
    sin_t = jnp.tile(jnp.concatenate([-sin, sin], axis=-1), (1, RET_HEADS))
    return cos_t, sin_t


def _decay_tables():
    c = RET_CHUNK
    gamma = 1.0 - 2.0 ** (-5.0 - jnp.arange(RET_HEADS, dtype=F32))
    lg = jnp.log(gamma)
    i = jnp.arange(c)
    diff = i[:, None] - i[None, :]
    decay_in = jnp.where(diff[None] >= 0, jnp.exp(jnp.maximum(diff, 0)[None] * lg[:, None, None]), 0.0).astype(F32)
    k_dec = jnp.exp((c - 1 - i)[None, :] * lg[:, None]).astype(F32)
    q_dec = jnp.exp((i + 1)[None, :] * lg[:, None]).astype(F32)
    chunk_dec = jnp.exp(c * lg).astype(F32)
    kdec_t = jnp.repeat(k_dec.T, RET_DK, axis=1)
    qdec_t = jnp.repeat(q_dec.T, RET_DV, axis=1)
    cdec_t = jnp.broadcast_to(chunk_dec[:, None, None], (RET_HEADS, RET_DK, RET_DV))
    return decay_in, kdec_t, qdec_t, cdec_t


def _layer(x, mem2d, ffn1_w_in, ffn1_w_out, ln1_g, ln1_b, w_in, t5_table, ret_gn_g, ret_gn_b,
           w_mem_kv, w_br_ret, w_br_dsa, w_br_mem, w_out, ln2_g, ln2_b,
           ffn2_w_in, ffn2_w_out, ln3_g, ln3_b, tables):
    cos_t, sin_t, din, kdec, qdec, cdec = tables
    row = lambda v: v.reshape(1, -1)
    bf = lambda w: w.astype(BF16)

    w_proj = jnp.concatenate(
        [w_in[:, :W_DV0], w_in[:, W_IQ0:W_IK0], w_in[:, W_MQ0:W_G0], w_in[:, W_IK0:W_MQ0],
         jnp.zeros((D_MODEL, LANES - IDX_DIM - IDX_HEADS), F32)], axis=1)
    w_dvt = w_in[:, W_DV0:W_IQ0].T

    h = _ffn_ln(x, bf(ffn1_w_in), bf(ffn1_w_out), row(ln1_g), row(ln1_b), "ffn1_ln1")
    rq, rk, rv, rg, dq, dk, dvt, iq, mq, ikw, ikb = _proj(h, bf(w_proj), bf(w_dvt), cos_t, sin_t)
    o_ret = _retention(rq, rk, rv, rg, din, kdec, qdec, cdec, row(ret_gn_g), row(ret_gn_b))
    o_mem = _mematt(mq, _memkv(mem2d, bf(w_mem_kv)))
    o_dsa = _dsa(t5_table, *_t5_bias(t5_table), iq, dq, ikw, ikb, dk, dvt)
    x2 = _merge(h, o_ret, o_dsa, o_mem, bf(w_in[:, W_G0:]), bf(w_br_ret), bf(w_br_dsa), bf(w_br_mem),
                bf(w_out), row(ln2_g), row(ln2_b))
    return _ffn_ln(x2, bf(ffn2_w_in), bf(ffn2_w_out), row(ln3_g), row(ln3_b), "ffn2_ln3")


def kernel(x, mem, ffn1_w_in, ffn1_w_out, ln1_g, ln1_b, w_in, t5_table, ret_gn_g, ret_gn_b,
           w_mem_kv, w_br_ret, w_br_dsa, w_br_mem, w_out, ln2_g, ln2_b,
           ffn2_w_in, ffn2_w_out, ln3_g, ln3_b):
    assert x.shape == (BATCH, SEQ, D_MODEL) and mem.shape == (BATCH, MEM_TOKENS, D_MODEL)
    tables = _rope_tables() + _decay_tables()
    y = x.reshape(N_TOK, D_MODEL)
    mem2d = mem.reshape(BATCH * MEM_TOKENS, D_MODEL)
    for l in range(DEPTH):
        y = _layer(y, mem2d, ffn1_w_in[l], ffn1_w_out[l], ln1_g[l], ln1_b[l], w_in[l], t5_table,
                   ret_gn_g[l], ret_gn_b[l], w_mem_kv[l], w_br_ret[l], w_br_dsa[l], w_br_mem[l],
                   w_out[l], ln2_g[l], ln2_b[l], ffn2_w_in[l], ffn2_w_out[l], ln3_g[l], ln3_b[l],
                   tables)
    return y.reshape(BATCH, SEQ, D_MODEL)
```

```python
import math

import jax
import jax.numpy as jnp
from jax import lax
from jax.experimental import pallas as pl
from jax.experimental.pallas import tpu as pltpu

F32 = jnp.float32
BF16 = jnp.bfloat16

D_MODEL = 1024
BATCH = 8
SEQ = 2048
MEM_TOKENS = 256
RET_HEADS, RET_DK, RET_DV, RET_CHUNK = 4, 64, 128, 128
DSA_HEADS, DSA_DH = 8, 64
IDX_HEADS, IDX_DIM = 8, 64
TOPK = min(256, SEQ // 4)
MEM_HEADS, MEM_DH = 4, 128
T5_BUCKETS, T5_MAX_DIST = 32, 128
D_FF = 2816
ROPE_BASE = 10000.0
LN_EPS = 1e-5
NEG_INF = -1e30
DEPTH = 1
ALPHA = (2.0 * DEPTH) ** 0.25

RET_QK_W = RET_HEADS * RET_DK
RET_V_W = RET_HEADS * RET_DV
DSA_W = DSA_HEADS * DSA_DH
IDX_Q_W = IDX_HEADS * IDX_DIM
MEM_W = MEM_HEADS * MEM_DH
N_TOK = BATCH * SEQ

V7X_VMEM_BYTES = 64 * 1024 * 1024
VMEM_LIMIT = V7X_VMEM_BYTES - 8 * 1024 * 1024
LANES = 128
SUBLANES = 8

FFN_ROWS = 1024
MXU_TILE = 256
FFN_CHUNK_EDGES = (0, 6 * MXU_TILE, D_FF)
PROJ_ROWS = 512
MERGE_ROWS = 1024
RET_ROWS = 512
MEMKV_ROWS = 512
MEMATT_ROWS = 512
DSA_T = 256
HEAD_PAD = LANES
DVT_HEAD_ROWS = DSA_DH + 16
DVT_ROWS = DSA_HEADS * DVT_HEAD_ROWS
BISECT_PLAIN_STEPS = 14
COUNT_ACCUMULATORS = 4
LOG2E = math.log2(math.e)
DEN_FLOOR = 2.0 ** -100

W_RQ0 = 0
W_RK0 = W_RQ0 + RET_QK_W
W_RV0 = W_RK0 + RET_QK_W
W_RG0 = W_RV0 + RET_V_W
W_DQ0 = W_RG0 + RET_V_W
W_DK0 = W_DQ0 + DSA_W
W_DV0 = W_DK0 + DSA_W
W_IQ0 = W_DV0 + DSA_W
W_IK0 = W_IQ0 + IDX_Q_W
W_IW0 = W_IK0 + IDX_DIM
W_MQ0 = W_IW0 + IDX_HEADS
W_G0 = W_MQ0 + MEM_W
IW_SCALE = IDX_HEADS ** -0.5 * IDX_DIM ** -0.5


def _cparams(sem):
    return pltpu.CompilerParams(dimension_semantics=sem, vmem_limit_bytes=VMEM_LIMIT)


def _const_spec(shape):
    nd = len(shape)
    return pl.BlockSpec(shape, lambda *_: (0,) * nd, pipeline_mode=pl.Buffered(1))


def _layer_norm(y, g, b):
    mu = jnp.mean(y, axis=-1, keepdims=True)
    yc = y - mu
    var = jnp.mean(yc * yc, axis=-1, keepdims=True)
    return yc * lax.rsqrt(var + LN_EPS) * g + b


def _dot(a, b):
    return jnp.dot(a, b, preferred_element_type=F32)


def _dot_nt(a, b):
    return lax.dot_general(a, b, (((1,), (1,)), ((), ())), preferred_element_type=F32)


def _ffn_ln_kernel(x_ref, wi_ref, wo_ref, g_ref, b_ref, o_ref):
    x = x_ref[...]
    xb = x.astype(BF16)
    acc = None
    for lo, hi in zip(FFN_CHUNK_EDGES[:-1], FFN_CHUNK_EDGES[1:]):
        a = _dot(xb, wi_ref[:, lo:hi])
        u = _dot(xb, wi_ref[:, D_FF + lo:D_FF + hi])
        act = (jax.nn.silu(a) * u).astype(BF16)
        part = _dot(act, wo_ref[lo:hi, :])
        acc = part if acc is None else acc + part
    o_ref[...] = _layer_norm(ALPHA * x + 0.5 * acc, g_ref[...], b_ref[...])


def _ffn_ln(x, w_in_bf, w_out_bf, g, b, name):
    return pl.pallas_call(
        _ffn_ln_kernel,
        out_shape=jax.ShapeDtypeStruct((N_TOK, D_MODEL), F32),
        grid=(N_TOK // FFN_ROWS,),
        in_specs=[
            pl.BlockSpec((FFN_ROWS, D_MODEL), lambda i: (i, 0)),
            _const_spec((D_MODEL, 2 * D_FF)),
            _const_spec((D_FF, D_MODEL)),
            _const_spec((1, D_MODEL)),
            _const_spec((1, D_MODEL)),
        ],
        out_specs=pl.BlockSpec((FFN_ROWS, D_MODEL), lambda i: (i, 0)),
        compiler_params=_cparams(("parallel",)),
        name=name,
    )(x, w_in_bf, w_out_bf, g, b)


def _rope(x, cos, sin_signed):
    width = x.shape[-1]
    lane = lax.broadcasted_iota(jnp.int32, x.shape, 1)
    first_half = (lane % RET_DK) < (RET_DK // 2)
    swapped = jnp.where(first_half,
                        pltpu.roll(x, width - RET_DK // 2, 1),
                        pltpu.roll(x, RET_DK // 2, 1))
    return x * cos + swapped * sin_signed


def _store_split_heads(dst_ref, val):
    low = lax.broadcasted_iota(jnp.int32, (val.shape[0], LANES), 1) < DSA_DH
    for j in range(val.shape[1] // LANES):
        pair = val[:, j * LANES:(j + 1) * LANES]
        dst_ref[:, (2 * j) * LANES:(2 * j + 1) * LANES] = jnp.where(low, pair, 0.0).astype(dst_ref.dtype)
        dst_ref[:, (2 * j + 1) * LANES:(2 * j + 2) * LANES] = jnp.where(low, 0.0, pair).astype(dst_ref.dtype)


def _store_low_heads(dst_ref, val):
    low = lax.broadcasted_iota(jnp.int32, (val.shape[0], LANES), 1) < RET_DK
    for j in range(val.shape[1] // LANES):
        pair = val[:, j * LANES:(j + 1) * LANES]
        dst_ref[:, (2 * j) * LANES:(2 * j + 1) * LANES] = jnp.where(low, pair, 0.0).astype(dst_ref.dtype)
        dst_ref[:, (2 * j + 1) * LANES:(2 * j + 2) * LANES] = jnp.where(
            low, pltpu.roll(pair, RET_DK, 1), 0.0).astype(dst_ref.dtype)


def _proj_kernel(h_ref, w_ref, wikw_ref, wmq_ref, wdvt_ref, cos_ref, sin_ref, kdec_ref,
                 rq_ref, rk_ref, rkdt_ref, rv_ref, rg_ref, dq_ref, dk_ref, dvt_ref, iq_ref, mq_ref,
                 ikw_ref, ikb_ref):
    hb = h_ref[...].astype(BF16)

    def proj(lo, width):
        return _dot(hb, w_ref[:, lo:lo + width])

    cos = cos_ref[...]
    sin = sin_ref[...]
    _store_low_heads(rq_ref, _rope(proj(W_RQ0, RET_QK_W), cos, sin))
    rk = _rope(proj(W_RK0, RET_QK_W), cos, sin) * (RET_DK ** -0.5)
    _store_low_heads(rk_ref, rk)
    rkdt_ref[...] = (rk * kdec_ref[...]).T.astype(BF16)
    rv_ref[...] = proj(W_RV0, RET_V_W).astype(BF16)
    rg_ref[...] = proj(W_RG0, RET_V_W)
    dq_ref[...] = (proj(W_DQ0, DSA_W) * (DSA_DH ** -0.5 * LOG2E)).astype(BF16)
    _store_split_heads(dk_ref, proj(W_DK0, DSA_W))
    iq_ref[...] = proj(W_IQ0, IDX_Q_W).astype(BF16)
    mq_ref[...] = _dot(hb, wmq_ref[...]).astype(BF16)
    ikw = _dot(hb, wikw_ref[...])
    ikw_ref[...] = ikw
    low = lax.broadcasted_iota(jnp.int32, ikw.shape, 1) < IDX_DIM
    ik_low = jnp.where(low, ikw, 0.0)
    ikb_ref[:, 0:LANES] = ik_low.astype(BF16)
    ikb_ref[:, LANES:2 * LANES] = pltpu.roll(ik_low, IDX_DIM, 1).astype(BF16)
    vt = _dot_nt(wdvt_ref[...], hb).astype(BF16)
    ones = jnp.ones((DVT_HEAD_ROWS - DSA_DH, DSA_T), BF16)
    for j in range(PROJ_ROWS // DSA_T):
        for h in range(DSA_HEADS):
            r0 = h * DVT_HEAD_ROWS
            dvt_ref[j, r0:r0 + DSA_DH, :] = vt[h * DSA_DH:(h + 1) * DSA_DH, j * DSA_T:(j + 1) * DSA_T]
            dvt_ref[j, r0 + DSA_DH:r0 + DVT_HEAD_ROWS, :] = ones


def _proj(h, w_in_bf, w_mq_bf, w_dvt_bf, cos_t, sin_t, kdec_rows):
    rows = lambda w: pl.BlockSpec((PROJ_ROWS, w), lambda i: (i, 0))
    seq_tiles = SEQ // PROJ_ROWS
    pos = lambda w: pl.BlockSpec((PROJ_ROWS, w), lambda i: (i % seq_tiles, 0))
    sd = jax.ShapeDtypeStruct
    ret_padded = RET_HEADS * HEAD_PAD
    dsa_padded = DSA_HEADS * HEAD_PAD
    slabs = PROJ_ROWS // DSA_T
    assert W_IK0 % LANES == 0
    return pl.pallas_call(
        _proj_kernel,
        out_shape=(
            sd((N_TOK, ret_padded), BF16), sd((N_TOK, ret_padded), BF16), sd((RET_QK_W, N_TOK), BF16),
            sd((N_TOK, RET_V_W), BF16), sd((N_TOK, RET_V_W), F32),
            sd((N_TOK, DSA_W), BF16), sd((N_TOK, dsa_padded), BF16),
            sd((N_TOK // DSA_T, DVT_ROWS, DSA_T), BF16),
            sd((N_TOK, IDX_Q_W), BF16), sd((N_TOK, MEM_W), BF16),
            sd((N_TOK, LANES), F32), sd((N_TOK, 2 * LANES), BF16),
        ),
        grid=(N_TOK // PROJ_ROWS,),
        in_specs=[rows(D_MODEL),
                  pl.BlockSpec((D_MODEL, W_IK0), lambda i: (0, 0), pipeline_mode=pl.Buffered(1)),
                  pl.BlockSpec((D_MODEL, LANES), lambda i: (0, W_IK0 // LANES), pipeline_mode=pl.Buffered(1)),
                  _const_spec((D_MODEL, MEM_W)), _const_spec((DSA_W, D_MODEL)),
                  pos(RET_QK_W), pos(RET_QK_W), _const_spec((PROJ_ROWS, RET_QK_W))],
        out_specs=(rows(ret_padded), rows(ret_padded),
                   pl.BlockSpec((RET_QK_W, PROJ_ROWS), lambda i: (0, i)),
                   rows(RET_V_W), rows(RET_V_W),
                   rows(DSA_W), rows(dsa_padded),
                   pl.BlockSpec((slabs, DVT_ROWS, DSA_T), lambda i: (i, 0, 0)),
                   rows(IDX_Q_W), rows(MEM_W), rows(LANES), rows(2 * LANES)),
        compiler_params=_cparams(("parallel",)),
        name="mixer_proj",
    )(h, w_in_bf, w_in_bf, w_mq_bf, w_dvt_bf, cos_t, sin_t, kdec_rows)


def _retention_kernel(q_ref, k_ref, kdt_ref, v_ref, g_ref, din_ref, qdec_ref, cdec_ref,
                      gng_ref, gnb_ref, o_ref, state_ref):
    @pl.when(pl.program_id(1) == 0)
    def _():
        state_ref[...] = jnp.zeros_like(state_ref)

    qdec = qdec_ref[...]
    gn_g = gng_ref[...]
    gn_b = gnb_ref[...]
    states = [state_ref[h] for h in range(RET_HEADS)]
    pad_rows = jnp.zeros((HEAD_PAD - RET_DK, RET_DV), BF16)
    for j in range(RET_ROWS // RET_CHUNK):
        rows = slice(j * RET_CHUNK, (j + 1) * RET_CHUNK)
        gate = g_ref[rows, :]
        for h in range(RET_HEADS):
            hs = slice(h * HEAD_PAD, (h + 1) * HEAD_PAD)
            vs = slice(h * RET_DV, (h + 1) * RET_DV)
            qh = q_ref[rows, hs]
            vh = v_ref[rows, vs]
            scores = _dot_nt(qh, k_ref[rows, hs]) * din_ref[h]
            intra = _dot(scores.astype(BF16), vh)
            state_padded = jnp.concatenate([states[h].astype(BF16), pad_rows], axis=0)
            cross = _dot(qh, state_padded) * qdec[:, vs]
            kv = _dot(kdt_ref[h * RET_DK:(h + 1) * RET_DK, rows], vh)
            states[h] = cdec_ref[h] * states[h] + kv
            o = intra + cross
            mu = jnp.mean(o, axis=-1, keepdims=True)
            oc = o - mu
            var = jnp.mean(oc * oc, axis=-1, keepdims=True)
            o = oc * lax.rsqrt(var + LN_EPS) * gn_g[:, vs] + gn_b[:, vs]
            o_ref[rows, vs] = (jax.nn.silu(gate[:, vs]) * o).astype(BF16)
    for h in range(RET_HEADS):
        state_ref[h] = states[h]


def _retention(rq, rk, rkdt, rv, rg, din, qdec, cdec, gn_g, gn_b):
    nc = SEQ // RET_ROWS
    rows = lambda w: pl.BlockSpec((RET_ROWS, w), lambda b, n: (b * nc + n, 0))
    padded = RET_HEADS * HEAD_PAD
    return pl.pallas_call(
        _retention_kernel,
        out_shape=jax.ShapeDtypeStruct((N_TOK, RET_V_W), BF16),
        grid=(BATCH, nc),
        in_specs=[rows(padded), rows(padded),
                  pl.BlockSpec((RET_QK_W, RET_ROWS), lambda b, n: (0, b * nc + n)),
                  rows(RET_V_W), rows(RET_V_W),
                  _const_spec((RET_HEADS, RET_CHUNK, RET_CHUNK)),
                  _const_spec((RET_CHUNK, RET_V_W)),
                  _const_spec((RET_HEADS, RET_DK, RET_DV)),
                  _const_spec((1, RET_V_W)), _const_spec((1, RET_V_W))],
        out_specs=rows(RET_V_W),
        scratch_shapes=[pltpu.VMEM((RET_HEADS, RET_DK, RET_DV), F32)],
        compiler_params=_cparams(("parallel", "arbitrary")),
        name="retention",
    )(rq, rk, rkdt, rv, rg, din, qdec, cdec, gn_g, gn_b)


def _memkv_kernel(m_ref, w_ref, o_ref):
    o_ref[...] = _dot(m_ref[...].astype(BF16), w_ref[...]).astype(BF16)


def _memkv(mem2d, w_bf):
    n = mem2d.shape[0]
    return pl.pallas_call(
        _memkv_kernel,
        out_shape=jax.ShapeDtypeStruct((n, 2 * MEM_W), BF16),
        grid=(n // MEMKV_ROWS,),
        in_specs=[pl.BlockSpec((MEMKV_ROWS, D_MODEL), lambda i: (i, 0)),
                  _const_spec((D_MODEL, 2 * MEM_W))],
        out_specs=pl.BlockSpec((MEMKV_ROWS, 2 * MEM_W), lambda i: (i, 0)),
        compiler_params=_cparams(("parallel",)),
        name="mem_kv",
    )(mem2d, w_bf)


def _mematt_kernel(q_ref, k_ref, v_ref, o_ref):
    q = q_ref[...]
    k = k_ref[...]
    v = v_ref[...]
    for h in range(MEM_HEADS):
        hs = slice(h * MEM_DH, (h + 1) * MEM_DH)
        logits = _dot_nt(q[:, hs], k[:, hs]) * (MEM_DH ** -0.5)
        m = jnp.max(logits, axis=-1, keepdims=True)
        p = jnp.exp(logits - m)
        denom = jnp.sum(p, axis=-1, keepdims=True)
        o_ref[:, hs] = (_dot(p.astype(BF16), v[:, hs]) / denom).astype(BF16)


def _mematt(mq, mkv):
    tiles = SEQ // MEMATT_ROWS
    return pl.pallas_call(
        _mematt_kernel,
        out_shape=jax.ShapeDtypeStruct((N_TOK, MEM_W), BF16),
        grid=(BATCH, tiles),
        in_specs=[pl.BlockSpec((MEMATT_ROWS, MEM_W), lambda b, i: (b * tiles + i, 0)),
                  pl.BlockSpec((MEM_TOKENS, MEM_W), lambda b, i: (b, 0)),
                  pl.BlockSpec((MEM_TOKENS, MEM_W), lambda b, i: (b, 1))],
        out_specs=pl.BlockSpec((MEMATT_ROWS, MEM_W), lambda b, i: (b * tiles + i, 0)),
        compiler_params=_cparams(("parallel", "parallel")),
        name="mem_attention",
    )(mq, mkv, mkv)


def _t5_bias_kernel(table_ref, o_ref, bmax_ref):
    j = lax.broadcasted_iota(jnp.int32, (DSA_T, DSA_T), 0)
    r = lax.broadcasted_iota(jnp.int32, (DSA_T, DSA_T), 1)
    max_exact = T5_BUCKETS // 2
    head_max = [None] * DSA_HEADS
    for w in range(2):
        n = jnp.maximum(r + (1 - w) * DSA_T - j, 0)
        nf = jnp.maximum(n, 1).astype(F32)
        large = max_exact + (jnp.log(nf / max_exact) / math.log(T5_MAX_DIST / max_exact)
                             * (T5_BUCKETS - max_exact)).astype(jnp.int32)
        large = jnp.minimum(large, T5_BUCKETS - 1)
        bucket = jnp.where(n < max_exact, n, large)
        for h in range(DSA_HEADS):
            acc = jnp.zeros((DSA_T, DSA_T), F32)
            for b in range(T5_BUCKETS):
                acc = jnp.where(bucket == b, table_ref[b, h] * LOG2E, acc)
            o_ref[w, h] = acc
            top = jnp.max(jnp.max(acc, axis=0, keepdims=True), axis=1, keepdims=True)
            head_max[h] = top if head_max[h] is None else jnp.maximum(head_max[h], top)
    for h in range(DSA_HEADS):
        bmax_ref[h:h + 1, :] = jnp.broadcast_to(head_max[h], (1, LANES))


def _t5_bias(t5_table):
    return pl.pallas_call(
        _t5_bias_kernel,
        out_shape=(jax.ShapeDtypeStruct((2, DSA_HEADS, DSA_T, DSA_T), F32),
                   jax.ShapeDtypeStruct((DSA_HEADS, LANES), F32)),
        in_specs=[pl.BlockSpec(memory_space=pltpu.SMEM)],
        out_specs=(pl.BlockSpec(memory_space=pltpu.VMEM), pl.BlockSpec(memory_space=pltpu.VMEM)),
        compiler_params=pltpu.CompilerParams(vmem_limit_bytes=VMEM_LIMIT),
        name="t5_bias",
    )(t5_table)


def _fold_groups(x, op):
    return op(x.reshape(DSA_T // SUBLANES, SUBLANES, DSA_T), axis=0)


def _dsa_kernel(table_ref, bias_ref, bmax_ref, iq_ref, dq_ref, ikw_q_ref, ikb_ref, dk_ref, dvt_ref,
                o_ref, score_ref, logit_ref, acc_ref):
    qb = pl.program_id(1)
    q0 = qb * DSA_T
    n_chunks = qb + 1
    groups = DSA_T // SUBLANES
    key_in_chunk = lax.broadcasted_iota(jnp.int32, (DSA_T, DSA_T), 0)
    query_in_step = lax.broadcasted_iota(jnp.int32, (DSA_T, DSA_T), 1)

    def key_rows(c):
        return pl.ds(pl.multiple_of(c * DSA_T, DSA_T), DSA_T)

    def pair_cols(j):
        return slice(j * LANES, (j + 1) * LANES)

    def pair_products(keys_even, keys_odd, queries):
        prod = _dot_nt(jnp.concatenate([keys_even, keys_odd], axis=0), queries)
        return prod[:DSA_T], prod[DSA_T:]

    iw_t = ikw_q_ref[...].T[IDX_DIM:IDX_DIM + IDX_HEADS, :] * IW_SCALE

    def scores_and_logits(c, carry, causal):
        raw_max, s_min, s_max = carry
        ik_low = ikb_ref[key_rows(c), 0:LANES]
        ik_high = ikb_ref[key_rows(c), LANES:2 * LANES]
        last = IDX_HEADS // 2 - 1
        for j in range(IDX_HEADS // 2):
            rel_even, rel_odd = pair_products(ik_low, ik_high, iq_ref[:, pair_cols(j)])
            term = (jnp.maximum(rel_even, 0.0) * iw_t[2 * j:2 * j + 1, :]
                    + jnp.maximum(rel_odd, 0.0) * iw_t[2 * j + 1:2 * j + 2, :])
            total = term if j == 0 else score_ref[c] + term
            if j < last:
                score_ref[c] = total
            elif causal is None:
                score_ref[c] = total
                s_min = jnp.minimum(s_min, _fold_groups(total, jnp.min))
                s_max = jnp.maximum(s_max, _fold_groups(total, jnp.max))
            else:
                score_ref[c] = jnp.where(causal, total, NEG_INF)
                s_min = jnp.minimum(s_min, _fold_groups(jnp.where(causal, total, -NEG_INF), jnp.min))
                s_max = jnp.maximum(s_max, _fold_groups(jnp.where(causal, total, NEG_INF), jnp.max))
        out = list(raw_max)
        for j in range(DSA_HEADS // 2):
            keys = [dk_ref[key_rows(c), (2 * j + i) * HEAD_PAD:(2 * j + i + 1) * HEAD_PAD] for i in range(2)]
            for h, qk in zip((2 * j, 2 * j + 1), pair_products(keys[0], keys[1], dq_ref[:, pair_cols(j)])):
                logit_ref[h, c] = qk
                out[h] = jnp.maximum(out[h], _fold_groups(qk, jnp.max))
        return tuple(out), s_min, s_max

    def paired_loop(start, stop, body, carry):
        n = stop - start
        carry = lax.fori_loop(
            0, lax.shift_right_logical(n, 1),
            lambda i, cr: body(start + 2 * i + 1, body(start + 2 * i, cr)), carry)
        return lax.cond(jnp.bitwise_and(n, 1) == 1, lambda cr: body(stop - 1, cr), lambda cr: cr, carry)

    low_init = jnp.full((SUBLANES, DSA_T), NEG_INF, F32)
    carry = (tuple(low_init for _ in range(DSA_HEADS)), -low_init, low_init)
    carry = paired_loop(0, qb, lambda c, cr: scores_and_logits(c, cr, None), carry)
    raw_max, s_min, s_max = scores_and_logits(qb, carry, key_in_chunk <= query_in_step)

    def count_where(pred):
        def body(c, acc):
            parts = [acc] + [jnp.zeros((SUBLANES, DSA_T), F32)] * (COUNT_ACCUMULATORS - 1)
            for g in range(groups):
                blk = score_ref[c, g * SUBLANES:(g + 1) * SUBLANES, :]
                a = g % COUNT_ACCUMULATORS
                parts[a] = parts[a] + jnp.where(pred(c, g * SUBLANES, blk), 1.0, 0.0)
            while len(parts) > 1:
                parts = [parts[i] + parts[i + 1] for i in range(0, len(parts), 2)]
            return parts[0]
        acc = lax.fori_loop(0, n_chunks, body, jnp.zeros((SUBLANES, DSA_T), F32))
        return jnp.sum(acc, axis=0, keepdims=True)

    def count_ge(thr):
        thr_b = jnp.broadcast_to(thr, (SUBLANES, DSA_T))
        return count_where(lambda c, r0, blk: blk >= thr_b)

    q_min = jnp.min(s_min, axis=0, keepdims=True)
    q_max = jnp.max(s_max, axis=0, keepdims=True)
    t_q = q0 + lax.broadcasted_iota(jnp.int32, (1, DSA_T), 1)

    kf = float(TOPK)
    init = (q_min, q_max + jnp.maximum(jnp.abs(q_max), 1.0) * 1e-6,
            (t_q + 1).astype(F32), jnp.zeros((1, DSA_T), F32))

    def bisect_step(_i, carry):
        lo, hi, cnt_lo, cnt_hi = carry
        mid = 0.5 * (lo + hi)
        cnt = count_ge(mid)
        ge = cnt >= kf
        return (jnp.where(ge, mid, lo), jnp.where(ge, hi, mid),
                jnp.where(ge, cnt, cnt_lo), jnp.where(ge, cnt_hi, cnt))

    lo, hi, cnt_lo, cnt_hi = lax.fori_loop(0, BISECT_PLAIN_STEPS, bisect_step, init)

    big = -NEG_INF

    def snap_body(c, carry):
        mn, mx = carry
        blk = score_ref[c]
        mn = jnp.minimum(mn, _fold_groups(jnp.where(blk >= lo, blk, big), jnp.min))
        mx = jnp.maximum(mx, _fold_groups(jnp.where(blk < hi, blk, -big), jnp.max))
        return mn, mx

    mn, mx = lax.fori_loop(
        0, n_chunks, snap_body,
        (jnp.full((SUBLANES, DSA_T), big, F32), jnp.full((SUBLANES, DSA_T), -big, F32)))
    lo = jnp.min(mn, axis=0, keepdims=True)
    top = jnp.max(mx, axis=0, keepdims=True)

    def pending_of(carry):
        lo, top, cnt_lo, _ = carry
        return jnp.max(jnp.where((cnt_lo > kf) & (lo < top), 1.0, 0.0))

    def value_step(state):
        (lo, top, cnt_lo, cnt_hi), _ = state
        unresolved = (cnt_lo > kf) & (lo < top)
        mid = lo + (top - lo) * 0.5
        mid = jnp.minimum(jnp.where(mid > lo, mid, top), top)

        def body(c, carry):
            cnt, mn, mx = carry
            blk = score_ref[c]
            is_ge = blk >= mid
            cnt = cnt + _fold_groups(jnp.where(is_ge, 1.0, 0.0), jnp.sum)
            mn = jnp.minimum(mn, _fold_groups(jnp.where(is_ge, blk, big), jnp.min))
            mx = jnp.maximum(mx, _fold_groups(jnp.where(is_ge, -big, blk), jnp.max))
            return cnt, mn, mx

        cnt, mn, mx = lax.fori_loop(
            0, n_chunks, body,
            (jnp.zeros((SUBLANES, DSA_T), F32), jnp.full((SUBLANES, DSA_T), big, F32),
             jnp.full((SUBLANES, DSA_T), -big, F32)))
        cnt = jnp.sum(cnt, axis=0, keepdims=True)
        ge = cnt >= kf
        raise_lo = unresolved & ge
        lower_top = unresolved & jnp.logical_not(ge)
        carry = (jnp.where(raise_lo, jnp.min(mn, axis=0, keepdims=True), lo),
                 jnp.where(lower_top, jnp.max(mx, axis=0, keepdims=True), top),
                 jnp.where(raise_lo, cnt, cnt_lo), jnp.where(lower_top, cnt, cnt_hi))
        return carry, pending_of(carry)

    start = (lo, top, cnt_lo, cnt_hi)
    (lo, top, cnt_lo, cnt_hi), _ = lax.while_loop(
        lambda state: state[1] > 0.5, value_step, (start, pending_of(start)))

    tied = cnt_lo > kf
    need = kf - cnt_hi
    lo_b = jnp.broadcast_to(lo, (SUBLANES, DSA_T))
    key_in_group = lax.broadcasted_iota(jnp.int32, (SUBLANES, DSA_T), 0)

    def tie_search(_):
        def body(_i, carry):
            lo_i, hi_i = carry
            mid_i = lax.shift_right_arithmetic(lo_i + hi_i, 1)
            mid_b = jnp.broadcast_to(mid_i, (SUBLANES, DSA_T))
            cnt = count_where(lambda c, r0, blk: (blk == lo_b) & (key_in_group + (c * DSA_T + r0) <= mid_b))
            enough = cnt >= need
            return jnp.where(enough, lo_i, mid_i), jnp.where(enough, mid_i, hi_i)
        lo_i = jnp.full((1, DSA_T), -1, jnp.int32)
        hi_i = jnp.full((1, DSA_T), SEQ - 1, jnp.int32)
        _, cut = lax.fori_loop(0, int(math.log2(SEQ)) + 1, body, (lo_i, hi_i))
        return jnp.where(tied, cut, SEQ)

    cut = lax.cond(jnp.max(jnp.where(tied, 1.0, 0.0)) > 0.5, tie_search,
                   lambda _: jnp.full((1, DSA_T), SEQ, jnp.int32), 0)

    def mask_body(c, carry):
        blk = score_ref[c]
        keep = (blk > lo) | ((blk == lo) & (key_in_chunk + c * DSA_T <= cut))
        score_ref[c] = jnp.where(keep, 0.0, NEG_INF)
        return carry

    lax.fori_loop(0, n_chunks, mask_body, 0)

    n_far = jnp.maximum(qb - 1, 0)

    def masked_logits(c, h, far):
        lg = logit_ref[h, c] + score_ref[c]
        return lg if far else lg + bias_ref[c - qb + 1, h]

    def far_bias(h):
        return table_ref[T5_BUCKETS - 1, h] * LOG2E

    def den_row(h):
        return h * DVT_HEAD_ROWS + DSA_DH

    def exp_and_pv(shifts):
        acc_ref[...] = jnp.zeros_like(acc_ref)

        def body(c, carry, far):
            for h in range(DSA_HEADS):
                shift = shifts[h] - far_bias(h) if far else shifts[h]
                p = jnp.exp2(masked_logits(c, h, far) - shift).astype(BF16)
                rows = slice(h * DVT_HEAD_ROWS, (h + 1) * DVT_HEAD_ROWS)
                acc_ref[rows, :] += _dot(dvt_ref[c, rows, :], p)
            return carry

        paired_loop(0, n_far, lambda c, carry: body(c, carry, True), 0)
        paired_loop(n_far, n_chunks, lambda c, carry: body(c, carry, False), 0)

    upper = [jnp.max(raw_max[h], axis=0, keepdims=True) + bmax_ref[h:h + 1, 0:1] for h in range(DSA_HEADS)]
    exp_and_pv(upper)

    den_min = acc_ref[den_row(0):den_row(0) + 1, :]
    for h in range(1, DSA_HEADS):
        den_min = jnp.minimum(den_min, acc_ref[den_row(h):den_row(h) + 1, :])

    @pl.when(jnp.min(den_min) < DEN_FLOOR)
    def _():
        def max_body(c, ms, far):
            return tuple(jnp.maximum(ms[h], _fold_groups(masked_logits(c, h, far), jnp.max) + (far_bias(h) if far else 0.0))
                         for h in range(DSA_HEADS))
        ms = tuple(jnp.full((SUBLANES, DSA_T), NEG_INF, F32) for _ in range(DSA_HEADS))
        ms = lax.fori_loop(0, n_far, lambda c, ms: max_body(c, ms, True), ms)
        ms = lax.fori_loop(n_far, n_chunks, lambda c, ms: max_body(c, ms, False), ms)
        exp_and_pv([jnp.max(m, axis=0, keepdims=True) for m in ms])

    heads = []
    for h in range(DSA_HEADS):
        r0 = h * DVT_HEAD_ROWS
        heads.append(acc_ref[r0:r0 + DSA_DH, :] / acc_ref[den_row(h):den_row(h) + 1, :])
    o_ref[...] = jnp.concatenate(heads, axis=0).T.astype(BF16)


def _dsa(t5_table, bias, bias_max, iq, dq, ikw, ikb, dk, dvt):
    nqb = SEQ // DSA_T
    padded = DSA_HEADS * HEAD_PAD
    qrows = lambda w: pl.BlockSpec((DSA_T, w), lambda b, i: (b * nqb + i, 0))
    krows = lambda w: pl.BlockSpec((SEQ, w), lambda b, i: (b, 0))
    return pl.pallas_call(
        _dsa_kernel,
        out_shape=jax.ShapeDtypeStruct((N_TOK, DSA_W), BF16),
        grid=(BATCH, nqb),
        in_specs=[pl.BlockSpec(memory_space=pltpu.SMEM),
                  _const_spec((2, DSA_HEADS, DSA_T, DSA_T)), _const_spec((DSA_HEADS, LANES)),
                  qrows(IDX_Q_W), qrows(DSA_W), qrows(LANES),
                  krows(2 * LANES), krows(padded),
                  pl.BlockSpec((nqb, DVT_ROWS, DSA_T), lambda b, i: (b, 0, 0))],
        out_specs=qrows(DSA_W),
        scratch_shapes=[pltpu.VMEM((nqb, DSA_T, DSA_T), F32),
                        pltpu.VMEM((DSA_HEADS, nqb, DSA_T, DSA_T), F32),
                        pltpu.VMEM((DVT_ROWS, DSA_T), F32)],
        compiler_params=_cparams(("parallel", "arbitrary")),
        name="dsa_attention",
    )(t5_table, bias, bias_max, iq, dq, ikw, ikb, dk, dvt)


def _merge_kernel(h_ref, oret_ref, odsa_ref, omem_ref, wg_ref, wr_ref, wd_ref, wm_ref, wo_ref,
                  g_ref, b_ref, o_ref):
    h = h_ref[...]
    hb = h.astype(BF16)
    merged = None
    for i, (src, w) in enumerate(((oret_ref, wr_ref), (odsa_ref, wd_ref), (omem_ref, wm_ref))):
        gate = jax.nn.sigmoid(_dot(hb, wg_ref[:, i * D_MODEL:(i + 1) * D_MODEL]))
        term = gate * _dot(src[...], w[...])
        merged = term if merged is None else merged + term
    mix = _dot(merged.astype(BF16), wo_ref[...])
    o_ref[...] = _layer_norm(ALPHA * h + mix, g_ref[...], b_ref[...])


def _merge(h, o_ret, o_dsa, o_mem, wg, wr, wd, wm, wo, g, b):
    rows = lambda w: pl.BlockSpec((MERGE_ROWS, w), lambda i: (i, 0))
    return pl.pallas_call(
        _merge_kernel,
        out_shape=jax.ShapeDtypeStruct((N_TOK, D_MODEL), F32),
        grid=(N_TOK // MERGE_ROWS,),
        in_specs=[rows(D_MODEL), rows(RET_V_W), rows(DSA_W), rows(MEM_W),
                  _const_spec((D_MODEL, 3 * D_MODEL)),
                  _const_spec((RET_V_W, D_MODEL)), _const_spec((DSA_W, D_MODEL)),
                  _const_spec((MEM_W, D_MODEL)), _const_spec((D_MODEL, D_MODEL)),
                  _const_spec((1, D_MODEL)), _const_spec((1, D_MODEL))],
        out_specs=rows(D_MODEL),
        compiler_params=_cparams(("parallel",)),
        name="merge_ln2",
    )(h, o_ret, o_dsa, o_mem, wg, wr, wd, wm, wo, g, b)


def _rope_tables():
    half = RET_DK // 2
    freqs = ROPE_BASE ** (-jnp.arange(half, dtype=F32) / half)
    ang = jnp.arange(SEQ).astype(F32)[:, None] * freqs[None, :]
    cos, sin = jnp.cos(ang), jnp.sin(ang)
    cos_t = jnp.tile(jnp.concatenate([cos, cos], axis=-1), (1, RET_HEADS))
    sin_t = jnp.tile(jnp.concatenate([-sin, sin], axis=-1), (1, RET_HEADS))
    return cos_t, sin_t


def _decay_tables():
    c = RET_CHUNK
    gamma = 1.0 - 2.0 ** (-5.0 - jnp.arange(RET_HEADS, dtype=F32))
    lg = jnp.log(gamma)
    i = jnp.arange(c)
    diff = i[:, None] - i[None, :]
    decay_in = jnp.where(diff[None] >= 0, jnp.exp(jnp.maximum(diff, 0)[None] * lg[:, None, None]), 0.0).astype(F32)
    k_dec = jnp.exp((c - 1 - i)[None, :] * lg[:, None]).astype(F32)
    q_dec = jnp.exp((i + 1)[None, :] * lg[:, None]).astype(F32)
    chunk_dec = jnp.exp(c * lg).astype(F32)
    kdec_t = jnp.repeat(k_dec.T, RET_DK, axis=1)
    qdec_t = jnp.repeat(q_dec.T, RET_DV, axis=1)
    cdec_t = jnp.broadcast_to(chunk_dec[:, None, None], (RET_HEADS, RET_DK, RET_DV))
    return decay_in, kdec_t, qdec_t, cdec_t


def _layer(x, mem2d, ffn1_w_in, ffn1_w_out, ln1_g, ln1_b, w_in, t5_table, ret_gn_g, ret_gn_b,
           w_mem_kv, w_br_ret, w_br_dsa, w_br_mem, w_out, ln2_g, ln2_b,
           ffn2_w_in, ffn2_w_out, ln3_g, ln3_b, tables):
    cos_t, sin_t, din, kdec, qdec, cdec = tables
    row = lambda v: v.reshape(1, -1)
    bf = lambda w: w.astype(BF16)

    w_in_bf = bf(w_in)
    w_mq = w_in_bf[:, W_MQ0:W_G0]
    w_dvt = w_in_bf[:, W_DV0:W_IQ0].T
    kdec_rows = jnp.tile(kdec, (PROJ_ROWS // RET_CHUNK, 1))

    h = _ffn_ln(x, bf(ffn1_w_in), bf(ffn1_w_out), row(ln1_g), row(ln1_b), "ffn1_ln1")
    rq, rk, rkdt, rv, rg, dq, dk, dvt, iq, mq, ikw, ikb = _proj(h, w_in_bf, w_mq, w_dvt, cos_t, sin_t, kdec_rows)
    o_ret = _retention(rq, rk, rkdt, rv, rg, din, qdec, cdec, row(ret_gn_g), row(ret_gn_b))
    o_mem = _mematt(mq, _memkv(mem2d, bf(w_mem_kv)))
    o_dsa = _dsa(t5_table, *_t5_bias(t5_table), iq, dq, ikw, ikb, dk, dvt)
    x2 = _merge(h, o_ret, o_dsa, o_mem, w_in_bf[:, W_G0:], bf(w_br_ret), bf(w_br_dsa), bf(w_br_mem),
                bf(w_out), row(ln2_g), row(ln2_b))
    return _ffn_ln(x2, bf(ffn2_w_in), bf(ffn2_w_out), row(ln3_g), row(ln3_b), "ffn2_ln3")


def kernel(x, mem, ffn1_w_in, ffn1_w_out, ln1_g, ln1_b, w_in, t5_table, ret_gn_g, ret_gn_b,
           w_mem_kv, w_br_ret, w_br_dsa, w_br_mem, w_out, ln2_g, ln2_b,
           ffn2_w_in, ffn2_w_out, ln3_g, ln3_b):
    assert x.shape == (BATCH, SEQ, D_MODEL) and mem.shape == (BATCH, MEM_TOKENS, D_MODEL)
    tables = _rope_tables() + _decay_tables()
    y = x.reshape(N_TOK, D_MODEL)
    mem2d = mem.reshape(BATCH * MEM_TOKENS, D_MODEL)
    for l in range(DEPTH):
        y = _layer(y, mem2d, ffn1_w_in[l], ffn1_w_out[l], ln1_g[l], ln1_b[l], w_in[l], t5_table,
                   ret_gn_g[l], ret_gn_b[l], w_mem_kv[l], w_br_ret[l], w_br_dsa[l], w_br_mem[l],
                   w_out[l], ln2_g[l], ln2_b[l], ffn2_w_in[l], ffn2_w_out[l], ln3_g[l], ln3_b[l],
                   tables)
    return y.reshape(BATCH, SEQ, D_MODEL)
```

```python
import math

import jax
import jax.numpy as jnp
from jax import lax
from jax.experimental import pallas as pl
from jax.experimental.pallas import tpu as pltpu

F32 = jnp.float32
BF16 = jnp.bfloat16

D_MODEL = 1024
BATCH = 8
SEQ = 2048
MEM_TOKENS = 256
RET_HEADS, RET_DK, RET_DV, RET_CHUNK = 4, 64, 128, 128
DSA_HEADS, DSA_DH = 8, 64
IDX_HEADS, IDX_DIM = 8, 64
TOPK = min(256, SEQ // 4)
MEM_HEADS, MEM_DH = 4, 128
T5_BUCKETS, T5_MAX_DIST = 32, 128
D_FF = 2816
ROPE_BASE = 10000.0
LN_EPS = 1e-5
NEG_INF = -1e30
DEPTH = 1
ALPHA = (2.0 * DEPTH) ** 0.25

RET_QK_W = RET_HEADS * RET_DK
RET_V_W = RET_HEADS * RET_DV
DSA_W = DSA_HEADS * DSA_DH
IDX_Q_W = IDX_HEADS * IDX_DIM
MEM_W = MEM_HEADS * MEM_DH
N_TOK = BATCH * SEQ

V7X_VMEM_BYTES = 64 * 1024 * 1024
VMEM_LIMIT = V7X_VMEM_BYTES - 8 * 1024 * 1024
LANES = 128
SUBLANES = 8

FFN_ROWS = 1024
MXU_TILE = 256
FFN_CHUNK_EDGES = (0, 6 * MXU_TILE, D_FF)
PROJ_ROWS = 512
MERGE_ROWS = 1024
RET_ROWS = 512
MEMKV_ROWS = 512
MEMATT_ROWS = 512
DSA_T = 256
HEAD_PAD = LANES
DVT_HEAD_ROWS = DSA_DH + 16
DVT_ROWS = DSA_HEADS * DVT_HEAD_ROWS
BISECT_PLAIN_STEPS = 16
COUNT_ACCUMULATORS = 4
LOG2E = math.log2(math.e)
DEN_FLOOR = 2.0 ** -100

W_RQ0 = 0
W_RK0 = W_RQ0 + RET_QK_W
W_RV0 = W_RK0 + RET_QK_W
W_RG0 = W_RV0 + RET_V_W
W_DQ0 = W_RG0 + RET_V_W
W_DK0 = W_DQ0 + DSA_W
W_DV0 = W_DK0 + DSA_W
W_IQ0 = W_DV0 + DSA_W
W_IK0 = W_IQ0 + IDX_Q_W
W_IW0 = W_IK0 + IDX_DIM
W_MQ0 = W_IW0 + IDX_HEADS
W_G0 = W_MQ0 + MEM_W
IW_SCALE = IDX_HEADS ** -0.5 * IDX_DIM ** -0.5


def _cparams(sem):
    return pltpu.CompilerParams(dimension_semantics=sem, vmem_limit_bytes=VMEM_LIMIT)


def _const_spec(shape):
    nd = len(shape)
    return pl.BlockSpec(shape, lambda *_: (0,) * nd, pipeline_mode=pl.Buffered(1))


def _layer_norm(y, g, b):
    mu = jnp.mean(y, axis=-1, keepdims=True)
    yc = y - mu
    var = jnp.mean(yc * yc, axis=-1, keepdims=True)
    return yc * lax.rsqrt(var + LN_EPS) * g + b


def _sigmoid(x):
    return 0.5 * jnp.tanh(0.5 * x) + 0.5


def _dot(a, b):
    return jnp.dot(a, b, preferred_element_type=F32)


def _dot_nt(a, b):
    return lax.dot_general(a, b, (((1,), (1,)), ((), ())), preferred_element_type=F32)


def _ffn_ln_kernel(x_ref, wi_ref, wo_ref, g_ref, b_ref, o_ref):
    x = x_ref[...]
    xb = x.astype(BF16)
    acc = None
    for lo, hi in zip(FFN_CHUNK_EDGES[:-1], FFN_CHUNK_EDGES[1:]):
        a = _dot(xb, wi_ref[:, lo:hi])
        u = _dot(xb, wi_ref[:, D_FF + lo:D_FF + hi])
        act = (a * _sigmoid(a) * u).astype(BF16)
        part = _dot(act, wo_ref[lo:hi, :])
        acc = part if acc is None else acc + part
    o_ref[...] = _layer_norm(ALPHA * x + 0.5 * acc, g_ref[...], b_ref[...])


def _ffn_ln(x, w_in_bf, w_out_bf, g, b, name):
    return pl.pallas_call(
        _ffn_ln_kernel,
        out_shape=jax.ShapeDtypeStruct((N_TOK, D_MODEL), F32),
        grid=(N_TOK // FFN_ROWS,),
        in_specs=[
            pl.BlockSpec((FFN_ROWS, D_MODEL), lambda i: (i, 0)),
            _const_spec((D_MODEL, 2 * D_FF)),
            _const_spec((D_FF, D_MODEL)),
            _const_spec((1, D_MODEL)),
            _const_spec((1, D_MODEL)),
        ],
        out_specs=pl.BlockSpec((FFN_ROWS, D_MODEL), lambda i: (i, 0)),
        compiler_params=_cparams(("parallel",)),
        name=name,
    )(x, w_in_bf, w_out_bf, g, b)


def _rope(x, cos, sin_signed):
    width = x.shape[-1]
    lane = lax.broadcasted_iota(jnp.int32, x.shape, 1)
    first_half = (lane % RET_DK) < (RET_DK // 2)
    swapped = jnp.where(first_half,
                        pltpu.roll(x, width - RET_DK // 2, 1),
                        pltpu.roll(x, RET_DK // 2, 1))
    return x * cos + swapped * sin_signed


def _store_split_heads(dst_ref, val):
    low = lax.broadcasted_iota(jnp.int32, (val.shape[0], LANES), 1) < DSA_DH
    for j in range(val.shape[1] // LANES):
        pair = val[:, j * LANES:(j + 1) * LANES]
        dst_ref[:, (2 * j) * LANES:(2 * j + 1) * LANES] = jnp.where(low, pair, 0.0).astype(dst_ref.dtype)
        dst_ref[:, (2 * j + 1) * LANES:(2 * j + 2) * LANES] = jnp.where(low, 0.0, pair).astype(dst_ref.dtype)


def _store_low_heads(dst_ref, val):
    low = lax.broadcasted_iota(jnp.int32, (val.shape[0], LANES), 1) < RET_DK
    for j in range(val.shape[1] // LANES):
        pair = val[:, j * LANES:(j + 1) * LANES]
        dst_ref[:, (2 * j) * LANES:(2 * j + 1) * LANES] = jnp.where(low, pair, 0.0).astype(dst_ref.dtype)
        dst_ref[:, (2 * j + 1) * LANES:(2 * j + 2) * LANES] = jnp.where(
            low, pltpu.roll(pair, RET_DK, 1), 0.0).astype(dst_ref.dtype)


def _proj_kernel(h_ref, w_ref, wikw_ref, wmq_ref, wdvt_ref, cos_ref, sin_ref, kdec_ref,
                 rq_ref, rk_ref, rkdt_ref, rv_ref, rg_ref, dq_ref, dk_ref, dvt_ref, iq_ref, mq_ref,
                 ikw_ref, ikb_ref):
    hb = h_ref[...].astype(BF16)

    def proj(lo, width):
        return _dot(hb, w_ref[:, lo:lo + width])

    cos = cos_ref[...]
    sin = sin_ref[...]
    _store_low_heads(rq_ref, _rope(proj(W_RQ0, RET_QK_W), cos, sin))
    rk = _rope(proj(W_RK0, RET_QK_W), cos, sin) * (RET_DK ** -0.5)
    _store_low_heads(rk_ref, rk)
    rkdt_ref[...] = (rk * kdec_ref[...]).T.astype(BF16)
    rv_ref[...] = proj(W_RV0, RET_V_W).astype(BF16)
    rg_ref[...] = proj(W_RG0, RET_V_W)
    dq_ref[...] = (proj(W_DQ0, DSA_W) * (DSA_DH ** -0.5 * LOG2E)).astype(BF16)
    _store_split_heads(dk_ref, proj(W_DK0, DSA_W))
    iq_ref[...] = proj(W_IQ0, IDX_Q_W).astype(BF16)
    mq_ref[...] = _dot(hb, wmq_ref[...]).astype(BF16)
    ikw = _dot(hb, wikw_ref[...])
    ikw_ref[...] = ikw
    low = lax.broadcasted_iota(jnp.int32, ikw.shape, 1) < IDX_DIM
    ik_low = jnp.where(low, ikw, 0.0)
    ikb_ref[:, 0:LANES] = ik_low.astype(BF16)
    ikb_ref[:, LANES:2 * LANES] = pltpu.roll(ik_low, IDX_DIM, 1).astype(BF16)
    vt = _dot_nt(wdvt_ref[...], hb).astype(BF16)
    ones = jnp.ones((DVT_HEAD_ROWS - DSA_DH, DSA_T), BF16)
    for j in range(PROJ_ROWS // DSA_T):
        for h in range(DSA_HEADS):
            r0 = h * DVT_HEAD_ROWS
            dvt_ref[j, r0:r0 + DSA_DH, :] = vt[h * DSA_DH:(h + 1) * DSA_DH, j * DSA_T:(j + 1) * DSA_T]
            dvt_ref[j, r0 + DSA_DH:r0 + DVT_HEAD_ROWS, :] = ones


def _proj(h, w_in_bf, w_mq_bf, w_dvt_bf, cos_t, sin_t, kdec_rows):
    rows = lambda w: pl.BlockSpec((PROJ_ROWS, w), lambda i: (i, 0))
    seq_tiles = SEQ // PROJ_ROWS
    pos = lambda w: pl.BlockSpec((PROJ_ROWS, w), lambda i: (i % seq_tiles, 0))
    sd = jax.ShapeDtypeStruct
    ret_padded = RET_HEADS * HEAD_PAD
    dsa_padded = DSA_HEADS * HEAD_PAD
    slabs = PROJ_ROWS // DSA_T
    assert W_IK0 % LANES == 0
    return pl.pallas_call(
        _proj_kernel,
        out_shape=(
            sd((N_TOK, ret_padded), BF16), sd((N_TOK, ret_padded), BF16), sd((RET_QK_W, N_TOK), BF16),
            sd((N_TOK, RET_V_W), BF16), sd((N_TOK, RET_V_W), F32),
            sd((N_TOK, DSA_W), BF16), sd((N_TOK, dsa_padded), BF16),
            sd((N_TOK // DSA_T, DVT_ROWS, DSA_T), BF16),
            sd((N_TOK, IDX_Q_W), BF16), sd((N_TOK, MEM_W), BF16),
            sd((N_TOK, LANES), F32), sd((N_TOK, 2 * LANES), BF16),
        ),
        grid=(N_TOK // PROJ_ROWS,),
        in_specs=[rows(D_MODEL),
                  pl.BlockSpec((D_MODEL, W_IK0), lambda i: (0, 0), pipeline_mode=pl.Buffered(1)),
                  pl.BlockSpec((D_MODEL, LANES), lambda i: (0, W_IK0 // LANES), pipeline_mode=pl.Buffered(1)),
                  _const_spec((D_MODEL, MEM_W)), _const_spec((DSA_W, D_MODEL)),
                  pos(RET_QK_W), pos(RET_QK_W), _const_spec((PROJ_ROWS, RET_QK_W))],
        out_specs=(rows(ret_padded), rows(ret_padded),
                   pl.BlockSpec((RET_QK_W, PROJ_ROWS), lambda i: (0, i)),
                   rows(RET_V_W), rows(RET_V_W),
                   rows(DSA_W), rows(dsa_padded),
                   pl.BlockSpec((slabs, DVT_ROWS, DSA_T), lambda i: (i, 0, 0)),
                   rows(IDX_Q_W), rows(MEM_W), rows(LANES), rows(2 * LANES)),
        compiler_params=_cparams(("parallel",)),
        name="mixer_proj",
    )(h, w_in_bf, w_in_bf, w_mq_bf, w_dvt_bf, cos_t, sin_t, kdec_rows)


def _retention_kernel(q_ref, k_ref, kdt_ref, v_ref, g_ref, din_ref, qdec_ref, cdec_ref,
                      gng_ref, gnb_ref, o_ref, state_ref):
    @pl.when(pl.program_id(1) == 0)
    def _():
        state_ref[...] = jnp.zeros_like(state_ref)

    qdec = qdec_ref[...]
    gn_g = gng_ref[...]
    gn_b = gnb_ref[...]
    states = [state_ref[h] for h in range(RET_HEADS)]
    pad_rows = jnp.zeros((HEAD_PAD - RET_DK, RET_DV), BF16)
    for j in range(RET_ROWS // RET_CHUNK):
        rows = slice(j * RET_CHUNK, (j + 1) * RET_CHUNK)
        gate = g_ref[rows, :]
        for h in range(RET_HEADS):
            hs = slice(h * HEAD_PAD, (h + 1) * HEAD_PAD)
            vs = slice(h * RET_DV, (h + 1) * RET_DV)
            qh = q_ref[rows, hs]
            vh = v_ref[rows, vs]
            scores = _dot_nt(qh, k_ref[rows, hs]) * din_ref[h]
            intra = _dot(scores.astype(BF16), vh)
            state_padded = jnp.concatenate([states[h].astype(BF16), pad_rows], axis=0)
            cross = _dot(qh, state_padded) * qdec[:, vs]
            kv = _dot(kdt_ref[h * RET_DK:(h + 1) * RET_DK, rows], vh)
            states[h] = cdec_ref[h] * states[h] + kv
            o = intra + cross
            mu = jnp.mean(o, axis=-1, keepdims=True)
            oc = o - mu
            var = jnp.mean(oc * oc, axis=-1, keepdims=True)
            o = oc * lax.rsqrt(var + LN_EPS) * gn_g[:, vs] + gn_b[:, vs]
            o_ref[rows, vs] = (jax.nn.silu(gate[:, vs]) * o).astype(BF16)
    for h in range(RET_HEADS):
        state_ref[h] = states[h]


def _retention(rq, rk, rkdt, rv, rg, din, qdec, cdec, gn_g, gn_b):
    nc = SEQ // RET_ROWS
    rows = lambda w: pl.BlockSpec((RET_ROWS, w), lambda b, n: (b * nc + n, 0))
    padded = RET_HEADS * HEAD_PAD
    return pl.pallas_call(
        _retention_kernel,
        out_shape=jax.ShapeDtypeStruct((N_TOK, RET_V_W), BF16),
        grid=(BATCH, nc),
        in_specs=[rows(padded), rows(padded),
                  pl.BlockSpec((RET_QK_W, RET_ROWS), lambda b, n: (0, b * nc + n)),
                  rows(RET_V_W), rows(RET_V_W),
                  _const_spec((RET_HEADS, RET_CHUNK, RET_CHUNK)),
                  _const_spec((RET_CHUNK, RET_V_W)),
                  _const_spec((RET_HEADS, RET_DK, RET_DV)),
                  _const_spec((1, RET_V_W)), _const_spec((1, RET_V_W))],
        out_specs=rows(RET_V_W),
        scratch_shapes=[pltpu.VMEM((RET_HEADS, RET_DK, RET_DV), F32)],
        compiler_params=_cparams(("parallel", "arbitrary")),
        name="retention",
    )(rq, rk, rkdt, rv, rg, din, qdec, cdec, gn_g, gn_b)


def _memkv_kernel(m_ref, w_ref, o_ref):
    o_ref[...] = _dot(m_ref[...].astype(BF16), w_ref[...]).astype(BF16)


def _memkv(mem2d, w_bf):
    n = mem2d.shape[0]
    return pl.pallas_call(
        _memkv_kernel,
        out_shape=jax.ShapeDtypeStruct((n, 2 * MEM_W), BF16),
        grid=(n // MEMKV_ROWS,),
        in_specs=[pl.BlockSpec((MEMKV_ROWS, D_MODEL), lambda i: (i, 0)),
                  _const_spec((D_MODEL, 2 * MEM_W))],
        out_specs=pl.BlockSpec((MEMKV_ROWS, 2 * MEM_W), lambda i: (i, 0)),
        compiler_params=_cparams(("parallel",)),
        name="mem_kv",
    )(mem2d, w_bf)


def _mematt_kernel(q_ref, k_ref, v_ref, o_ref):
    q = q_ref[...]
    k = k_ref[...]
    v = v_ref[...]
    for h in range(MEM_HEADS):
        hs = slice(h * MEM_DH, (h + 1) * MEM_DH)
        logits = _dot_nt(q[:, hs], k[:, hs]) * (MEM_DH ** -0.5)
        m = jnp.max(logits, axis=-1, keepdims=True)
        p = jnp.exp(logits - m)
        denom = jnp.sum(p, axis=-1, keepdims=True)
        o_ref[:, hs] = (_dot(p.astype(BF16), v[:, hs]) / denom).astype(BF16)


def _mematt(mq, mkv):
    tiles = SEQ // MEMATT_ROWS
    return pl.pallas_call(
        _mematt_kernel,
        out_shape=jax.ShapeDtypeStruct((N_TOK, MEM_W), BF16),
        grid=(BATCH, tiles),
        in_specs=[pl.BlockSpec((MEMATT_ROWS, MEM_W), lambda b, i: (b * tiles + i, 0)),
                  pl.BlockSpec((MEM_TOKENS, MEM_W), lambda b, i: (b, 0)),
                  pl.BlockSpec((MEM_TOKENS, MEM_W), lambda b, i: (b, 1))],
        out_specs=pl.BlockSpec((MEMATT_ROWS, MEM_W), lambda b, i: (b * tiles + i, 0)),
        compiler_params=_cparams(("parallel", "parallel")),
        name="mem_attention",
    )(mq, mkv, mkv)


def _t5_bias_kernel(table_ref, o_ref, bmax_ref):
    j = lax.broadcasted_iota(jnp.int32, (DSA_T, DSA_T), 0)
    r = lax.broadcasted_iota(jnp.int32, (DSA_T, DSA_T), 1)
    max_exact = T5_BUCKETS // 2
    head_max = [None] * DSA_HEADS
    for w in range(2):
        n = jnp.maximum(r + (1 - w) * DSA_T - j, 0)
        nf = jnp.maximum(n, 1).astype(F32)
        large = max_exact + (jnp.log(nf / max_exact) / math.log(T5_MAX_DIST / max_exact)
                             * (T5_BUCKETS - max_exact)).astype(jnp.int32)
        large = jnp.minimum(large, T5_BUCKETS - 1)
        bucket = jnp.where(n < max_exact, n, large)
        for h in range(DSA_HEADS):
            acc = jnp.zeros((DSA_T, DSA_T), F32)
            for b in range(T5_BUCKETS):
                acc = jnp.where(bucket == b, table_ref[b, h] * LOG2E, acc)
            o_ref[w, h] = acc
            top = jnp.max(jnp.max(acc, axis=0, keepdims=True), axis=1, keepdims=True)
            head_max[h] = top if head_max[h] is None else jnp.maximum(head_max[h], top)
    for h in range(DSA_HEADS):
        bmax_ref[h:h + 1, :] = jnp.broadcast_to(head_max[h], (1, LANES))


def _t5_bias(t5_table):
    return pl.pallas_call(
        _t5_bias_kernel,
        out_shape=(jax.ShapeDtypeStruct((2, DSA_HEADS, DSA_T, DSA_T), F32),
                   jax.ShapeDtypeStruct((DSA_HEADS, LANES), F32)),
        in_specs=[pl.BlockSpec(memory_space=pltpu.SMEM)],
        out_specs=(pl.BlockSpec(memory_space=pltpu.VMEM), pl.BlockSpec(memory_space=pltpu.VMEM)),
        compiler_params=pltpu.CompilerParams(vmem_limit_bytes=VMEM_LIMIT),
        name="t5_bias",
    )(t5_table)


def _fold_groups(x, op):
    return op(x.reshape(DSA_T // SUBLANES, SUBLANES, DSA_T), axis=0)


def _dsa_kernel(table_ref, bias_ref, bmax_ref, iq_ref, dq_ref, ikw_q_ref, ikb_ref, dk_ref, dvt_ref,
                o_ref, score_ref, logit_ref, acc_ref):
    qb = pl.program_id(1)
    q0 = qb * DSA_T
    n_chunks = qb + 1
    groups = DSA_T // SUBLANES
    key_in_chunk = lax.broadcasted_iota(jnp.int32, (DSA_T, DSA_T), 0)
    query_in_step = lax.broadcasted_iota(jnp.int32, (DSA_T, DSA_T), 1)

    def key_rows(c):
        return pl.ds(pl.multiple_of(c * DSA_T, DSA_T), DSA_T)

    def pair_cols(j):
        return slice(j * LANES, (j + 1) * LANES)

    def pair_products(keys_even, keys_odd, queries):
        prod = _dot_nt(jnp.concatenate([keys_even, keys_odd], axis=0), queries)
        return prod[:DSA_T], prod[DSA_T:]

    iw_t = ikw_q_ref[...].T[IDX_DIM:IDX_DIM + IDX_HEADS, :] * IW_SCALE

    def scores_and_logits(c, carry, causal):
        raw_max, s_min, s_max = carry
        ik_low = ikb_ref[key_rows(c), 0:LANES]
        ik_high = ikb_ref[key_rows(c), LANES:2 * LANES]
        last = IDX_HEADS // 2 - 1
        for j in range(IDX_HEADS // 2):
            rel_even, rel_odd = pair_products(ik_low, ik_high, iq_ref[:, pair_cols(j)])
            term = (jnp.maximum(rel_even, 0.0) * iw_t[2 * j:2 * j + 1, :]
                    + jnp.maximum(rel_odd, 0.0) * iw_t[2 * j + 1:2 * j + 2, :])
            total = term if j == 0 else score_ref[c] + term
            if j < last:
                score_ref[c] = total
            elif causal is None:
                score_ref[c] = total
                s_min = jnp.minimum(s_min, _fold_groups(total, jnp.min))
                s_max = jnp.maximum(s_max, _fold_groups(total, jnp.max))
            else:
                score_ref[c] = jnp.where(causal, total, NEG_INF)
                s_min = jnp.minimum(s_min, _fold_groups(jnp.where(causal, total, -NEG_INF), jnp.min))
                s_max = jnp.maximum(s_max, _fold_groups(jnp.where(causal, total, NEG_INF), jnp.max))
        out = list(raw_max)
        for j in range(DSA_HEADS // 2):
            keys = [dk_ref[key_rows(c), (2 * j + i) * HEAD_PAD:(2 * j + i + 1) * HEAD_PAD] for i in range(2)]
            for h, qk in zip((2 * j, 2 * j + 1), pair_products(keys[0], keys[1], dq_ref[:, pair_cols(j)])):
                logit_ref[h, c] = qk
                out[h] = jnp.maximum(out[h], _fold_groups(qk, jnp.max))
        return tuple(out), s_min, s_max

    def paired_loop(start, stop, body, carry):
        n = stop - start
        carry = lax.fori_loop(
            0, lax.shift_right_logical(n, 1),
            lambda i, cr: body(start + 2 * i + 1, body(start + 2 * i, cr)), carry)
        return lax.cond(jnp.bitwise_and(n, 1) == 1, lambda cr: body(stop - 1, cr), lambda cr: cr, carry)

    low_init = jnp.full((SUBLANES, DSA_T), NEG_INF, F32)
    carry = (tuple(low_init for _ in range(DSA_HEADS)), -low_init, low_init)
    carry = paired_loop(0, qb, lambda c, cr: scores_and_logits(c, cr, None), carry)
    raw_max, s_min, s_max = scores_and_logits(qb, carry, key_in_chunk <= query_in_step)

    def count_where(pred):
        def body(c, acc):
            parts = [acc] + [jnp.zeros((SUBLANES, DSA_T), F32)] * (COUNT_ACCUMULATORS - 1)
            for g in range(groups):
                blk = score_ref[c, g * SUBLANES:(g + 1) * SUBLANES, :]
                a = g % COUNT_ACCUMULATORS
                parts[a] = parts[a] + jnp.where(pred(c, g * SUBLANES, blk), 1.0, 0.0)
            while len(parts) > 1:
                parts = [parts[i] + parts[i + 1] for i in range(0, len(parts), 2)]
            return parts[0]
        acc = lax.fori_loop(0, n_chunks, body, jnp.zeros((SUBLANES, DSA_T), F32))
        return jnp.sum(acc, axis=0, keepdims=True)

    def count_ge(thr):
        thr_b = jnp.broadcast_to(thr, (SUBLANES, DSA_T))
        return count_where(lambda c, r0, blk: blk >= thr_b)

    q_min = jnp.min(s_min, axis=0, keepdims=True)
    q_max = jnp.max(s_max, axis=0, keepdims=True)
    t_q = q0 + lax.broadcasted_iota(jnp.int32, (1, DSA_T), 1)

    kf = float(TOPK)
    init = (q_min, q_max + jnp.maximum(jnp.abs(q_max), 1.0) * 1e-6,
            (t_q + 1).astype(F32), jnp.zeros((1, DSA_T), F32))

    def bisect_step(_i, carry):
        lo, hi, cnt_lo, cnt_hi = carry
        mid = 0.5 * (lo + hi)
        cnt = count_ge(mid)
        ge = cnt >= kf
        return (jnp.where(ge, mid, lo), jnp.where(ge, hi, mid),
                jnp.where(ge, cnt, cnt_lo), jnp.where(ge, cnt_hi, cnt))

    lo, hi, cnt_lo, cnt_hi = lax.fori_loop(0, BISECT_PLAIN_STEPS, bisect_step, init)

    big = -NEG_INF

    def snap_body(c, carry):
        mn, mx = carry
        blk = score_ref[c]
        mn = jnp.minimum(mn, _fold_groups(jnp.where(blk >= lo, blk, big), jnp.min))
        mx = jnp.maximum(mx, _fold_groups(jnp.where(blk < hi, blk, -big), jnp.max))
        return mn, mx

    mn, mx = lax.fori_loop(
        0, n_chunks, snap_body,
        (jnp.full((SUBLANES, DSA_T), big, F32), jnp.full((SUBLANES, DSA_T), -big, F32)))
    lo = jnp.min(mn, axis=0, keepdims=True)
    top = jnp.max(mx, axis=0, keepdims=True)

    def split_pair(carry):
        lo, top, cnt_lo, cnt_hi = carry
        pair = (cnt_lo > kf) & (lo < top) & (cnt_lo - cnt_hi == 2.0)
        return jnp.where(pair, top, lo), top, jnp.where(pair, cnt_hi + 1.0, cnt_lo), cnt_hi

    def pending_of(carry):
        lo, top, cnt_lo, _ = carry
        return jnp.max(jnp.where((cnt_lo > kf) & (lo < top), 1.0, 0.0))

    def value_step(state):
        (lo, top, cnt_lo, cnt_hi), _ = state
        unresolved = (cnt_lo > kf) & (lo < top)
        mid = lo + (top - lo) * 0.5
        mid = jnp.minimum(jnp.where(mid > lo, mid, top), top)

        def body(c, carry):
            cnt, mn, mx = carry
            blk = score_ref[c]
            is_ge = blk >= mid
            cnt = cnt + _fold_groups(jnp.where(is_ge, 1.0, 0.0), jnp.sum)
            mn = jnp.minimum(mn, _fold_groups(jnp.where(is_ge, blk, big), jnp.min))
            mx = jnp.maximum(mx, _fold_groups(jnp.where(is_ge, -big, blk), jnp.max))
            return cnt, mn, mx

        cnt, mn, mx = lax.fori_loop(
            0, n_chunks, body,
            (jnp.zeros((SUBLANES, DSA_T), F32), jnp.full((SUBLANES, DSA_T), big, F32),
             jnp.full((SUBLANES, DSA_T), -big, F32)))
        cnt = jnp.sum(cnt, axis=0, keepdims=True)
        ge = cnt >= kf
        raise_lo = unresolved & ge
        lower_top = unresolved & jnp.logical_not(ge)
        carry = split_pair((jnp.where(raise_lo, jnp.min(mn, axis=0, keepdims=True), lo),
                            jnp.where(lower_top, jnp.max(mx, axis=0, keepdims=True), top),
                            jnp.where(raise_lo, cnt, cnt_lo), jnp.where(lower_top, cnt, cnt_hi)))
        return carry, pending_of(carry)

    start = split_pair((lo, top, cnt_lo, cnt_hi))
    (lo, top, cnt_lo, cnt_hi), _ = lax.while_loop(
        lambda state: state[1] > 0.5, value_step, (start, pending_of(start)))

    tied = cnt_lo > kf
    need = kf - cnt_hi
    lo_b = jnp.broadcast_to(lo, (SUBLANES, DSA_T))
    key_in_group = lax.broadcasted_iota(jnp.int32, (SUBLANES, DSA_T), 0)

    def tie_search(_):
        def body(_i, carry):
            lo_i, hi_i = carry
            mid_i = lax.shift_right_arithmetic(lo_i + hi_i, 1)
            mid_b = jnp.broadcast_to(mid_i, (SUBLANES, DSA_T))
            cnt = count_where(lambda c, r0, blk: (blk == lo_b) & (key_in_group + (c * DSA_T + r0) <= mid_b))
            enough = cnt >= need
            return jnp.where(enough, lo_i, mid_i), jnp.where(enough, mid_i, hi_i)
        lo_i = jnp.full((1, DSA_T), -1, jnp.int32)
        hi_i = jnp.full((1, DSA_T), SEQ - 1, jnp.int32)
        _, cut = lax.fori_loop(0, int(math.log2(SEQ)) + 1, body, (lo_i, hi_i))
        return jnp.where(tied, cut, SEQ)

    cut = lax.cond(jnp.max(jnp.where(tied, 1.0, 0.0)) > 0.5, tie_search,
                   lambda _: jnp.full((1, DSA_T), SEQ, jnp.int32), 0)

    def mask_body(c, carry):
        blk = score_ref[c]
        keep = (blk > lo) | ((blk == lo) & (key_in_chunk + c * DSA_T <= cut))
        score_ref[c] = jnp.where(keep, 0.0, NEG_INF)
        return carry

    lax.fori_loop(0, n_chunks, mask_body, 0)

    n_far = jnp.maximum(qb - 1, 0)

    def masked_logits(c, h, far):
        lg = logit_ref[h, c] + score_ref[c]
        return lg if far else lg + bias_ref[c - qb + 1, h]

    def far_bias(h):
        return table_ref[T5_BUCKETS - 1, h] * LOG2E

    def den_row(h):
        return h * DVT_HEAD_ROWS + DSA_DH

    def exp_and_pv(shifts):
        acc_ref[...] = jnp.zeros_like(acc_ref)

        def body(c, carry, far):
            for h in range(DSA_HEADS):
                shift = shifts[h] - far_bias(h) if far else shifts[h]
                p = jnp.exp2(masked_logits(c, h, far) - shift).astype(BF16)
                rows = slice(h * DVT_HEAD_ROWS, (h + 1) * DVT_HEAD_ROWS)
                acc_ref[rows, :] += _dot(dvt_ref[c, rows, :], p)
            return carry

        paired_loop(0, n_far, lambda c, carry: body(c, carry, True), 0)
        paired_loop(n_far, n_chunks, lambda c, carry: body(c, carry, False), 0)

    upper = [jnp.max(raw_max[h], axis=0, keepdims=True) + bmax_ref[h:h + 1, 0:1] for h in range(DSA_HEADS)]
    exp_and_pv(upper)

    den_min = acc_ref[den_row(0):den_row(0) + 1, :]
    for h in range(1, DSA_HEADS):
        den_min = jnp.minimum(den_min, acc_ref[den_row(h):den_row(h) + 1, :])

    @pl.when(jnp.min(den_min) < DEN_FLOOR)
    def _():
        def max_body(c, ms, far):
            return tuple(jnp.maximum(ms[h], _fold_groups(masked_logits(c, h, far), jnp.max) + (far_bias(h) if far else 0.0))
                         for h in range(DSA_HEADS))
        ms = tuple(jnp.full((SUBLANES, DSA_T), NEG_INF, F32) for _ in range(DSA_HEADS))
        ms = lax.fori_loop(0, n_far, lambda c, ms: max_body(c, ms, True), ms)
        ms = lax.fori_loop(n_far, n_chunks, lambda c, ms: max_body(c, ms, False), ms)
        exp_and_pv([jnp.max(m, axis=0, keepdims=True) for m in ms])

    heads = []
    for h in range(DSA_HEADS):
        r0 = h * DVT_HEAD_ROWS
        heads.append(acc_ref[r0:r0 + DSA_DH, :] / acc_ref[den_row(h):den_row(h) + 1, :])
    o_ref[...] = jnp.concatenate(heads, axis=0).T.astype(BF16)


def _dsa(t5_table, bias, bias_max, iq, dq, ikw, ikb, dk, dvt):
    nqb = SEQ // DSA_T
    padded = DSA_HEADS * HEAD_PAD
    qrows = lambda w: pl.BlockSpec((DSA_T, w), lambda b, i: (b * nqb + i, 0))
    krows = lambda w: pl.BlockSpec((SEQ, w), lambda b, i: (b, 0))
    return pl.pallas_call(
        _dsa_kernel,
        out_shape=jax.ShapeDtypeStruct((N_TOK, DSA_W), BF16),
        grid=(BATCH, nqb),
        in_specs=[pl.BlockSpec(memory_space=pltpu.SMEM),
                  _const_spec((2, DSA_HEADS, DSA_T, DSA_T)), _const_spec((DSA_HEADS, LANES)),
                  qrows(IDX_Q_W), qrows(DSA_W), qrows(LANES),
                  krows(2 * LANES), krows(padded),
                  pl.BlockSpec((nqb, DVT_ROWS, DSA_T), lambda b, i: (b, 0, 0))],
        out_specs=qrows(DSA_W),
        scratch_shapes=[pltpu.VMEM((nqb, DSA_T, DSA_T), F32),
                        pltpu.VMEM((DSA_HEADS, nqb, DSA_T, DSA_T), F32),
                        pltpu.VMEM((DVT_ROWS, DSA_T), F32)],
        compiler_params=_cparams(("parallel", "arbitrary")),
        name="dsa_attention",
    )(t5_table, bias, bias_max, iq, dq, ikw, ikb, dk, dvt)


def _merge_kernel(h_ref, oret_ref, odsa_ref, omem_ref, wg_ref, wr_ref, wd_ref, wm_ref, wo_ref,
                  g_ref, b_ref, o_ref):
    h = h_ref[...]
    hb = h.astype(BF16)
    merged = None
    for i, (src, w) in enumerate(((oret_ref, wr_ref), (odsa_ref, wd_ref), (omem_ref, wm_ref))):
        gate = _sigmoid(_dot(hb, wg_ref[:, i * D_MODEL:(i + 1) * D_MODEL]))
        term = gate * _dot(src[...], w[...])
        merged = term if merged is None else merged + term
    mix = _dot(merged.astype(BF16), wo_ref[...])
    o_ref[...] = _layer_norm(ALPHA * h + mix, g_ref[...], b_ref[...])


def _merge(h, o_ret, o_dsa, o_mem, wg, wr, wd, wm, wo, g, b):
    rows = lambda w: pl.BlockSpec((MERGE_ROWS, w), lambda i: (i, 0))
    return pl.pallas_call(
        _merge_kernel,
        out_shape=jax.ShapeDtypeStruct((N_TOK, D_MODEL), F32),
        grid=(N_TOK // MERGE_ROWS,),
        in_specs=[rows(D_MODEL), rows(RET_V_W), rows(DSA_W), rows(MEM_W),
                  _const_spec((D_MODEL, 3 * D_MODEL)),
                  _const_spec((RET_V_W, D_MODEL)), _const_spec((DSA_W, D_MODEL)),
                  _const_spec((MEM_W, D_MODEL)), _const_spec((D_MODEL, D_MODEL)),
                  _const_spec((1, D_MODEL)), _const_spec((1, D_MODEL))],
        out_specs=rows(D_MODEL),
        compiler_params=_cparams(("parallel",)),
        name="merge_ln2",
    )(h, o_ret, o_dsa, o_mem, wg, wr, wd, wm, wo, g, b)


def _rope_tables():
    half = RET_DK // 2
    freqs = ROPE_BASE ** (-jnp.arange(half, dtype=F32) / half)
    ang = jnp.arange(SEQ).astype(F32)[:, None] * freqs[None, :]
    cos, sin = jnp.cos(ang), jnp.sin(ang)
    cos_t = jnp.tile(jnp.concatenate([cos, cos], axis=-1), (1, RET_HEADS))
    sin_t = jnp.tile(jnp.concatenate([-sin, sin], axis=-1), (1, RET_HEADS))
    return cos_t, sin_t


def _decay_tables():
    c = RET_CHUNK
    gamma = 1.0 - 2.0 ** (-5.0 - jnp.arange(RET_HEADS, dtype=F32))
    lg = jnp.log(gamma)
    i = jnp.arange(c)
    diff = i[:, None] - i[None, :]
    decay_in = jnp.where(diff[None] >= 0, jnp.exp(jnp.maximum(diff, 0)[None] * lg[:, None, None]), 0.0).astype(F32)
    k_dec = jnp.exp((c - 1 - i)[None, :] * lg[:, None]).astype(F32)
    q_dec = jnp.exp((i + 1)[None, :] * lg[:, None]).astype(F32)
    chunk_dec = jnp.exp(c * lg).astype(F32)
    kdec_t = jnp.repeat(k_dec.T, RET_DK, axis=1)
    qdec_t = jnp.repeat(q_dec.T, RET_DV, axis=1)
    cdec_t = jnp.broadcast_to(chunk_dec[:, None, None], (RET_HEADS, RET_DK, RET_DV))
    return decay_in, kdec_t, qdec_t, cdec_t


def _layer(x, mem2d, ffn1_w_in, ffn1_w_out, ln1_g, ln1_b, w_in, t5_table, ret_gn_g, ret_gn_b,
           w_mem_kv, w_br_ret, w_br_dsa, w_br_mem, w_out, ln2_g, ln2_b,
           ffn2_w_in, ffn2_w_out, ln3_g, ln3_b, tables):
    cos_t, sin_t, din, kdec, qdec, cdec = tables
    row = lambda v: v.reshape(1, -1)
    bf = lambda w: w.astype(BF16)

    w_in_bf = bf(w_in)
    w_mq = w_in_bf[:, W_MQ0:W_G0]
    w_dvt = w_in_bf[:, W_DV0:W_IQ0].T
    kdec_rows = jnp.tile(kdec, (PROJ_ROWS // RET_CHUNK, 1))

    h = _ffn_ln(x, bf(ffn1_w_in), bf(ffn1_w_out), row(ln1_g), row(ln1_b), "ffn1_ln1")
    rq, rk, rkdt, rv, rg, dq, dk, dvt, iq, mq, ikw, ikb = _proj(h, w_in_bf, w_mq, w_dvt, cos_t, sin_t, kdec_rows)
    o_ret = _retention(rq, rk, rkdt, rv, rg, din, qdec, cdec, row(ret_gn_g), row(ret_gn_b))
    o_mem = _mematt(mq, _memkv(mem2d, bf(w_mem_kv)))
    o_dsa = _dsa(t5_table, *_t5_bias(t5_table), iq, dq, ikw, ikb, dk, dvt)
    x2 = _merge(h, o_ret, o_dsa, o_mem, w_in_bf[:, W_G0:], bf(w_br_ret), bf(w_br_dsa), bf(w_br_mem),
                bf(w_out), row(ln2_g), row(ln2_b))
    return _ffn_ln(x2, bf(ffn2_w_in), bf(ffn2_w_out), row(ln3_g), row(ln3_b), "ffn2_ln3")


def kernel(x, mem, ffn1_w_in, ffn1_w_out, ln1_g, ln1_b, w_in, t5_table, ret_gn_g, ret_gn_b,
           w_mem_kv, w_br_ret, w_br_dsa, w_br_mem, w_out, ln2_g, ln2_b,
           ffn2_w_in, ffn2_w_out, ln3_g, ln3_b):
    assert x.shape == (BATCH, SEQ, D_MODEL) and mem.shape == (BATCH, MEM_TOKENS, D_MODEL)
    tables = _rope_tables() + _decay_tables()
    y = x.reshape(N_TOK, D_MODEL)
    mem2d = mem.reshape(BATCH * MEM_TOKENS, D_MODEL)
    for l in range(DEPTH):
        y = _layer(y, mem2d, ffn1_w_in[l], ffn1_w_out[l], ln1_g[l], ln1_b[l], w_in[l], t5_table,
                   ret_gn_g[l], ret_gn_b[l], w_mem_kv[l], w_br_ret[l], w_br_dsa[l], w_br_mem[l],
                   w_out[l], ln2_g[l], ln2_b[l], ffn2_w_in[l], ffn2_w_out[l], ln3_g[l], ln3_b[l],
                   tables)
    return y.reshape(BATCH, SEQ, D_MODEL)
```

```python
import math

import jax
import jax.numpy as jnp
from jax import lax
from jax.experimental import pallas as pl
from jax.experimental.pallas import tpu as pltpu

F32 = jnp.float32
BF16 = jnp.bfloat16

D_MODEL = 1024
BATCH = 8
SEQ = 2048
MEM_TOKENS = 256
RET_HEADS, RET_DK, RET_DV, RET_CHUNK = 4, 64, 128, 128
DSA_HEADS, DSA_DH = 8, 64
IDX_HEADS, IDX_DIM = 8, 64
TOPK = min(256, SEQ // 4)
MEM_HEADS, MEM_DH = 4, 128
T5_BUCKETS, T5_MAX_DIST = 32, 128
D_FF = 2816
ROPE_BASE = 10000.0
LN_EPS = 1e-5
NEG_INF = -1e30
DEPTH = 1
ALPHA = (2.0 * DEPTH) ** 0.25

RET_QK_W = RET_HEADS * RET_DK
RET_V_W = RET_HEADS * RET_DV
DSA_W = DSA_HEADS * DSA_DH
IDX_Q_W = IDX_HEADS * IDX_DIM
MEM_W = MEM_HEADS * MEM_DH
N_TOK = BATCH * SEQ

V7X_VMEM_BYTES = 64 * 1024 * 1024
VMEM_LIMIT = V7X_VMEM_BYTES - 8 * 1024 * 1024
LANES = 128
SUBLANES = 8

FFN_ROWS = 1024
MXU_TILE = 256
FFN_CHUNK_EDGES = (0, 6 * MXU_TILE, D_FF)
PROJ_ROWS = 512
MERGE_ROWS = 1024
RET_ROWS = 512
MEMKV_ROWS = 512
MEMATT_ROWS = 512
DSA_T = 256
HEAD_PAD = LANES
DVT_HEAD_ROWS = DSA_DH + 16
DVT_ROWS = DSA_HEADS * DVT_HEAD_ROWS
BISECT_PLAIN_STEPS = 16
COUNT_ACCUMULATORS = 4
LOG2E = math.log2(math.e)
DEN_FLOOR = 2.0 ** -100

W_RQ0 = 0
W_RK0 = W_RQ0 + RET_QK_W
W_RV0 = W_RK0 + RET_QK_W
W_RG0 = W_RV0 + RET_V_W
W_DQ0 = W_RG0 + RET_V_W
W_DK0 = W_DQ0 + DSA_W
W_DV0 = W_DK0 + DSA_W
W_IQ0 = W_DV0 + DSA_W
W_IK0 = W_IQ0 + IDX_Q_W
W_IW0 = W_IK0 + IDX_DIM
W_MQ0 = W_IW0 + IDX_HEADS
W_G0 = W_MQ0 + MEM_W
IW_SCALE = IDX_HEADS ** -0.5 * IDX_DIM ** -0.5


def _cparams(sem):
    return pltpu.CompilerParams(dimension_semantics=sem, vmem_limit_bytes=VMEM_LIMIT)


def _const_spec(shape):
    nd = len(shape)
    return pl.BlockSpec(shape, lambda *_: (0,) * nd, pipeline_mode=pl.Buffered(1))


def _layer_norm(y, g, b):
    mu = jnp.mean(y, axis=-1, keepdims=True)
    yc = y - mu
    var = jnp.mean(yc * yc, axis=-1, keepdims=True)
    return yc * lax.rsqrt(var + LN_EPS) * g + b


def _sigmoid(x):
    return 0.5 * jnp.tanh(0.5 * x) + 0.5


def _dot(a, b):
    return jnp.dot(a, b, preferred_element_type=F32)


def _dot_nt(a, b):
    return lax.dot_general(a, b, (((1,), (1,)), ((), ())), preferred_element_type=F32)


def _ffn_ln_kernel(x_ref, wi_ref, wo_ref, g_ref, b_ref, o_ref):
    x = x_ref[...]
    xb = x.astype(BF16)
    acc = None
    for lo, hi in zip(FFN_CHUNK_EDGES[:-1], FFN_CHUNK_EDGES[1:]):
        a = _dot(xb, wi_ref[:, lo:hi])
        u = _dot(xb, wi_ref[:, D_FF + lo:D_FF + hi])
        act = (a * _sigmoid(a) * u).astype(BF16)
        part = _dot(act, wo_ref[lo:hi, :])
        acc = part if acc is None else acc + part
    o_ref[...] = _layer_norm(ALPHA * x + 0.5 * acc, g_ref[...], b_ref[...])


def _ffn_ln(x, w_in_bf, w_out_bf, g, b, name):
    return pl.pallas_call(
        _ffn_ln_kernel,
        out_shape=jax.ShapeDtypeStruct((N_TOK, D_MODEL), F32),
        grid=(N_TOK // FFN_ROWS,),
        in_specs=[
            pl.BlockSpec((FFN_ROWS, D_MODEL), lambda i: (i, 0)),
            _const_spec((D_MODEL, 2 * D_FF)),
            _const_spec((D_FF, D_MODEL)),
            _const_spec((1, D_MODEL)),
            _const_spec((1, D_MODEL)),
        ],
        out_specs=pl.BlockSpec((FFN_ROWS, D_MODEL), lambda i: (i, 0)),
        compiler_params=_cparams(("parallel",)),
        name=name,
    )(x, w_in_bf, w_out_bf, g, b)


def _rope(x, cos, sin_signed):
    width = x.shape[-1]
    lane = lax.broadcasted_iota(jnp.int32, x.shape, 1)
    first_half = (lane % RET_DK) < (RET_DK // 2)
    swapped = jnp.where(first_half,
                        pltpu.roll(x, width - RET_DK // 2, 1),
                        pltpu.roll(x, RET_DK // 2, 1))
    return x * cos + swapped * sin_signed


def _store_split_heads(dst_ref, val):
    low = lax.broadcasted_iota(jnp.int32, (val.shape[0], LANES), 1) < DSA_DH
    for j in range(val.shape[1] // LANES):
        pair = val[:, j * LANES:(j + 1) * LANES]
        dst_ref[:, (2 * j) * LANES:(2 * j + 1) * LANES] = jnp.where(low, pair, 0.0).astype(dst_ref.dtype)
        dst_ref[:, (2 * j + 1) * LANES:(2 * j + 2) * LANES] = jnp.where(low, 0.0, pair).astype(dst_ref.dtype)


def _store_low_heads(dst_ref, val):
    low = lax.broadcasted_iota(jnp.int32, (val.shape[0], LANES), 1) < RET_DK
    for j in range(val.shape[1] // LANES):
        pair = val[:, j * LANES:(j + 1) * LANES]
        dst_ref[:, (2 * j) * LANES:(2 * j + 1) * LANES] = jnp.where(low, pair, 0.0).astype(dst_ref.dtype)
        dst_ref[:, (2 * j + 1) * LANES:(2 * j + 2) * LANES] = jnp.where(
            low, pltpu.roll(pair, RET_DK, 1), 0.0).astype(dst_ref.dtype)


def _proj_kernel(h_ref, w_ref, wikw_ref, wmq_ref, wdvt_ref, cos_ref, sin_ref, kdec_ref,
                 rq_ref, rk_ref, rkdt_ref, rv_ref, rg_ref, dq_ref, dk_ref, dvt_ref, iq_ref, mq_ref,
                 ikw_ref, ikb_ref):
    hb = h_ref[...].astype(BF16)

    def proj(lo, width):
        return _dot(hb, w_ref[:, lo:lo + width])

    cos = cos_ref[...]
    sin = sin_ref[...]
    _store_low_heads(rq_ref, _rope(proj(W_RQ0, RET_QK_W), cos, sin))
    rk = _rope(proj(W_RK0, RET_QK_W), cos, sin) * (RET_DK ** -0.5)
    _store_low_heads(rk_ref, rk)
    rkdt_ref[...] = (rk * kdec_ref[...]).T.astype(BF16)
    rv_ref[...] = proj(W_RV0, RET_V_W).astype(BF16)
    rg_ref[...] = proj(W_RG0, RET_V_W)
    dq_ref[...] = (proj(W_DQ0, DSA_W) * (DSA_DH ** -0.5 * LOG2E)).T.astype(BF16)
    _store_split_heads(dk_ref, proj(W_DK0, DSA_W))
    iq_ref[...] = proj(W_IQ0, IDX_Q_W).T.astype(BF16)
    mq_ref[...] = _dot(hb, wmq_ref[...]).astype(BF16)
    ikw = _dot(hb, wikw_ref[...])
    ikw_ref[...] = ikw
    low = lax.broadcasted_iota(jnp.int32, ikw.shape, 1) < IDX_DIM
    ik_low = jnp.where(low, ikw, 0.0)
    ikb_ref[:, 0:LANES] = ik_low.astype(BF16)
    ikb_ref[:, LANES:2 * LANES] = pltpu.roll(ik_low, IDX_DIM, 1).astype(BF16)
    vt = _dot_nt(wdvt_ref[...], hb).astype(BF16)
    ones = jnp.ones((DVT_HEAD_ROWS - DSA_DH, DSA_T), BF16)
    for j in range(PROJ_ROWS // DSA_T):
        for h in range(DSA_HEADS):
            r0 = h * DVT_HEAD_ROWS
            dvt_ref[j, r0:r0 + DSA_DH, :] = vt[h * DSA_DH:(h + 1) * DSA_DH, j * DSA_T:(j + 1) * DSA_T]
            dvt_ref[j, r0 + DSA_DH:r0 + DVT_HEAD_ROWS, :] = ones


def _proj(h, w_in_bf, w_mq_bf, w_dvt_bf, cos_t, sin_t, kdec_rows):
    rows = lambda w: pl.BlockSpec((PROJ_ROWS, w), lambda i: (i, 0))
    seq_tiles = SEQ // PROJ_ROWS
    pos = lambda w: pl.BlockSpec((PROJ_ROWS, w), lambda i: (i % seq_tiles, 0))
    sd = jax.ShapeDtypeStruct
    ret_padded = RET_HEADS * HEAD_PAD
    dsa_padded = DSA_HEADS * HEAD_PAD
    slabs = PROJ_ROWS // DSA_T
    assert W_IK0 % LANES == 0
    return pl.pallas_call(
        _proj_kernel,
        out_shape=(
            sd((N_TOK, ret_padded), BF16), sd((N_TOK, ret_padded), BF16), sd((RET_QK_W, N_TOK), BF16),
            sd((N_TOK, RET_V_W), BF16), sd((N_TOK, RET_V_W), F32),
            sd((DSA_W, N_TOK), BF16), sd((N_TOK, dsa_padded), BF16),
            sd((N_TOK // DSA_T, DVT_ROWS, DSA_T), BF16),
            sd((IDX_Q_W, N_TOK), BF16), sd((N_TOK, MEM_W), BF16),
            sd((N_TOK, LANES), F32), sd((N_TOK, 2 * LANES), BF16),
        ),
        grid=(N_TOK // PROJ_ROWS,),
        in_specs=[rows(D_MODEL),
                  pl.BlockSpec((D_MODEL, W_IK0), lambda i: (0, 0), pipeline_mode=pl.Buffered(1)),
                  pl.BlockSpec((D_MODEL, LANES), lambda i: (0, W_IK0 // LANES), pipeline_mode=pl.Buffered(1)),
                  _const_spec((D_MODEL, MEM_W)), _const_spec((DSA_W, D_MODEL)),
                  pos(RET_QK_W), pos(RET_QK_W), _const_spec((PROJ_ROWS, RET_QK_W))],
        out_specs=(rows(ret_padded), rows(ret_padded),
                   pl.BlockSpec((RET_QK_W, PROJ_ROWS), lambda i: (0, i)),
                   rows(RET_V_W), rows(RET_V_W),
                   pl.BlockSpec((DSA_W, PROJ_ROWS), lambda i: (0, i)), rows(dsa_padded),
                   pl.BlockSpec((slabs, DVT_ROWS, DSA_T), lambda i: (i, 0, 0)),
                   pl.BlockSpec((IDX_Q_W, PROJ_ROWS), lambda i: (0, i)),
                   rows(MEM_W), rows(LANES), rows(2 * LANES)),
        compiler_params=_cparams(("parallel",)),
        name="mixer_proj",
    )(h, w_in_bf, w_in_bf, w_mq_bf, w_dvt_bf, cos_t, sin_t, kdec_rows)


def _retention_kernel(q_ref, k_ref, kdt_ref, v_ref, g_ref, din_ref, qdec_ref, cdec_ref,
                      gng_ref, gnb_ref, o_ref, state_ref):
    @pl.when(pl.program_id(1) == 0)
    def _():
        state_ref[...] = jnp.zeros_like(state_ref)

    qdec = qdec_ref[...]
    gn_g = gng_ref[...]
    gn_b = gnb_ref[...]
    states = [state_ref[h] for h in range(RET_HEADS)]
    pad_rows = jnp.zeros((HEAD_PAD - RET_DK, RET_DV), BF16)
    for j in range(RET_ROWS // RET_CHUNK):
        rows = slice(j * RET_CHUNK, (j + 1) * RET_CHUNK)
        gate = g_ref[rows, :]
        for h in range(RET_HEADS):
            hs = slice(h * HEAD_PAD, (h + 1) * HEAD_PAD)
            vs = slice(h * RET_DV, (h + 1) * RET_DV)
            qh = q_ref[rows, hs]
            vh = v_ref[rows, vs]
            scores = _dot_nt(qh, k_ref[rows, hs]) * din_ref[h]
            intra = _dot(scores.astype(BF16), vh)
            state_padded = jnp.concatenate([states[h].astype(BF16), pad_rows], axis=0)
            cross = _dot(qh, state_padded) * qdec[:, vs]
            kv = _dot(kdt_ref[h * RET_DK:(h + 1) * RET_DK, rows], vh)
            states[h] = cdec_ref[h] * states[h] + kv
            o = intra + cross
            mu = jnp.mean(o, axis=-1, keepdims=True)
            oc = o - mu
            var = jnp.mean(oc * oc, axis=-1, keepdims=True)
            o = oc * lax.rsqrt(var + LN_EPS) * gn_g[:, vs] + gn_b[:, vs]
            o_ref[rows, vs] = (jax.nn.silu(gate[:, vs]) * o).astype(BF16)
    for h in range(RET_HEADS):
        state_ref[h] = states[h]


def _retention(rq, rk, rkdt, rv, rg, din, qdec, cdec, gn_g, gn_b):
    nc = SEQ // RET_ROWS
    rows = lambda w: pl.BlockSpec((RET_ROWS, w), lambda b, n: (b * nc + n, 0))
    padded = RET_HEADS * HEAD_PAD
    return pl.pallas_call(
        _retention_kernel,
        out_shape=jax.ShapeDtypeStruct((N_TOK, RET_V_W), BF16),
        grid=(BATCH, nc),
        in_specs=[rows(padded), rows(padded),
                  pl.BlockSpec((RET_QK_W, RET_ROWS), lambda b, n: (0, b * nc + n)),
                  rows(RET_V_W), rows(RET_V_W),
                  _const_spec((RET_HEADS, RET_CHUNK, RET_CHUNK)),
                  _const_spec((RET_CHUNK, RET_V_W)),
                  _const_spec((RET_HEADS, RET_DK, RET_DV)),
                  _const_spec((1, RET_V_W)), _const_spec((1, RET_V_W))],
        out_specs=rows(RET_V_W),
        scratch_shapes=[pltpu.VMEM((RET_HEADS, RET_DK, RET_DV), F32)],
        compiler_params=_cparams(("parallel", "arbitrary")),
        name="retention",
    )(rq, rk, rkdt, rv, rg, din, qdec, cdec, gn_g, gn_b)


def _memkv_kernel(m_ref, w_ref, o_ref):
    o_ref[...] = _dot(m_ref[...].astype(BF16), w_ref[...]).astype(BF16)


def _memkv(mem2d, w_bf):
    n = mem2d.shape[0]
    return pl.pallas_call(
        _memkv_kernel,
        out_shape=jax.ShapeDtypeStruct((n, 2 * MEM_W), BF16),
        grid=(n // MEMKV_ROWS,),
        in_specs=[pl.BlockSpec((MEMKV_ROWS, D_MODEL), lambda i: (i, 0)),
                  _const_spec((D_MODEL, 2 * MEM_W))],
        out_specs=pl.BlockSpec((MEMKV_ROWS, 2 * MEM_W), lambda i: (i, 0)),
        compiler_params=_cparams(("parallel",)),
        name="mem_kv",
    )(mem2d, w_bf)


def _mematt_kernel(q_ref, k_ref, v_ref, o_ref):
    q = q_ref[...]
    k = k_ref[...]
    v = v_ref[...]
    for h in range(MEM_HEADS):
        hs = slice(h * MEM_DH, (h + 1) * MEM_DH)
        logits = _dot_nt(q[:, hs], k[:, hs]) * (MEM_DH ** -0.5)
        m = jnp.max(logits, axis=-1, keepdims=True)
        p = jnp.exp(logits - m)
        denom = jnp.sum(p, axis=-1, keepdims=True)
        o_ref[:, hs] = (_dot(p.astype(BF16), v[:, hs]) / denom).astype(BF16)


def _mematt(mq, mkv):
    tiles = SEQ // MEMATT_ROWS
    return pl.pallas_call(
        _mematt_kernel,
        out_shape=jax.ShapeDtypeStruct((N_TOK, MEM_W), BF16),
        grid=(BATCH, tiles),
        in_specs=[pl.BlockSpec((MEMATT_ROWS, MEM_W), lambda b, i: (b * tiles + i, 0)),
                  pl.BlockSpec((MEM_TOKENS, MEM_W), lambda b, i: (b, 0)),
                  pl.BlockSpec((MEM_TOKENS, MEM_W), lambda b, i: (b, 1))],
        out_specs=pl.BlockSpec((MEMATT_ROWS, MEM_W), lambda b, i: (b * tiles + i, 0)),
        compiler_params=_cparams(("parallel", "parallel")),
        name="mem_attention",
    )(mq, mkv, mkv)


def _t5_bias_kernel(table_ref, o_ref, bmax_ref):
    j = lax.broadcasted_iota(jnp.int32, (DSA_T, DSA_T), 0)
    r = lax.broadcasted_iota(jnp.int32, (DSA_T, DSA_T), 1)
    max_exact = T5_BUCKETS // 2
    head_max = [None] * DSA_HEADS
    for w in range(2):
        n = jnp.maximum(r + (1 - w) * DSA_T - j, 0)
        nf = jnp.maximum(n, 1).astype(F32)
        large = max_exact + jnp.floor(jnp.log(nf / max_exact) / math.log(T5_MAX_DIST / max_exact)
                                      * (T5_BUCKETS - max_exact)).astype(jnp.int32)
        large = jnp.minimum(large, T5_BUCKETS - 1)
        bucket = jnp.where(n < max_exact, n, large)
        for h in range(DSA_HEADS):
            acc = jnp.zeros((DSA_T, DSA_T), F32)
            for b in range(T5_BUCKETS):
                acc = jnp.where(bucket == b, table_ref[b, h] * LOG2E, acc)
            o_ref[w, h] = acc
            top = jnp.max(jnp.max(acc, axis=0, keepdims=True), axis=1, keepdims=True)
            head_max[h] = top if head_max[h] is None else jnp.maximum(head_max[h], top)
    for h in range(DSA_HEADS):
        bmax_ref[h:h + 1, :] = jnp.broadcast_to(head_max[h], (1, LANES))


def _t5_bias(t5_table):
    return pl.pallas_call(
        _t5_bias_kernel,
        out_shape=(jax.ShapeDtypeStruct((2, DSA_HEADS, DSA_T, DSA_T), F32),
                   jax.ShapeDtypeStruct((DSA_HEADS, LANES), F32)),
        in_specs=[pl.BlockSpec(memory_space=pltpu.SMEM)],
        out_specs=(pl.BlockSpec(memory_space=pltpu.VMEM), pl.BlockSpec(memory_space=pltpu.VMEM)),
        compiler_params=pltpu.CompilerParams(vmem_limit_bytes=VMEM_LIMIT),
        name="t5_bias",
    )(t5_table)


def _fold_groups(x, op):
    return op(x.reshape(DSA_T // SUBLANES, SUBLANES, DSA_T), axis=0)


def _dsa_kernel(table_ref, bias_ref, bmax_ref, iq_ref, dq_ref, ikw_q_ref, ikb_ref, dk_ref, dvt_ref,
                o_ref, score_ref, logit_ref, acc_ref):
    qb = pl.program_id(1)
    q0 = qb * DSA_T
    n_chunks = qb + 1
    groups = DSA_T // SUBLANES
    key_in_chunk = lax.broadcasted_iota(jnp.int32, (DSA_T, DSA_T), 0)
    query_in_step = lax.broadcasted_iota(jnp.int32, (DSA_T, DSA_T), 1)

    def key_rows(c):
        return pl.ds(pl.multiple_of(c * DSA_T, DSA_T), DSA_T)

    def pair_rows(j):
        return slice(j * LANES, (j + 1) * LANES)

    def pair_products(keys_even, keys_odd, queries_t):
        prod = _dot(jnp.concatenate([keys_even, keys_odd], axis=0), queries_t)
        return prod[:DSA_T], prod[DSA_T:]

    iw_t = ikw_q_ref[...].T[IDX_DIM:IDX_DIM + IDX_HEADS, :] * IW_SCALE

    def scores_and_logits(c, carry, causal):
        raw_max, s_min, s_max = carry
        ik_low = ikb_ref[key_rows(c), 0:LANES]
        ik_high = ikb_ref[key_rows(c), LANES:2 * LANES]
        last = IDX_HEADS // 2 - 1
        for j in range(IDX_HEADS // 2):
            rel_even, rel_odd = pair_products(ik_low, ik_high, iq_ref[pair_rows(j), :])
            term = (jnp.maximum(rel_even, 0.0) * iw_t[2 * j:2 * j + 1, :]
                    + jnp.maximum(rel_odd, 0.0) * iw_t[2 * j + 1:2 * j + 2, :])
            total = term if j == 0 else score_ref[c] + term
            if j < last:
                score_ref[c] = total
            elif causal is None:
                score_ref[c] = total
                s_min = jnp.minimum(s_min, _fold_groups(total, jnp.min))
                s_max = jnp.maximum(s_max, _fold_groups(total, jnp.max))
            else:
                score_ref[c] = jnp.where(causal, total, NEG_INF)
                s_min = jnp.minimum(s_min, _fold_groups(jnp.where(causal, total, -NEG_INF), jnp.min))
                s_max = jnp.maximum(s_max, _fold_groups(jnp.where(causal, total, NEG_INF), jnp.max))
        out = list(raw_max)
        for j in range(DSA_HEADS // 2):
            keys = [dk_ref[key_rows(c), (2 * j + i) * HEAD_PAD:(2 * j + i + 1) * HEAD_PAD] for i in range(2)]
            for h, qk in zip((2 * j, 2 * j + 1), pair_products(keys[0], keys[1], dq_ref[pair_rows(j), :])):
                logit_ref[h, c] = qk
                out[h] = jnp.maximum(out[h], _fold_groups(qk, jnp.max))
        return tuple(out), s_min, s_max

    def paired_loop(start, stop, body, carry):
        n = stop - start
        carry = lax.fori_loop(
            0, lax.shift_right_logical(n, 1),
            lambda i, cr: body(start + 2 * i + 1, body(start + 2 * i, cr)), carry)
        return lax.cond(jnp.bitwise_and(n, 1) == 1, lambda cr: body(stop - 1, cr), lambda cr: cr, carry)

    low_init = jnp.full((SUBLANES, DSA_T), NEG_INF, F32)
    carry = (tuple(low_init for _ in range(DSA_HEADS)), -low_init, low_init)
    carry = paired_loop(0, qb, lambda c, cr: scores_and_logits(c, cr, None), carry)
    raw_max, s_min, s_max = scores_and_logits(qb, carry, key_in_chunk <= query_in_step)

    def count_where(pred):
        def body(c, acc):
            parts = [acc] + [jnp.zeros((SUBLANES, DSA_T), F32)] * (COUNT_ACCUMULATORS - 1)
            for g in range(groups):
                blk = score_ref[c, g * SUBLANES:(g + 1) * SUBLANES, :]
                a = g % COUNT_ACCUMULATORS
                parts[a] = parts[a] + jnp.where(pred(c, g * SUBLANES, blk), 1.0, 0.0)
            while len(parts) > 1:
                parts = [parts[i] + parts[i + 1] for i in range(0, len(parts), 2)]
            return parts[0]
        acc = lax.fori_loop(0, n_chunks, body, jnp.zeros((SUBLANES, DSA_T), F32))
        return jnp.sum(acc, axis=0, keepdims=True)

    def count_ge(thr):
        thr_b = jnp.broadcast_to(thr, (SUBLANES, DSA_T))
        return count_where(lambda c, r0, blk: blk >= thr_b)

    q_min = jnp.min(s_min, axis=0, keepdims=True)
    q_max = jnp.max(s_max, axis=0, keepdims=True)
    t_q = q0 + lax.broadcasted_iota(jnp.int32, (1, DSA_T), 1)

    kf = float(TOPK)
    init = (q_min, q_max + jnp.maximum(jnp.abs(q_max), 1.0) * 1e-6,
            (t_q + 1).astype(F32), jnp.zeros((1, DSA_T), F32))

    def bisect_step(_i, carry):
        lo, hi, cnt_lo, cnt_hi = carry
        mid = 0.5 * (lo + hi)
        cnt = count_ge(mid)
        ge = cnt >= kf
        return (jnp.where(ge, mid, lo), jnp.where(ge, hi, mid),
                jnp.where(ge, cnt, cnt_lo), jnp.where(ge, cnt_hi, cnt))

    lo, hi, cnt_lo, cnt_hi = lax.fori_loop(0, BISECT_PLAIN_STEPS, bisect_step, init)

    big = -NEG_INF

    def snap_body(c, carry):
        mn, mx = carry
        blk = score_ref[c]
        mn = jnp.minimum(mn, _fold_groups(jnp.where(blk >= lo, blk, big), jnp.min))
        mx = jnp.maximum(mx, _fold_groups(jnp.where(blk < hi, blk, -big), jnp.max))
        return mn, mx

    mn, mx = lax.fori_loop(
        0, n_chunks, snap_body,
        (jnp.full((SUBLANES, DSA_T), big, F32), jnp.full((SUBLANES, DSA_T), -big, F32)))
    lo = jnp.min(mn, axis=0, keepdims=True)
    top = jnp.max(mx, axis=0, keepdims=True)

    def split_pair(carry):
        lo, top, cnt_lo, cnt_hi = carry
        pair = (cnt_lo > kf) & (lo < top) & (cnt_lo - cnt_hi == 2.0)
        return jnp.where(pair, top, lo), top, jnp.where(pair, cnt_hi + 1.0, cnt_lo), cnt_hi

    def pending_of(carry):
        lo, top, cnt_lo, _ = carry
        return jnp.max(jnp.where((cnt_lo > kf) & (lo < top), 1.0, 0.0))

    def value_step(state):
        (lo, top, cnt_lo, cnt_hi), _ = state
        unresolved = (cnt_lo > kf) & (lo < top)
        mid = lo + (top - lo) * 0.5
        mid = jnp.minimum(jnp.where(mid > lo, mid, top), top)

        def body(c, carry):
            cnt, mn, mx = carry
            blk = score_ref[c]
            is_ge = blk >= mid
            cnt = cnt + _fold_groups(jnp.where(is_ge, 1.0, 0.0), jnp.sum)
            mn = jnp.minimum(mn, _fold_groups(jnp.where(is_ge, blk, big), jnp.min))
            mx = jnp.maximum(mx, _fold_groups(jnp.where(is_ge, -big, blk), jnp.max))
            return cnt, mn, mx

        cnt, mn, mx = lax.fori_loop(
            0, n_chunks, body,
            (jnp.zeros((SUBLANES, DSA_T), F32), jnp.full((SUBLANES, DSA_T), big, F32),
             jnp.full((SUBLANES, DSA_T), -big, F32)))
        cnt = jnp.sum(cnt, axis=0, keepdims=True)
        ge = cnt >= kf
        raise_lo = unresolved & ge
        lower_top = unresolved & jnp.logical_not(ge)
        carry = split_pair((jnp.where(raise_lo, jnp.min(mn, axis=0, keepdims=True), lo),
                            jnp.where(lower_top, jnp.max(mx, axis=0, keepdims=True), top),
                            jnp.where(raise_lo, cnt, cnt_lo), jnp.where(lower_top, cnt, cnt_hi)))
        return carry, pending_of(carry)

    start = split_pair((lo, top, cnt_lo, cnt_hi))
    (lo, top, cnt_lo, cnt_hi), _ = lax.while_loop(
        lambda state: state[1] > 0.5, value_step, (start, pending_of(start)))

    tied = cnt_lo > kf
    need = kf - cnt_hi
    lo_b = jnp.broadcast_to(lo, (SUBLANES, DSA_T))
    key_in_group = lax.broadcasted_iota(jnp.int32, (SUBLANES, DSA_T), 0)

    def tie_search(_):
        def body(_i, carry):
            lo_i, hi_i = carry
            mid_i = lax.shift_right_arithmetic(lo_i + hi_i, 1)
            mid_b = jnp.broadcast_to(mid_i, (SUBLANES, DSA_T))
            cnt = count_where(lambda c, r0, blk: (blk == lo_b) & (key_in_group + (c * DSA_T + r0) <= mid_b))
            enough = cnt >= need
            return jnp.where(enough, lo_i, mid_i), jnp.where(enough, mid_i, hi_i)
        lo_i = jnp.full((1, DSA_T), -1, jnp.int32)
        hi_i = jnp.full((1, DSA_T), SEQ - 1, jnp.int32)
        _, cut = lax.fori_loop(0, int(math.log2(SEQ)) + 1, body, (lo_i, hi_i))
        return jnp.where(tied, cut, SEQ)

    cut = lax.cond(jnp.max(jnp.where(tied, 1.0, 0.0)) > 0.5, tie_search,
                   lambda _: jnp.full((1, DSA_T), SEQ, jnp.int32), 0)

    def write_mask(c):
        blk = score_ref[c]
        keep = (blk > lo) | ((blk == lo) & (key_in_chunk + c * DSA_T <= cut))
        score_ref[c] = jnp.where(keep, 0.0, NEG_INF)

    n_far = jnp.maximum(qb - 1, 0)

    def masked_logits(c, h, far):
        lg = logit_ref[h, c] + score_ref[c]
        return lg if far else lg + bias_ref[c - qb + 1, h]

    def far_bias(h):
        return table_ref[T5_BUCKETS - 1, h] * LOG2E

    def den_row(h):
        return h * DVT_HEAD_ROWS + DSA_DH

    def exp_and_pv(shifts, scores_to_masks):
        acc_ref[...] = jnp.zeros_like(acc_ref)

        def body(c, carry, far):
            if scores_to_masks:
                write_mask(c)
            for h in range(DSA_HEADS):
                shift = shifts[h] - far_bias(h) if far else shifts[h]
                p = jnp.exp2(masked_logits(c, h, far) - shift).astype(BF16)
                rows = slice(h * DVT_HEAD_ROWS, (h + 1) * DVT_HEAD_ROWS)
                acc_ref[rows, :] += _dot(dvt_ref[c, rows, :], p)
            return carry

        paired_loop(0, n_far, lambda c, carry: body(c, carry, True), 0)
        paired_loop(n_far, n_chunks, lambda c, carry: body(c, carry, False), 0)

    upper = [jnp.max(raw_max[h], axis=0, keepdims=True) + bmax_ref[h:h + 1, 0:1] for h in range(DSA_HEADS)]
    exp_and_pv(upper, True)

    den_min = acc_ref[den_row(0):den_row(0) + 1, :]
    for h in range(1, DSA_HEADS):
        den_min = jnp.minimum(den_min, acc_ref[den_row(h):den_row(h) + 1, :])

    @pl.when(jnp.min(den_min) < DEN_FLOOR)
    def _():
        def max_body(c, ms, far):
            return tuple(jnp.maximum(ms[h], _fold_groups(masked_logits(c, h, far), jnp.max) + (far_bias(h) if far else 0.0))
                         for h in range(DSA_HEADS))
        ms = tuple(jnp.full((SUBLANES, DSA_T), NEG_INF, F32) for _ in range(DSA_HEADS))
        ms = lax.fori_loop(0, n_far, lambda c, ms: max_body(c, ms, True), ms)
        ms = lax.fori_loop(n_far, n_chunks, lambda c, ms: max_body(c, ms, False), ms)
        exp_and_pv([jnp.max(m, axis=0, keepdims=True) for m in ms], False)

    heads = []
    for h in range(DSA_HEADS):
        r0 = h * DVT_HEAD_ROWS
        heads.append(acc_ref[r0:r0 + DSA_DH, :] / acc_ref[den_row(h):den_row(h) + 1, :])
    o_ref[...] = jnp.concatenate(heads, axis=0).T.astype(BF16)


def _dsa(t5_table, bias, bias_max, iq, dq, ikw, ikb, dk, dvt):
    nqb = SEQ // DSA_T
    padded = DSA_HEADS * HEAD_PAD
    qrows = lambda w: pl.BlockSpec((DSA_T, w), lambda b, i: (b * nqb + i, 0))
    qcols = lambda h: pl.BlockSpec((h, DSA_T), lambda b, i: (0, b * nqb + i))
    krows = lambda w: pl.BlockSpec((SEQ, w), lambda b, i: (b, 0))
    return pl.pallas_call(
        _dsa_kernel,
        out_shape=jax.ShapeDtypeStruct((N_TOK, DSA_W), BF16),
        grid=(BATCH, nqb),
        in_specs=[pl.BlockSpec(memory_space=pltpu.SMEM),
                  _const_spec((2, DSA_HEADS, DSA_T, DSA_T)), _const_spec((DSA_HEADS, LANES)),
                  qcols(IDX_Q_W), qcols(DSA_W), qrows(LANES),
                  krows(2 * LANES), krows(padded),
                  pl.BlockSpec((nqb, DVT_ROWS, DSA_T), lambda b, i: (b, 0, 0))],
        out_specs=qrows(DSA_W),
        scratch_shapes=[pltpu.VMEM((nqb, DSA_T, DSA_T), F32),
                        pltpu.VMEM((DSA_HEADS, nqb, DSA_T, DSA_T), F32),
                        pltpu.VMEM((DVT_ROWS, DSA_T), F32)],
        compiler_params=_cparams(("parallel", "arbitrary")),
        name="dsa_attention",
    )(t5_table, bias, bias_max, iq, dq, ikw, ikb, dk, dvt)


def _merge_kernel(h_ref, oret_ref, odsa_ref, omem_ref, wg_ref, wr_ref, wd_ref, wm_ref, wo_ref,
                  g_ref, b_ref, o_ref):
    h = h_ref[...]
    hb = h.astype(BF16)
    merged = None
    for i, (src, w) in enumerate(((oret_ref, wr_ref), (odsa_ref, wd_ref), (omem_ref, wm_ref))):
        gate = _sigmoid(_dot(hb, wg_ref[:, i * D_MODEL:(i + 1) * D_MODEL]))
        term = gate * _dot(src[...], w[...])
        merged = term if merged is None else merged + term
    mix = _dot(merged.astype(BF16), wo_ref[...])
    o_ref[...] = _layer_norm(ALPHA * h + mix, g_ref[...], b_ref[...])


def _merge(h, o_ret, o_dsa, o_mem, wg, wr, wd, wm, wo, g, b):
    rows = lambda w: pl.BlockSpec((MERGE_ROWS, w), lambda i: (i, 0))
    return pl.pallas_call(
        _merge_kernel,
        out_shape=jax.ShapeDtypeStruct((N_TOK, D_MODEL), F32),
        grid=(N_TOK // MERGE_ROWS,),
        in_specs=[rows(D_MODEL), rows(RET_V_W), rows(DSA_W), rows(MEM_W),
                  _const_spec((D_MODEL, 3 * D_MODEL)),
                  _const_spec((RET_V_W, D_MODEL)), _const_spec((DSA_W, D_MODEL)),
                  _const_spec((MEM_W, D_MODEL)), _const_spec((D_MODEL, D_MODEL)),
                  _const_spec((1, D_MODEL)), _const_spec((1, D_MODEL))],
        out_specs=rows(D_MODEL),
        compiler_params=_cparams(("parallel",)),
        name="merge_ln2",
    )(h, o_ret, o_dsa, o_mem, wg, wr, wd, wm, wo, g, b)


def _rope_tables():
    half = RET_DK // 2
    freqs = ROPE_BASE ** (-jnp.arange(half, dtype=F32) / half)
    ang = jnp.arange(SEQ).astype(F32)[:, None] * freqs[None, :]
    cos, sin = jnp.cos(ang), jnp.sin(ang)
    cos_t = jnp.tile(jnp.concatenate([cos, cos], axis=-1), (1, RET_HEADS))
    sin_t = jnp.tile(jnp.concatenate([-sin, sin], axis=-1), (1, RET_HEADS))
    return cos_t, sin_t


def _decay_tables():
    c = RET_CHUNK
    gamma = 1.0 - 2.0 ** (-5.0 - jnp.arange(RET_HEADS, dtype=F32))
    lg = jnp.log(gamma)
    i = jnp.arange(c)
    diff = i[:, None] - i[None, :]
    decay_in = jnp.where(diff[None] >= 0, jnp.exp(jnp.maximum(diff, 0)[None] * lg[:, None, None]), 0.0).astype(F32)
    k_dec = jnp.exp((c - 1 - i)[None, :] * lg[:, None]).astype(F32)
    q_dec = jnp.exp((i + 1)[None, :] * lg[:, None]).astype(F32)
    chunk_dec = jnp.exp(c * lg).astype(F32)
    kdec_t = jnp.repeat(k_dec.T, RET_DK, axis=1)
    qdec_t = jnp.repeat(q_dec.T, RET_DV, axis=1)
    cdec_t = jnp.broadcast_to(chunk_dec[:, None, None], (RET_HEADS, RET_DK, RET_DV))
    return decay_in, kdec_t, qdec_t, cdec_t


def _layer(x, mem2d, ffn1_w_in, ffn1_w_out, ln1_g, ln1_b, w_in, t5_table, ret_gn_g, ret_gn_b,
           w_mem_kv, w_br_ret, w_br_dsa, w_br_mem, w_out, ln2_g, ln2_b,
           ffn2_w_in, ffn2_w_out, ln3_g, ln3_b, tables):
    cos_t, sin_t, din, kdec, qdec, cdec = tables
    row = lambda v: v.reshape(1, -1)
    bf = lambda w: w.astype(BF16)

    w_in_bf = bf(w_in)
    w_mq = w_in_bf[:, W_MQ0:W_G0]
    w_dvt = w_in_bf[:, W_DV0:W_IQ0].T
    kdec_rows = jnp.tile(kdec, (PROJ_ROWS // RET_CHUNK, 1))

    h = _ffn_ln(x, bf(ffn1_w_in), bf(ffn1_w_out), row(ln1_g), row(ln1_b), "ffn1_ln1")
    rq, rk, rkdt, rv, rg, dq, dk, dvt, iq, mq, ikw, ikb = _proj(h, w_in_bf, w_mq, w_dvt, cos_t, sin_t, kdec_rows)
    o_ret = _retention(rq, rk, rkdt, rv, rg, din, qdec, cdec, row(ret_gn_g), row(ret_gn_b))
    o_mem = _mematt(mq, _memkv(mem2d, bf(w_mem_kv)))
    o_dsa = _dsa(t5_table, *_t5_bias(t5_table), iq, dq, ikw, ikb, dk, dvt)
    x2 = _merge(h, o_ret, o_dsa, o_mem, w_in_bf[:, W_G0:], bf(w_br_ret), bf(w_br_dsa), bf(w_br_mem),
                bf(w_out), row(ln2_g), row(ln2_b))
    return _ffn_ln(x2, bf(ffn2_w_in), bf(ffn2_w_out), row(ln3_g), row(ln3_b), "ffn2_ln3")


def kernel(x, mem, ffn1_w_in, ffn1_w_out, ln1_g, ln1_b, w_in, t5_table, ret_gn_g, ret_gn_b,
           w_mem_kv, w_br_ret, w_br_dsa, w_br_mem, w_out, ln2_g, ln2_b,
           ffn2_w_in, ffn2_w_out, ln3_g, ln3_b):
    assert x.shape == (BATCH, SEQ, D_MODEL) and mem.shape == (BATCH, MEM_TOKENS, D_MODEL)
    tables = _rope_tables() + _decay_tables()
    y = x.reshape(N_TOK, D_MODEL)
    mem2d = mem.reshape(BATCH * MEM_TOKENS, D_MODEL)
    for l in range(DEPTH):
        y = _layer(y, mem2d, ffn1_w_in[l], ffn1_w_out[l], ln1_g[l], ln1_b[l], w_in[l], t5_table,
                   ret_gn_g[l], ret_gn_b[l], w_mem_kv[l], w_br_ret[l], w_br_dsa[l], w_br_mem[l],
                   w_out[l], ln2_g[l], ln2_b[l], ffn2_w_in[l], ffn2_w_out[l], ln3_g[l], ln3_b[l],
                   tables)
    return y.reshape(BATCH, SEQ, D_MODEL)
```

```python
import math

import jax
import jax.numpy as jnp
from jax import lax
from jax.experimental import pallas as pl
from jax.experimental.pallas import tpu as pltpu

F32 = jnp.float32
BF16 = jnp.bfloat16

D_MODEL = 1024
BATCH = 8
SEQ = 2048
MEM_TOKENS = 256
RET_HEADS, RET_DK, RET_DV, RET_CHUNK = 4, 64, 128, 128
DSA_HEADS, DSA_DH = 8, 64
IDX_HEADS, IDX_DIM = 8, 64
TOPK = min(256, SEQ // 4)
MEM_HEADS, MEM_DH = 4, 128
T5_BUCKETS, T5_MAX_DIST = 32, 128
D_FF = 2816
ROPE_BASE = 10000.0
LN_EPS = 1e-5
NEG_INF = -1e30
DEPTH = 1
ALPHA = (2.0 * DEPTH) ** 0.25

RET_QK_W = RET_HEADS * RET_DK
RET_V_W = RET_HEADS * RET_DV
DSA_W = DSA_HEADS * DSA_DH
IDX_Q_W = IDX_HEADS * IDX_DIM
MEM_W = MEM_HEADS * MEM_DH
N_TOK = BATCH * SEQ

V7X_VMEM_BYTES = 64 * 1024 * 1024
VMEM_LIMIT = V7X_VMEM_BYTES - 8 * 1024 * 1024
LANES = 128
SUBLANES = 8

FFN_ROWS = 1024
MXU_TILE = 256
FFN_CHUNK_EDGES = (0, 6 * MXU_TILE, D_FF)
PROJ_ROWS = 512
MERGE_ROWS = 1024
RET_ROWS = 512
MEMATT_ROWS = 512
DSA_T = 256
HEAD_PAD = LANES
DVT_HEAD_ROWS = DSA_DH + 16
DVT_ROWS = DSA_HEADS * DVT_HEAD_ROWS
BISECT_PLAIN_STEPS = 16
COUNT_ACCUMULATORS = 4
LOG2E = math.log2(math.e)
DEN_FLOOR = 2.0 ** -100

W_RQ0 = 0
W_RK0 = W_RQ0 + RET_QK_W
W_RV0 = W_RK0 + RET_QK_W
W_RG0 = W_RV0 + RET_V_W
W_DQ0 = W_RG0 + RET_V_W
W_DK0 = W_DQ0 + DSA_W
W_DV0 = W_DK0 + DSA_W
W_IQ0 = W_DV0 + DSA_W
W_IK0 = W_IQ0 + IDX_Q_W
W_IW0 = W_IK0 + IDX_DIM
W_MQ0 = W_IW0 + IDX_HEADS
W_G0 = W_MQ0 + MEM_W
IW_SCALE = IDX_HEADS ** -0.5 * IDX_DIM ** -0.5


def _cparams(sem):
    return pltpu.CompilerParams(dimension_semantics=sem, vmem_limit_bytes=VMEM_LIMIT)


def _const_spec(shape):
    nd = len(shape)
    return pl.BlockSpec(shape, lambda *_: (0,) * nd, pipeline_mode=pl.Buffered(1))


def _layer_norm(y, g, b):
    mu = jnp.mean(y, axis=-1, keepdims=True)
    yc = y - mu
    var = jnp.mean(yc * yc, axis=-1, keepdims=True)
    return yc * lax.rsqrt(var + LN_EPS) * g + b


def _sigmoid(x):
    return 0.5 * jnp.tanh(0.5 * x) + 0.5


def _dot(a, b):
    return jnp.dot(a, b, preferred_element_type=F32)


def _dot_nt(a, b):
    return lax.dot_general(a, b, (((1,), (1,)), ((), ())), preferred_element_type=F32)


def _ffn_ln_kernel(x_ref, wi_ref, wo_ref, g_ref, b_ref, o_ref):
    x = x_ref[...]
    xb = x.astype(BF16)
    acc = None
    for lo, hi in zip(FFN_CHUNK_EDGES[:-1], FFN_CHUNK_EDGES[1:]):
        a = _dot(xb, wi_ref[:, lo:hi])
        u = _dot(xb, wi_ref[:, D_FF + lo:D_FF + hi])
        act = (a * _sigmoid(a) * u).astype(BF16)
        part = _dot(act, wo_ref[lo:hi, :])
        acc = part if acc is None else acc + part
    o_ref[...] = _layer_norm(ALPHA * x + 0.5 * acc, g_ref[...], b_ref[...])


def _ffn_ln(x, w_in_bf, w_out_bf, g, b, name):
    return pl.pallas_call(
        _ffn_ln_kernel,
        out_shape=jax.ShapeDtypeStruct((N_TOK, D_MODEL), F32),
        grid=(N_TOK // FFN_ROWS,),
        in_specs=[
            pl.BlockSpec((FFN_ROWS, D_MODEL), lambda i: (i, 0)),
            _const_spec((D_MODEL, 2 * D_FF)),
            _const_spec((D_FF, D_MODEL)),
            _const_spec((1, D_MODEL)),
            _const_spec((1, D_MODEL)),
        ],
        out_specs=pl.BlockSpec((FFN_ROWS, D_MODEL), lambda i: (i, 0)),
        compiler_params=_cparams(("parallel",)),
        name=name,
    )(x, w_in_bf, w_out_bf, g, b)


def _rope(x, cos, sin_signed):
    width = x.shape[-1]
    lane = lax.broadcasted_iota(jnp.int32, x.shape, 1)
    first_half = (lane % RET_DK) < (RET_DK // 2)
    swapped = jnp.where(first_half,
                        pltpu.roll(x, width - RET_DK // 2, 1),
                        pltpu.roll(x, RET_DK // 2, 1))
    return x * cos + swapped * sin_signed


def _store_split_heads(dst_ref, val):
    low = lax.broadcasted_iota(jnp.int32, (val.shape[0], LANES), 1) < DSA_DH
    for j in range(val.shape[1] // LANES):
        pair = val[:, j * LANES:(j + 1) * LANES]
        dst_ref[:, (2 * j) * LANES:(2 * j + 1) * LANES] = jnp.where(low, pair, 0.0).astype(dst_ref.dtype)
        dst_ref[:, (2 * j + 1) * LANES:(2 * j + 2) * LANES] = jnp.where(low, 0.0, pair).astype(dst_ref.dtype)


def _store_low_heads(dst_ref, val):
    low = lax.broadcasted_iota(jnp.int32, (val.shape[0], LANES), 1) < RET_DK
    for j in range(val.shape[1] // LANES):
        pair = val[:, j * LANES:(j + 1) * LANES]
        dst_ref[:, (2 * j) * LANES:(2 * j + 1) * LANES] = jnp.where(low, pair, 0.0).astype(dst_ref.dtype)
        dst_ref[:, (2 * j + 1) * LANES:(2 * j + 2) * LANES] = jnp.where(
            low, pltpu.roll(pair, RET_DK, 1), 0.0).astype(dst_ref.dtype)


def _proj_kernel(h_ref, w_ref, wikw_ref, wmq_ref, wdvt_ref, cos_ref, sin_ref, kdec_ref,
                 rq_ref, rk_ref, rkdt_ref, rv_ref, rg_ref, dq_ref, dk_ref, dvt_ref, iq_ref, mq_ref,
                 ikw_ref, ikb_ref):
    hb = h_ref[...].astype(BF16)

    def proj(lo, width):
        return _dot(hb, w_ref[:, lo:lo + width])

    cos = cos_ref[...]
    sin = sin_ref[...]
    _store_low_heads(rq_ref, _rope(proj(W_RQ0, RET_QK_W), cos, sin))
    rk = _rope(proj(W_RK0, RET_QK_W), cos, sin) * (RET_DK ** -0.5)
    _store_low_heads(rk_ref, rk)
    rkdt_ref[...] = (rk * kdec_ref[...]).T.astype(BF16)
    rv_ref[...] = proj(W_RV0, RET_V_W).astype(BF16)
    rg_ref[...] = proj(W_RG0, RET_V_W)
    dq_ref[...] = (proj(W_DQ0, DSA_W) * (DSA_DH ** -0.5 * LOG2E)).T.astype(BF16)
    _store_split_heads(dk_ref, proj(W_DK0, DSA_W))
    iq_ref[...] = proj(W_IQ0, IDX_Q_W).T.astype(BF16)
    mq_ref[...] = _dot(hb, wmq_ref[...]).astype(BF16)
    ikw = _dot(hb, wikw_ref[...])
    ikw_ref[...] = ikw
    low = lax.broadcasted_iota(jnp.int32, ikw.shape, 1) < IDX_DIM
    ik_low = jnp.where(low, ikw, 0.0)
    ikb_ref[:, 0:LANES] = ik_low.astype(BF16)
    ikb_ref[:, LANES:2 * LANES] = pltpu.roll(ik_low, IDX_DIM, 1).astype(BF16)
    vt = _dot_nt(wdvt_ref[...], hb).astype(BF16)
    ones = jnp.ones((DVT_HEAD_ROWS - DSA_DH, DSA_T), BF16)
    for j in range(PROJ_ROWS // DSA_T):
        for h in range(DSA_HEADS):
            r0 = h * DVT_HEAD_ROWS
            dvt_ref[j, r0:r0 + DSA_DH, :] = vt[h * DSA_DH:(h + 1) * DSA_DH, j * DSA_T:(j + 1) * DSA_T]
            dvt_ref[j, r0 + DSA_DH:r0 + DVT_HEAD_ROWS, :] = ones


def _proj(h, w_in_bf, w_mq_bf, w_dvt_bf, cos_t, sin_t, kdec_rows):
    rows = lambda w: pl.BlockSpec((PROJ_ROWS, w), lambda i: (i, 0))
    seq_tiles = SEQ // PROJ_ROWS
    pos = lambda w: pl.BlockSpec((PROJ_ROWS, w), lambda i: (i % seq_tiles, 0))
    sd = jax.ShapeDtypeStruct
    ret_padded = RET_HEADS * HEAD_PAD
    dsa_padded = DSA_HEADS * HEAD_PAD
    slabs = PROJ_ROWS // DSA_T
    assert W_IK0 % LANES == 0
    return pl.pallas_call(
        _proj_kernel,
        out_shape=(
            sd((N_TOK, ret_padded), BF16), sd((N_TOK, ret_padded), BF16), sd((RET_QK_W, N_TOK), BF16),
            sd((N_TOK, RET_V_W), BF16), sd((N_TOK, RET_V_W), F32),
            sd((DSA_W, N_TOK), BF16), sd((N_TOK, dsa_padded), BF16),
            sd((N_TOK // DSA_T, DVT_ROWS, DSA_T), BF16),
            sd((IDX_Q_W, N_TOK), BF16), sd((N_TOK, MEM_W), BF16),
            sd((N_TOK, LANES), F32), sd((N_TOK, 2 * LANES), BF16),
        ),
        grid=(N_TOK // PROJ_ROWS,),
        in_specs=[rows(D_MODEL),
                  pl.BlockSpec((D_MODEL, W_IK0), lambda i: (0, 0), pipeline_mode=pl.Buffered(1)),
                  pl.BlockSpec((D_MODEL, LANES), lambda i: (0, W_IK0 // LANES), pipeline_mode=pl.Buffered(1)),
                  _const_spec((D_MODEL, MEM_W)), _const_spec((DSA_W, D_MODEL)),
                  pos(RET_QK_W), pos(RET_QK_W), _const_spec((PROJ_ROWS, RET_QK_W))],
        out_specs=(rows(ret_padded), rows(ret_padded),
                   pl.BlockSpec((RET_QK_W, PROJ_ROWS), lambda i: (0, i)),
                   rows(RET_V_W), rows(RET_V_W),
                   pl.BlockSpec((DSA_W, PROJ_ROWS), lambda i: (0, i)), rows(dsa_padded),
                   pl.BlockSpec((slabs, DVT_ROWS, DSA_T), lambda i: (i, 0, 0)),
                   pl.BlockSpec((IDX_Q_W, PROJ_ROWS), lambda i: (0, i)),
                   rows(MEM_W), rows(LANES), rows(2 * LANES)),
        compiler_params=_cparams(("parallel",)),
        name="mixer_proj",
    )(h, w_in_bf, w_in_bf, w_mq_bf, w_dvt_bf, cos_t, sin_t, kdec_rows)


def _retention_kernel(q_ref, k_ref, kdt_ref, v_ref, g_ref, din_ref, qdec_ref, cdec_ref,
                      gng_ref, gnb_ref, o_ref, state_ref):
    @pl.when(pl.program_id(1) == 0)
    def _():
        state_ref[...] = jnp.zeros_like(state_ref)

    qdec = qdec_ref[...]
    gn_g = gng_ref[...]
    gn_b = gnb_ref[...]
    states = [state_ref[h] for h in range(RET_HEADS)]
    pad_rows = jnp.zeros((HEAD_PAD - RET_DK, RET_DV), BF16)
    for j in range(RET_ROWS // RET_CHUNK):
        rows = slice(j * RET_CHUNK, (j + 1) * RET_CHUNK)
        gate = g_ref[rows, :]
        for h in range(RET_HEADS):
            hs = slice(h * HEAD_PAD, (h + 1) * HEAD_PAD)
            vs = slice(h * RET_DV, (h + 1) * RET_DV)
            qh = q_ref[rows, hs]
            vh = v_ref[rows, vs]
            scores = _dot_nt(qh, k_ref[rows, hs]) * din_ref[h]
            intra = _dot(scores.astype(BF16), vh)
            state_padded = jnp.concatenate([states[h].astype(BF16), pad_rows], axis=0)
            cross = _dot(qh, state_padded) * qdec[:, vs]
            kv = _dot(kdt_ref[h * RET_DK:(h + 1) * RET_DK, rows], vh)
            states[h] = cdec_ref[h] * states[h] + kv
            o = intra + cross
            mu = jnp.mean(o, axis=-1, keepdims=True)
            oc = o - mu
            var = jnp.mean(oc * oc, axis=-1, keepdims=True)
            o = oc * lax.rsqrt(var + LN_EPS) * gn_g[:, vs] + gn_b[:, vs]
            o_ref[rows, vs] = (jax.nn.silu(gate[:, vs]) * o).astype(BF16)
    for h in range(RET_HEADS):
        state_ref[h] = states[h]


def _retention(rq, rk, rkdt, rv, rg, din, qdec, cdec, gn_g, gn_b):
    nc = SEQ // RET_ROWS
    rows = lambda w: pl.BlockSpec((RET_ROWS, w), lambda b, n: (b * nc + n, 0))
    padded = RET_HEADS * HEAD_PAD
    return pl.pallas_call(
        _retention_kernel,
        out_shape=jax.ShapeDtypeStruct((N_TOK, RET_V_W), BF16),
        grid=(BATCH, nc),
        in_specs=[rows(padded), rows(padded),
                  pl.BlockSpec((RET_QK_W, RET_ROWS), lambda b, n: (0, b * nc + n)),
                  rows(RET_V_W), rows(RET_V_W),
                  _const_spec((RET_HEADS, RET_CHUNK, RET_CHUNK)),
                  _const_spec((RET_CHUNK, RET_V_W)),
                  _const_spec((RET_HEADS, RET_DK, RET_DV)),
                  _const_spec((1, RET_V_W)), _const_spec((1, RET_V_W))],
        out_specs=rows(RET_V_W),
        scratch_shapes=[pltpu.VMEM((RET_HEADS, RET_DK, RET_DV), F32)],
        compiler_params=_cparams(("parallel", "arbitrary")),
        name="retention",
    )(rq, rk, rkdt, rv, rg, din, qdec, cdec, gn_g, gn_b)


def _mematt_kernel(q_ref, mem_ref, w_ref, o_ref, kv_ref):
    @pl.when(pl.program_id(1) == 0)
    def _():
        kv_ref[...] = _dot(mem_ref[...].astype(BF16), w_ref[...]).astype(BF16)

    q = q_ref[...]
    for h in range(MEM_HEADS):
        hs = slice(h * MEM_DH, (h + 1) * MEM_DH)
        logits = _dot_nt(q[:, hs], kv_ref[:, hs]) * (MEM_DH ** -0.5)
        m = jnp.max(logits, axis=-1, keepdims=True)
        p = jnp.exp(logits - m)
        denom = jnp.sum(p, axis=-1, keepdims=True)
        o_ref[:, hs] = (_dot(p.astype(BF16), kv_ref[:, MEM_W + h * MEM_DH:MEM_W + (h + 1) * MEM_DH])
                        / denom).astype(BF16)


def _mematt(mq, mem2d, w_kv_bf):
    tiles = SEQ // MEMATT_ROWS
    return pl.pallas_call(
        _mematt_kernel,
        out_shape=jax.ShapeDtypeStruct((N_TOK, MEM_W), BF16),
        grid=(BATCH, tiles),
        in_specs=[pl.BlockSpec((MEMATT_ROWS, MEM_W), lambda b, i: (b * tiles + i, 0)),
                  pl.BlockSpec((MEM_TOKENS, D_MODEL), lambda b, i: (b, 0)),
                  _const_spec((D_MODEL, 2 * MEM_W))],
        out_specs=pl.BlockSpec((MEMATT_ROWS, MEM_W), lambda b, i: (b * tiles + i, 0)),
        scratch_shapes=[pltpu.VMEM((MEM_TOKENS, 2 * MEM_W), BF16)],
        compiler_params=_cparams(("parallel", "arbitrary")),
        name="mem_attention",
    )(mq, mem2d, w_kv_bf)


def _t5_bias_kernel(table_ref, o_ref, bmax_ref):
    m = lax.broadcasted_iota(jnp.int32, (SUBLANES, 2 * DSA_T), 1)
    max_exact = T5_BUCKETS // 2
    head_max = [None] * DSA_HEADS
    for w in range(2):
        n = jnp.maximum(m - w * DSA_T, 0)
        nf = jnp.maximum(n, 1).astype(F32)
        large = max_exact + jnp.floor(jnp.log(nf / max_exact) / math.log(T5_MAX_DIST / max_exact)
                                      * (T5_BUCKETS - max_exact)).astype(jnp.int32)
        large = jnp.minimum(large, T5_BUCKETS - 1)
        bucket = jnp.where(n < max_exact, n, large)
        for h in range(DSA_HEADS):
            vec = jnp.zeros((SUBLANES, 2 * DSA_T), F32)
            for b in range(T5_BUCKETS):
                vec = jnp.where(bucket == b, table_ref[b, h] * LOG2E, vec)
            rows = jnp.broadcast_to(vec[0:1, :], (DSA_T, 2 * DSA_T))
            o_ref[w, h] = pltpu.roll(rows, 0, 1, stride=1, stride_axis=0)[:, DSA_T:]
            top = jnp.max(vec[0:1, :], axis=1, keepdims=True)
            head_max[h] = top if head_max[h] is None else jnp.maximum(head_max[h], top)
    for h in range(DSA_HEADS):
        bmax_ref[h:h + 1, :] = jnp.broadcast_to(head_max[h], (1, LANES))


def _t5_bias(t5_table):
    return pl.pallas_call(
        _t5_bias_kernel,
        out_shape=(jax.ShapeDtypeStruct((2, DSA_HEADS, DSA_T, DSA_T), F32),
                   jax.ShapeDtypeStruct((DSA_HEADS, LANES), F32)),
        in_specs=[pl.BlockSpec(memory_space=pltpu.SMEM)],
        out_specs=(pl.BlockSpec(memory_space=pltpu.VMEM), pl.BlockSpec(memory_space=pltpu.VMEM)),
        compiler_params=pltpu.CompilerParams(vmem_limit_bytes=VMEM_LIMIT),
        name="t5_bias",
    )(t5_table)


def _fold_groups(x, op):
    return op(x.reshape(DSA_T // SUBLANES, SUBLANES, DSA_T), axis=0)


def _dsa_kernel(table_ref, bias_ref, bmax_ref, iq_ref, dq_ref, ikw_q_ref, ikb_ref, dk_ref, dvt_ref,
                o_ref, score_ref, logit_ref, acc_ref):
    qb = pl.program_id(1)
    q0 = qb * DSA_T
    n_chunks = qb + 1
    groups = DSA_T // SUBLANES
    key_in_chunk = lax.broadcasted_iota(jnp.int32, (DSA_T, DSA_T), 0)
    query_in_step = lax.broadcasted_iota(jnp.int32, (DSA_T, DSA_T), 1)

    def key_rows(c):
        return pl.ds(pl.multiple_of(c * DSA_T, DSA_T), DSA_T)

    def pair_rows(j):
        return slice(j * LANES, (j + 1) * LANES)

    def pair_products(keys_even, keys_odd, queries_t):
        prod = _dot(jnp.concatenate([keys_even, keys_odd], axis=0), queries_t)
        return prod[:DSA_T], prod[DSA_T:]

    iw_t = ikw_q_ref[...].T[IDX_DIM:IDX_DIM + IDX_HEADS, :] * IW_SCALE

    def scores_and_logits(c, carry, causal):
        raw_max, s_min, s_max = carry
        ik_low = ikb_ref[key_rows(c), 0:LANES]
        ik_high = ikb_ref[key_rows(c), LANES:2 * LANES]
        last = IDX_HEADS // 2 - 1
        for j in range(IDX_HEADS // 2):
            rel_even, rel_odd = pair_products(ik_low, ik_high, iq_ref[pair_rows(j), :])
            term = (jnp.maximum(rel_even, 0.0) * iw_t[2 * j:2 * j + 1, :]
                    + jnp.maximum(rel_odd, 0.0) * iw_t[2 * j + 1:2 * j + 2, :])
            total = term if j == 0 else score_ref[c] + term
            if j < last:
                score_ref[c] = total
            elif causal is None:
                score_ref[c] = total
                s_min = jnp.minimum(s_min, _fold_groups(total, jnp.min))
                s_max = jnp.maximum(s_max, _fold_groups(total, jnp.max))
            else:
                score_ref[c] = jnp.where(causal, total, NEG_INF)
                s_min = jnp.minimum(s_min, _fold_groups(jnp.where(causal, total, -NEG_INF), jnp.min))
                s_max = jnp.maximum(s_max, _fold_groups(jnp.where(causal, total, NEG_INF), jnp.max))
        out = list(raw_max)
        for j in range(DSA_HEADS // 2):
            keys = [dk_ref[key_rows(c), (2 * j + i) * HEAD_PAD:(2 * j + i + 1) * HEAD_PAD] for i in range(2)]
            for h, qk in zip((2 * j, 2 * j + 1), pair_products(keys[0], keys[1], dq_ref[pair_rows(j), :])):
                logit_ref[h, c] = qk
                out[h] = jnp.maximum(out[h], _fold_groups(qk, jnp.max))
        return tuple(out), s_min, s_max

    def paired_loop(start, stop, body, carry):
        n = stop - start
        carry = lax.fori_loop(
            0, lax.shift_right_logical(n, 1),
            lambda i, cr: body(start + 2 * i + 1, body(start + 2 * i, cr)), carry)
        return lax.cond(jnp.bitwise_and(n, 1) == 1, lambda cr: body(stop - 1, cr), lambda cr: cr, carry)

    low_init = jnp.full((SUBLANES, DSA_T), NEG_INF, F32)
    carry = (tuple(low_init for _ in range(DSA_HEADS)), -low_init, low_init)
    carry = paired_loop(0, qb, lambda c, cr: scores_and_logits(c, cr, None), carry)
    raw_max, s_min, s_max = scores_and_logits(qb, carry, key_in_chunk <= query_in_step)

    def count_where(pred):
        def body(c, acc):
            parts = [acc] + [jnp.zeros((SUBLANES, DSA_T), F32)] * (COUNT_ACCUMULATORS - 1)
            for g in range(groups):
                blk = score_ref[c, g * SUBLANES:(g + 1) * SUBLANES, :]
                a = g % COUNT_ACCUMULATORS
                parts[a] = parts[a] + jnp.where(pred(c, g * SUBLANES, blk), 1.0, 0.0)
            while len(parts) > 1:
                parts = [parts[i] + parts[i + 1] for i in range(0, len(parts), 2)]
            return parts[0]
        acc = lax.fori_loop(0, n_chunks, body, jnp.zeros((SUBLANES, DSA_T), F32))
        return jnp.sum(acc, axis=0, keepdims=True)

    def count_ge(thr):
        thr_b = jnp.broadcast_to(thr, (SUBLANES, DSA_T))
        return count_where(lambda c, r0, blk: blk >= thr_b)

    q_min = jnp.min(s_min, axis=0, keepdims=True)
    q_max = jnp.max(s_max, axis=0, keepdims=True)
    t_q = q0 + lax.broadcasted_iota(jnp.int32, (1, DSA_T), 1)

    kf = float(TOPK)
    init = (q_min, q_max + jnp.maximum(jnp.abs(q_max), 1.0) * 1e-6,
            (t_q + 1).astype(F32), jnp.zeros((1, DSA_T), F32))

    def bisect_step(_i, carry):
        lo, hi, cnt_lo, cnt_hi = carry
        mid = 0.5 * (lo + hi)
        cnt = count_ge(mid)
        ge = cnt >= kf
        return (jnp.where(ge, mid, lo), jnp.where(ge, hi, mid),
                jnp.where(ge, cnt, cnt_lo), jnp.where(ge, cnt_hi, cnt))

    n_plain = jnp.where(qb == 0, 0, BISECT_PLAIN_STEPS)
    lo, hi, cnt_lo, cnt_hi = lax.fori_loop(0, n_plain, bisect_step, init)

    big = -NEG_INF

    def snap_body(c, carry):
        mn, mx = carry
        blk = score_ref[c]
        mn = jnp.minimum(mn, _fold_groups(jnp.where(blk >= lo, blk, big), jnp.min))
        mx = jnp.maximum(mx, _fold_groups(jnp.where(blk < hi, blk, -big), jnp.max))
        return mn, mx

    mn, mx = lax.fori_loop(
        0, n_chunks, snap_body,
        (jnp.full((SUBLANES, DSA_T), big, F32), jnp.full((SUBLANES, DSA_T), -big, F32)))
    lo = jnp.min(mn, axis=0, keepdims=True)
    top = jnp.max(mx, axis=0, keepdims=True)

    def split_pair(carry):
        lo, top, cnt_lo, cnt_hi = carry
        pair = (cnt_lo > kf) & (lo < top) & (cnt_lo - cnt_hi == 2.0)
        return jnp.where(pair, top, lo), top, jnp.where(pair, cnt_hi + 1.0, cnt_lo), cnt_hi

    def pending_of(carry):
        lo, top, cnt_lo, _ = carry
        return jnp.max(jnp.where((cnt_lo > kf) & (lo < top), 1.0, 0.0))

    def value_step(state):
        (lo, top, cnt_lo, cnt_hi), _ = state
        unresolved = (cnt_lo > kf) & (lo < top)
        mid = lo + (top - lo) * 0.5
        mid = jnp.minimum(jnp.where(mid > lo, mid, top), top)

        def body(c, carry):
            cnt, mn, mx = carry
            blk = score_ref[c]
            is_ge = blk >= mid
            cnt = cnt + _fold_groups(jnp.where(is_ge, 1.0, 0.0), jnp.sum)
            mn = jnp.minimum(mn, _fold_groups(jnp.where(is_ge, blk, big), jnp.min))
            mx = jnp.maximum(mx, _fold_groups(jnp.where(is_ge, -big, blk), jnp.max))
            return cnt, mn, mx

        cnt, mn, mx = lax.fori_loop(
            0, n_chunks, body,
            (jnp.zeros((SUBLANES, DSA_T), F32), jnp.full((SUBLANES, DSA_T), big, F32),
             jnp.full((SUBLANES, DSA_T), -big, F32)))
        cnt = jnp.sum(cnt, axis=0, keepdims=True)
        ge = cnt >= kf
        raise_lo = unresolved & ge
        lower_top = unresolved & jnp.logical_not(ge)
        carry = split_pair((jnp.where(raise_lo, jnp.min(mn, axis=0, keepdims=True), lo),
                            jnp.where(lower_top, jnp.max(mx, axis=0, keepdims=True), top),
                            jnp.where(raise_lo, cnt, cnt_lo), jnp.where(lower_top, cnt, cnt_hi)))
        return carry, pending_of(carry)

    start = split_pair((lo, top, cnt_lo, cnt_hi))
    (lo, top, cnt_lo, cnt_hi), _ = lax.while_loop(
        lambda state: state[1] > 0.5, value_step, (start, pending_of(start)))

    tied = cnt_lo > kf
    need = kf - cnt_hi
    lo_b = jnp.broadcast_to(lo, (SUBLANES, DSA_T))
    key_in_group = lax.broadcasted_iota(jnp.int32, (SUBLANES, DSA_T), 0)

    def tie_search(_):
        def body(_i, carry):
            lo_i, hi_i = carry
            mid_i = lax.shift_right_arithmetic(lo_i + hi_i, 1)
            mid_b = jnp.broadcast_to(mid_i, (SUBLANES, DSA_T))
            cnt = count_where(lambda c, r0, blk: (blk == lo_b) & (key_in_group + (c * DSA_T + r0) <= mid_b))
            enough = cnt >= need
            return jnp.where(enough, lo_i, mid_i), jnp.where(enough, mid_i, hi_i)
        lo_i = jnp.full((1, DSA_T), -1, jnp.int32)
        hi_i = jnp.full((1, DSA_T), SEQ - 1, jnp.int32)
        _, cut = lax.fori_loop(0, int(math.log2(SEQ)) + 1, body, (lo_i, hi_i))
        return jnp.where(tied, cut, SEQ)

    cut = lax.cond(jnp.max(jnp.where(tied, 1.0, 0.0)) > 0.5, tie_search,
                   lambda _: jnp.full((1, DSA_T), SEQ, jnp.int32), 0)

    def write_mask(c):
        blk = score_ref[c]
        keep = (blk > lo) | ((blk == lo) & (key_in_chunk + c * DSA_T <= cut))
        score_ref[c] = jnp.where(keep, 0.0, NEG_INF)

    n_far = jnp.maximum(qb - 1, 0)

    def masked_logits(c, h, far):
        lg = logit_ref[h, c] + score_ref[c]
        return lg if far else lg + bias_ref[c - qb + 1, h]

    def far_bias(h):
        return table_ref[T5_BUCKETS - 1, h] * LOG2E

    def den_row(h):
        return h * DVT_HEAD_ROWS + DSA_DH

    def exp_and_pv(shifts, scores_to_masks):
        acc_ref[...] = jnp.zeros_like(acc_ref)

        def body(c, carry, far):
            if scores_to_masks:
                write_mask(c)
            for h in range(DSA_HEADS):
                shift = shifts[h] - far_bias(h) if far else shifts[h]
                p = jnp.exp2(masked_logits(c, h, far) - shift).astype(BF16)
                rows = slice(h * DVT_HEAD_ROWS, (h + 1) * DVT_HEAD_ROWS)
                acc_ref[rows, :] += _dot(dvt_ref[c, rows, :], p)
            return carry

        paired_loop(0, n_far, lambda c, carry: body(c, carry, True), 0)
        paired_loop(n_far, n_chunks, lambda c, carry: body(c, carry, False), 0)

    upper = [jnp.max(raw_max[h], axis=0, keepdims=True) + bmax_ref[h:h + 1, 0:1] for h in range(DSA_HEADS)]
    exp_and_pv(upper, True)

    den_min = acc_ref[den_row(0):den_row(0) + 1, :]
    for h in range(1, DSA_HEADS):
        den_min = jnp.minimum(den_min, acc_ref[den_row(h):den_row(h) + 1, :])

    @pl.when(jnp.min(den_min) < DEN_FLOOR)
    def _():
        def max_body(c, ms, far):
            return tuple(jnp.maximum(ms[h], _fold_groups(masked_logits(c, h, far), jnp.max) + (far_bias(h) if far else 0.0))
                         for h in range(DSA_HEADS))
        ms = tuple(jnp.full((SUBLANES, DSA_T), NEG_INF, F32) for _ in range(DSA_HEADS))
        ms = lax.fori_loop(0, n_far, lambda c, ms: max_body(c, ms, True), ms)
        ms = lax.fori_loop(n_far, n_chunks, lambda c, ms: max_body(c, ms, False), ms)
        exp_and_pv([jnp.max(m, axis=0, keepdims=True) for m in ms], False)

    heads = []
    for h in range(DSA_HEADS):
        r0 = h * DVT_HEAD_ROWS
        heads.append(acc_ref[r0:r0 + DSA_DH, :] / acc_ref[den_row(h):den_row(h) + 1, :])
    o_ref[...] = jnp.concatenate(heads, axis=0).T.astype(BF16)


def _dsa(t5_table, bias, bias_max, iq, dq, ikw, ikb, dk, dvt):
    nqb = SEQ // DSA_T
    padded = DSA_HEADS * HEAD_PAD
    assert DSA_T <= TOPK
    qrows = lambda w: pl.BlockSpec((DSA_T, w), lambda b, i: (b * nqb + i, 0))
    qcols = lambda h: pl.BlockSpec((h, DSA_T), lambda b, i: (0, b * nqb + i))
    krows = lambda w: pl.BlockSpec((SEQ, w), lambda b, i: (b, 0))
    return pl.pallas_call(
        _dsa_kernel,
        out_shape=jax.ShapeDtypeStruct((N_TOK, DSA_W), BF16),
        grid=(BATCH, nqb),
        in_specs=[pl.BlockSpec(memory_space=pltpu.SMEM),
                  _const_spec((2, DSA_HEADS, DSA_T, DSA_T)), _const_spec((DSA_HEADS, LANES)),
                  qcols(IDX_Q_W), qcols(DSA_W), qrows(LANES),
                  krows(2 * LANES), krows(padded),
                  pl.BlockSpec((nqb, DVT_ROWS, DSA_T), lambda b, i: (b, 0, 0))],
        out_specs=qrows(DSA_W),
        scratch_shapes=[pltpu.VMEM((nqb, DSA_T, DSA_T), F32),
                        pltpu.VMEM((DSA_HEADS, nqb, DSA_T, DSA_T), F32),
                        pltpu.VMEM((DVT_ROWS, DSA_T), F32)],
        compiler_params=_cparams(("parallel", "arbitrary")),
        name="dsa_attention",
    )(t5_table, bias, bias_max, iq, dq, ikw, ikb, dk, dvt)


def _merge_kernel(h_ref, oret_ref, odsa_ref, omem_ref, wg_ref, wr_ref, wd_ref, wm_ref, wo_ref,
                  g_ref, b_ref, o_ref):
    h = h_ref[...]
    hb = h.astype(BF16)
    merged = None
    for i, (src, w) in enumerate(((oret_ref, wr_ref), (odsa_ref, wd_ref), (omem_ref, wm_ref))):
        gate = _sigmoid(_dot(hb, wg_ref[:, i * D_MODEL:(i + 1) * D_MODEL]))
        term = gate * _dot(src[...], w[...])
        merged = term if merged is None else merged + term
    mix = _dot(merged.astype(BF16), wo_ref[...])
    o_ref[...] = _layer_norm(ALPHA * h + mix, g_ref[...], b_ref[...])


def _merge(h, o_ret, o_dsa, o_mem, wg, wr, wd, wm, wo, g, b):
    rows = lambda w: pl.BlockSpec((MERGE_ROWS, w), lambda i: (i, 0))
    return pl.pallas_call(
        _merge_kernel,
        out_shape=jax.ShapeDtypeStruct((N_TOK, D_MODEL), F32),
        grid=(N_TOK // MERGE_ROWS,),
        in_specs=[rows(D_MODEL), rows(RET_V_W), rows(DSA_W), rows(MEM_W),
                  _const_spec((D_MODEL, 3 * D_MODEL)),
                  _const_spec((RET_V_W, D_MODEL)), _const_spec((DSA_W, D_MODEL)),
                  _const_spec((MEM_W, D_MODEL)), _const_spec((D_MODEL, D_MODEL)),
                  _const_spec((1, D_MODEL)), _const_spec((1, D_MODEL))],
        out_specs=rows(D_MODEL),
        compiler_params=_cparams(("parallel",)),
        name="merge_ln2",
    )(h, o_ret, o_dsa, o_mem, wg, wr, wd, wm, wo, g, b)


def _rope_tables():
    half = RET_DK // 2
    freqs = ROPE_BASE ** (-jnp.arange(half, dtype=F32) / half)
    ang = jnp.arange(SEQ).astype(F32)[:, None] * freqs[None, :]
    cos, sin = jnp.cos(ang), jnp.sin(ang)
    cos_t = jnp.tile(jnp.concatenate([cos, cos], axis=-1), (1, RET_HEADS))
    sin_t = jnp.tile(jnp.concatenate([-sin, sin], axis=-1), (1, RET_HEADS))
    return cos_t, sin_t


def _decay_tables():
    c = RET_CHUNK
    gamma = 1.0 - 2.0 ** (-5.0 - jnp.arange(RET_HEADS, dtype=F32))
    lg = jnp.log(gamma)
    i = jnp.arange(c)
    diff = i[:, None] - i[None, :]
    decay_in = jnp.where(diff[None] >= 0, jnp.exp(jnp.maximum(diff, 0)[None] * lg[:, None, None]), 0.0).astype(F32)
    k_dec = jnp.exp((c - 1 - i)[None, :] * lg[:, None]).astype(F32)
    q_dec = jnp.exp((i + 1)[None, :] * lg[:, None]).astype(F32)
    chunk_dec = jnp.exp(c * lg).astype(F32)
    kdec_t = jnp.repeat(k_dec.T, RET_DK, axis=1)
    qdec_t = jnp.repeat(q_dec.T, RET_DV, axis=1)
    cdec_t = jnp.broadcast_to(chunk_dec[:, None, None], (RET_HEADS, RET_DK, RET_DV))
    return decay_in, kdec_t, qdec_t, cdec_t


def _layer(x, mem2d, ffn1_w_in, ffn1_w_out, ln1_g, ln1_b, w_in, t5_table, ret_gn_g, ret_gn_b,
           w_mem_kv, w_br_ret, w_br_dsa, w_br_mem, w_out, ln2_g, ln2_b,
           ffn2_w_in, ffn2_w_out, ln3_g, ln3_b, tables):
    cos_t, sin_t, din, kdec, qdec, cdec = tables
    row = lambda v: v.reshape(1, -1)
    bf = lambda w: w.astype(BF16)

    w_in_bf = bf(w_in)
    w_mq = w_in_bf[:, W_MQ0:W_G0]
    w_dvt = w_in_bf[:, W_DV0:W_IQ0].T
    kdec_rows = jnp.tile(kdec, (PROJ_ROWS // RET_CHUNK, 1))

    h = _ffn_ln(x, bf(ffn1_w_in), bf(ffn1_w_out), row(ln1_g), row(ln1_b), "ffn1_ln1")
    rq, rk, rkdt, rv, rg, dq, dk, dvt, iq, mq, ikw, ikb = _proj(h, w_in_bf, w_mq, w_dvt, cos_t, sin_t, kdec_rows)
    o_ret = _retention(rq, rk, rkdt, rv, rg, din, qdec, cdec, row(ret_gn_g), row(ret_gn_b))
    o_mem = _mematt(mq, mem2d, bf(w_mem_kv))
    o_dsa = _dsa(t5_table, *_t5_bias(t5_table), iq, dq, ikw, ikb, dk, dvt)
    x2 = _merge(h, o_ret, o_dsa, o_mem, w_in_bf[:, W_G0:], bf(w_br_ret), bf(w_br_dsa), bf(w_br_mem),
                bf(w_out), row(ln2_g), row(ln2_b))
    return _ffn_ln(x2, bf(ffn2_w_in), bf(ffn2_w_out), row(ln3_g), row(ln3_b), "ffn2_ln3")


def kernel(x, mem, ffn1_w_in, ffn1_w_out, ln1_g, ln1_b, w_in, t5_table, ret_gn_g, ret_gn_b,
           w_mem_kv, w_br_ret, w_br_dsa, w_br_mem, w_out, ln2_g, ln2_b,
           ffn2_w_in, ffn2_w_out, ln3_g, ln3_b):
    assert x.shape == (BATCH, SEQ, D_MODEL) and mem.shape == (BATCH, MEM_TOKENS, D_MODEL)
    tables = _rope_tables() + _decay_tables()
    y = x.reshape(N_TOK, D_MODEL)
    mem2d = mem.reshape(BATCH * MEM_TOKENS, D_MODEL)
    for l in range(DEPTH):
        y = _layer(y, mem2d, ffn1_w_in[l], ffn1_w_out[l], ln1_g[l], ln1_b[l], w_in[l], t5_table,
                   ret_gn_g[l], ret_gn_b[l], w_mem_kv[l], w_br_ret[l], w_br_dsa[l], w_br_mem[l],
                   w_out[l], ln2_g[l], ln2_b[l], ffn2_w_in[l], ffn2_w_out[l], ln3_g[l], ln3_b[l],
                   tables)
    return y.reshape(BATCH, SEQ, D_MODEL)
```

```python
import math

import jax
import jax.numpy as jnp
from jax import lax
from jax.experimental import pallas as pl
from jax.experimental.pallas import tpu as pltpu

F32 = jnp.float32
BF16 = jnp.bfloat16

D_MODEL = 1024
BATCH = 8
SEQ = 2048
MEM_TOKENS = 256
RET_HEADS, RET_DK, RET_DV, RET_CHUNK = 4, 64, 128, 128
DSA_HEADS, DSA_DH = 8, 64
IDX_HEADS, IDX_DIM = 8, 64
TOPK = min(256, SEQ // 4)
MEM_HEADS, MEM_DH = 4, 128
T5_BUCKETS, T5_MAX_DIST = 32, 128
D_FF = 2816
ROPE_BASE = 10000.0
LN_EPS = 1e-5
NEG_INF = -1e30
DEPTH = 1
ALPHA = (2.0 * DEPTH) ** 0.25

RET_QK_W = RET_HEADS * RET_DK
RET_V_W = RET_HEADS * RET_DV
DSA_W = DSA_HEADS * DSA_DH
IDX_Q_W = IDX_HEADS * IDX_DIM
MEM_W = MEM_HEADS * MEM_DH
N_TOK = BATCH * SEQ

V7X_VMEM_BYTES = 64 * 1024 * 1024
VMEM_LIMIT = V7X_VMEM_BYTES - 8 * 1024 * 1024
LANES = 128
SUBLANES = 8

FFN_ROWS = 1024
MXU_TILE = 256
FFN_CHUNK_EDGES = (0, 6 * MXU_TILE, D_FF)
PROJ_ROWS = 512
MERGE_ROWS = 1024
RET_ROWS = 512
MEMATT_ROWS = 512
DSA_T = 256
HEAD_PAD = LANES
DVT_HEAD_ROWS = DSA_DH + 16
DVT_ROWS = DSA_HEADS * DVT_HEAD_ROWS
BISECT_PLAIN_STEPS = 16
COUNT_ACCUMULATORS = 4
LOG2E = math.log2(math.e)
DEN_FLOOR = 2.0 ** -100

W_RQ0 = 0
W_RK0 = W_RQ0 + RET_QK_W
W_RV0 = W_RK0 + RET_QK_W
W_RG0 = W_RV0 + RET_V_W
W_DQ0 = W_RG0 + RET_V_W
W_DK0 = W_DQ0 + DSA_W
W_DV0 = W_DK0 + DSA_W
W_IQ0 = W_DV0 + DSA_W
W_IK0 = W_IQ0 + IDX_Q_W
W_IW0 = W_IK0 + IDX_DIM
W_MQ0 = W_IW0 + IDX_HEADS
W_G0 = W_MQ0 + MEM_W
IW_SCALE = IDX_HEADS ** -0.5 * IDX_DIM ** -0.5


def _cparams(sem):
    return pltpu.CompilerParams(dimension_semantics=sem, vmem_limit_bytes=VMEM_LIMIT)


def _const_spec(shape):
    nd = len(shape)
    return pl.BlockSpec(shape, lambda *_: (0,) * nd, pipeline_mode=pl.Buffered(1))


def _layer_norm(y, g, b):
    mu = jnp.mean(y, axis=-1, keepdims=True)
    yc = y - mu
    var = jnp.mean(yc * yc, axis=-1, keepdims=True)
    return yc * lax.rsqrt(var + LN_EPS) * g + b


def _sigmoid(x):
    return 0.5 * jnp.tanh(0.5 * x) + 0.5


def _dot(a, b):
    return jnp.dot(a, b, preferred_element_type=F32)


def _dot_nt(a, b):
    return lax.dot_general(a, b, (((1,), (1,)), ((), ())), preferred_element_type=F32)


def _ffn_ln_kernel(x_ref, wi_ref, wo_ref, g_ref, b_ref, o_ref):
    x = x_ref[...]
    xb = x.astype(BF16)
    acc = None
    for lo, hi in zip(FFN_CHUNK_EDGES[:-1], FFN_CHUNK_EDGES[1:]):
        a = _dot(xb, wi_ref[:, lo:hi])
        u = _dot(xb, wi_ref[:, D_FF + lo:D_FF + hi])
        act = (a * _sigmoid(a) * u).astype(BF16)
        part = _dot(act, wo_ref[lo:hi, :])
        acc = part if acc is None else acc + part
    o_ref[...] = _layer_norm(ALPHA * x + 0.5 * acc, g_ref[...], b_ref[...])


def _ffn_ln(x, w_in_bf, w_out_bf, g, b, name):
    return pl.pallas_call(
        _ffn_ln_kernel,
        out_shape=jax.ShapeDtypeStruct((N_TOK, D_MODEL), F32),
        grid=(N_TOK // FFN_ROWS,),
        in_specs=[
            pl.BlockSpec((FFN_ROWS, D_MODEL), lambda i: (i, 0)),
            _const_spec((D_MODEL, 2 * D_FF)),
            _const_spec((D_FF, D_MODEL)),
            _const_spec((1, D_MODEL)),
            _const_spec((1, D_MODEL)),
        ],
        out_specs=pl.BlockSpec((FFN_ROWS, D_MODEL), lambda i: (i, 0)),
        compiler_params=_cparams(("parallel",)),
        name=name,
    )(x, w_in_bf, w_out_bf, g, b)


def _rope(x, cos, sin_signed):
    width = x.shape[-1]
    lane = lax.broadcasted_iota(jnp.int32, x.shape, 1)
    first_half = (lane % RET_DK) < (RET_DK // 2)
    swapped = jnp.where(first_half,
                        pltpu.roll(x, width - RET_DK // 2, 1),
                        pltpu.roll(x, RET_DK // 2, 1))
    return x * cos + swapped * sin_signed


def _store_split_heads(dst_ref, val):
    low = lax.broadcasted_iota(jnp.int32, (val.shape[0], LANES), 1) < DSA_DH
    for j in range(val.shape[1] // LANES):
        pair = val[:, j * LANES:(j + 1) * LANES]
        dst_ref[:, (2 * j) * LANES:(2 * j + 1) * LANES] = jnp.where(low, pair, 0.0).astype(dst_ref.dtype)
        dst_ref[:, (2 * j + 1) * LANES:(2 * j + 2) * LANES] = jnp.where(low, 0.0, pair).astype(dst_ref.dtype)


def _store_low_heads(dst_ref, val):
    low = lax.broadcasted_iota(jnp.int32, (val.shape[0], LANES), 1) < RET_DK
    for j in range(val.shape[1] // LANES):
        pair = val[:, j * LANES:(j + 1) * LANES]
        dst_ref[:, (2 * j) * LANES:(2 * j + 1) * LANES] = jnp.where(low, pair, 0.0).astype(dst_ref.dtype)
        dst_ref[:, (2 * j + 1) * LANES:(2 * j + 2) * LANES] = jnp.where(
            low, pltpu.roll(pair, RET_DK, 1), 0.0).astype(dst_ref.dtype)


def _proj_kernel(h_ref, w_ref, wikw_ref, wmq_ref, wdvt_ref, cos_ref, sin_ref, kdec_ref,
                 rq_ref, rk_ref, rkdt_ref, rv_ref, rg_ref, dq_ref, dk_ref, dvt_ref, iq_ref, mq_ref,
                 ikw_ref, ikb_ref):
    hb = h_ref[...].astype(BF16)

    def proj(lo, width):
        return _dot(hb, w_ref[:, lo:lo + width])

    cos = cos_ref[...]
    sin = sin_ref[...]
    _store_low_heads(rq_ref, _rope(proj(W_RQ0, RET_QK_W), cos, sin))
    rk = _rope(proj(W_RK0, RET_QK_W), cos, sin) * (RET_DK ** -0.5)
    _store_low_heads(rk_ref, rk)
    rkdt_ref[...] = (rk * kdec_ref[...]).T.astype(BF16)
    rv_ref[...] = proj(W_RV0, RET_V_W).astype(BF16)
    rg_ref[...] = proj(W_RG0, RET_V_W)
    dq_ref[...] = (proj(W_DQ0, DSA_W) * (DSA_DH ** -0.5 * LOG2E)).T.astype(BF16)
    _store_split_heads(dk_ref, proj(W_DK0, DSA_W))
    iq_ref[...] = proj(W_IQ0, IDX_Q_W).T.astype(BF16)
    mq_ref[...] = _dot(hb, wmq_ref[...]).astype(BF16)
    ikw = _dot(hb, wikw_ref[...])
    ikw_ref[...] = ikw
    low = lax.broadcasted_iota(jnp.int32, ikw.shape, 1) < IDX_DIM
    ik_low = jnp.where(low, ikw, 0.0)
    ikb_ref[:, 0:LANES] = ik_low.astype(BF16)
    ikb_ref[:, LANES:2 * LANES] = pltpu.roll(ik_low, IDX_DIM, 1).astype(BF16)
    vt = _dot_nt(wdvt_ref[...], hb).astype(BF16)
    ones = jnp.ones((DVT_HEAD_ROWS - DSA_DH, DSA_T), BF16)
    for j in range(PROJ_ROWS // DSA_T):
        for h in range(DSA_HEADS):
            r0 = h * DVT_HEAD_ROWS
            dvt_ref[j, r0:r0 + DSA_DH, :] = vt[h * DSA_DH:(h + 1) * DSA_DH, j * DSA_T:(j + 1) * DSA_T]
            dvt_ref[j, r0 + DSA_DH:r0 + DVT_HEAD_ROWS, :] = ones


def _proj(h, w_in_bf, w_mq_bf, w_dvt_bf, cos_t, sin_t, kdec_rows):
    rows = lambda w: pl.BlockSpec((PROJ_ROWS, w), lambda i: (i, 0))
    seq_tiles = SEQ // PROJ_ROWS
    pos = lambda w: pl.BlockSpec((PROJ_ROWS, w), lambda i: (i % seq_tiles, 0))
    sd = jax.ShapeDtypeStruct
    ret_padded = RET_HEADS * HEAD_PAD
    dsa_padded = DSA_HEADS * HEAD_PAD
    slabs = PROJ_ROWS // DSA_T
    assert W_IK0 % LANES == 0
    return pl.pallas_call(
        _proj_kernel,
        out_shape=(
            sd((N_TOK, ret_padded), BF16), sd((N_TOK, ret_padded), BF16), sd((RET_QK_W, N_TOK), BF16),
            sd((N_TOK, RET_V_W), BF16), sd((N_TOK, RET_V_W), F32),
            sd((DSA_W, N_TOK), BF16), sd((N_TOK, dsa_padded), BF16),
            sd((N_TOK // DSA_T, DVT_ROWS, DSA_T), BF16),
            sd((IDX_Q_W, N_TOK), BF16), sd((N_TOK, MEM_W), BF16),
            sd((N_TOK, LANES), F32), sd((N_TOK, 2 * LANES), BF16),
        ),
        grid=(N_TOK // PROJ_ROWS,),
        in_specs=[rows(D_MODEL),
                  pl.BlockSpec((D_MODEL, W_IK0), lambda i: (0, 0), pipeline_mode=pl.Buffered(1)),
                  pl.BlockSpec((D_MODEL, LANES), lambda i: (0, W_IK0 // LANES), pipeline_mode=pl.Buffered(1)),
                  _const_spec((D_MODEL, MEM_W)), _const_spec((DSA_W, D_MODEL)),
                  pos(RET_QK_W), pos(RET_QK_W), _const_spec((PROJ_ROWS, RET_QK_W))],
        out_specs=(rows(ret_padded), rows(ret_padded),
                   pl.BlockSpec((RET_QK_W, PROJ_ROWS), lambda i: (0, i)),
                   rows(RET_V_W), rows(RET_V_W),
                   pl.BlockSpec((DSA_W, PROJ_ROWS), lambda i: (0, i)), rows(dsa_padded),
                   pl.BlockSpec((slabs, DVT_ROWS, DSA_T), lambda i: (i, 0, 0)),
                   pl.BlockSpec((IDX_Q_W, PROJ_ROWS), lambda i: (0, i)),
                   rows(MEM_W), rows(LANES), rows(2 * LANES)),
        compiler_params=_cparams(("parallel",)),
        name="mixer_proj",
    )(h, w_in_bf, w_in_bf, w_mq_bf, w_dvt_bf, cos_t, sin_t, kdec_rows)


def _retention_kernel(q_ref, k_ref, kdt_ref, v_ref, g_ref, din_ref, qdec_ref, cdec_ref,
                      gng_ref, gnb_ref, o_ref, state_ref):
    @pl.when(pl.program_id(1) == 0)
    def _():
        state_ref[...] = jnp.zeros_like(state_ref)

    qdec = qdec_ref[...]
    gn_g = gng_ref[...]
    gn_b = gnb_ref[...]
    states = [state_ref[h] for h in range(RET_HEADS)]
    pad_rows = jnp.zeros((HEAD_PAD - RET_DK, RET_DV), BF16)
    for j in range(RET_ROWS // RET_CHUNK):
        rows = slice(j * RET_CHUNK, (j + 1) * RET_CHUNK)
        gate = g_ref[rows, :]
        for h in range(RET_HEADS):
            hs = slice(h * HEAD_PAD, (h + 1) * HEAD_PAD)
            vs = slice(h * RET_DV, (h + 1) * RET_DV)
            qh = q_ref[rows, hs]
            vh = v_ref[rows, vs]
            scores = _dot_nt(qh, k_ref[rows, hs]) * din_ref[h]
            intra = _dot(scores.astype(BF16), vh)
            state_padded = jnp.concatenate([states[h].astype(BF16), pad_rows], axis=0)
            cross = _dot(qh, state_padded) * qdec[:, vs]
            kv = _dot(kdt_ref[h * RET_DK:(h + 1) * RET_DK, rows], vh)
            states[h] = cdec_ref[h] * states[h] + kv
            o = intra + cross
            mu = jnp.mean(o, axis=-1, keepdims=True)
            oc = o - mu
            var = jnp.mean(oc * oc, axis=-1, keepdims=True)
            o = oc * lax.rsqrt(var + LN_EPS) * gn_g[:, vs] + gn_b[:, vs]
            o_ref[rows, vs] = (jax.nn.silu(gate[:, vs]) * o).astype(BF16)
    for h in range(RET_HEADS):
        state_ref[h] = states[h]


def _retention(rq, rk, rkdt, rv, rg, din, qdec, cdec, gn_g, gn_b):
    nc = SEQ // RET_ROWS
    rows = lambda w: pl.BlockSpec((RET_ROWS, w), lambda b, n: (b * nc + n, 0))
    padded = RET_HEADS * HEAD_PAD
    return pl.pallas_call(
        _retention_kernel,
        out_shape=jax.ShapeDtypeStruct((N_TOK, RET_V_W), BF16),
        grid=(BATCH, nc),
        in_specs=[rows(padded), rows(padded),
                  pl.BlockSpec((RET_QK_W, RET_ROWS), lambda b, n: (0, b * nc + n)),
                  rows(RET_V_W), rows(RET_V_W),
                  _const_spec((RET_HEADS, RET_CHUNK, RET_CHUNK)),
                  _const_spec((RET_CHUNK, RET_V_W)),
                  _const_spec((RET_HEADS, RET_DK, RET_DV)),
                  _const_spec((1, RET_V_W)), _const_spec((1, RET_V_W))],
        out_specs=rows(RET_V_W),
        scratch_shapes=[pltpu.VMEM((RET_HEADS, RET_DK, RET_DV), F32)],
        compiler_params=_cparams(("parallel", "arbitrary")),
        name="retention",
    )(rq, rk, rkdt, rv, rg, din, qdec, cdec, gn_g, gn_b)


def _mematt_kernel(q_ref, mem_ref, w_ref, o_ref, kv_ref):
    @pl.when(pl.program_id(1) == 0)
    def _():
        kv_ref[...] = _dot(mem_ref[...].astype(BF16), w_ref[...]).astype(BF16)

    q = q_ref[...]
    for h in range(MEM_HEADS):
        hs = slice(h * MEM_DH, (h + 1) * MEM_DH)
        logits = _dot_nt(q[:, hs], kv_ref[:, hs]) * (MEM_DH ** -0.5)
        m = jnp.max(logits, axis=-1, keepdims=True)
        p = jnp.exp(logits - m)
        denom = jnp.sum(p, axis=-1, keepdims=True)
        o_ref[:, hs] = (_dot(p.astype(BF16), kv_ref[:, MEM_W + h * MEM_DH:MEM_W + (h + 1) * MEM_DH])
                        / denom).astype(BF16)


def _mematt(mq, mem2d, w_kv_bf):
    tiles = SEQ // MEMATT_ROWS
    return pl.pallas_call(
        _mematt_kernel,
        out_shape=jax.ShapeDtypeStruct((N_TOK, MEM_W), BF16),
        grid=(BATCH, tiles),
        in_specs=[pl.BlockSpec((MEMATT_ROWS, MEM_W), lambda b, i: (b * tiles + i, 0)),
                  pl.BlockSpec((MEM_TOKENS, D_MODEL), lambda b, i: (b, 0)),
                  _const_spec((D_MODEL, 2 * MEM_W))],
        out_specs=pl.BlockSpec((MEMATT_ROWS, MEM_W), lambda b, i: (b * tiles + i, 0)),
        scratch_shapes=[pltpu.VMEM((MEM_TOKENS, 2 * MEM_W), BF16)],
        compiler_params=_cparams(("parallel", "arbitrary")),
        name="mem_attention",
    )(mq, mem2d, w_kv_bf)


def _t5_bias_kernel(table_ref, o_ref, bmax_ref):
    m = lax.broadcasted_iota(jnp.int32, (SUBLANES, 2 * DSA_T), 1)
    max_exact = T5_BUCKETS // 2
    head_max = [None] * DSA_HEADS
    for w in range(2):
        n = jnp.maximum(m - w * DSA_T, 0)
        nf = jnp.maximum(n, 1).astype(F32)
        large = max_exact + jnp.floor(jnp.log(nf / max_exact) / math.log(T5_MAX_DIST / max_exact)
                                      * (T5_BUCKETS - max_exact)).astype(jnp.int32)
        large = jnp.minimum(large, T5_BUCKETS - 1)
        bucket = jnp.where(n < max_exact, n, large)
        for h in range(DSA_HEADS):
            vec = jnp.zeros((SUBLANES, 2 * DSA_T), F32)
            for b in range(T5_BUCKETS):
                vec = jnp.where(bucket == b, table_ref[b, h] * LOG2E, vec)
            rows = jnp.broadcast_to(vec[0:1, :], (DSA_T, 2 * DSA_T))
            o_ref[w, h] = pltpu.roll(rows, 0, 1, stride=1, stride_axis=0)[:, DSA_T:]
            top = jnp.max(vec[0:1, :], axis=1, keepdims=True)
            head_max[h] = top if head_max[h] is None else jnp.maximum(head_max[h], top)
    for h in range(DSA_HEADS):
        bmax_ref[h:h + 1, :] = jnp.broadcast_to(head_max[h], (1, LANES))


def _t5_bias(t5_table):
    return pl.pallas_call(
        _t5_bias_kernel,
        out_shape=(jax.ShapeDtypeStruct((2, DSA_HEADS, DSA_T, DSA_T), F32),
                   jax.ShapeDtypeStruct((DSA_HEADS, LANES), F32)),
        in_specs=[pl.BlockSpec(memory_space=pltpu.SMEM)],
        out_specs=(pl.BlockSpec(memory_space=pltpu.VMEM), pl.BlockSpec(memory_space=pltpu.VMEM)),
        compiler_params=pltpu.CompilerParams(vmem_limit_bytes=VMEM_LIMIT),
        name="t5_bias",
    )(t5_table)


def _fold_groups(x, op):
    return op(x.reshape(DSA_T // SUBLANES, SUBLANES, DSA_T), axis=0)


def _dsa_kernel(table_ref, bias_ref, bmax_ref, iq_ref, dq_ref, ikw_q_ref, ikb_ref, dk_ref, dvt_ref,
                o_ref, score_ref, logit_ref, acc_ref):
    qb = pl.program_id(1)
    q0 = qb * DSA_T
    n_chunks = qb + 1
    groups = DSA_T // SUBLANES
    key_in_chunk = lax.broadcasted_iota(jnp.int32, (DSA_T, DSA_T), 0)
    query_in_step = lax.broadcasted_iota(jnp.int32, (DSA_T, DSA_T), 1)

    def key_rows(c):
        return pl.ds(pl.multiple_of(c * DSA_T, DSA_T), DSA_T)

    def pair_rows(j):
        return slice(j * LANES, (j + 1) * LANES)

    def pair_products(keys_even, keys_odd, queries_t):
        prod = _dot(jnp.concatenate([keys_even, keys_odd], axis=0), queries_t)
        return prod[:DSA_T], prod[DSA_T:]

    iw_t = ikw_q_ref[...].T[IDX_DIM:IDX_DIM + IDX_HEADS, :] * IW_SCALE

    def scores_and_logits(c, carry, causal):
        raw_max, s_min, s_max = carry
        ik_low = ikb_ref[key_rows(c), 0:LANES]
        ik_high = ikb_ref[key_rows(c), LANES:2 * LANES]
        last = IDX_HEADS // 2 - 1
        for j in range(IDX_HEADS // 2):
            rel_even, rel_odd = pair_products(ik_low, ik_high, iq_ref[pair_rows(j), :])
            term = (jnp.maximum(rel_even, 0.0) * iw_t[2 * j:2 * j + 1, :]
                    + jnp.maximum(rel_odd, 0.0) * iw_t[2 * j + 1:2 * j + 2, :])
            total = term if j == 0 else score_ref[c] + term
            if j < last:
                score_ref[c] = total
            elif causal is None:
                score_ref[c] = total
                s_min = jnp.minimum(s_min, _fold_groups(total, jnp.min))
                s_max = jnp.maximum(s_max, _fold_groups(total, jnp.max))
            else:
                score_ref[c] = jnp.where(causal, total, NEG_INF)
                s_min = jnp.minimum(s_min, _fold_groups(jnp.where(causal, total, -NEG_INF), jnp.min))
                s_max = jnp.maximum(s_max, _fold_groups(jnp.where(causal, total, NEG_INF), jnp.max))
        out = list(raw_max)
        for j in range(DSA_HEADS // 2):
            keys = [dk_ref[key_rows(c), (2 * j + i) * HEAD_PAD:(2 * j + i + 1) * HEAD_PAD] for i in range(2)]
            for h, qk in zip((2 * j, 2 * j + 1), pair_products(keys[0], keys[1], dq_ref[pair_rows(j), :])):
                logit_ref[h, c] = qk
                out[h] = jnp.maximum(out[h], _fold_groups(qk, jnp.max))
        return tuple(out), s_min, s_max

    def paired_loop(start, stop, body, carry):
        n = stop - start
        carry = lax.fori_loop(
            0, lax.shift_right_logical(n, 1),
            lambda i, cr: body(start + 2 * i + 1, body(start + 2 * i, cr)), carry)
        return lax.cond(jnp.bitwise_and(n, 1) == 1, lambda cr: body(stop - 1, cr), lambda cr: cr, carry)

    low_init = jnp.full((SUBLANES, DSA_T), NEG_INF, F32)
    carry = (tuple(low_init for _ in range(DSA_HEADS)), -low_init, low_init)
    carry = paired_loop(0, qb, lambda c, cr: scores_and_logits(c, cr, None), carry)
    raw_max, s_min, s_max = scores_and_logits(qb, carry, key_in_chunk <= query_in_step)

    def count_where(pred):
        def body(c, acc):
            parts = [acc] + [jnp.zeros((SUBLANES, DSA_T), F32)] * (COUNT_ACCUMULATORS - 1)
            for g in range(groups):
                blk = score_ref[c, g * SUBLANES:(g + 1) * SUBLANES, :]
                a = g % COUNT_ACCUMULATORS
                parts[a] = parts[a] + jnp.where(pred(c, g * SUBLANES, blk), 1.0, 0.0)
            while len(parts) > 1:
                parts = [parts[i] + parts[i + 1] for i in range(0, len(parts), 2)]
            return parts[0]
        acc = lax.fori_loop(0, n_chunks, body, jnp.zeros((SUBLANES, DSA_T), F32))
        return jnp.sum(acc, axis=0, keepdims=True)

    def count_ge(thr):
        thr_b = jnp.broadcast_to(thr, (SUBLANES, DSA_T))
        return count_where(lambda c, r0, blk: blk >= thr_b)

    q_min = jnp.min(s_min, axis=0, keepdims=True)
    q_max = jnp.max(s_max, axis=0, keepdims=True)
    t_q = q0 + lax.broadcasted_iota(jnp.int32, (1, DSA_T), 1)

    kf = float(TOPK)
    init = (q_min, q_max + jnp.maximum(jnp.abs(q_max), 1.0) * 1e-6,
            (t_q + 1).astype(F32), jnp.zeros((1, DSA_T), F32))

    def bisect_step(_i, carry):
        lo, hi, cnt_lo, cnt_hi = carry
        mid = 0.5 * (lo + hi)
        cnt = count_ge(mid)
        ge = cnt >= kf
        return (jnp.where(ge, mid, lo), jnp.where(ge, hi, mid),
                jnp.where(ge, cnt, cnt_lo), jnp.where(ge, cnt_hi, cnt))

    n_plain = jnp.where(qb == 0, 0, BISECT_PLAIN_STEPS)
    lo, hi, cnt_lo, cnt_hi = lax.fori_loop(0, n_plain, bisect_step, init)

    big = -NEG_INF

    def snap_body(c, carry):
        mn, mx = carry
        blk = score_ref[c]
        mn = jnp.minimum(mn, _fold_groups(jnp.where(blk >= lo, blk, big), jnp.min))
        mx = jnp.maximum(mx, _fold_groups(jnp.where(blk < hi, blk, -big), jnp.max))
        return mn, mx

    mn, mx = lax.fori_loop(
        0, n_chunks, snap_body,
        (jnp.full((SUBLANES, DSA_T), big, F32), jnp.full((SUBLANES, DSA_T), -big, F32)))
    lo = jnp.min(mn, axis=0, keepdims=True)
    top = jnp.max(mx, axis=0, keepdims=True)

    def split_pair(carry):
        lo, top, cnt_lo, cnt_hi = carry
        pair = (cnt_lo > kf) & (lo < top) & (cnt_lo - cnt_hi == 2.0)
        return jnp.where(pair, top, lo), top, jnp.where(pair, cnt_hi + 1.0, cnt_lo), cnt_hi

    def pending_of(carry):
        lo, top, cnt_lo, _ = carry
        over = cnt_lo > kf
        return jnp.max(jnp.where(over & (lo < top), 2.0, jnp.where(over, 1.0, 0.0)))

    def value_step(state):
        (lo, top, cnt_lo, cnt_hi), _ = state
        unresolved = (cnt_lo > kf) & (lo < top)
        mid = lo + (top - lo) * 0.5
        mid = jnp.minimum(jnp.where(mid > lo, mid, top), top)

        def body(c, carry):
            cnt, mn, mx = carry
            blk = score_ref[c]
            is_ge = blk >= mid
            cnt = cnt + _fold_groups(jnp.where(is_ge, 1.0, 0.0), jnp.sum)
            mn = jnp.minimum(mn, _fold_groups(jnp.where(is_ge, blk, big), jnp.min))
            mx = jnp.maximum(mx, _fold_groups(jnp.where(is_ge, -big, blk), jnp.max))
            return cnt, mn, mx

        cnt, mn, mx = lax.fori_loop(
            0, n_chunks, body,
            (jnp.zeros((SUBLANES, DSA_T), F32), jnp.full((SUBLANES, DSA_T), big, F32),
             jnp.full((SUBLANES, DSA_T), -big, F32)))
        cnt = jnp.sum(cnt, axis=0, keepdims=True)
        ge = cnt >= kf
        raise_lo = unresolved & ge
        lower_top = unresolved & jnp.logical_not(ge)
        carry = split_pair((jnp.where(raise_lo, jnp.min(mn, axis=0, keepdims=True), lo),
                            jnp.where(lower_top, jnp.max(mx, axis=0, keepdims=True), top),
                            jnp.where(raise_lo, cnt, cnt_lo), jnp.where(lower_top, cnt, cnt_hi)))
        return carry, pending_of(carry)

    start = split_pair((lo, top, cnt_lo, cnt_hi))
    (lo, top, cnt_lo, cnt_hi), status = lax.while_loop(
        lambda state: state[1] > 1.5, value_step, (start, pending_of(start)))

    tied = cnt_lo > kf
    need = kf - cnt_hi
    lo_b = jnp.broadcast_to(lo, (SUBLANES, DSA_T))
    key_in_group = lax.broadcasted_iota(jnp.int32, (SUBLANES, DSA_T), 0)

    def tie_search(_):
        def body(_i, carry):
            lo_i, hi_i = carry
            mid_i = lax.shift_right_arithmetic(lo_i + hi_i, 1)
            mid_b = jnp.broadcast_to(mid_i, (SUBLANES, DSA_T))
            cnt = count_where(lambda c, r0, blk: (blk == lo_b) & (key_in_group + (c * DSA_T + r0) <= mid_b))
            enough = cnt >= need
            return jnp.where(enough, lo_i, mid_i), jnp.where(enough, mid_i, hi_i)
        lo_i = jnp.full((1, DSA_T), -1, jnp.int32)
        hi_i = jnp.full((1, DSA_T), SEQ - 1, jnp.int32)
        _, cut = lax.fori_loop(0, int(math.log2(SEQ)) + 1, body, (lo_i, hi_i))
        return jnp.where(tied, cut, SEQ)

    cut = lax.cond(status > 0.5, tie_search, lambda _: jnp.full((1, DSA_T), SEQ, jnp.int32), 0)

    def write_mask(c):
        blk = score_ref[c]
        keep = (blk > lo) | ((blk == lo) & (key_in_chunk + c * DSA_T <= cut))
        score_ref[c] = jnp.where(keep, 0.0, NEG_INF)

    n_far = jnp.maximum(qb - 1, 0)

    def masked_logits(c, h, far):
        lg = logit_ref[h, c] + score_ref[c]
        return lg if far else lg + bias_ref[c - qb + 1, h]

    def far_bias(h):
        return table_ref[T5_BUCKETS - 1, h] * LOG2E

    def den_row(h):
        return h * DVT_HEAD_ROWS + DSA_DH

    def exp_and_pv(shifts, scores_to_masks):
        acc_ref[...] = jnp.zeros_like(acc_ref)

        def body(c, carry, far):
            if scores_to_masks:
                write_mask(c)
            for h in range(DSA_HEADS):
                shift = shifts[h] - far_bias(h) if far else shifts[h]
                p = jnp.exp2(masked_logits(c, h, far) - shift).astype(BF16)
                rows = slice(h * DVT_HEAD_ROWS, (h + 1) * DVT_HEAD_ROWS)
                acc_ref[rows, :] += _dot(dvt_ref[c, rows, :], p)
            return carry

        paired_loop(0, n_far, lambda c, carry: body(c, carry, True), 0)
        paired_loop(n_far, n_chunks, lambda c, carry: body(c, carry, False), 0)

    upper = [jnp.max(raw_max[h], axis=0, keepdims=True) + bmax_ref[h:h + 1, 0:1] for h in range(DSA_HEADS)]
    exp_and_pv(upper, True)

    den_min = acc_ref[den_row(0):den_row(0) + 1, :]
    for h in range(1, DSA_HEADS):
        den_min = jnp.minimum(den_min, acc_ref[den_row(h):den_row(h) + 1, :])

    @pl.when(jnp.min(den_min) < DEN_FLOOR)
    def _():
        def max_body(c, ms, far):
            return tuple(jnp.maximum(ms[h], _fold_groups(masked_logits(c, h, far), jnp.max) + (far_bias(h) if far else 0.0))
                         for h in range(DSA_HEADS))
        ms = tuple(jnp.full((SUBLANES, DSA_T), NEG_INF, F32) for _ in range(DSA_HEADS))
        ms = lax.fori_loop(0, n_far, lambda c, ms: max_body(c, ms, True), ms)
        ms = lax.fori_loop(n_far, n_chunks, lambda c, ms: max_body(c, ms, False), ms)
        exp_and_pv([jnp.max(m, axis=0, keepdims=True) for m in ms], False)

    heads = []
    for h in range(DSA_HEADS):
        r0 = h * DVT_HEAD_ROWS
        heads.append(acc_ref[r0:r0 + DSA_DH, :] / acc_ref[den_row(h):den_row(h) + 1, :])
    o_ref[...] = jnp.concatenate(heads, axis=0).T.astype(BF16)


def _dsa(t5_table, bias, bias_max, iq, dq, ikw, ikb, dk, dvt):
    nqb = SEQ // DSA_T
    padded = DSA_HEADS * HEAD_PAD
    assert DSA_T <= TOPK
    qrows = lambda w: pl.BlockSpec((DSA_T, w), lambda b, i: (b * nqb + i, 0))
    qcols = lambda h: pl.BlockSpec((h, DSA_T), lambda b, i: (0, b * nqb + i))
    krows = lambda w: pl.BlockSpec((SEQ, w), lambda b, i: (b, 0))
    return pl.pallas_call(
        _dsa_kernel,
        out_shape=jax.ShapeDtypeStruct((N_TOK, DSA_W), BF16),
        grid=(BATCH, nqb),
        in_specs=[pl.BlockSpec(memory_space=pltpu.SMEM),
                  _const_spec((2, DSA_HEADS, DSA_T, DSA_T)), _const_spec((DSA_HEADS, LANES)),
                  qcols(IDX_Q_W), qcols(DSA_W), qrows(LANES),
                  krows(2 * LANES), krows(padded),
                  pl.BlockSpec((nqb, DVT_ROWS, DSA_T), lambda b, i: (b, 0, 0))],
        out_specs=qrows(DSA_W),
        scratch_shapes=[pltpu.VMEM((nqb, DSA_T, DSA_T), F32),
                        pltpu.VMEM((DSA_HEADS, nqb, DSA_T, DSA_T), F32),
                        pltpu.VMEM((DVT_ROWS, DSA_T), F32)],
        compiler_params=_cparams(("parallel", "arbitrary")),
        name="dsa_attention",
    )(t5_table, bias, bias_max, iq, dq, ikw, ikb, dk, dvt)


def _merge_kernel(h_ref, oret_ref, odsa_ref, omem_ref, wg_ref, wr_ref, wd_ref, wm_ref, wo_ref,
                  g_ref, b_ref, o_ref):
    h = h_ref[...]
    hb = h.astype(BF16)
    merged = None
    for i, (src, w) in enumerate(((oret_ref, wr_ref), (odsa_ref, wd_ref), (omem_ref, wm_ref))):
        gate = _sigmoid(_dot(hb, wg_ref[:, i * D_MODEL:(i + 1) * D_MODEL]))
        term = gate * _dot(src[...], w[...])
        merged = term if merged is None else merged + term
    mix = _dot(merged.astype(BF16), wo_ref[...])
    o_ref[...] = _layer_norm(ALPHA * h + mix, g_ref[...], b_ref[...])


def _merge(h, o_ret, o_dsa, o_mem, wg, wr, wd, wm, wo, g, b):
    rows = lambda w: pl.BlockSpec((MERGE_ROWS, w), lambda i: (i, 0))
    return pl.pallas_call(
        _merge_kernel,
        out_shape=jax.ShapeDtypeStruct((N_TOK, D_MODEL), F32),
        grid=(N_TOK // MERGE_ROWS,),
        in_specs=[rows(D_MODEL), rows(RET_V_W), rows(DSA_W), rows(MEM_W),
                  _const_spec((D_MODEL, 3 * D_MODEL)),
                  _const_spec((RET_V_W, D_MODEL)), _const_spec((DSA_W, D_MODEL)),
                  _const_spec((MEM_W, D_MODEL)), _const_spec((D_MODEL, D_MODEL)),
                  _const_spec((1, D_MODEL)), _const_spec((1, D_MODEL))],
        out_specs=rows(D_MODEL),
        compiler_params=_cparams(("parallel",)),
        name="merge_ln2",
    )(h, o_ret, o_dsa, o_mem, wg, wr, wd, wm, wo, g, b)


def _rope_tables():
    half = RET_DK // 2
    freqs = ROPE_BASE ** (-jnp.arange(half, dtype=F32) / half)
    ang = jnp.arange(SEQ).astype(F32)[:, None] * freqs[None, :]
    cos, sin = jnp.cos(ang), jnp.sin(ang)
    cos_t = jnp.tile(jnp.concatenate([cos, cos], axis=-1), (1, RET_HEADS))
    sin_t = jnp.tile(jnp.concatenate([-sin, sin], axis=-1), (1, RET_HEADS))
    return cos_t, sin_t


def _decay_tables():
    c = RET_CHUNK
    gamma = 1.0 - 2.0 ** (-5.0 - jnp.arange(RET_HEADS, dtype=F32))
    lg = jnp.log(gamma)
    i = jnp.arange(c)
    diff = i[:, None] - i[None, :]
    decay_in = jnp.where(diff[None] >= 0, jnp.exp(jnp.maximum(diff, 0)[None] * lg[:, None, None]), 0.0).astype(F32)
    k_dec = jnp.exp((c - 1 - i)[None, :] * lg[:, None]).astype(F32)
    q_dec = jnp.exp((i + 1)[None, :] * lg[:, None]).astype(F32)
    chunk_dec = jnp.exp(c * lg).astype(F32)
    kdec_t = jnp.repeat(k_dec.T, RET_DK, axis=1)
    qdec_t = jnp.repeat(q_dec.T, RET_DV, axis=1)
    cdec_t = jnp.broadcast_to(chunk_dec[:, None, None], (RET_HEADS, RET_DK, RET_DV))
    return decay_in, kdec_t, qdec_t, cdec_t


def _layer(x, mem2d, ffn1_w_in, ffn1_w_out, ln1_g, ln1_b, w_in, t5_table, ret_gn_g, ret_gn_b,
           w_mem_kv, w_br_ret, w_br_dsa, w_br_mem, w_out, ln2_g, ln2_b,
           ffn2_w_in, ffn2_w_out, ln3_g, ln3_b, tables):
    cos_t, sin_t, din, kdec, qdec, cdec = tables
    row = lambda v: v.reshape(1, -1)
    bf = lambda w: w.astype(BF16)

    w_in_bf = bf(w_in)
    w_mq = w_in_bf[:, W_MQ0:W_G0]
    w_dvt = w_in_bf[:, W_DV0:W_IQ0].T
    kdec_rows = jnp.tile(kdec, (PROJ_ROWS // RET_CHUNK, 1))

    h = _ffn_ln(x, bf(ffn1_w_in), bf(ffn1_w_out), row(ln1_g), row(ln1_b), "ffn1_ln1")
    rq, rk, rkdt, rv, rg, dq, dk, dvt, iq, mq, ikw, ikb = _proj(h, w_in_bf, w_mq, w_dvt, cos_t, sin_t, kdec_rows)
    o_ret = _retention(rq, rk, rkdt, rv, rg, din, qdec, cdec, row(ret_gn_g), row(ret_gn_b))
    o_mem = _mematt(mq, mem2d, bf(w_mem_kv))
    o_dsa = _dsa(t5_table, *_t5_bias(t5_table), iq, dq, ikw, ikb, dk, dvt)
    x2 = _merge(h, o_ret, o_dsa, o_mem, w_in_bf[:, W_G0:], bf(w_br_ret), bf(w_br_dsa), bf(w_br_mem),
                bf(w_out), row(ln2_g), row(ln2_b))
    return _ffn_ln(x2, bf(ffn2_w_in), bf(ffn2_w_out), row(ln3_g), row(ln3_b), "ffn2_ln3")


def kernel(x, mem, ffn1_w_in, ffn1_w_out, ln1_g, ln1_b, w_in, t5_table, ret_gn_g, ret_gn_b,
           w_mem_kv, w_br_ret, w_br_dsa, w_br_mem, w_out, ln2_g, ln2_b,
           ffn2_w_in, ffn2_w_out, ln3_g, ln3_b):
    assert x.shape == (BATCH, SEQ, D_MODEL) and mem.shape == (BATCH, MEM_TOKENS, D_MODEL)
    tables = _rope_tables() + _decay_tables()
    y = x.reshape(N_TOK, D_MODEL)
    mem2d = mem.reshape(BATCH * MEM_TOKENS, D_MODEL)
    for l in range(DEPTH):
        y = _layer(y, mem2d, ffn1_w_in[l], ffn1_w_out[l], ln1_g[l], ln1_b[l], w_in[l], t5_table,
                   ret_gn_g[l], ret_gn_b[l], w_mem_kv[l], w_br_ret[l], w_br_dsa[l], w_br_mem[l],
                   w_out[l], ln2_g[l], ln2_b[l], ffn2_w_in[l], ffn2_w_out[l], ln3_g[l], ln3_b[l],
                   tables)
    return y.reshape(BATCH, SEQ, D_MODEL)
```

```python
import math

import jax
import jax.numpy as jnp
from jax import lax
from jax.experimental import pallas as pl
from jax.experimental.pallas import tpu as pltpu

F32 = jnp.float32
BF16 = jnp.bfloat16

D_MODEL = 1024
BATCH = 8
SEQ = 2048
MEM_TOKENS = 256
RET_HEADS, RET_DK, RET_DV, RET_CHUNK = 4, 64, 128, 128
DSA_HEADS, DSA_DH = 8, 64
IDX_HEADS, IDX_DIM = 8, 64
TOPK = min(256, SEQ // 4)
MEM_HEADS, MEM_DH = 4, 128
T5_BUCKETS, T5_MAX_DIST = 32, 128
D_FF = 2816
ROPE_BASE = 10000.0
LN_EPS = 1e-5
NEG_INF = -1e30
DEPTH = 1
ALPHA = (2.0 * DEPTH) ** 0.25

RET_QK_W = RET_HEADS * RET_DK
RET_V_W = RET_HEADS * RET_DV
DSA_W = DSA_HEADS * DSA_DH
IDX_Q_W = IDX_HEADS * IDX_DIM
MEM_W = MEM_HEADS * MEM_DH
N_TOK = BATCH * SEQ

V7X_VMEM_BYTES = 64 * 1024 * 1024
VMEM_LIMIT = V7X_VMEM_BYTES - 8 * 1024 * 1024
LANES = 128
SUBLANES = 8

FFN_ROWS = 1024
MXU_TILE = 256
FFN_CHUNK_EDGES = (0, 6 * MXU_TILE, D_FF)
PROJ_ROWS = 1024
MERGE_ROWS = 1024
RET_ROWS = 512
MEMATT_ROWS = 512
DSA_T = 256
HEAD_PAD = LANES
DVT_HEAD_ROWS = DSA_DH + 16
DVT_ROWS = DSA_HEADS * DVT_HEAD_ROWS
BISECT_PLAIN_STEPS = 16
COUNT_ACCUMULATORS = 4
LOG2E = math.log2(math.e)
DEN_FLOOR = 2.0 ** -100

W_RQ0 = 0
W_RK0 = W_RQ0 + RET_QK_W
W_RV0 = W_RK0 + RET_QK_W
W_RG0 = W_RV0 + RET_V_W
W_DQ0 = W_RG0 + RET_V_W
W_DK0 = W_DQ0 + DSA_W
W_DV0 = W_DK0 + DSA_W
W_IQ0 = W_DV0 + DSA_W
W_IK0 = W_IQ0 + IDX_Q_W
W_IW0 = W_IK0 + IDX_DIM
W_MQ0 = W_IW0 + IDX_HEADS
W_G0 = W_MQ0 + MEM_W
IW_SCALE = IDX_HEADS ** -0.5 * IDX_DIM ** -0.5


def _cparams(sem):
    return pltpu.CompilerParams(dimension_semantics=sem, vmem_limit_bytes=VMEM_LIMIT)


def _const_spec(shape):
    nd = len(shape)
    return pl.BlockSpec(shape, lambda *_: (0,) * nd, pipeline_mode=pl.Buffered(1))


def _layer_norm(y, g, b):
    mu = jnp.mean(y, axis=-1, keepdims=True)
    yc = y - mu
    var = jnp.mean(yc * yc, axis=-1, keepdims=True)
    return yc * lax.rsqrt(var + LN_EPS) * g + b


def _sigmoid(x):
    return 0.5 * jnp.tanh(0.5 * x) + 0.5


def _dot(a, b):
    return jnp.dot(a, b, preferred_element_type=F32)


def _dot_nt(a, b):
    return lax.dot_general(a, b, (((1,), (1,)), ((), ())), preferred_element_type=F32)


def _ffn_ln_kernel(x_ref, wi_ref, wo_ref, g_ref, b_ref, o_ref):
    x = x_ref[...]
    xb = x.astype(BF16)
    acc = None
    for lo, hi in zip(FFN_CHUNK_EDGES[:-1], FFN_CHUNK_EDGES[1:]):
        a = _dot(xb, wi_ref[:, lo:hi])
        u = _dot(xb, wi_ref[:, D_FF + lo:D_FF + hi])
        act = (a * _sigmoid(a) * u).astype(BF16)
        part = _dot(act, wo_ref[lo:hi, :])
        acc = part if acc is None else acc + part
    o_ref[...] = _layer_norm(ALPHA * x + 0.5 * acc, g_ref[...], b_ref[...])


def _ffn_ln(x, w_in_bf, w_out_bf, g, b, name):
    return pl.pallas_call(
        _ffn_ln_kernel,
        out_shape=jax.ShapeDtypeStruct((N_TOK, D_MODEL), F32),
        grid=(N_TOK // FFN_ROWS,),
        in_specs=[
            pl.BlockSpec((FFN_ROWS, D_MODEL), lambda i: (i, 0)),
            _const_spec((D_MODEL, 2 * D_FF)),
            _const_spec((D_FF, D_MODEL)),
            _const_spec((1, D_MODEL)),
            _const_spec((1, D_MODEL)),
        ],
        out_specs=pl.BlockSpec((FFN_ROWS, D_MODEL), lambda i: (i, 0)),
        compiler_params=_cparams(("parallel",)),
        name=name,
    )(x, w_in_bf, w_out_bf, g, b)


def _rope(x, cos, sin_signed):
    width = x.shape[-1]
    lane = lax.broadcasted_iota(jnp.int32, x.shape, 1)
    first_half = (lane % RET_DK) < (RET_DK // 2)
    swapped = jnp.where(first_half,
                        pltpu.roll(x, width - RET_DK // 2, 1),
                        pltpu.roll(x, RET_DK // 2, 1))
    return x * cos + swapped * sin_signed


def _store_split_heads(dst_ref, val):
    low = lax.broadcasted_iota(jnp.int32, (val.shape[0], LANES), 1) < DSA_DH
    for j in range(val.shape[1] // LANES):
        pair = val[:, j * LANES:(j + 1) * LANES]
        dst_ref[:, (2 * j) * LANES:(2 * j + 1) * LANES] = jnp.where(low, pair, 0.0).astype(dst_ref.dtype)
        dst_ref[:, (2 * j + 1) * LANES:(2 * j + 2) * LANES] = jnp.where(low, 0.0, pair).astype(dst_ref.dtype)


def _store_low_heads(dst_ref, val):
    low = lax.broadcasted_iota(jnp.int32, (val.shape[0], LANES), 1) < RET_DK
    for j in range(val.shape[1] // LANES):
        pair = val[:, j * LANES:(j + 1) * LANES]
        dst_ref[:, (2 * j) * LANES:(2 * j + 1) * LANES] = jnp.where(low, pair, 0.0).astype(dst_ref.dtype)
        dst_ref[:, (2 * j + 1) * LANES:(2 * j + 2) * LANES] = jnp.where(
            low, pltpu.roll(pair, RET_DK, 1), 0.0).astype(dst_ref.dtype)


def _proj_kernel(h_ref, w_ref, wikw_ref, wmq_ref, wdvt_ref, cos_ref, sin_ref, kdec_ref,
                 rq_ref, rk_ref, rkdt_ref, rv_ref, rg_ref, dq_ref, dk_ref, dvt_ref, iq_ref, mq_ref,
                 ikw_ref, ikb_ref):
    hb = h_ref[...].astype(BF16)

    def proj(lo, width):
        return _dot(hb, w_ref[:, lo:lo + width])

    cos = cos_ref[...]
    sin = sin_ref[...]
    _store_low_heads(rq_ref, _rope(proj(W_RQ0, RET_QK_W), cos, sin))
    rk = _rope(proj(W_RK0, RET_QK_W), cos, sin) * (RET_DK ** -0.5)
    _store_low_heads(rk_ref, rk)
    rkdt_ref[...] = (rk * kdec_ref[...]).T.astype(BF16)
    rv_ref[...] = proj(W_RV0, RET_V_W).astype(BF16)
    rg_ref[...] = proj(W_RG0, RET_V_W)
    dq_ref[...] = (proj(W_DQ0, DSA_W) * (DSA_DH ** -0.5 * LOG2E)).T.astype(BF16)
    _store_split_heads(dk_ref, proj(W_DK0, DSA_W))
    iq_ref[...] = proj(W_IQ0, IDX_Q_W).T.astype(BF16)
    mq_ref[...] = _dot(hb, wmq_ref[...]).astype(BF16)
    ikw = _dot(hb, wikw_ref[...])
    ikw_ref[...] = ikw
    low = lax.broadcasted_iota(jnp.int32, ikw.shape, 1) < IDX_DIM
    ik_low = jnp.where(low, ikw, 0.0)
    ikb_ref[:, 0:LANES] = ik_low.astype(BF16)
    ikb_ref[:, LANES:2 * LANES] = pltpu.roll(ik_low, IDX_DIM, 1).astype(BF16)
    vt = _dot_nt(wdvt_ref[...], hb).astype(BF16)
    ones = jnp.ones((DVT_HEAD_ROWS - DSA_DH, DSA_T), BF16)
    for j in range(PROJ_ROWS // DSA_T):
        for h in range(DSA_HEADS):
            r0 = h * DVT_HEAD_ROWS
            dvt_ref[j, r0:r0 + DSA_DH, :] = vt[h * DSA_DH:(h + 1) * DSA_DH, j * DSA_T:(j + 1) * DSA_T]
            dvt_ref[j, r0 + DSA_DH:r0 + DVT_HEAD_ROWS, :] = ones


def _proj(h, w_in_bf, w_mq_bf, w_dvt_bf, cos_t, sin_t, kdec_rows):
    rows = lambda w: pl.BlockSpec((PROJ_ROWS, w), lambda i: (i, 0))
    seq_tiles = SEQ // PROJ_ROWS
    pos = lambda w: pl.BlockSpec((PROJ_ROWS, w), lambda i: (i % seq_tiles, 0))
    sd = jax.ShapeDtypeStruct
    ret_padded = RET_HEADS * HEAD_PAD
    dsa_padded = DSA_HEADS * HEAD_PAD
    slabs = PROJ_ROWS // DSA_T
    assert W_IK0 % LANES == 0
    return pl.pallas_call(
        _proj_kernel,
        out_shape=(
            sd((N_TOK, ret_padded), BF16), sd((N_TOK, ret_padded), BF16), sd((RET_QK_W, N_TOK), BF16),
            sd((N_TOK, RET_V_W), BF16), sd((N_TOK, RET_V_W), F32),
            sd((DSA_W, N_TOK), BF16), sd((N_TOK, dsa_padded), BF16),
            sd((N_TOK // DSA_T, DVT_ROWS, DSA_T), BF16),
            sd((IDX_Q_W, N_TOK), BF16), sd((N_TOK, MEM_W), BF16),
            sd((N_TOK, LANES), F32), sd((N_TOK, 2 * LANES), BF16),
        ),
        grid=(N_TOK // PROJ_ROWS,),
        in_specs=[rows(D_MODEL),
                  pl.BlockSpec((D_MODEL, W_IK0), lambda i: (0, 0), pipeline_mode=pl.Buffered(1)),
                  pl.BlockSpec((D_MODEL, LANES), lambda i: (0, W_IK0 // LANES), pipeline_mode=pl.Buffered(1)),
                  _const_spec((D_MODEL, MEM_W)), _const_spec((DSA_W, D_MODEL)),
                  pos(RET_QK_W), pos(RET_QK_W), _const_spec((PROJ_ROWS, RET_QK_W))],
        out_specs=(rows(ret_padded), rows(ret_padded),
                   pl.BlockSpec((RET_QK_W, PROJ_ROWS), lambda i: (0, i)),
                   rows(RET_V_W), rows(RET_V_W),
                   pl.BlockSpec((DSA_W, PROJ_ROWS), lambda i: (0, i)), rows(dsa_padded),
                   pl.BlockSpec((slabs, DVT_ROWS, DSA_T), lambda i: (i, 0, 0)),
                   pl.BlockSpec((IDX_Q_W, PROJ_ROWS), lambda i: (0, i)),
                   rows(MEM_W), rows(LANES), rows(2 * LANES)),
        compiler_params=_cparams(("parallel",)),
        name="mixer_proj",
    )(h, w_in_bf, w_in_bf, w_mq_bf, w_dvt_bf, cos_t, sin_t, kdec_rows)


def _retention_kernel(q_ref, k_ref, kdt_ref, v_ref, g_ref, din_ref, qdec_ref, cdec_ref,
                      gng_ref, gnb_ref, o_ref, state_ref):
    @pl.when(pl.program_id(1) == 0)
    def _():
        state_ref[...] = jnp.zeros_like(state_ref)

    qdec = qdec_ref[...]
    gn_g = gng_ref[...]
    gn_b = gnb_ref[...]
    states = [state_ref[h] for h in range(RET_HEADS)]
    pad_rows = jnp.zeros((HEAD_PAD - RET_DK, RET_DV), BF16)
    for j in range(RET_ROWS // RET_CHUNK):
        rows = slice(j * RET_CHUNK, (j + 1) * RET_CHUNK)
        gate = g_ref[rows, :]
        for h in range(RET_HEADS):
            hs = slice(h * HEAD_PAD, (h + 1) * HEAD_PAD)
            vs = slice(h * RET_DV, (h + 1) * RET_DV)
            qh = q_ref[rows, hs]
            vh = v_ref[rows, vs]
            scores = _dot_nt(qh, k_ref[rows, hs]) * din_ref[h]
            intra = _dot(scores.astype(BF16), vh)
            state_padded = jnp.concatenate([states[h].astype(BF16), pad_rows], axis=0)
            cross = _dot(qh, state_padded) * qdec[:, vs]
            kv = _dot(kdt_ref[h * RET_DK:(h + 1) * RET_DK, rows], vh)
            states[h] = cdec_ref[h] * states[h] + kv
            o = intra + cross
            mu = jnp.mean(o, axis=-1, keepdims=True)
            oc = o - mu
            var = jnp.mean(oc * oc, axis=-1, keepdims=True)
            o = oc * lax.rsqrt(var + LN_EPS) * gn_g[:, vs] + gn_b[:, vs]
            o_ref[rows, vs] = (jax.nn.silu(gate[:, vs]) * o).astype(BF16)
    for h in range(RET_HEADS):
        state_ref[h] = states[h]


def _retention(rq, rk, rkdt, rv, rg, din, qdec, cdec, gn_g, gn_b):
    nc = SEQ // RET_ROWS
    rows = lambda w: pl.BlockSpec((RET_ROWS, w), lambda b, n: (b * nc + n, 0))
    padded = RET_HEADS * HEAD_PAD
    return pl.pallas_call(
        _retention_kernel,
        out_shape=jax.ShapeDtypeStruct((N_TOK, RET_V_W), BF16),
        grid=(BATCH, nc),
        in_specs=[rows(padded), rows(padded),
                  pl.BlockSpec((RET_QK_W, RET_ROWS), lambda b, n: (0, b * nc + n)),
                  rows(RET_V_W), rows(RET_V_W),
                  _const_spec((RET_HEADS, RET_CHUNK, RET_CHUNK)),
                  _const_spec((RET_CHUNK, RET_V_W)),
                  _const_spec((RET_HEADS, RET_DK, RET_DV)),
                  _const_spec((1, RET_V_W)), _const_spec((1, RET_V_W))],
        out_specs=rows(RET_V_W),
        scratch_shapes=[pltpu.VMEM((RET_HEADS, RET_DK, RET_DV), F32)],
        compiler_params=_cparams(("parallel", "arbitrary")),
        name="retention",
    )(rq, rk, rkdt, rv, rg, din, qdec, cdec, gn_g, gn_b)


def _mematt_kernel(q_ref, mem_ref, w_ref, o_ref, kv_ref):
    @pl.when(pl.program_id(1) == 0)
    def _():
        kv_ref[...] = _dot(mem_ref[...].astype(BF16), w_ref[...]).astype(BF16)

    q = q_ref[...]
    for h in range(MEM_HEADS):
        hs = slice(h * MEM_DH, (h + 1) * MEM_DH)
        logits = _dot_nt(q[:, hs], kv_ref[:, hs]) * (MEM_DH ** -0.5)
        m = jnp.max(logits, axis=-1, keepdims=True)
        p = jnp.exp(logits - m)
        denom = jnp.sum(p, axis=-1, keepdims=True)
        o_ref[:, hs] = (_dot(p.astype(BF16), kv_ref[:, MEM_W + h * MEM_DH:MEM_W + (h + 1) * MEM_DH])
                        / denom).astype(BF16)


def _mematt(mq, mem2d, w_kv_bf):
    tiles = SEQ // MEMATT_ROWS
    return pl.pallas_call(
        _mematt_kernel,
        out_shape=jax.ShapeDtypeStruct((N_TOK, MEM_W), BF16),
        grid=(BATCH, tiles),
        in_specs=[pl.BlockSpec((MEMATT_ROWS, MEM_W), lambda b, i: (b * tiles + i, 0)),
                  pl.BlockSpec((MEM_TOKENS, D_MODEL), lambda b, i: (b, 0)),
                  _const_spec((D_MODEL, 2 * MEM_W))],
        out_specs=pl.BlockSpec((MEMATT_ROWS, MEM_W), lambda b, i: (b * tiles + i, 0)),
        scratch_shapes=[pltpu.VMEM((MEM_TOKENS, 2 * MEM_W), BF16)],
        compiler_params=_cparams(("parallel", "arbitrary")),
        name="mem_attention",
    )(mq, mem2d, w_kv_bf)


def _t5_bias_kernel(table_ref, o_ref, bmax_ref):
    m = lax.broadcasted_iota(jnp.int32, (SUBLANES, 2 * DSA_T), 1)
    max_exact = T5_BUCKETS // 2
    head_max = [None] * DSA_HEADS
    for w in range(2):
        n = jnp.maximum(m - w * DSA_T, 0)
        nf = jnp.maximum(n, 1).astype(F32)
        large = max_exact + jnp.floor(jnp.log(nf / max_exact) / math.log(T5_MAX_DIST / max_exact)
                                      * (T5_BUCKETS - max_exact)).astype(jnp.int32)
        large = jnp.minimum(large, T5_BUCKETS - 1)
        bucket = jnp.where(n < max_exact, n, large)
        for h in range(DSA_HEADS):
            vec = jnp.zeros((SUBLANES, 2 * DSA_T), F32)
            for b in range(T5_BUCKETS):
                vec = jnp.where(bucket == b, table_ref[b, h] * LOG2E, vec)
            rows = jnp.broadcast_to(vec[0:1, :], (DSA_T, 2 * DSA_T))
            o_ref[w, h] = pltpu.roll(rows, 0, 1, stride=1, stride_axis=0)[:, DSA_T:]
            top = jnp.max(vec[0:1, :], axis=1, keepdims=True)
            head_max[h] = top if head_max[h] is None else jnp.maximum(head_max[h], top)
    for h in range(DSA_HEADS):
        bmax_ref[h:h + 1, :] = jnp.broadcast_to(head_max[h], (1, LANES))


def _t5_bias(t5_table):
    return pl.pallas_call(
        _t5_bias_kernel,
        out_shape=(jax.ShapeDtypeStruct((2, DSA_HEADS, DSA_T, DSA_T), F32),
                   jax.ShapeDtypeStruct((DSA_HEADS, LANES), F32)),
        in_specs=[pl.BlockSpec(memory_space=pltpu.SMEM)],
        out_specs=(pl.BlockSpec(memory_space=pltpu.VMEM), pl.BlockSpec(memory_space=pltpu.VMEM)),
        compiler_params=pltpu.CompilerParams(vmem_limit_bytes=VMEM_LIMIT),
        name="t5_bias",
    )(t5_table)


def _fold_groups(x, op):
    return op(x.reshape(DSA_T // SUBLANES, SUBLANES, DSA_T), axis=0)


def _dsa_kernel(table_ref, bias_ref, bmax_ref, iq_ref, dq_ref, ikw_q_ref, ikb_ref, dk_ref, dvt_ref,
                o_ref, score_ref, logit_ref, acc_ref):
    qb = pl.program_id(1)
    q0 = qb * DSA_T
    n_chunks = qb + 1
    groups = DSA_T // SUBLANES
    key_in_chunk = lax.broadcasted_iota(jnp.int32, (DSA_T, DSA_T), 0)
    query_in_step = lax.broadcasted_iota(jnp.int32, (DSA_T, DSA_T), 1)

    def key_rows(c):
        return pl.ds(pl.multiple_of(c * DSA_T, DSA_T), DSA_T)

    def pair_rows(j):
        return slice(j * LANES, (j + 1) * LANES)

    def pair_products(keys_even, keys_odd, queries_t):
        prod = _dot(jnp.concatenate([keys_even, keys_odd], axis=0), queries_t)
        return prod[:DSA_T], prod[DSA_T:]

    iw_t = ikw_q_ref[...].T[IDX_DIM:IDX_DIM + IDX_HEADS, :] * IW_SCALE

    def scores_and_logits(c, carry, causal):
        raw_max, s_min, s_max = carry
        ik_low = ikb_ref[key_rows(c), 0:LANES]
        ik_high = ikb_ref[key_rows(c), LANES:2 * LANES]
        last = IDX_HEADS // 2 - 1
        for j in range(IDX_HEADS // 2):
            rel_even, rel_odd = pair_products(ik_low, ik_high, iq_ref[pair_rows(j), :])
            term = (jnp.maximum(rel_even, 0.0) * iw_t[2 * j:2 * j + 1, :]
                    + jnp.maximum(rel_odd, 0.0) * iw_t[2 * j + 1:2 * j + 2, :])
            total = term if j == 0 else score_ref[c] + term
            if j < last:
                score_ref[c] = total
            elif causal is None:
                score_ref[c] = total
                s_min = jnp.minimum(s_min, _fold_groups(total, jnp.min))
                s_max = jnp.maximum(s_max, _fold_groups(total, jnp.max))
            else:
                score_ref[c] = jnp.where(causal, total, NEG_INF)
                s_min = jnp.minimum(s_min, _fold_groups(jnp.where(causal, total, -NEG_INF), jnp.min))
                s_max = jnp.maximum(s_max, _fold_groups(jnp.where(causal, total, NEG_INF), jnp.max))
        out = list(raw_max)
        for j in range(DSA_HEADS // 2):
            keys = [dk_ref[key_rows(c), (2 * j + i) * HEAD_PAD:(2 * j + i + 1) * HEAD_PAD] for i in range(2)]
            for h, qk in zip((2 * j, 2 * j + 1), pair_products(keys[0], keys[1], dq_ref[pair_rows(j), :])):
                logit_ref[h, c] = qk
                out[h] = jnp.maximum(out[h], _fold_groups(qk, jnp.max))
        return tuple(out), s_min, s_max

    def paired_loop(start, stop, body, carry):
        n = stop - start
        carry = lax.fori_loop(
            0, lax.shift_right_logical(n, 1),
            lambda i, cr: body(start + 2 * i + 1, body(start + 2 * i, cr)), carry)
        return lax.cond(jnp.bitwise_and(n, 1) == 1, lambda cr: body(stop - 1, cr), lambda cr: cr, carry)

    low_init = jnp.full((SUBLANES, DSA_T), NEG_INF, F32)
    carry = (tuple(low_init for _ in range(DSA_HEADS)), -low_init, low_init)
    carry = paired_loop(0, qb, lambda c, cr: scores_and_logits(c, cr, None), carry)
    raw_max, s_min, s_max = scores_and_logits(qb, carry, key_in_chunk <= query_in_step)

    def count_where(pred):
        def body(c, acc):
            parts = [acc] + [jnp.zeros((SUBLANES, DSA_T), F32)] * (COUNT_ACCUMULATORS - 1)
            for g in range(groups):
                blk = score_ref[c, g * SUBLANES:(g + 1) * SUBLANES, :]
                a = g % COUNT_ACCUMULATORS
                parts[a] = parts[a] + jnp.where(pred(c, g * SUBLANES, blk), 1.0, 0.0)
            while len(parts) > 1:
                parts = [parts[i] + parts[i + 1] for i in range(0, len(parts), 2)]
            return parts[0]
        acc = lax.fori_loop(0, n_chunks, body, jnp.zeros((SUBLANES, DSA_T), F32))
        return jnp.sum(acc, axis=0, keepdims=True)

    def count_ge(thr):
        thr_b = jnp.broadcast_to(thr, (SUBLANES, DSA_T))
        return count_where(lambda c, r0, blk: blk >= thr_b)

    q_min = jnp.min(s_min, axis=0, keepdims=True)
    q_max = jnp.max(s_max, axis=0, keepdims=True)
    t_q = q0 + lax.broadcasted_iota(jnp.int32, (1, DSA_T), 1)

    kf = float(TOPK)
    init = (q_min, q_max + jnp.maximum(jnp.abs(q_max), 1.0) * 1e-6,
            (t_q + 1).astype(F32), jnp.zeros((1, DSA_T), F32))

    def bisect_step(_i, carry):
        lo, hi, cnt_lo, cnt_hi = carry
        mid = 0.5 * (lo + hi)
        cnt = count_ge(mid)
        ge = cnt >= kf
        return (jnp.where(ge, mid, lo), jnp.where(ge, hi, mid),
                jnp.where(ge, cnt, cnt_lo), jnp.where(ge, cnt_hi, cnt))

    n_plain = jnp.where(qb == 0, 0, BISECT_PLAIN_STEPS)
    lo, hi, cnt_lo, cnt_hi = lax.fori_loop(0, n_plain, bisect_step, init)

    big = -NEG_INF

    def snap_body(c, carry):
        mn, mx = carry
        blk = score_ref[c]
        mn = jnp.minimum(mn, _fold_groups(jnp.where(blk >= lo, blk, big), jnp.min))
        mx = jnp.maximum(mx, _fold_groups(jnp.where(blk < hi, blk, -big), jnp.max))
        return mn, mx

    mn, mx = lax.fori_loop(
        0, n_chunks, snap_body,
        (jnp.full((SUBLANES, DSA_T), big, F32), jnp.full((SUBLANES, DSA_T), -big, F32)))
    lo = jnp.min(mn, axis=0, keepdims=True)
    top = jnp.max(mx, axis=0, keepdims=True)

    def split_pair(carry):
        lo, top, cnt_lo, cnt_hi = carry
        pair = (cnt_lo > kf) & (lo < top) & (cnt_lo - cnt_hi == 2.0)
        return jnp.where(pair, top, lo), top, jnp.where(pair, cnt_hi + 1.0, cnt_lo), cnt_hi

    def pending_of(carry):
        lo, top, cnt_lo, _ = carry
        over = cnt_lo > kf
        return jnp.max(jnp.where(over & (lo < top), 2.0, jnp.where(over, 1.0, 0.0)))

    def value_step(state):
        (lo, top, cnt_lo, cnt_hi), _ = state
        unresolved = (cnt_lo > kf) & (lo < top)
        mid = lo + (top - lo) * 0.5
        mid = jnp.minimum(jnp.where(mid > lo, mid, top), top)

        def body(c, carry):
            cnt, mn, mx = carry
            blk = score_ref[c]
            is_ge = blk >= mid
            cnt = cnt + _fold_groups(jnp.where(is_ge, 1.0, 0.0), jnp.sum)
            mn = jnp.minimum(mn, _fold_groups(jnp.where(is_ge, blk, big), jnp.min))
            mx = jnp.maximum(mx, _fold_groups(jnp.where(is_ge, -big, blk), jnp.max))
            return cnt, mn, mx

        cnt, mn, mx = lax.fori_loop(
            0, n_chunks, body,
            (jnp.zeros((SUBLANES, DSA_T), F32), jnp.full((SUBLANES, DSA_T), big, F32),
             jnp.full((SUBLANES, DSA_T), -big, F32)))
        cnt = jnp.sum(cnt, axis=0, keepdims=True)
        ge = cnt >= kf
        raise_lo = unresolved & ge
        lower_top = unresolved & jnp.logical_not(ge)
        carry = split_pair((jnp.where(raise_lo, jnp.min(mn, axis=0, keepdims=True), lo),
                            jnp.where(lower_top, jnp.max(mx, axis=0, keepdims=True), top),
                            jnp.where(raise_lo, cnt, cnt_lo), jnp.where(lower_top, cnt, cnt_hi)))
        return carry, pending_of(carry)

    start = split_pair((lo, top, cnt_lo, cnt_hi))
    (lo, top, cnt_lo, cnt_hi), status = lax.while_loop(
        lambda state: state[1] > 1.5, value_step, (start, pending_of(start)))

    tied = cnt_lo > kf
    need = kf - cnt_hi
    lo_b = jnp.broadcast_to(lo, (SUBLANES, DSA_T))
    key_in_group = lax.broadcasted_iota(jnp.int32, (SUBLANES, DSA_T), 0)

    def tie_search(_):
        def body(_i, carry):
            lo_i, hi_i = carry
            mid_i = lax.shift_right_arithmetic(lo_i + hi_i, 1)
            mid_b = jnp.broadcast_to(mid_i, (SUBLANES, DSA_T))
            cnt = count_where(lambda c, r0, blk: (blk == lo_b) & (key_in_group + (c * DSA_T + r0) <= mid_b))
            enough = cnt >= need
            return jnp.where(enough, lo_i, mid_i), jnp.where(enough, mid_i, hi_i)
        lo_i = jnp.full((1, DSA_T), -1, jnp.int32)
        hi_i = jnp.full((1, DSA_T), SEQ - 1, jnp.int32)
        _, cut = lax.fori_loop(0, int(math.log2(SEQ)) + 1, body, (lo_i, hi_i))
        return jnp.where(tied, cut, SEQ)

    cut = lax.cond(status > 0.5, tie_search, lambda _: jnp.full((1, DSA_T), SEQ, jnp.int32), 0)

    def write_mask(c):
        blk = score_ref[c]
        keep = (blk > lo) | ((blk == lo) & (key_in_chunk + c * DSA_T <= cut))
        score_ref[c] = jnp.where(keep, 0.0, NEG_INF)

    n_far = jnp.maximum(qb - 1, 0)

    def masked_logits(c, h, far):
        lg = logit_ref[h, c] + score_ref[c]
        return lg if far else lg + bias_ref[c - qb + 1, h]

    def far_bias(h):
        return table_ref[T5_BUCKETS - 1, h] * LOG2E

    def den_row(h):
        return h * DVT_HEAD_ROWS + DSA_DH

    def exp_and_pv(shifts, scores_to_masks):
        acc_ref[...] = jnp.zeros_like(acc_ref)

        def body(c, carry, far):
            if scores_to_masks:
                write_mask(c)
            for h in range(DSA_HEADS):
                shift = shifts[h] - far_bias(h) if far else shifts[h]
                p = jnp.exp2(masked_logits(c, h, far) - shift).astype(BF16)
                rows = slice(h * DVT_HEAD_ROWS, (h + 1) * DVT_HEAD_ROWS)
                acc_ref[rows, :] += _dot(dvt_ref[c, rows, :], p)
            return carry

        paired_loop(0, n_far, lambda c, carry: body(c, carry, True), 0)
        paired_loop(n_far, n_chunks, lambda c, carry: body(c, carry, False), 0)

    upper = [jnp.max(raw_max[h], axis=0, keepdims=True) + bmax_ref[h:h + 1, 0:1] for h in range(DSA_HEADS)]
    exp_and_pv(upper, True)

    den_min = acc_ref[den_row(0):den_row(0) + 1, :]
    for h in range(1, DSA_HEADS):
        den_min = jnp.minimum(den_min, acc_ref[den_row(h):den_row(h) + 1, :])

    @pl.when(jnp.min(den_min) < DEN_FLOOR)
    def _():
        def max_body(c, ms, far):
            return tuple(jnp.maximum(ms[h], _fold_groups(masked_logits(c, h, far), jnp.max) + (far_bias(h) if far else 0.0))
                         for h in range(DSA_HEADS))
        ms = tuple(jnp.full((SUBLANES, DSA_T), NEG_INF, F32) for _ in range(DSA_HEADS))
        ms = lax.fori_loop(0, n_far, lambda c, ms: max_body(c, ms, True), ms)
        ms = lax.fori_loop(n_far, n_chunks, lambda c, ms: max_body(c, ms, False), ms)
        exp_and_pv([jnp.max(m, axis=0, keepdims=True) for m in ms], False)

    heads = []
    for h in range(DSA_HEADS):
        r0 = h * DVT_HEAD_ROWS
        heads.append(acc_ref[r0:r0 + DSA_DH, :] / acc_ref[den_row(h):den_row(h) + 1, :])
    o_ref[...] = jnp.concatenate(heads, axis=0).T.astype(BF16)


def _dsa(t5_table, bias, bias_max, iq, dq, ikw, ikb, dk, dvt):
    nqb = SEQ // DSA_T
    padded = DSA_HEADS * HEAD_PAD
    assert DSA_T <= TOPK
    qrows = lambda w: pl.BlockSpec((DSA_T, w), lambda b, i: (b * nqb + i, 0))
    qcols = lambda h: pl.BlockSpec((h, DSA_T), lambda b, i: (0, b * nqb + i))
    krows = lambda w: pl.BlockSpec((SEQ, w), lambda b, i: (b, 0))
    return pl.pallas_call(
        _dsa_kernel,
        out_shape=jax.ShapeDtypeStruct((N_TOK, DSA_W), BF16),
        grid=(BATCH, nqb),
        in_specs=[pl.BlockSpec(memory_space=pltpu.SMEM),
                  _const_spec((2, DSA_HEADS, DSA_T, DSA_T)), _const_spec((DSA_HEADS, LANES)),
                  qcols(IDX_Q_W), qcols(DSA_W), qrows(LANES),
                  krows(2 * LANES), krows(padded),
                  pl.BlockSpec((nqb, DVT_ROWS, DSA_T), lambda b, i: (b, 0, 0))],
        out_specs=qrows(DSA_W),
        scratch_shapes=[pltpu.VMEM((nqb, DSA_T, DSA_T), F32),
                        pltpu.VMEM((DSA_HEADS, nqb, DSA_T, DSA_T), F32),
                        pltpu.VMEM((DVT_ROWS, DSA_T), F32)],
        compiler_params=_cparams(("parallel", "arbitrary")),
        name="dsa_attention",
    )(t5_table, bias, bias_max, iq, dq, ikw, ikb, dk, dvt)


def _merge_kernel(h_ref, oret_ref, odsa_ref, omem_ref, wg_ref, wr_ref, wd_ref, wm_ref, wo_ref,
                  g_ref, b_ref, o_ref):
    h = h_ref[...]
    hb = h.astype(BF16)
    merged = None
    for i, (src, w) in enumerate(((oret_ref, wr_ref), (odsa_ref, wd_ref), (omem_ref, wm_ref))):
        gate = _sigmoid(_dot(hb, wg_ref[:, i * D_MODEL:(i + 1) * D_MODEL]))
        term = gate * _dot(src[...], w[...])
        merged = term if merged is None else merged + term
    mix = _dot(merged.astype(BF16), wo_ref[...])
    o_ref[...] = _layer_norm(ALPHA * h + mix, g_ref[...], b_ref[...])


def _merge(h, o_ret, o_dsa, o_mem, wg, wr, wd, wm, wo, g, b):
    rows = lambda w: pl.BlockSpec((MERGE_ROWS, w), lambda i: (i, 0))
    return pl.pallas_call(
        _merge_kernel,
        out_shape=jax.ShapeDtypeStruct((N_TOK, D_MODEL), F32),
        grid=(N_TOK // MERGE_ROWS,),
        in_specs=[rows(D_MODEL), rows(RET_V_W), rows(DSA_W), rows(MEM_W),
                  _const_spec((D_MODEL, 3 * D_MODEL)),
                  _const_spec((RET_V_W, D_MODEL)), _const_spec((DSA_W, D_MODEL)),
                  _const_spec((MEM_W, D_MODEL)), _const_spec((D_MODEL, D_MODEL)),
                  _const_spec((1, D_MODEL)), _const_spec((1, D_MODEL))],
        out_specs=rows(D_MODEL),
        compiler_params=_cparams(("parallel",)),
        name="merge_ln2",
    )(h, o_ret, o_dsa, o_mem, wg, wr, wd, wm, wo, g, b)


def _rope_tables():
    half = RET_DK // 2
    freqs = ROPE_BASE ** (-jnp.arange(half, dtype=F32) / half)
    ang = jnp.arange(SEQ).astype(F32)[:, None] * freqs[None, :]
    cos, sin = jnp.cos(ang), jnp.sin(ang)
    cos_t = jnp.tile(jnp.concatenate([cos, cos], axis=-1), (1, RET_HEADS))
    sin_t = jnp.tile(jnp.concatenate([-sin, sin], axis=-1), (1, RET_HEADS))
    return cos_t, sin_t


def _decay_tables():
    c = RET_CHUNK
    gamma = 1.0 - 2.0 ** (-5.0 - jnp.arange(RET_HEADS, dtype=F32))
    lg = jnp.log(gamma)
    i = jnp.arange(c)
    diff = i[:, None] - i[None, :]
    decay_in = jnp.where(diff[None] >= 0, jnp.exp(jnp.maximum(diff, 0)[None] * lg[:, None, None]), 0.0).astype(F32)
    k_dec = jnp.exp((c - 1 - i)[None, :] * lg[:, None]).astype(F32)
    q_dec = jnp.exp((i + 1)[None, :] * lg[:, None]).astype(F32)
    chunk_dec = jnp.exp(c * lg).astype(F32)
    kdec_t = jnp.repeat(k_dec.T, RET_DK, axis=1)
    qdec_t = jnp.repeat(q_dec.T, RET_DV, axis=1)
    cdec_t = jnp.broadcast_to(chunk_dec[:, None, None], (RET_HEADS, RET_DK, RET_DV))
    return decay_in, kdec_t, qdec_t, cdec_t


def _layer(x, mem2d, ffn1_w_in, ffn1_w_out, ln1_g, ln1_b, w_in, t5_table, ret_gn_g, ret_gn_b,
           w_mem_kv, w_br_ret, w_br_dsa, w_br_mem, w_out, ln2_g, ln2_b,
           ffn2_w_in, ffn2_w_out, ln3_g, ln3_b, tables):
    cos_t, sin_t, din, kdec, qdec, cdec = tables
    row = lambda v: v.reshape(1, -1)
    bf = lambda w: w.astype(BF16)

    w_in_bf = bf(w_in)
    w_mq = w_in_bf[:, W_MQ0:W_G0]
    w_dvt = w_in_bf[:, W_DV0:W_IQ0].T
    kdec_rows = jnp.tile(kdec, (PROJ_ROWS // RET_CHUNK, 1))

    h = _ffn_ln(x, bf(ffn1_w_in), bf(ffn1_w_out), row(ln1_g), row(ln1_b), "ffn1_ln1")
    rq, rk, rkdt, rv, rg, dq, dk, dvt, iq, mq, ikw, ikb = _proj(h, w_in_bf, w_mq, w_dvt, cos_t, sin_t, kdec_rows)
    o_ret = _retention(rq, rk, rkdt, rv, rg, din, qdec, cdec, row(ret_gn_g), row(ret_gn_b))
    o_mem = _mematt(mq, mem2d, bf(w_mem_kv))
    o_dsa = _dsa(t5_table, *_t5_bias(t5_table), iq, dq, ikw, ikb, dk, dvt)
    x2 = _merge(h, o_ret, o_dsa, o_mem, w_in_bf[:, W_G0:], bf(w_br_ret), bf(w_br_dsa), bf(w_br_mem),
                bf(w_out), row(ln2_g), row(ln2_b))
    return _ffn_ln(x2, bf(ffn2_w_in), bf(ffn2_w_out), row(ln3_g), row(ln3_b), "ffn2_ln3")


def kernel(x, mem, ffn1_w_in, ffn1_w_out, ln1_g, ln1_b, w_in, t5_table, ret_gn_g, ret_gn_b,
           w_mem_kv, w_br_ret, w_br_dsa, w_br_mem, w_out, ln2_g, ln2_b,
           ffn2_w_in, ffn2_w_out, ln3_g, ln3_b):
    assert x.shape == (BATCH, SEQ, D_MODEL) and mem.shape == (BATCH, MEM_TOKENS, D_MODEL)
    tables = _rope_tables() + _decay_tables()
    y = x.reshape(N_TOK, D_MODEL)
    mem2d = mem.reshape(BATCH * MEM_TOKENS, D_MODEL)
    for l in range(DEPTH):
        y = _layer(y, mem2d, ffn1_w_in[l], ffn1_w_out[l], ln1_g[l], ln1_b[l], w_in[l], t5_table,
                   ret_gn_g[l], ret_gn_b[l], w_mem_kv[l], w_br_ret[l], w_br_dsa[l], w_br_mem[l],
                   w_out[l], ln2_g[l], ln2_b[l], ffn2_w_in[l], ffn2_w_out[l], ln3_g[l], ln3_b[l],
                   tables)
    return y.reshape(BATCH, SEQ, D_MODEL)
```

```python
import math

import jax
import jax.numpy as jnp
from jax import lax
from jax.experimental import pallas as pl
from jax.experimental.pallas import tpu as pltpu

F32 = jnp.float32
BF16 = jnp.bfloat16

D_MODEL = 1024
BATCH = 8
SEQ = 2048
MEM_TOKENS = 256
RET_HEADS, RET_DK, RET_DV, RET_CHUNK = 4, 64, 128, 128
DSA_HEADS, DSA_DH = 8, 64
IDX_HEADS, IDX_DIM = 8, 64
TOPK = min(256, SEQ // 4)
MEM_HEADS, MEM_DH = 4, 128
T5_BUCKETS, T5_MAX_DIST = 32, 128
D_FF = 2816
ROPE_BASE = 10000.0
LN_EPS = 1e-5
NEG_INF = -1e30
DEPTH = 1
ALPHA = (2.0 * DEPTH) ** 0.25

RET_QK_W = RET_HEADS * RET_DK
RET_V_W = RET_HEADS * RET_DV
DSA_W = DSA_HEADS * DSA_DH
IDX_Q_W = IDX_HEADS * IDX_DIM
MEM_W = MEM_HEADS * MEM_DH
N_TOK = BATCH * SEQ

V7X_VMEM_BYTES = 64 * 1024 * 1024
VMEM_LIMIT = V7X_VMEM_BYTES - 8 * 1024 * 1024
LANES = 128
SUBLANES = 8

FFN_ROWS = 1024
MXU_TILE = 256
FFN_CHUNK_EDGES = (0, 6 * MXU_TILE, D_FF)
PROJ_ROWS = 1024
MERGE_ROWS = 1024
RET_ROWS = 1024
MEMATT_ROWS = 1024
DSA_T = 256
HEAD_PAD = LANES
DVT_HEAD_ROWS = DSA_DH + 16
DVT_ROWS = DSA_HEADS * DVT_HEAD_ROWS
BISECT_PLAIN_STEPS = 16
COUNT_ACCUMULATORS = 4
LOG2E = math.log2(math.e)
DEN_FLOOR = 2.0 ** -100

W_RQ0 = 0
W_RK0 = W_RQ0 + RET_QK_W
W_RV0 = W_RK0 + RET_QK_W
W_RG0 = W_RV0 + RET_V_W
W_DQ0 = W_RG0 + RET_V_W
W_DK0 = W_DQ0 + DSA_W
W_DV0 = W_DK0 + DSA_W
W_IQ0 = W_DV0 + DSA_W
W_IK0 = W_IQ0 + IDX_Q_W
W_IW0 = W_IK0 + IDX_DIM
W_MQ0 = W_IW0 + IDX_HEADS
W_G0 = W_MQ0 + MEM_W
IW_SCALE = IDX_HEADS ** -0.5 * IDX_DIM ** -0.5


def _cparams(sem):
    return pltpu.CompilerParams(dimension_semantics=sem, vmem_limit_bytes=VMEM_LIMIT)


def _const_spec(shape):
    nd = len(shape)
    return pl.BlockSpec(shape, lambda *_: (0,) * nd, pipeline_mode=pl.Buffered(1))


def _layer_norm(y, g, b):
    mu = jnp.mean(y, axis=-1, keepdims=True)
    yc = y - mu
    var = jnp.mean(yc * yc, axis=-1, keepdims=True)
    return yc * lax.rsqrt(var + LN_EPS) * g + b


def _sigmoid(x):
    return 0.5 * jnp.tanh(0.5 * x) + 0.5


def _dot(a, b):
    return jnp.dot(a, b, preferred_element_type=F32)


def _dot_nt(a, b):
    return lax.dot_general(a, b, (((1,), (1,)), ((), ())), preferred_element_type=F32)


def _ffn_ln_kernel(x_ref, wi_ref, wo_ref, g_ref, b_ref, o_ref):
    x = x_ref[...]
    xb = x.astype(BF16)
    acc = None
    for lo, hi in zip(FFN_CHUNK_EDGES[:-1], FFN_CHUNK_EDGES[1:]):
        a = _dot(xb, wi_ref[:, lo:hi])
        u = _dot(xb, wi_ref[:, D_FF + lo:D_FF + hi])
        act = (a * _sigmoid(a) * u).astype(BF16)
        part = _dot(act, wo_ref[lo:hi, :])
        acc = part if acc is None else acc + part
    o_ref[...] = _layer_norm(ALPHA * x + 0.5 * acc, g_ref[...], b_ref[...])


def _ffn_ln(x, w_in_bf, w_out_bf, g, b, name):
    return pl.pallas_call(
        _ffn_ln_kernel,
        out_shape=jax.ShapeDtypeStruct((N_TOK, D_MODEL), F32),
        grid=(N_TOK // FFN_ROWS,),
        in_specs=[
            pl.BlockSpec((FFN_ROWS, D_MODEL), lambda i: (i, 0)),
            _const_spec((D_MODEL, 2 * D_FF)),
            _const_spec((D_FF, D_MODEL)),
            _const_spec((1, D_MODEL)),
            _const_spec((1, D_MODEL)),
        ],
        out_specs=pl.BlockSpec((FFN_ROWS, D_MODEL), lambda i: (i, 0)),
        compiler_params=_cparams(("parallel",)),
        name=name,
    )(x, w_in_bf, w_out_bf, g, b)


def _rope(x, cos, sin_signed):
    width = x.shape[-1]
    lane = lax.broadcasted_iota(jnp.int32, x.shape, 1)
    first_half = (lane % RET_DK) < (RET_DK // 2)
    swapped = jnp.where(first_half,
                        pltpu.roll(x, width - RET_DK // 2, 1),
                        pltpu.roll(x, RET_DK // 2, 1))
    return x * cos + swapped * sin_signed


def _store_split_heads(dst_ref, val):
    low = lax.broadcasted_iota(jnp.int32, (val.shape[0], LANES), 1) < DSA_DH
    for j in range(val.shape[1] // LANES):
        pair = val[:, j * LANES:(j + 1) * LANES]
        dst_ref[:, (2 * j) * LANES:(2 * j + 1) * LANES] = jnp.where(low, pair, 0.0).astype(dst_ref.dtype)
        dst_ref[:, (2 * j + 1) * LANES:(2 * j + 2) * LANES] = jnp.where(low, 0.0, pair).astype(dst_ref.dtype)


def _store_low_heads(dst_ref, val):
    low = lax.broadcasted_iota(jnp.int32, (val.shape[0], LANES), 1) < RET_DK
    for j in range(val.shape[1] // LANES):
        pair = val[:, j * LANES:(j + 1) * LANES]
        dst_ref[:, (2 * j) * LANES:(2 * j + 1) * LANES] = jnp.where(low, pair, 0.0).astype(dst_ref.dtype)
        dst_ref[:, (2 * j + 1) * LANES:(2 * j + 2) * LANES] = jnp.where(
            low, pltpu.roll(pair, RET_DK, 1), 0.0).astype(dst_ref.dtype)


def _proj_kernel(h_ref, w_ref, wikw_ref, wmq_ref, wdvt_ref, cos_ref, sin_ref, kdec_ref,
                 rq_ref, rk_ref, rkdt_ref, rv_ref, rg_ref, dq_ref, dk_ref, dvt_ref, iq_ref, mq_ref,
                 ikw_ref, ikb_ref):
    hb = h_ref[...].astype(BF16)

    def proj(lo, width):
        return _dot(hb, w_ref[:, lo:lo + width])

    cos = cos_ref[...]
    sin = sin_ref[...]
    _store_low_heads(rq_ref, _rope(proj(W_RQ0, RET_QK_W), cos, sin))
    rk = _rope(proj(W_RK0, RET_QK_W), cos, sin) * (RET_DK ** -0.5)
    _store_low_heads(rk_ref, rk)
    rkdt_ref[...] = (rk * kdec_ref[...]).T.astype(BF16)
    rv_ref[...] = proj(W_RV0, RET_V_W).astype(BF16)
    rg_ref[...] = proj(W_RG0, RET_V_W)
    dq_ref[...] = (proj(W_DQ0, DSA_W) * (DSA_DH ** -0.5 * LOG2E)).T.astype(BF16)
    _store_split_heads(dk_ref, proj(W_DK0, DSA_W))
    iq_ref[...] = proj(W_IQ0, IDX_Q_W).T.astype(BF16)
    mq_ref[...] = _dot(hb, wmq_ref[...]).astype(BF16)
    ikw = _dot(hb, wikw_ref[...])
    ikw_ref[...] = ikw
    low = lax.broadcasted_iota(jnp.int32, ikw.shape, 1) < IDX_DIM
    ik_low = jnp.where(low, ikw, 0.0)
    ikb_ref[:, 0:LANES] = ik_low.astype(BF16)
    ikb_ref[:, LANES:2 * LANES] = pltpu.roll(ik_low, IDX_DIM, 1).astype(BF16)
    vt = _dot_nt(wdvt_ref[...], hb).astype(BF16)
    ones = jnp.ones((DVT_HEAD_ROWS - DSA_DH, DSA_T), BF16)
    for j in range(PROJ_ROWS // DSA_T):
        for h in range(DSA_HEADS):
            r0 = h * DVT_HEAD_ROWS
            dvt_ref[j, r0:r0 + DSA_DH, :] = vt[h * DSA_DH:(h + 1) * DSA_DH, j * DSA_T:(j + 1) * DSA_T]
            dvt_ref[j, r0 + DSA_DH:r0 + DVT_HEAD_ROWS, :] = ones


def _proj(h, w_in_bf, w_mq_bf, w_dvt_bf, cos_t, sin_t, kdec_rows):
    rows = lambda w: pl.BlockSpec((PROJ_ROWS, w), lambda i: (i, 0))
    seq_tiles = SEQ // PROJ_ROWS
    pos = lambda w: pl.BlockSpec((PROJ_ROWS, w), lambda i: (i % seq_tiles, 0))
    sd = jax.ShapeDtypeStruct
    ret_padded = RET_HEADS * HEAD_PAD
    dsa_padded = DSA_HEADS * HEAD_PAD
    slabs = PROJ_ROWS // DSA_T
    assert W_IK0 % LANES == 0
    return pl.pallas_call(
        _proj_kernel,
        out_shape=(
            sd((N_TOK, ret_padded), BF16), sd((N_TOK, ret_padded), BF16), sd((RET_QK_W, N_TOK), BF16),
            sd((N_TOK, RET_V_W), BF16), sd((N_TOK, RET_V_W), F32),
            sd((DSA_W, N_TOK), BF16), sd((N_TOK, dsa_padded), BF16),
            sd((N_TOK // DSA_T, DVT_ROWS, DSA_T), BF16),
            sd((IDX_Q_W, N_TOK), BF16), sd((N_TOK, MEM_W), BF16),
            sd((N_TOK, LANES), F32), sd((N_TOK, 2 * LANES), BF16),
        ),
        grid=(N_TOK // PROJ_ROWS,),
        in_specs=[rows(D_MODEL),
                  pl.BlockSpec((D_MODEL, W_IK0), lambda i: (0, 0), pipeline_mode=pl.Buffered(1)),
                  pl.BlockSpec((D_MODEL, LANES), lambda i: (0, W_IK0 // LANES), pipeline_mode=pl.Buffered(1)),
                  _const_spec((D_MODEL, MEM_W)), _const_spec((DSA_W, D_MODEL)),
                  pos(RET_QK_W), pos(RET_QK_W), _const_spec((PROJ_ROWS, RET_QK_W))],
        out_specs=(rows(ret_padded), rows(ret_padded),
                   pl.BlockSpec((RET_QK_W, PROJ_ROWS), lambda i: (0, i)),
                   rows(RET_V_W), rows(RET_V_W),
                   pl.BlockSpec((DSA_W, PROJ_ROWS), lambda i: (0, i)), rows(dsa_padded),
                   pl.BlockSpec((slabs, DVT_ROWS, DSA_T), lambda i: (i, 0, 0)),
                   pl.BlockSpec((IDX_Q_W, PROJ_ROWS), lambda i: (0, i)),
                   rows(MEM_W), rows(LANES), rows(2 * LANES)),
        compiler_params=_cparams(("parallel",)),
        name="mixer_proj",
    )(h, w_in_bf, w_in_bf, w_mq_bf, w_dvt_bf, cos_t, sin_t, kdec_rows)


def _retention_kernel(q_ref, k_ref, kdt_ref, v_ref, g_ref, din_ref, qdec_ref, cdec_ref,
                      gng_ref, gnb_ref, o_ref, state_ref):
    @pl.when(pl.program_id(1) == 0)
    def _():
        state_ref[...] = jnp.zeros_like(state_ref)

    qdec = qdec_ref[...]
    gn_g = gng_ref[...]
    gn_b = gnb_ref[...]
    states = [state_ref[h] for h in range(RET_HEADS)]
    pad_rows = jnp.zeros((HEAD_PAD - RET_DK, RET_DV), BF16)
    for j in range(RET_ROWS // RET_CHUNK):
        rows = slice(j * RET_CHUNK, (j + 1) * RET_CHUNK)
        gate = g_ref[rows, :]
        for h in range(RET_HEADS):
            hs = slice(h * HEAD_PAD, (h + 1) * HEAD_PAD)
            vs = slice(h * RET_DV, (h + 1) * RET_DV)
            qh = q_ref[rows, hs]
            vh = v_ref[rows, vs]
            scores = _dot_nt(qh, k_ref[rows, hs]) * din_ref[h]
            intra = _dot(scores.astype(BF16), vh)
            state_padded = jnp.concatenate([states[h].astype(BF16), pad_rows], axis=0)
            cross = _dot(qh, state_padded) * qdec[:, vs]
            kv = _dot(kdt_ref[h * RET_DK:(h + 1) * RET_DK, rows], vh)
            states[h] = cdec_ref[h] * states[h] + kv
            o = intra + cross
            mu = jnp.mean(o, axis=-1, keepdims=True)
            oc = o - mu
            var = jnp.mean(oc * oc, axis=-1, keepdims=True)
            o = oc * lax.rsqrt(var + LN_EPS) * gn_g[:, vs] + gn_b[:, vs]
            o_ref[rows, vs] = (jax.nn.silu(gate[:, vs]) * o).astype(BF16)
    for h in range(RET_HEADS):
        state_ref[h] = states[h]


def _retention(rq, rk, rkdt, rv, rg, din, qdec, cdec, gn_g, gn_b):
    nc = SEQ // RET_ROWS
    rows = lambda w: pl.BlockSpec((RET_ROWS, w), lambda b, n: (b * nc + n, 0))
    padded = RET_HEADS * HEAD_PAD
    return pl.pallas_call(
        _retention_kernel,
        out_shape=jax.ShapeDtypeStruct((N_TOK, RET_V_W), BF16),
        grid=(BATCH, nc),
        in_specs=[rows(padded), rows(padded),
                  pl.BlockSpec((RET_QK_W, RET_ROWS), lambda b, n: (0, b * nc + n)),
                  rows(RET_V_W), rows(RET_V_W),
                  _const_spec((RET_HEADS, RET_CHUNK, RET_CHUNK)),
                  _const_spec((RET_CHUNK, RET_V_W)),
                  _const_spec((RET_HEADS, RET_DK, RET_DV)),
                  _const_spec((1, RET_V_W)), _const_spec((1, RET_V_W))],
        out_specs=rows(RET_V_W),
        scratch_shapes=[pltpu.VMEM((RET_HEADS, RET_DK, RET_DV), F32)],
        compiler_params=_cparams(("parallel", "arbitrary")),
        name="retention",
    )(rq, rk, rkdt, rv, rg, din, qdec, cdec, gn_g, gn_b)


def _mematt_kernel(q_ref, mem_ref, w_ref, o_ref, kv_ref):
    @pl.when(pl.program_id(1) == 0)
    def _():
        kv_ref[...] = _dot(mem_ref[...].astype(BF16), w_ref[...]).astype(BF16)

    q = q_ref[...]
    for h in range(MEM_HEADS):
        hs = slice(h * MEM_DH, (h + 1) * MEM_DH)
        logits = _dot_nt(q[:, hs], kv_ref[:, hs]) * (MEM_DH ** -0.5)
        m = jnp.max(logits, axis=-1, keepdims=True)
        p = jnp.exp(logits - m)
        denom = jnp.sum(p, axis=-1, keepdims=True)
        o_ref[:, hs] = (_dot(p.astype(BF16), kv_ref[:, MEM_W + h * MEM_DH:MEM_W + (h + 1) * MEM_DH])
                        / denom).astype(BF16)


def _mematt(mq, mem2d, w_kv_bf):
    tiles = SEQ // MEMATT_ROWS
    return pl.pallas_call(
        _mematt_kernel,
        out_shape=jax.ShapeDtypeStruct((N_TOK, MEM_W), BF16),
        grid=(BATCH, tiles),
        in_specs=[pl.BlockSpec((MEMATT_ROWS, MEM_W), lambda b, i: (b * tiles + i, 0)),
                  pl.BlockSpec((MEM_TOKENS, D_MODEL), lambda b, i: (b, 0)),
                  _const_spec((D_MODEL, 2 * MEM_W))],
        out_specs=pl.BlockSpec((MEMATT_ROWS, MEM_W), lambda b, i: (b * tiles + i, 0)),
        scratch_shapes=[pltpu.VMEM((MEM_TOKENS, 2 * MEM_W), BF16)],
        compiler_params=_cparams(("parallel", "arbitrary")),
        name="mem_attention",
    )(mq, mem2d, w_kv_bf)


def _t5_bias_kernel(table_ref, o_ref, bmax_ref):
    m = lax.broadcasted_iota(jnp.int32, (SUBLANES, 2 * DSA_T), 1)
    max_exact = T5_BUCKETS // 2
    head_max = [None] * DSA_HEADS
    for w in range(2):
        n = jnp.maximum(m - w * DSA_T, 0)
        nf = jnp.maximum(n, 1).astype(F32)
        large = max_exact + jnp.floor(jnp.log(nf / max_exact) / math.log(T5_MAX_DIST / max_exact)
                                      * (T5_BUCKETS - max_exact)).astype(jnp.int32)
        large = jnp.minimum(large, T5_BUCKETS - 1)
        bucket = jnp.where(n < max_exact, n, large)
        for h in range(DSA_HEADS):
            vec = jnp.zeros((SUBLANES, 2 * DSA_T), F32)
            for b in range(T5_BUCKETS):
                vec = jnp.where(bucket == b, table_ref[b, h] * LOG2E, vec)
            rows = jnp.broadcast_to(vec[0:1, :], (DSA_T, 2 * DSA_T))
            o_ref[w, h] = pltpu.roll(rows, 0, 1, stride=1, stride_axis=0)[:, DSA_T:]
            top = jnp.max(vec[0:1, :], axis=1, keepdims=True)
            head_max[h] = top if head_max[h] is None else jnp.maximum(head_max[h], top)
    for h in range(DSA_HEADS):
        bmax_ref[h:h + 1, :] = jnp.broadcast_to(head_max[h], (1, LANES))


def _t5_bias(t5_table):
    return pl.pallas_call(
        _t5_bias_kernel,
        out_shape=(jax.ShapeDtypeStruct((2, DSA_HEADS, DSA_T, DSA_T), F32),
                   jax.ShapeDtypeStruct((DSA_HEADS, LANES), F32)),
        in_specs=[pl.BlockSpec(memory_space=pltpu.SMEM)],
        out_specs=(pl.BlockSpec(memory_space=pltpu.VMEM), pl.BlockSpec(memory_space=pltpu.VMEM)),
        compiler_params=pltpu.CompilerParams(vmem_limit_bytes=VMEM_LIMIT),
        name="t5_bias",
    )(t5_table)


def _fold_groups(x, op):
    return op(x.reshape(DSA_T // SUBLANES, SUBLANES, DSA_T), axis=0)


def _dsa_kernel(table_ref, bias_ref, bmax_ref, iq_ref, dq_ref, ikw_q_ref, ikb_ref, dk_ref, dvt_ref,
                o_ref, score_ref, logit_ref, acc_ref):
    qb = pl.program_id(1)
    q0 = qb * DSA_T
    n_chunks = qb + 1
    groups = DSA_T // SUBLANES
    key_in_chunk = lax.broadcasted_iota(jnp.int32, (DSA_T, DSA_T), 0)
    query_in_step = lax.broadcasted_iota(jnp.int32, (DSA_T, DSA_T), 1)

    def key_rows(c):
        return pl.ds(pl.multiple_of(c * DSA_T, DSA_T), DSA_T)

    def pair_rows(j):
        return slice(j * LANES, (j + 1) * LANES)

    def pair_products(keys_even, keys_odd, queries_t):
        prod = _dot(jnp.concatenate([keys_even, keys_odd], axis=0), queries_t)
        return prod[:DSA_T], prod[DSA_T:]

    iw_t = ikw_q_ref[...].T[IDX_DIM:IDX_DIM + IDX_HEADS, :] * IW_SCALE

    def scores_and_logits(c, carry, causal):
        raw_max, s_min, s_max = carry
        ik_low = ikb_ref[key_rows(c), 0:LANES]
        ik_high = ikb_ref[key_rows(c), LANES:2 * LANES]
        last = IDX_HEADS // 2 - 1
        for j in range(IDX_HEADS // 2):
            rel_even, rel_odd = pair_products(ik_low, ik_high, iq_ref[pair_rows(j), :])
            term = (jnp.maximum(rel_even, 0.0) * iw_t[2 * j:2 * j + 1, :]
                    + jnp.maximum(rel_odd, 0.0) * iw_t[2 * j + 1:2 * j + 2, :])
            total = term if j == 0 else score_ref[c] + term
            if j < last:
                score_ref[c] = total
            elif causal is None:
                score_ref[c] = total
                s_min = jnp.minimum(s_min, _fold_groups(total, jnp.min))
                s_max = jnp.maximum(s_max, _fold_groups(total, jnp.max))
            else:
                score_ref[c] = jnp.where(causal, total, NEG_INF)
                s_min = jnp.minimum(s_min, _fold_groups(jnp.where(causal, total, -NEG_INF), jnp.min))
                s_max = jnp.maximum(s_max, _fold_groups(jnp.where(causal, total, NEG_INF), jnp.max))
        out = list(raw_max)
        for j in range(DSA_HEADS // 2):
            keys = [dk_ref[key_rows(c), (2 * j + i) * HEAD_PAD:(2 * j + i + 1) * HEAD_PAD] for i in range(2)]
            for h, qk in zip((2 * j, 2 * j + 1), pair_products(keys[0], keys[1], dq_ref[pair_rows(j), :])):
                logit_ref[h, c] = qk
                out[h] = jnp.maximum(out[h], _fold_groups(qk, jnp.max))
        return tuple(out), s_min, s_max

    def paired_loop(start, stop, body, carry):
        n = stop - start
        carry = lax.fori_loop(
            0, lax.shift_right_logical(n, 1),
            lambda i, cr: body(start + 2 * i + 1, body(start + 2 * i, cr)), carry)
        return lax.cond(jnp.bitwise_and(n, 1) == 1, lambda cr: body(stop - 1, cr), lambda cr: cr, carry)

    low_init = jnp.full((SUBLANES, DSA_T), NEG_INF, F32)
    carry = (tuple(low_init for _ in range(DSA_HEADS)), -low_init, low_init)
    carry = paired_loop(0, qb, lambda c, cr: scores_and_logits(c, cr, None), carry)
    raw_max, s_min, s_max = scores_and_logits(qb, carry, key_in_chunk <= query_in_step)

    def count_where(pred):
        def body(c, acc):
            parts = [acc] + [jnp.zeros((SUBLANES, DSA_T), F32)] * (COUNT_ACCUMULATORS - 1)
            for g in range(groups):
                blk = score_ref[c, g * SUBLANES:(g + 1) * SUBLANES, :]
                a = g % COUNT_ACCUMULATORS
                parts[a] = parts[a] + jnp.where(pred(c, g * SUBLANES, blk), 1.0, 0.0)
            while len(parts) > 1:
                parts = [parts[i] + parts[i + 1] for i in range(0, len(parts), 2)]
            return parts[0]
        acc = paired_loop(0, n_chunks, body, jnp.zeros((SUBLANES, DSA_T), F32))
        return jnp.sum(acc, axis=0, keepdims=True)

    def count_ge(thr):
        thr_b = jnp.broadcast_to(thr, (SUBLANES, DSA_T))
        return count_where(lambda c, r0, blk: blk >= thr_b)

    q_min = jnp.min(s_min, axis=0, keepdims=True)
    q_max = jnp.max(s_max, axis=0, keepdims=True)
    t_q = q0 + lax.broadcasted_iota(jnp.int32, (1, DSA_T), 1)

    kf = float(TOPK)
    init = (q_min, q_max + jnp.maximum(jnp.abs(q_max), 1.0) * 1e-6,
            (t_q + 1).astype(F32), jnp.zeros((1, DSA_T), F32))

    def bisect_step(_i, carry):
        lo, hi, cnt_lo, cnt_hi = carry
        mid = 0.5 * (lo + hi)
        cnt = count_ge(mid)
        ge = cnt >= kf
        return (jnp.where(ge, mid, lo), jnp.where(ge, hi, mid),
                jnp.where(ge, cnt, cnt_lo), jnp.where(ge, cnt_hi, cnt))

    n_plain = jnp.where(qb == 0, 0, BISECT_PLAIN_STEPS)
    lo, hi, cnt_lo, cnt_hi = lax.fori_loop(0, n_plain, bisect_step, init)

    big = -NEG_INF

    def snap_body(c, carry):
        mn, mx = carry
        blk = score_ref[c]
        mn = jnp.minimum(mn, _fold_groups(jnp.where(blk >= lo, blk, big), jnp.min))
        mx = jnp.maximum(mx, _fold_groups(jnp.where(blk < hi, blk, -big), jnp.max))
        return mn, mx

    mn, mx = paired_loop(
        0, n_chunks, snap_body,
        (jnp.full((SUBLANES, DSA_T), big, F32), jnp.full((SUBLANES, DSA_T), -big, F32)))
    lo = jnp.min(mn, axis=0, keepdims=True)
    top = jnp.max(mx, axis=0, keepdims=True)

    def split_pair(carry):
        lo, top, cnt_lo, cnt_hi = carry
        pair = (cnt_lo > kf) & (lo < top) & (cnt_lo - cnt_hi == 2.0)
        return jnp.where(pair, top, lo), top, jnp.where(pair, cnt_hi + 1.0, cnt_lo), cnt_hi

    def pending_of(carry):
        lo, top, cnt_lo, _ = carry
        over = cnt_lo > kf
        return jnp.max(jnp.where(over & (lo < top), 2.0, jnp.where(over, 1.0, 0.0)))

    def value_step(state):
        (lo, top, cnt_lo, cnt_hi), _ = state
        unresolved = (cnt_lo > kf) & (lo < top)
        mid = lo + (top - lo) * 0.5
        mid = jnp.minimum(jnp.where(mid > lo, mid, top), top)

        def body(c, carry):
            cnt, mn, mx = carry
            blk = score_ref[c]
            is_ge = blk >= mid
            cnt = cnt + _fold_groups(jnp.where(is_ge, 1.0, 0.0), jnp.sum)
            mn = jnp.minimum(mn, _fold_groups(jnp.where(is_ge, blk, big), jnp.min))
            mx = jnp.maximum(mx, _fold_groups(jnp.where(is_ge, -big, blk), jnp.max))
            return cnt, mn, mx

        cnt, mn, mx = lax.fori_loop(
            0, n_chunks, body,
            (jnp.zeros((SUBLANES, DSA_T), F32), jnp.full((SUBLANES, DSA_T), big, F32),
             jnp.full((SUBLANES, DSA_T), -big, F32)))
        cnt = jnp.sum(cnt, axis=0, keepdims=True)
        ge = cnt >= kf
        raise_lo = unresolved & ge
        lower_top = unresolved & jnp.logical_not(ge)
        carry = split_pair((jnp.where(raise_lo, jnp.min(mn, axis=0, keepdims=True), lo),
                            jnp.where(lower_top, jnp.max(mx, axis=0, keepdims=True), top),
                            jnp.where(raise_lo, cnt, cnt_lo), jnp.where(lower_top, cnt, cnt_hi)))
        return carry, pending_of(carry)

    start = split_pair((lo, top, cnt_lo, cnt_hi))
    (lo, top, cnt_lo, cnt_hi), status = lax.while_loop(
        lambda state: state[1] > 1.5, value_step, (start, pending_of(start)))

    tied = cnt_lo > kf
    need = kf - cnt_hi
    lo_b = jnp.broadcast_to(lo, (SUBLANES, DSA_T))
    key_in_group = lax.broadcasted_iota(jnp.int32, (SUBLANES, DSA_T), 0)

    def tie_search(_):
        def body(_i, carry):
            lo_i, hi_i = carry
            mid_i = lax.shift_right_arithmetic(lo_i + hi_i, 1)
            mid_b = jnp.broadcast_to(mid_i, (SUBLANES, DSA_T))
            cnt = count_where(lambda c, r0, blk: (blk == lo_b) & (key_in_group + (c * DSA_T + r0) <= mid_b))
            enough = cnt >= need
            return jnp.where(enough, lo_i, mid_i), jnp.where(enough, mid_i, hi_i)
        lo_i = jnp.full((1, DSA_T), -1, jnp.int32)
        hi_i = jnp.full((1, DSA_T), SEQ - 1, jnp.int32)
        _, cut = lax.fori_loop(0, int(math.log2(SEQ)) + 1, body, (lo_i, hi_i))
        return jnp.where(tied, cut, SEQ)

    cut = lax.cond(status > 0.5, tie_search, lambda _: jnp.full((1, DSA_T), SEQ, jnp.int32), 0)

    def write_mask(c):
        blk = score_ref[c]
        keep = (blk > lo) | ((blk == lo) & (key_in_chunk + c * DSA_T <= cut))
        score_ref[c] = jnp.where(keep, 0.0, NEG_INF)

    n_far = jnp.maximum(qb - 1, 0)

    def masked_logits(c, h, far):
        lg = logit_ref[h, c] + score_ref[c]
        return lg if far else lg + bias_ref[c - qb + 1, h]

    def far_bias(h):
        return table_ref[T5_BUCKETS - 1, h] * LOG2E

    def den_row(h):
        return h * DVT_HEAD_ROWS + DSA_DH

    def exp_and_pv(shifts, scores_to_masks):
        acc_ref[...] = jnp.zeros_like(acc_ref)

        def body(c, carry, far):
            if scores_to_masks:
                write_mask(c)
            for h in range(DSA_HEADS):
                shift = shifts[h] - far_bias(h) if far else shifts[h]
                p = jnp.exp2(masked_logits(c, h, far) - shift).astype(BF16)
                rows = slice(h * DVT_HEAD_ROWS, (h + 1) * DVT_HEAD_ROWS)
                acc_ref[rows, :] += _dot(dvt_ref[c, rows, :], p)
            return carry

        paired_loop(0, n_far, lambda c, carry: body(c, carry, True), 0)
        paired_loop(n_far, n_chunks, lambda c, carry: body(c, carry, False), 0)

    upper = [jnp.max(raw_max[h], axis=0, keepdims=True) + bmax_ref[h:h + 1, 0:1] for h in range(DSA_HEADS)]
    exp_and_pv(upper, True)

    den_min = acc_ref[den_row(0):den_row(0) + 1, :]
    for h in range(1, DSA_HEADS):
        den_min = jnp.minimum(den_min, acc_ref[den_row(h):den_row(h) + 1, :])

    @pl.when(jnp.min(den_min) < DEN_FLOOR)
    def _():
        def max_body(c, ms, far):
            return tuple(jnp.maximum(ms[h], _fold_groups(masked_logits(c, h, far), jnp.max) + (far_bias(h) if far else 0.0))
                         for h in range(DSA_HEADS))
        ms = tuple(jnp.full((SUBLANES, DSA_T), NEG_INF, F32) for _ in range(DSA_HEADS))
        ms = lax.fori_loop(0, n_far, lambda c, ms: max_body(c, ms, True), ms)
        ms = lax.fori_loop(n_far, n_chunks, lambda c, ms: max_body(c, ms, False), ms)
        exp_and_pv([jnp.max(m, axis=0, keepdims=True) for m in ms], False)

    heads = []
    for h in range(DSA_HEADS):
        r0 = h * DVT_HEAD_ROWS
        heads.append(acc_ref[r0:r0 + DSA_DH, :] / acc_ref[den_row(h):den_row(h) + 1, :])
    o_ref[...] = jnp.concatenate(heads, axis=0).T.astype(BF16)


def _dsa(t5_table, bias, bias_max, iq, dq, ikw, ikb, dk, dvt):
    nqb = SEQ // DSA_T
    padded = DSA_HEADS * HEAD_PAD
    assert DSA_T <= TOPK
    qrows = lambda w: pl.BlockSpec((DSA_T, w), lambda b, i: (b * nqb + i, 0))
    qcols = lambda h: pl.BlockSpec((h, DSA_T), lambda b, i: (0, b * nqb + i))
    krows = lambda w: pl.BlockSpec((SEQ, w), lambda b, i: (b, 0))
    return pl.pallas_call(
        _dsa_kernel,
        out_shape=jax.ShapeDtypeStruct((N_TOK, DSA_W), BF16),
        grid=(BATCH, nqb),
        in_specs=[pl.BlockSpec(memory_space=pltpu.SMEM),
                  _const_spec((2, DSA_HEADS, DSA_T, DSA_T)), _const_spec((DSA_HEADS, LANES)),
                  qcols(IDX_Q_W), qcols(DSA_W), qrows(LANES),
                  krows(2 * LANES), krows(padded),
                  pl.BlockSpec((nqb, DVT_ROWS, DSA_T), lambda b, i: (b, 0, 0))],
        out_specs=qrows(DSA_W),
        scratch_shapes=[pltpu.VMEM((nqb, DSA_T, DSA_T), F32),
                        pltpu.VMEM((DSA_HEADS, nqb, DSA_T, DSA_T), F32),
                        pltpu.VMEM((DVT_ROWS, DSA_T), F32)],
        compiler_params=_cparams(("parallel", "arbitrary")),
        name="dsa_attention",
    )(t5_table, bias, bias_max, iq, dq, ikw, ikb, dk, dvt)


def _merge_kernel(h_ref, oret_ref, odsa_ref, omem_ref, wg_ref, wr_ref, wd_ref, wm_ref, wo_ref,
                  g_ref, b_ref, o_ref):
    h = h_ref[...]
    hb = h.astype(BF16)
    merged = None
    for i, (src, w) in enumerate(((oret_ref, wr_ref), (odsa_ref, wd_ref), (omem_ref, wm_ref))):
        gate = _sigmoid(_dot(hb, wg_ref[:, i * D_MODEL:(i + 1) * D_MODEL]))
        term = gate * _dot(src[...], w[...])
        merged = term if merged is None else merged + term
    mix = _dot(merged.astype(BF16), wo_ref[...])
    o_ref[...] = _layer_norm(ALPHA * h + mix, g_ref[...], b_ref[...])


def _merge(h, o_ret, o_dsa, o_mem, wg, wr, wd, wm, wo, g, b):
    rows = lambda w: pl.BlockSpec((MERGE_ROWS, w), lambda i: (i, 0))
    return pl.pallas_call(
        _merge_kernel,
        out_shape=jax.ShapeDtypeStruct((N_TOK, D_MODEL), F32),
        grid=(N_TOK // MERGE_ROWS,),
        in_specs=[rows(D_MODEL), rows(RET_V_W), rows(DSA_W), rows(MEM_W),
                  _const_spec((D_MODEL, 3 * D_MODEL)),
                  _const_spec((RET_V_W, D_MODEL)), _const_spec((DSA_W, D_MODEL)),
                  _const_spec((MEM_W, D_MODEL)), _const_spec((D_MODEL, D_MODEL)),
                  _const_spec((1, D_MODEL)), _const_spec((1, D_MODEL))],
        out_specs=rows(D_MODEL),
        compiler_params=_cparams(("parallel",)),
        name="merge_ln2",
    )(h, o_ret, o_dsa, o_mem, wg, wr, wd, wm, wo, g, b)


def _rope_tables():
    half = RET_DK // 2
    freqs = ROPE_BASE ** (-jnp.arange(half, dtype=F32) / half)
    ang = jnp.arange(SEQ).astype(F32)[:, None] * freqs[None, :]
    cos, sin = jnp.cos(ang), jnp.sin(ang)
    cos_t = jnp.tile(jnp.concatenate([cos, cos], axis=-1), (1, RET_HEADS))
    sin_t = jnp.tile(jnp.concatenate([-sin, sin], axis=-1), (1, RET_HEADS))
    return cos_t, sin_t


def _decay_tables():
    c = RET_CHUNK
    gamma = 1.0 - 2.0 ** (-5.0 - jnp.arange(RET_HEADS, dtype=F32))
    lg = jnp.log(gamma)
    i = jnp.arange(c)
    diff = i[:, None] - i[None, :]
    decay_in = jnp.where(diff[None] >= 0, jnp.exp(jnp.maximum(diff, 0)[None] * lg[:, None, None]), 0.0).astype(F32)
    k_dec = jnp.exp((c - 1 - i)[None, :] * lg[:, None]).astype(F32)
    q_dec = jnp.exp((i + 1)[None, :] * lg[:, None]).astype(F32)
    chunk_dec = jnp.exp(c * lg).astype(F32)
    kdec_t = jnp.repeat(k_dec.T, RET_DK, axis=1)
    qdec_t = jnp.repeat(q_dec.T, RET_DV, axis=1)
    cdec_t = jnp.broadcast_to(chunk_dec[:, None, None], (RET_HEADS, RET_DK, RET_DV))
    return decay_in, kdec_t, qdec_t, cdec_t


def _layer(x, mem2d, ffn1_w_in, ffn1_w_out, ln1_g, ln1_b, w_in, t5_table, ret_gn_g, ret_gn_b,
           w_mem_kv, w_br_ret, w_br_dsa, w_br_mem, w_out, ln2_g, ln2_b,
           ffn2_w_in, ffn2_w_out, ln3_g, ln3_b, tables):
    cos_t, sin_t, din, kdec, qdec, cdec = tables
    row = lambda v: v.reshape(1, -1)
    bf = lambda w: w.astype(BF16)

    w_in_bf = bf(w_in)
    w_mq = w_in_bf[:, W_MQ0:W_G0]
    w_dvt = w_in_bf[:, W_DV0:W_IQ0].T
    kdec_rows = jnp.tile(kdec, (PROJ_ROWS // RET_CHUNK, 1))

    h = _ffn_ln(x, bf(ffn1_w_in), bf(ffn1_w_out), row(ln1_g), row(ln1_b), "ffn1_ln1")
    rq, rk, rkdt, rv, rg, dq, dk, dvt, iq, mq, ikw, ikb = _proj(h, w_in_bf, w_mq, w_dvt, cos_t, sin_t, kdec_rows)
    o_ret = _retention(rq, rk, rkdt, rv, rg, din, qdec, cdec, row(ret_gn_g), row(ret_gn_b))
    o_mem = _mematt(mq, mem2d, bf(w_mem_kv))
    o_dsa = _dsa(t5_table, *_t5_bias(t5_table), iq, dq, ikw, ikb, dk, dvt)
    x2 = _merge(h, o_ret, o_dsa, o_mem, w_in_bf[:, W_G0:], bf(w_br_ret), bf(w_br_dsa), bf(w_br_mem),
                bf(w_out), row(ln2_g), row(ln2_b))
    return _ffn_ln(x2, bf(ffn2_w_in), bf(ffn2_w_out), row(ln3_g), row(ln3_b), "ffn2_ln3")


def kernel(x, mem, ffn1_w_in, ffn1_w_out, ln1_g, ln1_b, w_in, t5_table, ret_gn_g, ret_gn_b,
           w_mem_kv, w_br_ret, w_br_dsa, w_br_mem, w_out, ln2_g, ln2_b,
           ffn2_w_in, ffn2_w_out, ln3_g, ln3_b):
    assert x.shape == (BATCH, SEQ, D_MODEL) and mem.shape == (BATCH, MEM_TOKENS, D_MODEL)
    tables = _rope_tables() + _decay_tables()
    y = x.reshape(N_TOK, D_MODEL)
    mem2d = mem.reshape(BATCH * MEM_TOKENS, D_MODEL)
    for l in range(DEPTH):
        y = _layer(y, mem2d, ffn1_w_in[l], ffn1_w_out[l], ln1_g[l], ln1_b[l], w_in[l], t5_table,
                   ret_gn_g[l], ret_gn_b[l], w_mem_kv[l], w_br_ret[l], w_br_dsa[l], w_br_mem[l],
                   w_out[l], ln2_g[l], ln2_b[l], ffn2_w_in[l], ffn2_w_out[l], ln3_g[l], ln3_b[l],
                   tables)
    return y.reshape(BATCH, SEQ, D_MODEL)
```

```python
import math

import jax
import jax.numpy as jnp
from jax import lax
from jax.experimental import pallas as pl
from jax.experimental.pallas import tpu as pltpu

F32 = jnp.float32
BF16 = jnp.bfloat16

D_MODEL = 1024
BATCH = 8
SEQ = 2048
MEM_TOKENS = 256
RET_HEADS, RET_DK, RET_DV, RET_CHUNK = 4, 64, 128, 128
DSA_HEADS, DSA_DH = 8, 64
IDX_HEADS, IDX_DIM = 8, 64
TOPK = min(256, SEQ // 4)
MEM_HEADS, MEM_DH = 4, 128
T5_BUCKETS, T5_MAX_DIST = 32, 128
D_FF = 2816
ROPE_BASE = 10000.0
LN_EPS = 1e-5
NEG_INF = -1e30
DEPTH = 1
ALPHA = (2.0 * DEPTH) ** 0.25

RET_QK_W = RET_HEADS * RET_DK
RET_V_W = RET_HEADS * RET_DV
DSA_W = DSA_HEADS * DSA_DH
IDX_Q_W = IDX_HEADS * IDX_DIM
MEM_W = MEM_HEADS * MEM_DH
N_TOK = BATCH * SEQ

V7X_VMEM_BYTES = 64 * 1024 * 1024
VMEM_LIMIT = V7X_VMEM_BYTES - 8 * 1024 * 1024
LANES = 128
SUBLANES = 8

FFN_ROWS = 1024
MXU_TILE = 256
FFN_CHUNK_EDGES = (0, 6 * MXU_TILE, D_FF)
PROJ_ROWS = 1024
MERGE_ROWS = 1024
RET_ROWS = 2048
MEMATT_ROWS = 2048
DSA_T = 256
HEAD_PAD = LANES
DVT_HEAD_ROWS = DSA_DH + 16
DVT_ROWS = DSA_HEADS * DVT_HEAD_ROWS
BISECT_PLAIN_STEPS = 16
COUNT_ACCUMULATORS = 4
LOG2E = math.log2(math.e)
DEN_FLOOR = 2.0 ** -100

W_RQ0 = 0
W_RK0 = W_RQ0 + RET_QK_W
W_RV0 = W_RK0 + RET_QK_W
W_RG0 = W_RV0 + RET_V_W
W_DQ0 = W_RG0 + RET_V_W
W_DK0 = W_DQ0 + DSA_W
W_DV0 = W_DK0 + DSA_W
W_IQ0 = W_DV0 + DSA_W
W_IK0 = W_IQ0 + IDX_Q_W
W_IW0 = W_IK0 + IDX_DIM
W_MQ0 = W_IW0 + IDX_HEADS
W_G0 = W_MQ0 + MEM_W
IW_SCALE = IDX_HEADS ** -0.5 * IDX_DIM ** -0.5


def _cparams(sem):
    return pltpu.CompilerParams(dimension_semantics=sem, vmem_limit_bytes=VMEM_LIMIT)


def _const_spec(shape):
    nd = len(shape)
    return pl.BlockSpec(shape, lambda *_: (0,) * nd, pipeline_mode=pl.Buffered(1))


def _layer_norm(y, g, b):
    mu = jnp.mean(y, axis=-1, keepdims=True)
    yc = y - mu
    var = jnp.mean(yc * yc, axis=-1, keepdims=True)
    return yc * lax.rsqrt(var + LN_EPS) * g + b


def _sigmoid(x):
    return 0.5 * jnp.tanh(0.5 * x) + 0.5


def _dot(a, b):
    return jnp.dot(a, b, preferred_element_type=F32)


def _dot_nt(a, b):
    return lax.dot_general(a, b, (((1,), (1,)), ((), ())), preferred_element_type=F32)


def _ffn_ln_kernel(x_ref, wi_ref, wo_ref, g_ref, b_ref, o_ref):
    x = x_ref[...]
    xb = x.astype(BF16)
    acc = None
    for lo, hi in zip(FFN_CHUNK_EDGES[:-1], FFN_CHUNK_EDGES[1:]):
        a = _dot(xb, wi_ref[:, lo:hi])
        u = _dot(xb, wi_ref[:, D_FF + lo:D_FF + hi])
        act = (a * _sigmoid(a) * u).astype(BF16)
        part = _dot(act, wo_ref[lo:hi, :])
        acc = part if acc is None else acc + part
    o_ref[...] = _layer_norm(ALPHA * x + 0.5 * acc, g_ref[...], b_ref[...])


def _ffn_ln(x, w_in_bf, w_out_bf, g, b, name):
    return pl.pallas_call(
        _ffn_ln_kernel,
        out_shape=jax.ShapeDtypeStruct((N_TOK, D_MODEL), F32),
        grid=(N_TOK // FFN_ROWS,),
        in_specs=[
            pl.BlockSpec((FFN_ROWS, D_MODEL), lambda i: (i, 0)),
            _const_spec((D_MODEL, 2 * D_FF)),
            _const_spec((D_FF, D_MODEL)),
            _const_spec((1, D_MODEL)),
            _const_spec((1, D_MODEL)),
        ],
        out_specs=pl.BlockSpec((FFN_ROWS, D_MODEL), lambda i: (i, 0)),
        compiler_params=_cparams(("parallel",)),
        name=name,
    )(x, w_in_bf, w_out_bf, g, b)


def _rope(x, cos, sin_signed):
    width = x.shape[-1]
    lane = lax.broadcasted_iota(jnp.int32, x.shape, 1)
    first_half = (lane % RET_DK) < (RET_DK // 2)
    swapped = jnp.where(first_half,
                        pltpu.roll(x, width - RET_DK // 2, 1),
                        pltpu.roll(x, RET_DK // 2, 1))
    return x * cos + swapped * sin_signed


def _store_split_heads(dst_ref, val):
    low = lax.broadcasted_iota(jnp.int32, (val.shape[0], LANES), 1) < DSA_DH
    for j in range(val.shape[1] // LANES):
        pair = val[:, j * LANES:(j + 1) * LANES]
        dst_ref[:, (2 * j) * LANES:(2 * j + 1) * LANES] = jnp.where(low, pair, 0.0).astype(dst_ref.dtype)
        dst_ref[:, (2 * j + 1) * LANES:(2 * j + 2) * LANES] = jnp.where(low, 0.0, pair).astype(dst_ref.dtype)


def _store_low_heads(dst_ref, val):
    low = lax.broadcasted_iota(jnp.int32, (val.shape[0], LANES), 1) < RET_DK
    for j in range(val.shape[1] // LANES):
        pair = val[:, j * LANES:(j + 1) * LANES]
        dst_ref[:, (2 * j) * LANES:(2 * j + 1) * LANES] = jnp.where(low, pair, 0.0).astype(dst_ref.dtype)
        dst_ref[:, (2 * j + 1) * LANES:(2 * j + 2) * LANES] = jnp.where(
            low, pltpu.roll(pair, RET_DK, 1), 0.0).astype(dst_ref.dtype)


def _proj_kernel(h_ref, w_ref, wikw_ref, wmq_ref, wdvt_ref, cos_ref, sin_ref, kdec_ref,
                 rq_ref, rk_ref, rkdt_ref, rv_ref, rg_ref, dq_ref, dk_ref, dvt_ref, iq_ref, mq_ref,
                 ikw_ref, ikb_ref):
    hb = h_ref[...].astype(BF16)

    def proj(lo, width):
        return _dot(hb, w_ref[:, lo:lo + width])

    cos = cos_ref[...]
    sin = sin_ref[...]
    _store_low_heads(rq_ref, _rope(proj(W_RQ0, RET_QK_W), cos, sin))
    rk = _rope(proj(W_RK0, RET_QK_W), cos, sin) * (RET_DK ** -0.5)
    _store_low_heads(rk_ref, rk)
    rkdt_ref[...] = (rk * kdec_ref[...]).T.astype(BF16)
    rv_ref[...] = proj(W_RV0, RET_V_W).astype(BF16)
    rg_ref[...] = proj(W_RG0, RET_V_W)
    dq_ref[...] = (proj(W_DQ0, DSA_W) * (DSA_DH ** -0.5 * LOG2E)).T.astype(BF16)
    _store_split_heads(dk_ref, proj(W_DK0, DSA_W))
    iq_ref[...] = proj(W_IQ0, IDX_Q_W).T.astype(BF16)
    mq_ref[...] = _dot(hb, wmq_ref[...]).astype(BF16)
    ikw = _dot(hb, wikw_ref[...])
    ikw_ref[...] = ikw
    low = lax.broadcasted_iota(jnp.int32, ikw.shape, 1) < IDX_DIM
    ik_low = jnp.where(low, ikw, 0.0)
    ikb_ref[:, 0:LANES] = ik_low.astype(BF16)
    ikb_ref[:, LANES:2 * LANES] = pltpu.roll(ik_low, IDX_DIM, 1).astype(BF16)
    vt = _dot_nt(wdvt_ref[...], hb).astype(BF16)
    ones = jnp.ones((DVT_HEAD_ROWS - DSA_DH, DSA_T), BF16)
    for j in range(PROJ_ROWS // DSA_T):
        for h in range(DSA_HEADS):
            r0 = h * DVT_HEAD_ROWS
            dvt_ref[j, r0:r0 + DSA_DH, :] = vt[h * DSA_DH:(h + 1) * DSA_DH, j * DSA_T:(j + 1) * DSA_T]
            dvt_ref[j, r0 + DSA_DH:r0 + DVT_HEAD_ROWS, :] = ones


def _proj(h, w_in_bf, w_mq_bf, w_dvt_bf, cos_t, sin_t, kdec_rows):
    rows = lambda w: pl.BlockSpec((PROJ_ROWS, w), lambda i: (i, 0))
    seq_tiles = SEQ // PROJ_ROWS
    pos = lambda w: pl.BlockSpec((PROJ_ROWS, w), lambda i: (i % seq_tiles, 0))
    sd = jax.ShapeDtypeStruct
    ret_padded = RET_HEADS * HEAD_PAD
    dsa_padded = DSA_HEADS * HEAD_PAD
    slabs = PROJ_ROWS // DSA_T
    assert W_IK0 % LANES == 0
    return pl.pallas_call(
        _proj_kernel,
        out_shape=(
            sd((N_TOK, ret_padded), BF16), sd((N_TOK, ret_padded), BF16), sd((RET_QK_W, N_TOK), BF16),
            sd((N_TOK, RET_V_W), BF16), sd((N_TOK, RET_V_W), F32),
            sd((DSA_W, N_TOK), BF16), sd((N_TOK, dsa_padded), BF16),
            sd((N_TOK // DSA_T, DVT_ROWS, DSA_T), BF16),
            sd((IDX_Q_W, N_TOK), BF16), sd((N_TOK, MEM_W), BF16),
            sd((N_TOK, LANES), F32), sd((N_TOK, 2 * LANES), BF16),
        ),
        grid=(N_TOK // PROJ_ROWS,),
        in_specs=[rows(D_MODEL),
                  pl.BlockSpec((D_MODEL, W_IK0), lambda i: (0, 0), pipeline_mode=pl.Buffered(1)),
                  pl.BlockSpec((D_MODEL, LANES), lambda i: (0, W_IK0 // LANES), pipeline_mode=pl.Buffered(1)),
                  _const_spec((D_MODEL, MEM_W)), _const_spec((DSA_W, D_MODEL)),
                  pos(RET_QK_W), pos(RET_QK_W), _const_spec((PROJ_ROWS, RET_QK_W))],
        out_specs=(rows(ret_padded), rows(ret_padded),
                   pl.BlockSpec((RET_QK_W, PROJ_ROWS), lambda i: (0, i)),
                   rows(RET_V_W), rows(RET_V_W),
                   pl.BlockSpec((DSA_W, PROJ_ROWS), lambda i: (0, i)), rows(dsa_padded),
                   pl.BlockSpec((slabs, DVT_ROWS, DSA_T), lambda i: (i, 0, 0)),
                   pl.BlockSpec((IDX_Q_W, PROJ_ROWS), lambda i: (0, i)),
                   rows(MEM_W), rows(LANES), rows(2 * LANES)),
        compiler_params=_cparams(("parallel",)),
        name="mixer_proj",
    )(h, w_in_bf, w_in_bf, w_mq_bf, w_dvt_bf, cos_t, sin_t, kdec_rows)


def _retention_kernel(q_ref, k_ref, kdt_ref, v_ref, g_ref, din_ref, qdec_ref, cdec_ref,
                      gng_ref, gnb_ref, o_ref, state_ref):
    @pl.when(pl.program_id(1) == 0)
    def _():
        state_ref[...] = jnp.zeros_like(state_ref)

    qdec = qdec_ref[...]
    gn_g = gng_ref[...]
    gn_b = gnb_ref[...]
    states = [state_ref[h] for h in range(RET_HEADS)]
    pad_rows = jnp.zeros((HEAD_PAD - RET_DK, RET_DV), BF16)
    for j in range(RET_ROWS // RET_CHUNK):
        rows = slice(j * RET_CHUNK, (j + 1) * RET_CHUNK)
        gate = g_ref[rows, :]
        for h in range(RET_HEADS):
            hs = slice(h * HEAD_PAD, (h + 1) * HEAD_PAD)
            vs = slice(h * RET_DV, (h + 1) * RET_DV)
            qh = q_ref[rows, hs]
            vh = v_ref[rows, vs]
            scores = _dot_nt(qh, k_ref[rows, hs]) * din_ref[h]
            intra = _dot(scores.astype(BF16), vh)
            state_padded = jnp.concatenate([states[h].astype(BF16), pad_rows], axis=0)
            cross = _dot(qh, state_padded) * qdec[:, vs]
            kv = _dot(kdt_ref[h * RET_DK:(h + 1) * RET_DK, rows], vh)
            states[h] = cdec_ref[h] * states[h] + kv
            o = intra + cross
            mu = jnp.mean(o, axis=-1, keepdims=True)
            oc = o - mu
            var = jnp.mean(oc * oc, axis=-1, keepdims=True)
            o = oc * lax.rsqrt(var + LN_EPS) * gn_g[:, vs] + gn_b[:, vs]
            o_ref[rows, vs] = (jax.nn.silu(gate[:, vs]) * o).astype(BF16)
    for h in range(RET_HEADS):
        state_ref[h] = states[h]


def _retention(rq, rk, rkdt, rv, rg, din, qdec, cdec, gn_g, gn_b):
    nc = SEQ // RET_ROWS
    rows = lambda w: pl.BlockSpec((RET_ROWS, w), lambda b, n: (b * nc + n, 0))
    padded = RET_HEADS * HEAD_PAD
    return pl.pallas_call(
        _retention_kernel,
        out_shape=jax.ShapeDtypeStruct((N_TOK, RET_V_W), BF16),
        grid=(BATCH, nc),
        in_specs=[rows(padded), rows(padded),
                  pl.BlockSpec((RET_QK_W, RET_ROWS), lambda b, n: (0, b * nc + n)),
                  rows(RET_V_W), rows(RET_V_W),
                  _const_spec((RET_HEADS, RET_CHUNK, RET_CHUNK)),
                  _const_spec((RET_CHUNK, RET_V_W)),
                  _const_spec((RET_HEADS, RET_DK, RET_DV)),
                  _const_spec((1, RET_V_W)), _const_spec((1, RET_V_W))],
        out_specs=rows(RET_V_W),
        scratch_shapes=[pltpu.VMEM((RET_HEADS, RET_DK, RET_DV), F32)],
        compiler_params=_cparams(("parallel", "arbitrary")),
        name="retention",
    )(rq, rk, rkdt, rv, rg, din, qdec, cdec, gn_g, gn_b)


def _mematt_kernel(q_ref, mem_ref, w_ref, o_ref, kv_ref):
    @pl.when(pl.program_id(1) == 0)
    def _():
        kv_ref[...] = _dot(mem_ref[...].astype(BF16), w_ref[...]).astype(BF16)

    q = q_ref[...]
    for h in range(MEM_HEADS):
        hs = slice(h * MEM_DH, (h + 1) * MEM_DH)
        logits = _dot_nt(q[:, hs], kv_ref[:, hs]) * (MEM_DH ** -0.5)
        m = jnp.max(logits, axis=-1, keepdims=True)
        p = jnp.exp(logits - m)
        denom = jnp.sum(p, axis=-1, keepdims=True)
        o_ref[:, hs] = (_dot(p.astype(BF16), kv_ref[:, MEM_W + h * MEM_DH:MEM_W + (h + 1) * MEM_DH])
                        / denom).astype(BF16)


def _mematt(mq, mem2d, w_kv_bf):
    tiles = SEQ // MEMATT_ROWS
    return pl.pallas_call(
        _mematt_kernel,
        out_shape=jax.ShapeDtypeStruct((N_TOK, MEM_W), BF16),
        grid=(BATCH, tiles),
        in_specs=[pl.BlockSpec((MEMATT_ROWS, MEM_W), lambda b, i: (b * tiles + i, 0)),
                  pl.BlockSpec((MEM_TOKENS, D_MODEL), lambda b, i: (b, 0)),
                  _const_spec((D_MODEL, 2 * MEM_W))],
        out_specs=pl.BlockSpec((MEMATT_ROWS, MEM_W), lambda b, i: (b * tiles + i, 0)),
        scratch_shapes=[pltpu.VMEM((MEM_TOKENS, 2 * MEM_W), BF16)],
        compiler_params=_cparams(("parallel", "arbitrary")),
        name="mem_attention",
    )(mq, mem2d, w_kv_bf)


def _t5_bias_kernel(table_ref, o_ref, bmax_ref):
    m = lax.broadcasted_iota(jnp.int32, (SUBLANES, 2 * DSA_T), 1)
    max_exact = T5_BUCKETS // 2
    head_max = [None] * DSA_HEADS
    for w in range(2):
        n = jnp.maximum(m - w * DSA_T, 0)
        nf = jnp.maximum(n, 1).astype(F32)
        large = max_exact + jnp.floor(jnp.log(nf / max_exact) / math.log(T5_MAX_DIST / max_exact)
                                      * (T5_BUCKETS - max_exact)).astype(jnp.int32)
        large = jnp.minimum(large, T5_BUCKETS - 1)
        bucket = jnp.where(n < max_exact, n, large)
        for h in range(DSA_HEADS):
            vec = jnp.zeros((SUBLANES, 2 * DSA_T), F32)
            for b in range(T5_BUCKETS):
                vec = jnp.where(bucket == b, table_ref[b, h] * LOG2E, vec)
            rows = jnp.broadcast_to(vec[0:1, :], (DSA_T, 2 * DSA_T))
            o_ref[w, h] = pltpu.roll(rows, 0, 1, stride=1, stride_axis=0)[:, DSA_T:]
            top = jnp.max(vec[0:1, :], axis=1, keepdims=True)
            head_max[h] = top if head_max[h] is None else jnp.maximum(head_max[h], top)
    for h in range(DSA_HEADS):
        bmax_ref[h:h + 1, :] = jnp.broadcast_to(head_max[h], (1, LANES))


def _t5_bias(t5_table):
    return pl.pallas_call(
        _t5_bias_kernel,
        out_shape=(jax.ShapeDtypeStruct((2, DSA_HEADS, DSA_T, DSA_T), F32),
                   jax.ShapeDtypeStruct((DSA_HEADS, LANES), F32)),
        in_specs=[pl.BlockSpec(memory_space=pltpu.SMEM)],
        out_specs=(pl.BlockSpec(memory_space=pltpu.VMEM), pl.BlockSpec(memory_space=pltpu.VMEM)),
        compiler_params=pltpu.CompilerParams(vmem_limit_bytes=VMEM_LIMIT),
        name="t5_bias",
    )(t5_table)


def _fold_groups(x, op):
    return op(x.reshape(DSA_T // SUBLANES, SUBLANES, DSA_T), axis=0)


def _dsa_kernel(table_ref, bias_ref, bmax_ref, iq_ref, dq_ref, ikw_q_ref, ikb_ref, dk_ref, dvt_ref,
                o_ref, score_ref, logit_ref, acc_ref):
    qb = pl.program_id(1)
    q0 = qb * DSA_T
    n_chunks = qb + 1
    groups = DSA_T // SUBLANES
    key_in_chunk = lax.broadcasted_iota(jnp.int32, (DSA_T, DSA_T), 0)
    query_in_step = lax.broadcasted_iota(jnp.int32, (DSA_T, DSA_T), 1)

    def key_rows(c):
        return pl.ds(pl.multiple_of(c * DSA_T, DSA_T), DSA_T)

    def pair_rows(j):
        return slice(j * LANES, (j + 1) * LANES)

    def pair_products(keys_even, keys_odd, queries_t):
        prod = _dot(jnp.concatenate([keys_even, keys_odd], axis=0), queries_t)
        return prod[:DSA_T], prod[DSA_T:]

    iw_t = ikw_q_ref[...].T[IDX_DIM:IDX_DIM + IDX_HEADS, :] * IW_SCALE

    def scores_and_logits(c, carry, causal):
        raw_max, s_min, s_max = carry
        ik_low = ikb_ref[key_rows(c), 0:LANES]
        ik_high = ikb_ref[key_rows(c), LANES:2 * LANES]
        last = IDX_HEADS // 2 - 1
        for j in range(IDX_HEADS // 2):
            rel_even, rel_odd = pair_products(ik_low, ik_high, iq_ref[pair_rows(j), :])
            term = (jnp.maximum(rel_even, 0.0) * iw_t[2 * j:2 * j + 1, :]
                    + jnp.maximum(rel_odd, 0.0) * iw_t[2 * j + 1:2 * j + 2, :])
            total = term if j == 0 else score_ref[c] + term
            if j < last:
                score_ref[c] = total
            elif causal is None:
                score_ref[c] = total
                s_min = jnp.minimum(s_min, _fold_groups(total, jnp.min))
                s_max = jnp.maximum(s_max, _fold_groups(total, jnp.max))
            else:
                score_ref[c] = jnp.where(causal, total, NEG_INF)
                s_min = jnp.minimum(s_min, _fold_groups(jnp.where(causal, total, -NEG_INF), jnp.min))
                s_max = jnp.maximum(s_max, _fold_groups(jnp.where(causal, total, NEG_INF), jnp.max))
        out = list(raw_max)
        for j in range(DSA_HEADS // 2):
            keys = [dk_ref[key_rows(c), (2 * j + i) * HEAD_PAD:(2 * j + i + 1) * HEAD_PAD] for i in range(2)]
            for h, qk in zip((2 * j, 2 * j + 1), pair_products(keys[0], keys[1], dq_ref[pair_rows(j), :])):
                logit_ref[h, c] = qk
                out[h] = jnp.maximum(out[h], _fold_groups(qk, jnp.max))
        return tuple(out), s_min, s_max

    def paired_loop(start, stop, body, carry):
        n = stop - start
        carry = lax.fori_loop(
            0, lax.shift_right_logical(n, 1),
            lambda i, cr: body(start + 2 * i + 1, body(start + 2 * i, cr)), carry)
        return lax.cond(jnp.bitwise_and(n, 1) == 1, lambda cr: body(stop - 1, cr), lambda cr: cr, carry)

    low_init = jnp.full((SUBLANES, DSA_T), NEG_INF, F32)
    carry = (tuple(low_init for _ in range(DSA_HEADS)), -low_init, low_init)
    carry = paired_loop(0, qb, lambda c, cr: scores_and_logits(c, cr, None), carry)
    raw_max, s_min, s_max = scores_and_logits(qb, carry, key_in_chunk <= query_in_step)

    def count_where(pred):
        def body(c, acc):
            parts = [acc] + [jnp.zeros((SUBLANES, DSA_T), F32)] * (COUNT_ACCUMULATORS - 1)
            for g in range(groups):
                blk = score_ref[c, g * SUBLANES:(g + 1) * SUBLANES, :]
                a = g % COUNT_ACCUMULATORS
                parts[a] = parts[a] + jnp.where(pred(c, g * SUBLANES, blk), 1.0, 0.0)
            while len(parts) > 1:
                parts = [parts[i] + parts[i + 1] for i in range(0, len(parts), 2)]
            return parts[0]
        acc = paired_loop(0, n_chunks, body, jnp.zeros((SUBLANES, DSA_T), F32))
        return jnp.sum(acc, axis=0, keepdims=True)

    def count_ge(thr):
        thr_b = jnp.broadcast_to(thr, (SUBLANES, DSA_T))
        return count_where(lambda c, r0, blk: blk >= thr_b)

    q_min = jnp.min(s_min, axis=0, keepdims=True)
    q_max = jnp.max(s_max, axis=0, keepdims=True)
    t_q = q0 + lax.broadcasted_iota(jnp.int32, (1, DSA_T), 1)

    kf = float(TOPK)
    init = (q_min, q_max + jnp.maximum(jnp.abs(q_max), 1.0) * 1e-6,
            (t_q + 1).astype(F32), jnp.zeros((1, DSA_T), F32))

    def bisect_step(_i, carry):
        lo, hi, cnt_lo, cnt_hi = carry
        mid = 0.5 * (lo + hi)
        cnt = count_ge(mid)
        ge = cnt >= kf
        return (jnp.where(ge, mid, lo), jnp.where(ge, hi, mid),
                jnp.where(ge, cnt, cnt_lo), jnp.where(ge, cnt_hi, cnt))

    n_plain = jnp.where(qb == 0, 0, BISECT_PLAIN_STEPS)
    lo, hi, cnt_lo, cnt_hi = lax.fori_loop(0, n_plain, bisect_step, init)

    big = -NEG_INF

    def snap_body(c, carry):
        mn, mx = carry
        blk = score_ref[c]
        mn = jnp.minimum(mn, _fold_groups(jnp.where(blk >= lo, blk, big), jnp.min))
        mx = jnp.maximum(mx, _fold_groups(jnp.where(blk < hi, blk, -big), jnp.max))
        return mn, mx

    mn, mx = paired_loop(
        0, n_chunks, snap_body,
        (jnp.full((SUBLANES, DSA_T), big, F32), jnp.full((SUBLANES, DSA_T), -big, F32)))
    lo = jnp.min(mn, axis=0, keepdims=True)
    top = jnp.max(mx, axis=0, keepdims=True)

    def split_pair(carry):
        lo, top, cnt_lo, cnt_hi = carry
        pair = (cnt_lo > kf) & (lo < top) & (cnt_lo - cnt_hi == 2.0)
        return jnp.where(pair, top, lo), top, jnp.where(pair, cnt_hi + 1.0, cnt_lo), cnt_hi

    def pending_of(carry):
        lo, top, cnt_lo, _ = carry
        over = cnt_lo > kf
        return jnp.max(jnp.where(over & (lo < top), 2.0, jnp.where(over, 1.0, 0.0)))

    def value_step(state):
        (lo, top, cnt_lo, cnt_hi), _ = state
        unresolved = (cnt_lo > kf) & (lo < top)
        mid = lo + (top - lo) * 0.5
        mid = jnp.minimum(jnp.where(mid > lo, mid, top), top)

        def body(c, carry):
            cnt, mn, mx = carry
            blk = score_ref[c]
            is_ge = blk >= mid
            cnt = cnt + _fold_groups(jnp.where(is_ge, 1.0, 0.0), jnp.sum)
            mn = jnp.minimum(mn, _fold_groups(jnp.where(is_ge, blk, big), jnp.min))
            mx = jnp.maximum(mx, _fold_groups(jnp.where(is_ge, -big, blk), jnp.max))
            return cnt, mn, mx

        cnt, mn, mx = lax.fori_loop(
            0, n_chunks, body,
            (jnp.zeros((SUBLANES, DSA_T), F32), jnp.full((SUBLANES, DSA_T), big, F32),
             jnp.full((SUBLANES, DSA_T), -big, F32)))
        cnt = jnp.sum(cnt, axis=0, keepdims=True)
        ge = cnt >= kf
        raise_lo = unresolved & ge
        lower_top = unresolved & jnp.logical_not(ge)
        carry = split_pair((jnp.where(raise_lo, jnp.min(mn, axis=0, keepdims=True), lo),
                            jnp.where(lower_top, jnp.max(mx, axis=0, keepdims=True), top),
                            jnp.where(raise_lo, cnt, cnt_lo), jnp.where(lower_top, cnt, cnt_hi)))
        return carry, pending_of(carry)

    start = split_pair((lo, top, cnt_lo, cnt_hi))
    (lo, top, cnt_lo, cnt_hi), status = lax.while_loop(
        lambda state: state[1] > 1.5, value_step, (start, pending_of(start)))

    tied = cnt_lo > kf
    need = kf - cnt_hi
    lo_b = jnp.broadcast_to(lo, (SUBLANES, DSA_T))
    key_in_group = lax.broadcasted_iota(jnp.int32, (SUBLANES, DSA_T), 0)

    def tie_search(_):
        def body(_i, carry):
            lo_i, hi_i = carry
            mid_i = lax.shift_right_arithmetic(lo_i + hi_i, 1)
            mid_b = jnp.broadcast_to(mid_i, (SUBLANES, DSA_T))
            cnt = count_where(lambda c, r0, blk: (blk == lo_b) & (key_in_group + (c * DSA_T + r0) <= mid_b))
            enough = cnt >= need
            return jnp.where(enough, lo_i, mid_i), jnp.where(enough, mid_i, hi_i)
        lo_i = jnp.full((1, DSA_T), -1, jnp.int32)
        hi_i = jnp.full((1, DSA_T), SEQ - 1, jnp.int32)
        _, cut = lax.fori_loop(0, int(math.log2(SEQ)) + 1, body, (lo_i, hi_i))
        return jnp.where(tied, cut, SEQ)

    cut = lax.cond(status > 0.5, tie_search, lambda _: jnp.full((1, DSA_T), SEQ, jnp.int32), 0)

    def write_mask(c):
        blk = score_ref[c]
        keep = (blk > lo) | ((blk == lo) & (key_in_chunk + c * DSA_T <= cut))
        score_ref[c] = jnp.where(keep, 0.0, NEG_INF)

    n_far = jnp.maximum(qb - 1, 0)

    def masked_logits(c, h, far):
        lg = logit_ref[h, c] + score_ref[c]
        return lg if far else lg + bias_ref[c - qb + 1, h]

    def far_bias(h):
        return table_ref[T5_BUCKETS - 1, h] * LOG2E

    def den_row(h):
        return h * DVT_HEAD_ROWS + DSA_DH

    def exp_and_pv(shifts, scores_to_masks):
        acc_ref[...] = jnp.zeros_like(acc_ref)

        def body(c, carry, far):
            if scores_to_masks:
                write_mask(c)
            for h in range(DSA_HEADS):
                shift = shifts[h] - far_bias(h) if far else shifts[h]
                p = jnp.exp2(masked_logits(c, h, far) - shift).astype(BF16)
                rows = slice(h * DVT_HEAD_ROWS, (h + 1) * DVT_HEAD_ROWS)
                acc_ref[rows, :] += _dot(dvt_ref[c, rows, :], p)
            return carry

        paired_loop(0, n_far, lambda c, carry: body(c, carry, True), 0)
        paired_loop(n_far, n_chunks, lambda c, carry: body(c, carry, False), 0)

    upper = [jnp.max(raw_max[h], axis=0, keepdims=True) + bmax_ref[h:h + 1, 0:1] for h in range(DSA_HEADS)]
    exp_and_pv(upper, True)

    den_min = acc_ref[den_row(0):den_row(0) + 1, :]
    for h in range(1, DSA_HEADS):
        den_min = jnp.minimum(den_min, acc_ref[den_row(h):den_row(h) + 1, :])

    @pl.when(jnp.min(den_min) < DEN_FLOOR)
    def _():
        def max_body(c, ms, far):
            return tuple(jnp.maximum(ms[h], _fold_groups(masked_logits(c, h, far), jnp.max) + (far_bias(h) if far else 0.0))
                         for h in range(DSA_HEADS))
        ms = tuple(jnp.full((SUBLANES, DSA_T), NEG_INF, F32) for _ in range(DSA_HEADS))
        ms = lax.fori_loop(0, n_far, lambda c, ms: max_body(c, ms, True), ms)
        ms = lax.fori_loop(n_far, n_chunks, lambda c, ms: max_body(c, ms, False), ms)
        exp_and_pv([jnp.max(m, axis=0, keepdims=True) for m in ms], False)

    heads = []
    for h in range(DSA_HEADS):
        r0 = h * DVT_HEAD_ROWS
        heads.append(acc_ref[r0:r0 + DSA_DH, :] / acc_ref[den_row(h):den_row(h) + 1, :])
    o_ref[...] = jnp.concatenate(heads, axis=0).T.astype(BF16)


def _dsa(t5_table, bias, bias_max, iq, dq, ikw, ikb, dk, dvt):
    nqb = SEQ // DSA_T
    padded = DSA_HEADS * HEAD_PAD
    assert DSA_T <= TOPK
    qrows = lambda w: pl.BlockSpec((DSA_T, w), lambda b, i: (b * nqb + i, 0))
    qcols = lambda h: pl.BlockSpec((h, DSA_T), lambda b, i: (0, b * nqb + i))
    krows = lambda w: pl.BlockSpec((SEQ, w), lambda b, i: (b, 0))
    return pl.pallas_call(
        _dsa_kernel,
        out_shape=jax.ShapeDtypeStruct((N_TOK, DSA_W), BF16),
        grid=(BATCH, nqb),
        in_specs=[pl.BlockSpec(memory_space=pltpu.SMEM),
                  _const_spec((2, DSA_HEADS, DSA_T, DSA_T)), _const_spec((DSA_HEADS, LANES)),
                  qcols(IDX_Q_W), qcols(DSA_W), qrows(LANES),
                  krows(2 * LANES), krows(padded),
                  pl.BlockSpec((nqb, DVT_ROWS, DSA_T), lambda b, i: (b, 0, 0))],
        out_specs=qrows(DSA_W),
        scratch_shapes=[pltpu.VMEM((nqb, DSA_T, DSA_T), F32),
                        pltpu.VMEM((DSA_HEADS, nqb, DSA_T, DSA_T), F32),
                        pltpu.VMEM((DVT_ROWS, DSA_T), F32)],
        compiler_params=_cparams(("parallel", "arbitrary")),
        name="dsa_attention",
    )(t5_table, bias, bias_max, iq, dq, ikw, ikb, dk, dvt)


def _merge_kernel(h_ref, oret_ref, odsa_ref, omem_ref, wg_ref, wr_ref, wd_ref, wm_ref, wo_ref,
                  g_ref, b_ref, o_ref):
    h = h_ref[...]
    hb = h.astype(BF16)
    merged = None
    for i, (src, w) in enumerate(((oret_ref, wr_ref), (odsa_ref, wd_ref), (omem_ref, wm_ref))):
        gate = _sigmoid(_dot(hb, wg_ref[:, i * D_MODEL:(i + 1) * D_MODEL]))
        term = gate * _dot(src[...], w[...])
        merged = term if merged is None else merged + term
    mix = _dot(merged.astype(BF16), wo_ref[...])
    o_ref[...] = _layer_norm(ALPHA * h + mix, g_ref[...], b_ref[...])


def _merge(h, o_ret, o_dsa, o_mem, wg, wr, wd, wm, wo, g, b):
    rows = lambda w: pl.BlockSpec((MERGE_ROWS, w), lambda i: (i, 0))
    return pl.pallas_call(
        _merge_kernel,
        out_shape=jax.ShapeDtypeStruct((N_TOK, D_MODEL), F32),
        grid=(N_TOK // MERGE_ROWS,),
        in_specs=[rows(D_MODEL), rows(RET_V_W), rows(DSA_W), rows(MEM_W),
                  _const_spec((D_MODEL, 3 * D_MODEL)),
                  _const_spec((RET_V_W, D_MODEL)), _const_spec((DSA_W, D_MODEL)),
                  _const_spec((MEM_W, D_MODEL)), _const_spec((D_MODEL, D_MODEL)),
                  _const_spec((1, D_MODEL)), _const_spec((1, D_MODEL))],
        out_specs=rows(D_MODEL),
        compiler_params=_cparams(("parallel",)),
        name="merge_ln2",
    )(h, o_ret, o_dsa, o_mem, wg, wr, wd, wm, wo, g, b)


def _rope_tables():
    half = RET_DK // 2
    freqs = ROPE_BASE ** (-jnp.arange(half, dtype=F32) / half)
    ang = jnp.arange(SEQ).astype(F32)[:, None] * freqs[None, :]
    cos, sin = jnp.cos(ang), jnp.sin(ang)
    cos_t = jnp.tile(jnp.concatenate([cos, cos], axis=-1), (1, RET_HEADS))
    sin_t = jnp.tile(jnp.concatenate([-sin, sin], axis=-1), (1, RET_HEADS))
    return cos_t, sin_t


def _decay_tables():
    c = RET_CHUNK
    gamma = 1.0 - 2.0 ** (-5.0 - jnp.arange(RET_HEADS, dtype=F32))
    lg = jnp.log(gamma)
    i = jnp.arange(c)
    diff = i[:, None] - i[None, :]
    decay_in = jnp.where(diff[None] >= 0, jnp.exp(jnp.maximum(diff, 0)[None] * lg[:, None, None]), 0.0).astype(F32)
    k_dec = jnp.exp((c - 1 - i)[None, :] * lg[:, None]).astype(F32)
    q_dec = jnp.exp((i + 1)[None, :] * lg[:, None]).astype(F32)
    chunk_dec = jnp.exp(c * lg).astype(F32)
    kdec_t = jnp.repeat(k_dec.T, RET_DK, axis=1)
    qdec_t = jnp.repeat(q_dec.T, RET_DV, axis=1)
    cdec_t = jnp.broadcast_to(chunk_dec[:, None, None], (RET_HEADS, RET_DK, RET_DV))
    return decay_in, kdec_t, qdec_t, cdec_t


def _layer(x, mem2d, ffn1_w_in, ffn1_w_out, ln1_g, ln1_b, w_in, t5_table, ret_gn_g, ret_gn_b,
           w_mem_kv, w_br_ret, w_br_dsa, w_br_mem, w_out, ln2_g, ln2_b,
           ffn2_w_in, ffn2_w_out, ln3_g, ln3_b, tables):
    cos_t, sin_t, din, kdec, qdec, cdec = tables
    row = lambda v: v.reshape(1, -1)
    bf = lambda w: w.astype(BF16)

    w_in_bf = bf(w_in)
    w_mq = w_in_bf[:, W_MQ0:W_G0]
    w_dvt = w_in_bf[:, W_DV0:W_IQ0].T
    kdec_rows = jnp.tile(kdec, (PROJ_ROWS // RET_CHUNK, 1))

    h = _ffn_ln(x, bf(ffn1_w_in), bf(ffn1_w_out), row(ln1_g), row(ln1_b), "ffn1_ln1")
    rq, rk, rkdt, rv, rg, dq, dk, dvt, iq, mq, ikw, ikb = _proj(h, w_in_bf, w_mq, w_dvt, cos_t, sin_t, kdec_rows)
    o_ret = _retention(rq, rk, rkdt, rv, rg, din, qdec, cdec, row(ret_gn_g), row(ret_gn_b))
    o_mem = _mematt(mq, mem2d, bf(w_mem_kv))
    o_dsa = _dsa(t5_table, *_t5_bias(t5_table), iq, dq, ikw, ikb, dk, dvt)
    x2 = _merge(h, o_ret, o_dsa, o_mem, w_in_bf[:, W_G0:], bf(w_br_ret), bf(w_br_dsa), bf(w_br_mem),
                bf(w_out), row(ln2_g), row(ln2_b))
    return _ffn_ln(x2, bf(ffn2_w_in), bf(ffn2_w_out), row(ln3_g), row(ln3_b), "ffn2_ln3")


def kernel(x, mem, ffn1_w_in, ffn1_w_out, ln1_g, ln1_b, w_in, t5_table, ret_gn_g, ret_gn_b,
           w_mem_kv, w_br_ret, w_br_dsa, w_br_mem, w_out, ln2_g, ln2_b,
           ffn2_w_in, ffn2_w_out, ln3_g, ln3_b):
    assert x.shape == (BATCH, SEQ, D_MODEL) and mem.shape == (BATCH, MEM_TOKENS, D_MODEL)
    tables = _rope_tables() + _decay_tables()
    y = x.reshape(N_TOK, D_MODEL)
    mem2d = mem.reshape(BATCH * MEM_TOKENS, D_MODEL)
    for l in range(DEPTH):
        y = _layer(y, mem2d, ffn1_w_in[l], ffn1_w_out[l], ln1_g[l], ln1_b[l], w_in[l], t5_table,
                   ret_gn_g[l], ret_gn_b[l], w_mem_kv[l], w_br_ret[l], w_br_dsa[l], w_br_mem[l],
                   w_out[l], ln2_g[l], ln2_b[l], ffn2_w_in[l], ffn2_w_out[l], ln3_g[l], ln3_b[l],
                   tables)
    return y.reshape(BATCH, SEQ, D_MODEL)
```

```python
import math

import jax
import jax.numpy as jnp
from jax import lax
from jax.experimental import pallas as pl
from jax.experimental.pallas import tpu as pltpu

F32 = jnp.float32
BF16 = jnp.bfloat16

D_MODEL = 1024
BATCH = 8
SEQ = 2048
MEM_TOKENS = 256
RET_HEADS, RET_DK, RET_DV, RET_CHUNK = 4, 64, 128, 128
DSA_HEADS, DSA_DH = 8, 64
IDX_HEADS, IDX_DIM = 8, 64
TOPK = min(256, SEQ // 4)
MEM_HEADS, MEM_DH = 4, 128
T5_BUCKETS, T5_MAX_DIST = 32, 128
D_FF = 2816
ROPE_BASE = 10000.0
LN_EPS = 1e-5
NEG_INF = -1e30
DEPTH = 1
ALPHA = (2.0 * DEPTH) ** 0.25

RET_QK_W = RET_HEADS * RET_DK
RET_V_W = RET_HEADS * RET_DV
DSA_W = DSA_HEADS * DSA_DH
IDX_Q_W = IDX_HEADS * IDX_DIM
MEM_W = MEM_HEADS * MEM_DH
N_TOK = BATCH * SEQ

V7X_VMEM_BYTES = 64 * 1024 * 1024
VMEM_LIMIT = V7X_VMEM_BYTES - 8 * 1024 * 1024
LANES = 128
SUBLANES = 8

FFN_ROWS = 1024
MXU_TILE = 256
FFN_CHUNK_EDGES = (0, 6 * MXU_TILE, D_FF)
PROJ_ROWS = 1024
MERGE_ROWS = 1024
RET_ROWS = 1024
MEMATT_ROWS = 2048
DSA_T = 256
HEAD_PAD = LANES
DVT_HEAD_ROWS = DSA_DH + 16
DVT_ROWS = DSA_HEADS * DVT_HEAD_ROWS
BISECT_PLAIN_STEPS = 16
COUNT_ACCUMULATORS = 4
LOG2E = math.log2(math.e)
DEN_FLOOR = 2.0 ** -100

W_RQ0 = 0
W_RK0 = W_RQ0 + RET_QK_W
W_RV0 = W_RK0 + RET_QK_W
W_RG0 = W_RV0 + RET_V_W
W_DQ0 = W_RG0 + RET_V_W
W_DK0 = W_DQ0 + DSA_W
W_DV0 = W_DK0 + DSA_W
W_IQ0 = W_DV0 + DSA_W
W_IK0 = W_IQ0 + IDX_Q_W
W_IW0 = W_IK0 + IDX_DIM
W_MQ0 = W_IW0 + IDX_HEADS
W_G0 = W_MQ0 + MEM_W
IW_SCALE = IDX_HEADS ** -0.5 * IDX_DIM ** -0.5


def _cparams(sem):
    return pltpu.CompilerParams(dimension_semantics=sem, vmem_limit_bytes=VMEM_LIMIT)


def _const_spec(shape):
    nd = len(shape)
    return pl.BlockSpec(shape, lambda *_: (0,) * nd, pipeline_mode=pl.Buffered(1))


def _layer_norm(y, g, b):
    mu = jnp.mean(y, axis=-1, keepdims=True)
    yc = y - mu
    var = jnp.mean(yc * yc, axis=-1, keepdims=True)
    return yc * lax.rsqrt(var + LN_EPS) * g + b


def _sigmoid(x):
    return 0.5 * jnp.tanh(0.5 * x) + 0.5


def _dot(a, b):
    return jnp.dot(a, b, preferred_element_type=F32)


def _dot_nt(a, b):
    return lax.dot_general(a, b, (((1,), (1,)), ((), ())), preferred_element_type=F32)


def _ffn_ln_kernel(x_ref, wi_ref, wo_ref, g_ref, b_ref, o_ref):
    x = x_ref[...]
    xb = x.astype(BF16)
    acc = None
    for lo, hi in zip(FFN_CHUNK_EDGES[:-1], FFN_CHUNK_EDGES[1:]):
        a = _dot(xb, wi_ref[:, lo:hi])
        u = _dot(xb, wi_ref[:, D_FF + lo:D_FF + hi])
        act = (a * _sigmoid(a) * u).astype(BF16)
        part = _dot(act, wo_ref[lo:hi, :])
        acc = part if acc is None else acc + part
    o_ref[...] = _layer_norm(ALPHA * x + 0.5 * acc, g_ref[...], b_ref[...])


def _ffn_ln(x, w_in_bf, w_out_bf, g, b, name):
    return pl.pallas_call(
        _ffn_ln_kernel,
        out_shape=jax.ShapeDtypeStruct((N_TOK, D_MODEL), F32),
        grid=(N_TOK // FFN_ROWS,),
        in_specs=[
            pl.BlockSpec((FFN_ROWS, D_MODEL), lambda i: (i, 0)),
            _const_spec((D_MODEL, 2 * D_FF)),
            _const_spec((D_FF, D_MODEL)),
            _const_spec((1, D_MODEL)),
            _const_spec((1, D_MODEL)),
        ],
        out_specs=pl.BlockSpec((FFN_ROWS, D_MODEL), lambda i: (i, 0)),
        compiler_params=_cparams(("parallel",)),
        name=name,
    )(x, w_in_bf, w_out_bf, g, b)


def _rope(x, cos, sin_signed):
    width = x.shape[-1]
    lane = lax.broadcasted_iota(jnp.int32, x.shape, 1)
    first_half = (lane % RET_DK) < (RET_DK // 2)
    swapped = jnp.where(first_half,
                        pltpu.roll(x, width - RET_DK // 2, 1),
                        pltpu.roll(x, RET_DK // 2, 1))
    return x * cos + swapped * sin_signed


def _store_split_heads(dst_ref, val):
    low = lax.broadcasted_iota(jnp.int32, (val.shape[0], LANES), 1) < DSA_DH
    for j in range(val.shape[1] // LANES):
        pair = val[:, j * LANES:(j + 1) * LANES]
        dst_ref[:, (2 * j) * LANES:(2 * j + 1) * LANES] = jnp.where(low, pair, 0.0).astype(dst_ref.dtype)
        dst_ref[:, (2 * j + 1) * LANES:(2 * j + 2) * LANES] = jnp.where(low, 0.0, pair).astype(dst_ref.dtype)


def _store_low_heads(dst_ref, val):
    low = lax.broadcasted_iota(jnp.int32, (val.shape[0], LANES), 1) < RET_DK
    for j in range(val.shape[1] // LANES):
        pair = val[:, j * LANES:(j + 1) * LANES]
        dst_ref[:, (2 * j) * LANES:(2 * j + 1) * LANES] = jnp.where(low, pair, 0.0).astype(dst_ref.dtype)
        dst_ref[:, (2 * j + 1) * LANES:(2 * j + 2) * LANES] = jnp.where(
            low, pltpu.roll(pair, RET_DK, 1), 0.0).astype(dst_ref.dtype)


def _proj_kernel(h_ref, w_ref, wikw_ref, wmq_ref, wdvt_ref, cos_ref, sin_ref, kdec_ref,
                 rq_ref, rk_ref, rkdt_ref, rv_ref, rg_ref, dq_ref, dk_ref, dvt_ref, iq_ref, mq_ref,
                 ikw_ref, ikb_ref):
    hb = h_ref[...].astype(BF16)

    def proj(lo, width):
        return _dot(hb, w_ref[:, lo:lo + width])

    cos = cos_ref[...]
    sin = sin_ref[...]
    _store_low_heads(rq_ref, _rope(proj(W_RQ0, RET_QK_W), cos, sin))
    rk = _rope(proj(W_RK0, RET_QK_W), cos, sin) * (RET_DK ** -0.5)
    _store_low_heads(rk_ref, rk)
    rkdt_ref[...] = (rk * kdec_ref[...]).T.astype(BF16)
    rv_ref[...] = proj(W_RV0, RET_V_W).astype(BF16)
    rg_ref[...] = proj(W_RG0, RET_V_W)
    dq_ref[...] = (proj(W_DQ0, DSA_W) * (DSA_DH ** -0.5 * LOG2E)).T.astype(BF16)
    _store_split_heads(dk_ref, proj(W_DK0, DSA_W))
    iq_ref[...] = proj(W_IQ0, IDX_Q_W).T.astype(BF16)
    mq_ref[...] = _dot(hb, wmq_ref[...]).astype(BF16)
    ikw = _dot(hb, wikw_ref[...])
    ikw_ref[...] = ikw
    low = lax.broadcasted_iota(jnp.int32, ikw.shape, 1) < IDX_DIM
    ik_low = jnp.where(low, ikw, 0.0)
    ikb_ref[:, 0:LANES] = ik_low.astype(BF16)
    ikb_ref[:, LANES:2 * LANES] = pltpu.roll(ik_low, IDX_DIM, 1).astype(BF16)
    vt = _dot_nt(wdvt_ref[...], hb).astype(BF16)
    ones = jnp.ones((DVT_HEAD_ROWS - DSA_DH, DSA_T), BF16)
    for j in range(PROJ_ROWS // DSA_T):
        for h in range(DSA_HEADS):
            r0 = h * DVT_HEAD_ROWS
            dvt_ref[j, r0:r0 + DSA_DH, :] = vt[h * DSA_DH:(h + 1) * DSA_DH, j * DSA_T:(j + 1) * DSA_T]
            dvt_ref[j, r0 + DSA_DH:r0 + DVT_HEAD_ROWS, :] = ones


def _proj(h, w_in_bf, w_mq_bf, w_dvt_bf, cos_t, sin_t, kdec_rows):
    rows = lambda w: pl.BlockSpec((PROJ_ROWS, w), lambda i: (i, 0))
    seq_tiles = SEQ // PROJ_ROWS
    pos = lambda w: pl.BlockSpec((PROJ_ROWS, w), lambda i: (i % seq_tiles, 0))
    sd = jax.ShapeDtypeStruct
    ret_padded = RET_HEADS * HEAD_PAD
    dsa_padded = DSA_HEADS * HEAD_PAD
    slabs = PROJ_ROWS // DSA_T
    assert W_IK0 % LANES == 0
    return pl.pallas_call(
        _proj_kernel,
        out_shape=(
            sd((N_TOK, ret_padded), BF16), sd((N_TOK, ret_padded), BF16), sd((RET_QK_W, N_TOK), BF16),
            sd((N_TOK, RET_V_W), BF16), sd((N_TOK, RET_V_W), F32),
            sd((DSA_W, N_TOK), BF16), sd((N_TOK, dsa_padded), BF16),
            sd((N_TOK // DSA_T, DVT_ROWS, DSA_T), BF16),
            sd((IDX_Q_W, N_TOK), BF16), sd((N_TOK, MEM_W), BF16),
            sd((N_TOK, LANES), F32), sd((N_TOK, 2 * LANES), BF16),
        ),
        grid=(N_TOK // PROJ_ROWS,),
        in_specs=[rows(D_MODEL),
                  pl.BlockSpec((D_MODEL, W_IK0), lambda i: (0, 0), pipeline_mode=pl.Buffered(1)),
                  pl.BlockSpec((D_MODEL, LANES), lambda i: (0, W_IK0 // LANES), pipeline_mode=pl.Buffered(1)),
                  _const_spec((D_MODEL, MEM_W)), _const_spec((DSA_W, D_MODEL)),
                  pos(RET_QK_W), pos(RET_QK_W), _const_spec((PROJ_ROWS, RET_QK_W))],
        out_specs=(rows(ret_padded), rows(ret_padded),
                   pl.BlockSpec((RET_QK_W, PROJ_ROWS), lambda i: (0, i)),
                   rows(RET_V_W), rows(RET_V_W),
                   pl.BlockSpec((DSA_W, PROJ_ROWS), lambda i: (0, i)), rows(dsa_padded),
                   pl.BlockSpec((slabs, DVT_ROWS, DSA_T), lambda i: (i, 0, 0)),
                   pl.BlockSpec((IDX_Q_W, PROJ_ROWS), lambda i: (0, i)),
                   rows(MEM_W), rows(LANES), rows(2 * LANES)),
        compiler_params=_cparams(("parallel",)),
        name="mixer_proj",
    )(h, w_in_bf, w_in_bf, w_mq_bf, w_dvt_bf, cos_t, sin_t, kdec_rows)


def _retention_kernel(q_ref, k_ref, kdt_ref, v_ref, g_ref, din_ref, qdec_ref, cdec_ref,
                      gng_ref, gnb_ref, o_ref, state_ref):
    @pl.when(pl.program_id(1) == 0)
    def _():
        state_ref[...] = jnp.zeros_like(state_ref)

    qdec = qdec_ref[...]
    gn_g = gng_ref[...]
    gn_b = gnb_ref[...]
    states = [state_ref[h] for h in range(RET_HEADS)]
    pad_rows = jnp.zeros((HEAD_PAD - RET_DK, RET_DV), BF16)
    for j in range(RET_ROWS // RET_CHUNK):
        rows = slice(j * RET_CHUNK, (j + 1) * RET_CHUNK)
        gate = g_ref[rows, :]
        for h in range(RET_HEADS):
            hs = slice(h * HEAD_PAD, (h + 1) * HEAD_PAD)
            vs = slice(h * RET_DV, (h + 1) * RET_DV)
            qh = q_ref[rows, hs]
            vh = v_ref[rows, vs]
            scores = _dot_nt(qh, k_ref[rows, hs]) * din_ref[h]
            intra = _dot(scores.astype(BF16), vh)
            state_padded = jnp.concatenate([states[h].astype(BF16), pad_rows], axis=0)
            cross = _dot(qh, state_padded) * qdec[:, vs]
            kv = _dot(kdt_ref[h * RET_DK:(h + 1) * RET_DK, rows], vh)
            states[h] = cdec_ref[h] * states[h] + kv
            o = intra + cross
            mu = jnp.mean(o, axis=-1, keepdims=True)
            oc = o - mu
            var = jnp.mean(oc * oc, axis=-1, keepdims=True)
            o = oc * lax.rsqrt(var + LN_EPS) * gn_g[:, vs] + gn_b[:, vs]
            o_ref[rows, vs] = (jax.nn.silu(gate[:, vs]) * o).astype(BF16)
    for h in range(RET_HEADS):
        state_ref[h] = states[h]


def _retention(rq, rk, rkdt, rv, rg, din, qdec, cdec, gn_g, gn_b):
    nc = SEQ // RET_ROWS
    rows = lambda w: pl.BlockSpec((RET_ROWS, w), lambda b, n: (b * nc + n, 0))
    padded = RET_HEADS * HEAD_PAD
    return pl.pallas_call(
        _retention_kernel,
        out_shape=jax.ShapeDtypeStruct((N_TOK, RET_V_W), BF16),
        grid=(BATCH, nc),
        in_specs=[rows(padded), rows(padded),
                  pl.BlockSpec((RET_QK_W, RET_ROWS), lambda b, n: (0, b * nc + n)),
                  rows(RET_V_W), rows(RET_V_W),
                  _const_spec((RET_HEADS, RET_CHUNK, RET_CHUNK)),
                  _const_spec((RET_CHUNK, RET_V_W)),
                  _const_spec((RET_HEADS, RET_DK, RET_DV)),
                  _const_spec((1, RET_V_W)), _const_spec((1, RET_V_W))],
        out_specs=rows(RET_V_W),
        scratch_shapes=[pltpu.VMEM((RET_HEADS, RET_DK, RET_DV), F32)],
        compiler_params=_cparams(("parallel", "arbitrary")),
        name="retention",
    )(rq, rk, rkdt, rv, rg, din, qdec, cdec, gn_g, gn_b)


def _mematt_kernel(q_ref, mem_ref, w_ref, o_ref, kv_ref):
    @pl.when(pl.program_id(1) == 0)
    def _():
        kv_ref[...] = _dot(mem_ref[...].astype(BF16), w_ref[...]).astype(BF16)

    q = q_ref[...]
    for h in range(MEM_HEADS):
        hs = slice(h * MEM_DH, (h + 1) * MEM_DH)
        logits = _dot_nt(q[:, hs], kv_ref[:, hs]) * (MEM_DH ** -0.5)
        m = jnp.max(logits, axis=-1, keepdims=True)
        p = jnp.exp(logits - m)
        denom = jnp.sum(p, axis=-1, keepdims=True)
        o_ref[:, hs] = (_dot(p.astype(BF16), kv_ref[:, MEM_W + h * MEM_DH:MEM_W + (h + 1) * MEM_DH])
                        / denom).astype(BF16)


def _mematt(mq, mem2d, w_kv_bf):
    tiles = SEQ // MEMATT_ROWS
    return pl.pallas_call(
        _mematt_kernel,
        out_shape=jax.ShapeDtypeStruct((N_TOK, MEM_W), BF16),
        grid=(BATCH, tiles),
        in_specs=[pl.BlockSpec((MEMATT_ROWS, MEM_W), lambda b, i: (b * tiles + i, 0)),
                  pl.BlockSpec((MEM_TOKENS, D_MODEL), lambda b, i: (b, 0)),
                  _const_spec((D_MODEL, 2 * MEM_W))],
        out_specs=pl.BlockSpec((MEMATT_ROWS, MEM_W), lambda b, i: (b * tiles + i, 0)),
        scratch_shapes=[pltpu.VMEM((MEM_TOKENS, 2 * MEM_W), BF16)],
        compiler_params=_cparams(("parallel", "arbitrary")),
        name="mem_attention",
    )(mq, mem2d, w_kv_bf)


def _t5_bias_kernel(table_ref, o_ref, bmax_ref):
    m = lax.broadcasted_iota(jnp.int32, (SUBLANES, 2 * DSA_T), 1)
    max_exact = T5_BUCKETS // 2
    head_max = [None] * DSA_HEADS
    for w in range(2):
        n = jnp.maximum(m - w * DSA_T, 0)
        nf = jnp.maximum(n, 1).astype(F32)
        large = max_exact + jnp.floor(jnp.log(nf / max_exact) / math.log(T5_MAX_DIST / max_exact)
                                      * (T5_BUCKETS - max_exact)).astype(jnp.int32)
        large = jnp.minimum(large, T5_BUCKETS - 1)
        bucket = jnp.where(n < max_exact, n, large)
        for h in range(DSA_HEADS):
            vec = jnp.zeros((SUBLANES, 2 * DSA_T), F32)
            for b in range(T5_BUCKETS):
                vec = jnp.where(bucket == b, table_ref[b, h] * LOG2E, vec)
            rows = jnp.broadcast_to(vec[0:1, :], (DSA_T, 2 * DSA_T))
            o_ref[w, h] = pltpu.roll(rows, 0, 1, stride=1, stride_axis=0)[:, DSA_T:]
            top = jnp.max(vec[0:1, :], axis=1, keepdims=True)
            head_max[h] = top if head_max[h] is None else jnp.maximum(head_max[h], top)
    for h in range(DSA_HEADS):
        bmax_ref[h:h + 1, :] = jnp.broadcast_to(head_max[h], (1, LANES))


def _t5_bias(t5_table):
    return pl.pallas_call(
        _t5_bias_kernel,
        out_shape=(jax.ShapeDtypeStruct((2, DSA_HEADS, DSA_T, DSA_T), F32),
                   jax.ShapeDtypeStruct((DSA_HEADS, LANES), F32)),
        in_specs=[pl.BlockSpec(memory_space=pltpu.SMEM)],
        out_specs=(pl.BlockSpec(memory_space=pltpu.VMEM), pl.BlockSpec(memory_space=pltpu.VMEM)),
        compiler_params=pltpu.CompilerParams(vmem_limit_bytes=VMEM_LIMIT),
        name="t5_bias",
    )(t5_table)


def _fold_groups(x, op):
    return op(x.reshape(DSA_T // SUBLANES, SUBLANES, DSA_T), axis=0)


def _dsa_kernel(table_ref, bias_ref, bmax_ref, iq_ref, dq_ref, ikw_q_ref, ikb_ref, dk_ref, dvt_ref,
                o_ref, score_ref, logit_ref, acc_ref):
    qb = pl.program_id(1)
    q0 = qb * DSA_T
    n_chunks = qb + 1
    groups = DSA_T // SUBLANES
    key_in_chunk = lax.broadcasted_iota(jnp.int32, (DSA_T, DSA_T), 0)
    query_in_step = lax.broadcasted_iota(jnp.int32, (DSA_T, DSA_T), 1)

    def key_rows(c):
        return pl.ds(pl.multiple_of(c * DSA_T, DSA_T), DSA_T)

    def pair_rows(j):
        return slice(j * LANES, (j + 1) * LANES)

    def pair_products(keys_even, keys_odd, queries_t):
        prod = _dot(jnp.concatenate([keys_even, keys_odd], axis=0), queries_t)
        return prod[:DSA_T], prod[DSA_T:]

    iw_t = ikw_q_ref[...].T[IDX_DIM:IDX_DIM + IDX_HEADS, :] * IW_SCALE

    def scores_and_logits(c, carry, causal):
        raw_max, s_min, s_max = carry
        ik_low = ikb_ref[key_rows(c), 0:LANES]
        ik_high = ikb_ref[key_rows(c), LANES:2 * LANES]
        last = IDX_HEADS // 2 - 1
        for j in range(IDX_HEADS // 2):
            rel_even, rel_odd = pair_products(ik_low, ik_high, iq_ref[pair_rows(j), :])
            term = (jnp.maximum(rel_even, 0.0) * iw_t[2 * j:2 * j + 1, :]
                    + jnp.maximum(rel_odd, 0.0) * iw_t[2 * j + 1:2 * j + 2, :])
            total = term if j == 0 else score_ref[c] + term
            if j < last:
                score_ref[c] = total
            elif causal is None:
                score_ref[c] = total
                s_min = jnp.minimum(s_min, _fold_groups(total, jnp.min))
                s_max = jnp.maximum(s_max, _fold_groups(total, jnp.max))
            else:
                score_ref[c] = jnp.where(causal, total, NEG_INF)
                s_min = jnp.minimum(s_min, _fold_groups(jnp.where(causal, total, -NEG_INF), jnp.min))
                s_max = jnp.maximum(s_max, _fold_groups(jnp.where(causal, total, NEG_INF), jnp.max))
        out = list(raw_max)
        for j in range(DSA_HEADS // 2):
            keys = [dk_ref[key_rows(c), (2 * j + i) * HEAD_PAD:(2 * j + i + 1) * HEAD_PAD] for i in range(2)]
            for h, qk in zip((2 * j, 2 * j + 1), pair_products(keys[0], keys[1], dq_ref[pair_rows(j), :])):
                logit_ref[h, c] = qk
                out[h] = jnp.maximum(out[h], _fold_groups(qk, jnp.max))
        return tuple(out), s_min, s_max

    def paired_loop(start, stop, body, carry):
        n = stop - start
        carry = lax.fori_loop(
            0, lax.shift_right_logical(n, 1),
            lambda i, cr: body(start + 2 * i + 1, body(start + 2 * i, cr)), carry)
        return lax.cond(jnp.bitwise_and(n, 1) == 1, lambda cr: body(stop - 1, cr), lambda cr: cr, carry)

    low_init = jnp.full((SUBLANES, DSA_T), NEG_INF, F32)
    carry = (tuple(low_init for _ in range(DSA_HEADS)), -low_init, low_init)
    carry = paired_loop(0, qb, lambda c, cr: scores_and_logits(c, cr, None), carry)
    raw_max, s_min, s_max = scores_and_logits(qb, carry, key_in_chunk <= query_in_step)

    def count_where(pred):
        def body(c, acc):
            parts = [acc] + [jnp.zeros((SUBLANES, DSA_T), F32)] * (COUNT_ACCUMULATORS - 1)
            for g in range(groups):
                blk = score_ref[c, g * SUBLANES:(g + 1) * SUBLANES, :]
                a = g % COUNT_ACCUMULATORS
                parts[a] = parts[a] + jnp.where(pred(c, g * SUBLANES, blk), 1.0, 0.0)
            while len(parts) > 1:
                parts = [parts[i] + parts[i + 1] for i in range(0, len(parts), 2)]
            return parts[0]
        acc = paired_loop(0, n_chunks, body, jnp.zeros((SUBLANES, DSA_T), F32))
        return jnp.sum(acc, axis=0, keepdims=True)

    def count_ge(thr):
        thr_b = jnp.broadcast_to(thr, (SUBLANES, DSA_T))
        return count_where(lambda c, r0, blk: blk >= thr_b)

    q_min = jnp.min(s_min, axis=0, keepdims=True)
    q_max = jnp.max(s_max, axis=0, keepdims=True)
    t_q = q0 + lax.broadcasted_iota(jnp.int32, (1, DSA_T), 1)

    kf = float(TOPK)
    init = (q_min, q_max + jnp.maximum(jnp.abs(q_max), 1.0) * 1e-6,
            (t_q + 1).astype(F32), jnp.zeros((1, DSA_T), F32))

    def bisect_step(_i, carry):
        lo, hi, cnt_lo, cnt_hi = carry
        mid = 0.5 * (lo + hi)
        cnt = count_ge(mid)
        ge = cnt >= kf
        return (jnp.where(ge, mid, lo), jnp.where(ge, hi, mid),
                jnp.where(ge, cnt, cnt_lo), jnp.where(ge, cnt_hi, cnt))

    n_plain = jnp.where(qb == 0, 0, BISECT_PLAIN_STEPS)
    lo, hi, cnt_lo, cnt_hi = lax.fori_loop(0, n_plain, bisect_step, init)

    big = -NEG_INF

    def snap_body(c, carry):
        mn, mx = carry
        blk = score_ref[c]
        mn = jnp.minimum(mn, _fold_groups(jnp.where(blk >= lo, blk, big), jnp.min))
        mx = jnp.maximum(mx, _fold_groups(jnp.where(blk < hi, blk, -big), jnp.max))
        return mn, mx

    mn, mx = paired_loop(
        0, n_chunks, snap_body,
        (jnp.full((SUBLANES, DSA_T), big, F32), jnp.full((SUBLANES, DSA_T), -big, F32)))
    lo = jnp.min(mn, axis=0, keepdims=True)
    top = jnp.max(mx, axis=0, keepdims=True)

    def split_pair(carry):
        lo, top, cnt_lo, cnt_hi = carry
        pair = (cnt_lo > kf) & (lo < top) & (cnt_lo - cnt_hi == 2.0)
        return jnp.where(pair, top, lo), top, jnp.where(pair, cnt_hi + 1.0, cnt_lo), cnt_hi

    def pending_of(carry):
        lo, top, cnt_lo, _ = carry
        over = cnt_lo > kf
        return jnp.max(jnp.where(over & (lo < top), 2.0, jnp.where(over, 1.0, 0.0)))

    def value_step(state):
        (lo, top, cnt_lo, cnt_hi), _ = state
        unresolved = (cnt_lo > kf) & (lo < top)
        mid = lo + (top - lo) * 0.5
        mid = jnp.minimum(jnp.where(mid > lo, mid, top), top)

        def body(c, carry):
            cnt, mn, mx = carry
            blk = score_ref[c]
            is_ge = blk >= mid
            cnt = cnt + _fold_groups(jnp.where(is_ge, 1.0, 0.0), jnp.sum)
            mn = jnp.minimum(mn, _fold_groups(jnp.where(is_ge, blk, big), jnp.min))
            mx = jnp.maximum(mx, _fold_groups(jnp.where(is_ge, -big, blk), jnp.max))
            return cnt, mn, mx

        cnt, mn, mx = lax.fori_loop(
            0, n_chunks, body,
            (jnp.zeros((SUBLANES, DSA_T), F32), jnp.full((SUBLANES, DSA_T), big, F32),
             jnp.full((SUBLANES, DSA_T), -big, F32)))
        cnt = jnp.sum(cnt, axis=0, keepdims=True)
        ge = cnt >= kf
        raise_lo = unresolved & ge
        lower_top = unresolved & jnp.logical_not(ge)
        carry = split_pair((jnp.where(raise_lo, jnp.min(mn, axis=0, keepdims=True), lo),
                            jnp.where(lower_top, jnp.max(mx, axis=0, keepdims=True), top),
                            jnp.where(raise_lo, cnt, cnt_lo), jnp.where(lower_top, cnt, cnt_hi)))
        return carry, pending_of(carry)

    start = split_pair((lo, top, cnt_lo, cnt_hi))
    (lo, top, cnt_lo, cnt_hi), status = lax.while_loop(
        lambda state: state[1] > 1.5, value_step, (start, pending_of(start)))

    tied = cnt_lo > kf
    need = kf - cnt_hi
    lo_b = jnp.broadcast_to(lo, (SUBLANES, DSA_T))
    key_in_group = lax.broadcasted_iota(jnp.int32, (SUBLANES, DSA_T), 0)

    def tie_search(_):
        def body(_i, carry):
            lo_i, hi_i = carry
            mid_i = lax.shift_right_arithmetic(lo_i + hi_i, 1)
            mid_b = jnp.broadcast_to(mid_i, (SUBLANES, DSA_T))
            cnt = count_where(lambda c, r0, blk: (blk == lo_b) & (key_in_group + (c * DSA_T + r0) <= mid_b))
            enough = cnt >= need
            return jnp.where(enough, lo_i, mid_i), jnp.where(enough, mid_i, hi_i)
        lo_i = jnp.full((1, DSA_T), -1, jnp.int32)
        hi_i = jnp.full((1, DSA_T), SEQ - 1, jnp.int32)
        _, cut = lax.fori_loop(0, int(math.log2(SEQ)) + 1, body, (lo_i, hi_i))
        return jnp.where(tied, cut, SEQ)

    cut = lax.cond(status > 0.5, tie_search, lambda _: jnp.full((1, DSA_T), SEQ, jnp.int32), 0)

    def write_mask(c):
        blk = score_ref[c]
        keep = (blk > lo) | ((blk == lo) & (key_in_chunk + c * DSA_T <= cut))
        score_ref[c] = jnp.where(keep, 0.0, NEG_INF)

    n_far = jnp.maximum(qb - 1, 0)

    def masked_logits(c, h, far):
        lg = logit_ref[h, c] + score_ref[c]
        return lg if far else lg + bias_ref[c - qb + 1, h]

    def far_bias(h):
        return table_ref[T5_BUCKETS - 1, h] * LOG2E

    def den_row(h):
        return h * DVT_HEAD_ROWS + DSA_DH

    def exp_and_pv(shifts, scores_to_masks):
        acc_ref[...] = jnp.zeros_like(acc_ref)

        def body(c, carry, far):
            if scores_to_masks:
                write_mask(c)
            for h in range(DSA_HEADS):
                shift = shifts[h] - far_bias(h) if far else shifts[h]
                p = jnp.exp2(masked_logits(c, h, far) - shift).astype(BF16)
                rows = slice(h * DVT_HEAD_ROWS, (h + 1) * DVT_HEAD_ROWS)
                acc_ref[rows, :] += _dot(dvt_ref[c, rows, :], p)
            return carry

        paired_loop(0, n_far, lambda c, carry: body(c, carry, True), 0)
        paired_loop(n_far, n_chunks, lambda c, carry: body(c, carry, False), 0)

    upper = [jnp.max(raw_max[h], axis=0, keepdims=True) + bmax_ref[h:h + 1, 0:1] for h in range(DSA_HEADS)]
    exp_and_pv(upper, True)

    den_min = acc_ref[den_row(0):den_row(0) + 1, :]
    for h in range(1, DSA_HEADS):
        den_min = jnp.minimum(den_min, acc_ref[den_row(h):den_row(h) + 1, :])

    @pl.when(jnp.min(den_min) < DEN_FLOOR)
    def _():
        def max_body(c, ms, far):
            return tuple(jnp.maximum(ms[h], _fold_groups(masked_logits(c, h, far), jnp.max) + (far_bias(h) if far else 0.0))
                         for h in range(DSA_HEADS))
        ms = tuple(jnp.full((SUBLANES, DSA_T), NEG_INF, F32) for _ in range(DSA_HEADS))
        ms = lax.fori_loop(0, n_far, lambda c, ms: max_body(c, ms, True), ms)
        ms = lax.fori_loop(n_far, n_chunks, lambda c, ms: max_body(c, ms, False), ms)
        exp_and_pv([jnp.max(m, axis=0, keepdims=True) for m in ms], False)

    heads = []
    for h in range(DSA_HEADS):
        r0 = h * DVT_HEAD_ROWS
        heads.append(acc_ref[r0:r0 + DSA_DH, :] / acc_ref[den_row(h):den_row(h) + 1, :])
    o_ref[...] = jnp.concatenate(heads, axis=0).T.astype(BF16)


def _dsa(t5_table, bias, bias_max, iq, dq, ikw, ikb, dk, dvt):
    nqb = SEQ // DSA_T
    padded = DSA_HEADS * HEAD_PAD
    assert DSA_T <= TOPK
    qrows = lambda w: pl.BlockSpec((DSA_T, w), lambda b, i: (b * nqb + i, 0))
    qcols = lambda h: pl.BlockSpec((h, DSA_T), lambda b, i: (0, b * nqb + i))
    krows = lambda w: pl.BlockSpec((SEQ, w), lambda b, i: (b, 0))
    return pl.pallas_call(
        _dsa_kernel,
        out_shape=jax.ShapeDtypeStruct((N_TOK, DSA_W), BF16),
        grid=(BATCH, nqb),
        in_specs=[pl.BlockSpec(memory_space=pltpu.SMEM),
                  _const_spec((2, DSA_HEADS, DSA_T, DSA_T)), _const_spec((DSA_HEADS, LANES)),
                  qcols(IDX_Q_W), qcols(DSA_W), qrows(LANES),
                  krows(2 * LANES), krows(padded),
                  pl.BlockSpec((nqb, DVT_ROWS, DSA_T), lambda b, i: (b, 0, 0))],
        out_specs=qrows(DSA_W),
        scratch_shapes=[pltpu.VMEM((nqb, DSA_T, DSA_T), F32),
                        pltpu.VMEM((DSA_HEADS, nqb, DSA_T, DSA_T), F32),
                        pltpu.VMEM((DVT_ROWS, DSA_T), F32)],
        compiler_params=_cparams(("parallel", "arbitrary")),
        name="dsa_attention",
    )(t5_table, bias, bias_max, iq, dq, ikw, ikb, dk, dvt)


def _merge_kernel(h_ref, oret_ref, odsa_ref, omem_ref, wg_ref, wr_ref, wd_ref, wm_ref, wo_ref,
                  g_ref, b_ref, o_ref):
    h = h_ref[...]
    hb = h.astype(BF16)
    merged = None
    for i, (src, w) in enumerate(((oret_ref, wr_ref), (odsa_ref, wd_ref), (omem_ref, wm_ref))):
        gate = _sigmoid(_dot(hb, wg_ref[:, i * D_MODEL:(i + 1) * D_MODEL]))
        term = gate * _dot(src[...], w[...])
        merged = term if merged is None else merged + term
    mix = _dot(merged.astype(BF16), wo_ref[...])
    o_ref[...] = _layer_norm(ALPHA * h + mix, g_ref[...], b_ref[...])


def _merge(h, o_ret, o_dsa, o_mem, wg, wr, wd, wm, wo, g, b):
    rows = lambda w: pl.BlockSpec((MERGE_ROWS, w), lambda i: (i, 0))
    return pl.pallas_call(
        _merge_kernel,
        out_shape=jax.ShapeDtypeStruct((N_TOK, D_MODEL), F32),
        grid=(N_TOK // MERGE_ROWS,),
        in_specs=[rows(D_MODEL), rows(RET_V_W), rows(DSA_W), rows(MEM_W),
                  _const_spec((D_MODEL, 3 * D_MODEL)),
                  _const_spec((RET_V_W, D_MODEL)), _const_spec((DSA_W, D_MODEL)),
                  _const_spec((MEM_W, D_MODEL)), _const_spec((D_MODEL, D_MODEL)),
                  _const_spec((1, D_MODEL)), _const_spec((1, D_MODEL))],
        out_specs=rows(D_MODEL),
        compiler_params=_cparams(("parallel",)),
        name="merge_ln2",
    )(h, o_ret, o_dsa, o_mem, wg, wr, wd, wm, wo, g, b)


def _rope_tables():
    half = RET_DK // 2
    freqs = ROPE_BASE ** (-jnp.arange(half, dtype=F32) / half)
    ang = jnp.arange(SEQ).astype(F32)[:, None] * freqs[None, :]
    cos, sin = jnp.cos(ang), jnp.sin(ang)
    cos_t = jnp.tile(jnp.concatenate([cos, cos], axis=-1), (1, RET_HEADS))
    sin_t = jnp.tile(jnp.concatenate([-sin, sin], axis=-1), (1, RET_HEADS))
    return cos_t, sin_t


def _decay_tables():
    c = RET_CHUNK
    gamma = 1.0 - 2.0 ** (-5.0 - jnp.arange(RET_HEADS, dtype=F32))
    lg = jnp.log(gamma)
    i = jnp.arange(c)
    diff = i[:, None] - i[None, :]
    decay_in = jnp.where(diff[None] >= 0, jnp.exp(jnp.maximum(diff, 0)[None] * lg[:, None, None]), 0.0).astype(F32)
    k_dec = jnp.exp((c - 1 - i)[None, :] * lg[:, None]).astype(F32)
    q_dec = jnp.exp((i + 1)[None, :] * lg[:, None]).astype(F32)
    chunk_dec = jnp.exp(c * lg).astype(F32)
    kdec_t = jnp.repeat(k_dec.T, RET_DK, axis=1)
    qdec_t = jnp.repeat(q_dec.T, RET_DV, axis=1)
    cdec_t = jnp.broadcast_to(chunk_dec[:, None, None], (RET_HEADS, RET_DK, RET_DV))
    return decay_in, kdec_t, qdec_t, cdec_t


def _layer(x, mem2d, ffn1_w_in, ffn1_w_out, ln1_g, ln1_b, w_in, t5_table, ret_gn_g, ret_gn_b,
           w_mem_kv, w_br_ret, w_br_dsa, w_br_mem, w_out, ln2_g, ln2_b,
           ffn2_w_in, ffn2_w_out, ln3_g, ln3_b, tables):
    cos_t, sin_t, din, kdec, qdec, cdec = tables
    row = lambda v: v.reshape(1, -1)
    bf = lambda w: w.astype(BF16)

    w_in_bf = bf(w_in)
    w_mq = w_in_bf[:, W_MQ0:W_G0]
    w_dvt = w_in_bf[:, W_DV0:W_IQ0].T
    kdec_rows = jnp.tile(kdec, (PROJ_ROWS // RET_CHUNK, 1))

    h = _ffn_ln(x, bf(ffn1_w_in), bf(ffn1_w_out), row(ln1_g), row(ln1_b), "ffn1_ln1")
    rq, rk, rkdt, rv, rg, dq, dk, dvt, iq, mq, ikw, ikb = _proj(h, w_in_bf, w_mq, w_dvt, cos_t, sin_t, kdec_rows)
    o_ret = _retention(rq, rk, rkdt, rv, rg, din, qdec, cdec, row(ret_gn_g), row(ret_gn_b))
    o_mem = _mematt(mq, mem2d, bf(w_mem_kv))
    o_dsa = _dsa(t5_table, *_t5_bias(t5_table), iq, dq, ikw, ikb, dk, dvt)
    x2 = _merge(h, o_ret, o_dsa, o_mem, w_in_bf[:, W_G0:], bf(w_br_ret), bf(w_br_dsa), bf(w_br_mem),
                bf(w_out), row(ln2_g), row(ln2_b))
    return _ffn_ln(x2, bf(ffn2_w_in), bf(ffn2_w_out), row(ln3_g), row(ln3_b), "ffn2_ln3")


def kernel(x, mem, ffn1_w_in, ffn1_w_out, ln1_g, ln1_b, w_in, t5_table, ret_gn_g, ret_gn_b,
           w_mem_kv, w_br_ret, w_br_dsa, w_br_mem, w_out, ln2_g, ln2_b,
           ffn2_w_in, ffn2_w_out, ln3_g, ln3_b):
    assert x.shape == (BATCH, SEQ, D_MODEL) and mem.shape == (BATCH, MEM_TOKENS, D_MODEL)
    tables = _rope_tables() + _decay_tables()
    y = x.reshape(N_TOK, D_MODEL)
    mem2d = mem.reshape(BATCH * MEM_TOKENS, D_MODEL)
    for l in range(DEPTH):
        y = _layer(y, mem2d, ffn1_w_in[l], ffn1_w_out[l], ln1_g[l], ln1_b[l], w_in[l], t5_table,
                   ret_gn_g[l], ret_gn_b[l], w_mem_kv[l], w_br_ret[l], w_br_dsa[l], w_br_mem[l],
                   w_out[l], ln2_g[l], ln2_b[l], ffn2_w_in[l], ffn2_w_out[l], ln3_g[l], ln3_b[l],
                   tables)
    return y.reshape(BATCH, SEQ, D_MODEL)
```

```python
import math

import jax
import jax.numpy as jnp
from jax import lax
from jax.experimental import pallas as pl
from jax.experimental.pallas import tpu as pltpu

F32 = jnp.float32
BF16 = jnp.bfloat16

D_MODEL = 1024
BATCH = 8
SEQ = 2048
MEM_TOKENS = 256
RET_HEADS, RET_DK, RET_DV, RET_CHUNK = 4, 64, 128, 128
DSA_HEADS, DSA_DH = 8, 64
IDX_HEADS, IDX_DIM = 8, 64
TOPK = min(256, SEQ // 4)
MEM_HEADS, MEM_DH = 4, 128
T5_BUCKETS, T5_MAX_DIST = 32, 128
D_FF = 2816
ROPE_BASE = 10000.0
LN_EPS = 1e-5
NEG_INF = -1e30
DEPTH = 1
ALPHA = (2.0 * DEPTH) ** 0.25

RET_QK_W = RET_HEADS * RET_DK
RET_V_W = RET_HEADS * RET_DV
DSA_W = DSA_HEADS * DSA_DH
IDX_Q_W = IDX_HEADS * IDX_DIM
MEM_W = MEM_HEADS * MEM_DH
N_TOK = BATCH * SEQ

V7X_VMEM_BYTES = 64 * 1024 * 1024
VMEM_LIMIT = V7X_VMEM_BYTES - 8 * 1024 * 1024
LANES = 128
SUBLANES = 8

FFN_ROWS = 1024
MXU_TILE = 256
FFN_CHUNK_EDGES = (0, 6 * MXU_TILE, D_FF)
PROJ_ROWS = 1024
MERGE_ROWS = 1024
RET_ROWS = 1024
MEMATT_ROWS = 2048
DSA_T = 256
HEAD_PAD = LANES
DVT_HEAD_ROWS = DSA_DH + 16
DVT_ROWS = DSA_HEADS * DVT_HEAD_ROWS
CAST_STEPS = 8
BISECT_PLAIN_STEPS = 16
COUNT_ACCUMULATORS = 4
LOG2E = math.log2(math.e)
DEN_FLOOR = 2.0 ** -100

W_RQ0 = 0
W_RK0 = W_RQ0 + RET_QK_W
W_RV0 = W_RK0 + RET_QK_W
W_RG0 = W_RV0 + RET_V_W
W_DQ0 = W_RG0 + RET_V_W
W_DK0 = W_DQ0 + DSA_W
W_DV0 = W_DK0 + DSA_W
W_IQ0 = W_DV0 + DSA_W
W_IK0 = W_IQ0 + IDX_Q_W
W_IW0 = W_IK0 + IDX_DIM
W_MQ0 = W_IW0 + IDX_HEADS
W_G0 = W_MQ0 + MEM_W
IW_SCALE = IDX_HEADS ** -0.5 * IDX_DIM ** -0.5


def _cparams(sem):
    return pltpu.CompilerParams(dimension_semantics=sem, vmem_limit_bytes=VMEM_LIMIT)


def _const_spec(shape):
    nd = len(shape)
    return pl.BlockSpec(shape, lambda *_: (0,) * nd, pipeline_mode=pl.Buffered(1))


def _layer_norm(y, g, b):
    mu = jnp.mean(y, axis=-1, keepdims=True)
    yc = y - mu
    var = jnp.mean(yc * yc, axis=-1, keepdims=True)
    return yc * lax.rsqrt(var + LN_EPS) * g + b


def _sigmoid(x):
    return 0.5 * jnp.tanh(0.5 * x) + 0.5


def _dot(a, b):
    return jnp.dot(a, b, preferred_element_type=F32)


def _dot_nt(a, b):
    return lax.dot_general(a, b, (((1,), (1,)), ((), ())), preferred_element_type=F32)


def _cast_kernel(w_ref, o_ref):
    o_ref[...] = w_ref[...].astype(BF16)


def _to_bf16(w, name):
    rows, cols = w.shape
    block_rows = rows // CAST_STEPS
    assert block_rows * CAST_STEPS == rows and block_rows % 16 == 0
    return pl.pallas_call(
        _cast_kernel,
        out_shape=jax.ShapeDtypeStruct((rows, cols), BF16),
        grid=(CAST_STEPS,),
        in_specs=[pl.BlockSpec((block_rows, cols), lambda i: (i, 0))],
        out_specs=pl.BlockSpec((block_rows, cols), lambda i: (i, 0)),
        compiler_params=_cparams(("parallel",)),
        name=name,
    )(w)


def _ffn_ln_kernel(x_ref, wi_ref, wo_ref, g_ref, b_ref, o_ref):
    x = x_ref[...]
    xb = x.astype(BF16)
    acc = None
    for lo, hi in zip(FFN_CHUNK_EDGES[:-1], FFN_CHUNK_EDGES[1:]):
        a = _dot(xb, wi_ref[:, lo:hi])
        u = _dot(xb, wi_ref[:, D_FF + lo:D_FF + hi])
        act = (a * _sigmoid(a) * u).astype(BF16)
        part = _dot(act, wo_ref[lo:hi, :])
        acc = part if acc is None else acc + part
    o_ref[...] = _layer_norm(ALPHA * x + 0.5 * acc, g_ref[...], b_ref[...])


def _ffn_ln(x, w_in_bf, w_out_bf, g, b, name):
    return pl.pallas_call(
        _ffn_ln_kernel,
        out_shape=jax.ShapeDtypeStruct((N_TOK, D_MODEL), F32),
        grid=(N_TOK // FFN_ROWS,),
        in_specs=[
            pl.BlockSpec((FFN_ROWS, D_MODEL), lambda i: (i, 0)),
            _const_spec((D_MODEL, 2 * D_FF)),
            _const_spec((D_FF, D_MODEL)),
            _const_spec((1, D_MODEL)),
            _const_spec((1, D_MODEL)),
        ],
        out_specs=pl.BlockSpec((FFN_ROWS, D_MODEL), lambda i: (i, 0)),
        compiler_params=_cparams(("parallel",)),
        name=name,
    )(x, w_in_bf, w_out_bf, g, b)


def _rope(x, cos, sin_signed):
    width = x.shape[-1]
    lane = lax.broadcasted_iota(jnp.int32, x.shape, 1)
    first_half = (lane % RET_DK) < (RET_DK // 2)
    swapped = jnp.where(first_half,
                        pltpu.roll(x, width - RET_DK // 2, 1),
                        pltpu.roll(x, RET_DK // 2, 1))
    return x * cos + swapped * sin_signed


def _store_split_heads(dst_ref, val):
    low = lax.broadcasted_iota(jnp.int32, (val.shape[0], LANES), 1) < DSA_DH
    for j in range(val.shape[1] // LANES):
        pair = val[:, j * LANES:(j + 1) * LANES]
        dst_ref[:, (2 * j) * LANES:(2 * j + 1) * LANES] = jnp.where(low, pair, 0.0).astype(dst_ref.dtype)
        dst_ref[:, (2 * j + 1) * LANES:(2 * j + 2) * LANES] = jnp.where(low, 0.0, pair).astype(dst_ref.dtype)


def _store_low_heads(dst_ref, val):
    low = lax.broadcasted_iota(jnp.int32, (val.shape[0], LANES), 1) < RET_DK
    for j in range(val.shape[1] // LANES):
        pair = val[:, j * LANES:(j + 1) * LANES]
        dst_ref[:, (2 * j) * LANES:(2 * j + 1) * LANES] = jnp.where(low, pair, 0.0).astype(dst_ref.dtype)
        dst_ref[:, (2 * j + 1) * LANES:(2 * j + 2) * LANES] = jnp.where(
            low, pltpu.roll(pair, RET_DK, 1), 0.0).astype(dst_ref.dtype)


def _proj_kernel(h_ref, w_ref, wikw_ref, wmq_ref, wdvt_ref, cos_ref, sin_ref, kdec_ref,
                 rq_ref, rk_ref, rkdt_ref, rv_ref, rg_ref, dq_ref, dk_ref, dvt_ref, iq_ref, mq_ref,
                 ikw_ref, ikb_ref):
    hb = h_ref[...].astype(BF16)

    def proj(lo, width):
        return _dot(hb, w_ref[:, lo:lo + width])

    cos = cos_ref[...]
    sin = sin_ref[...]
    _store_low_heads(rq_ref, _rope(proj(W_RQ0, RET_QK_W), cos, sin))
    rk = _rope(proj(W_RK0, RET_QK_W), cos, sin) * (RET_DK ** -0.5)
    _store_low_heads(rk_ref, rk)
    rkdt_ref[...] = (rk * kdec_ref[...]).T.astype(BF16)
    rv_ref[...] = proj(W_RV0, RET_V_W).astype(BF16)
    rg_ref[...] = proj(W_RG0, RET_V_W)
    dq_ref[...] = (proj(W_DQ0, DSA_W) * (DSA_DH ** -0.5 * LOG2E)).T.astype(BF16)
    _store_split_heads(dk_ref, proj(W_DK0, DSA_W))
    iq_ref[...] = proj(W_IQ0, IDX_Q_W).T.astype(BF16)
    mq_ref[...] = _dot(hb, wmq_ref[...]).astype(BF16)
    ikw = _dot(hb, wikw_ref[...])
    ikw_ref[...] = ikw
    low = lax.broadcasted_iota(jnp.int32, ikw.shape, 1) < IDX_DIM
    ik_low = jnp.where(low, ikw, 0.0)
    ikb_ref[:, 0:LANES] = ik_low.astype(BF16)
    ikb_ref[:, LANES:2 * LANES] = pltpu.roll(ik_low, IDX_DIM, 1).astype(BF16)
    vt = _dot_nt(wdvt_ref[...], hb).astype(BF16)
    ones = jnp.ones((DVT_HEAD_ROWS - DSA_DH, DSA_T), BF16)
    for j in range(PROJ_ROWS // DSA_T):
        for h in range(DSA_HEADS):
            r0 = h * DVT_HEAD_ROWS
            dvt_ref[j, r0:r0 + DSA_DH, :] = vt[h * DSA_DH:(h + 1) * DSA_DH, j * DSA_T:(j + 1) * DSA_T]
            dvt_ref[j, r0 + DSA_DH:r0 + DVT_HEAD_ROWS, :] = ones


def _proj(h, w_in_bf, w_mq_bf, w_dvt_bf, cos_t, sin_t, kdec_rows):
    rows = lambda w: pl.BlockSpec((PROJ_ROWS, w), lambda i: (i, 0))
    seq_tiles = SEQ // PROJ_ROWS
    pos = lambda w: pl.BlockSpec((PROJ_ROWS, w), lambda i: (i % seq_tiles, 0))
    sd = jax.ShapeDtypeStruct
    ret_padded = RET_HEADS * HEAD_PAD
    dsa_padded = DSA_HEADS * HEAD_PAD
    slabs = PROJ_ROWS // DSA_T
    assert W_IK0 % LANES == 0
    return pl.pallas_call(
        _proj_kernel,
        out_shape=(
            sd((N_TOK, ret_padded), BF16), sd((N_TOK, ret_padded), BF16), sd((RET_QK_W, N_TOK), BF16),
            sd((N_TOK, RET_V_W), BF16), sd((N_TOK, RET_V_W), F32),
            sd((DSA_W, N_TOK), BF16), sd((N_TOK, dsa_padded), BF16),
            sd((N_TOK // DSA_T, DVT_ROWS, DSA_T), BF16),
            sd((IDX_Q_W, N_TOK), BF16), sd((N_TOK, MEM_W), BF16),
            sd((N_TOK, LANES), F32), sd((N_TOK, 2 * LANES), BF16),
        ),
        grid=(N_TOK // PROJ_ROWS,),
        in_specs=[rows(D_MODEL),
                  pl.BlockSpec((D_MODEL, W_IK0), lambda i: (0, 0), pipeline_mode=pl.Buffered(1)),
                  pl.BlockSpec((D_MODEL, LANES), lambda i: (0, W_IK0 // LANES), pipeline_mode=pl.Buffered(1)),
                  _const_spec((D_MODEL, MEM_W)), _const_spec((DSA_W, D_MODEL)),
                  pos(RET_QK_W), pos(RET_QK_W), _const_spec((PROJ_ROWS, RET_QK_W))],
        out_specs=(rows(ret_padded), rows(ret_padded),
                   pl.BlockSpec((RET_QK_W, PROJ_ROWS), lambda i: (0, i)),
                   rows(RET_V_W), rows(RET_V_W),
                   pl.BlockSpec((DSA_W, PROJ_ROWS), lambda i: (0, i)), rows(dsa_padded),
                   pl.BlockSpec((slabs, DVT_ROWS, DSA_T), lambda i: (i, 0, 0)),
                   pl.BlockSpec((IDX_Q_W, PROJ_ROWS), lambda i: (0, i)),
                   rows(MEM_W), rows(LANES), rows(2 * LANES)),
        compiler_params=_cparams(("parallel",)),
        name="mixer_proj",
    )(h, w_in_bf, w_in_bf, w_mq_bf, w_dvt_bf, cos_t, sin_t, kdec_rows)


def _retention_kernel(q_ref, k_ref, kdt_ref, v_ref, g_ref, din_ref, qdec_ref, cdec_ref,
                      gng_ref, gnb_ref, o_ref, state_ref):
    @pl.when(pl.program_id(1) == 0)
    def _():
        state_ref[...] = jnp.zeros_like(state_ref)

    qdec = qdec_ref[...]
    gn_g = gng_ref[...]
    gn_b = gnb_ref[...]
    states = [state_ref[h] for h in range(RET_HEADS)]
    pad_rows = jnp.zeros((HEAD_PAD - RET_DK, RET_DV), BF16)
    for j in range(RET_ROWS // RET_CHUNK):
        rows = slice(j * RET_CHUNK, (j + 1) * RET_CHUNK)
        gate = g_ref[rows, :]
        for h in range(RET_HEADS):
            hs = slice(h * HEAD_PAD, (h + 1) * HEAD_PAD)
            vs = slice(h * RET_DV, (h + 1) * RET_DV)
            qh = q_ref[rows, hs]
            vh = v_ref[rows, vs]
            scores = _dot_nt(qh, k_ref[rows, hs]) * din_ref[h]
            intra = _dot(scores.astype(BF16), vh)
            state_padded = jnp.concatenate([states[h].astype(BF16), pad_rows], axis=0)
            cross = _dot(qh, state_padded) * qdec[:, vs]
            kv = _dot(kdt_ref[h * RET_DK:(h + 1) * RET_DK, rows], vh)
            states[h] = cdec_ref[h] * states[h] + kv
            o = intra + cross
            mu = jnp.mean(o, axis=-1, keepdims=True)
            oc = o - mu
            var = jnp.mean(oc * oc, axis=-1, keepdims=True)
            o = oc * lax.rsqrt(var + LN_EPS) * gn_g[:, vs] + gn_b[:, vs]
            o_ref[rows, vs] = (jax.nn.silu(gate[:, vs]) * o).astype(BF16)
    for h in range(RET_HEADS):
        state_ref[h] = states[h]


def _retention(rq, rk, rkdt, rv, rg, din, qdec, cdec, gn_g, gn_b):
    nc = SEQ // RET_ROWS
    rows = lambda w: pl.BlockSpec((RET_ROWS, w), lambda b, n: (b * nc + n, 0))
    padded = RET_HEADS * HEAD_PAD
    return pl.pallas_call(
        _retention_kernel,
        out_shape=jax.ShapeDtypeStruct((N_TOK, RET_V_W), BF16),
        grid=(BATCH, nc),
        in_specs=[rows(padded), rows(padded),
                  pl.BlockSpec((RET_QK_W, RET_ROWS), lambda b, n: (0, b * nc + n)),
                  rows(RET_V_W), rows(RET_V_W),
                  _const_spec((RET_HEADS, RET_CHUNK, RET_CHUNK)),
                  _const_spec((RET_CHUNK, RET_V_W)),
                  _const_spec((RET_HEADS, RET_DK, RET_DV)),
                  _const_spec((1, RET_V_W)), _const_spec((1, RET_V_W))],
        out_specs=rows(RET_V_W),
        scratch_shapes=[pltpu.VMEM((RET_HEADS, RET_DK, RET_DV), F32)],
        compiler_params=_cparams(("parallel", "arbitrary")),
        name="retention",
    )(rq, rk, rkdt, rv, rg, din, qdec, cdec, gn_g, gn_b)


def _mematt_kernel(q_ref, mem_ref, w_ref, o_ref, kv_ref):
    @pl.when(pl.program_id(1) == 0)
    def _():
        kv_ref[...] = _dot(mem_ref[...].astype(BF16), w_ref[...]).astype(BF16)

    q = q_ref[...]
    for h in range(MEM_HEADS):
        hs = slice(h * MEM_DH, (h + 1) * MEM_DH)
        logits = _dot_nt(q[:, hs], kv_ref[:, hs]) * (MEM_DH ** -0.5)
        m = jnp.max(logits, axis=-1, keepdims=True)
        p = jnp.exp(logits - m)
        denom = jnp.sum(p, axis=-1, keepdims=True)
        o_ref[:, hs] = (_dot(p.astype(BF16), kv_ref[:, MEM_W + h * MEM_DH:MEM_W + (h + 1) * MEM_DH])
                        / denom).astype(BF16)


def _mematt(mq, mem2d, w_kv_bf):
    tiles = SEQ // MEMATT_ROWS
    return pl.pallas_call(
        _mematt_kernel,
        out_shape=jax.ShapeDtypeStruct((N_TOK, MEM_W), BF16),
        grid=(BATCH, tiles),
        in_specs=[pl.BlockSpec((MEMATT_ROWS, MEM_W), lambda b, i: (b * tiles + i, 0)),
                  pl.BlockSpec((MEM_TOKENS, D_MODEL), lambda b, i: (b, 0)),
                  _const_spec((D_MODEL, 2 * MEM_W))],
        out_specs=pl.BlockSpec((MEMATT_ROWS, MEM_W), lambda b, i: (b * tiles + i, 0)),
        scratch_shapes=[pltpu.VMEM((MEM_TOKENS, 2 * MEM_W), BF16)],
        compiler_params=_cparams(("parallel", "arbitrary")),
        name="mem_attention",
    )(mq, mem2d, w_kv_bf)


def _t5_bias_kernel(table_ref, o_ref, bmax_ref):
    m = lax.broadcasted_iota(jnp.int32, (SUBLANES, 2 * DSA_T), 1)
    max_exact = T5_BUCKETS // 2
    head_max = [None] * DSA_HEADS
    for w in range(2):
        n = jnp.maximum(m - w * DSA_T, 0)
        nf = jnp.maximum(n, 1).astype(F32)
        large = max_exact + jnp.floor(jnp.log(nf / max_exact) / math.log(T5_MAX_DIST / max_exact)
                                      * (T5_BUCKETS - max_exact)).astype(jnp.int32)
        large = jnp.minimum(large, T5_BUCKETS - 1)
        bucket = jnp.where(n < max_exact, n, large)
        for h in range(DSA_HEADS):
            vec = jnp.zeros((SUBLANES, 2 * DSA_T), F32)
            for b in range(T5_BUCKETS):
                vec = jnp.where(bucket == b, table_ref[b, h] * LOG2E, vec)
            rows = jnp.broadcast_to(vec[0:1, :], (DSA_T, 2 * DSA_T))
            o_ref[w, h] = pltpu.roll(rows, 0, 1, stride=1, stride_axis=0)[:, DSA_T:]
            top = jnp.max(vec[0:1, :], axis=1, keepdims=True)
            head_max[h] = top if head_max[h] is None else jnp.maximum(head_max[h], top)
    for h in range(DSA_HEADS):
        bmax_ref[h:h + 1, :] = jnp.broadcast_to(head_max[h], (1, LANES))


def _t5_bias(t5_table):
    return pl.pallas_call(
        _t5_bias_kernel,
        out_shape=(jax.ShapeDtypeStruct((2, DSA_HEADS, DSA_T, DSA_T), F32),
                   jax.ShapeDtypeStruct((DSA_HEADS, LANES), F32)),
        in_specs=[pl.BlockSpec(memory_space=pltpu.SMEM)],
        out_specs=(pl.BlockSpec(memory_space=pltpu.VMEM), pl.BlockSpec(memory_space=pltpu.VMEM)),
        compiler_params=pltpu.CompilerParams(vmem_limit_bytes=VMEM_LIMIT),
        name="t5_bias",
    )(t5_table)


def _fold_groups(x, op):
    return op(x.reshape(DSA_T // SUBLANES, SUBLANES, DSA_T), axis=0)


def _dsa_kernel(table_ref, bias_ref, bmax_ref, iq_ref, dq_ref, ikw_q_ref, ikb_ref, dk_ref, dvt_ref,
                o_ref, score_ref, logit_ref, acc_ref):
    qb = pl.program_id(1)
    q0 = qb * DSA_T
    n_chunks = qb + 1
    groups = DSA_T // SUBLANES
    key_in_chunk = lax.broadcasted_iota(jnp.int32, (DSA_T, DSA_T), 0)
    query_in_step = lax.broadcasted_iota(jnp.int32, (DSA_T, DSA_T), 1)

    def key_rows(c):
        return pl.ds(pl.multiple_of(c * DSA_T, DSA_T), DSA_T)

    def pair_rows(j):
        return slice(j * LANES, (j + 1) * LANES)

    def pair_products(keys_even, keys_odd, queries_t):
        prod = _dot(jnp.concatenate([keys_even, keys_odd], axis=0), queries_t)
        return prod[:DSA_T], prod[DSA_T:]

    iw_t = ikw_q_ref[...].T[IDX_DIM:IDX_DIM + IDX_HEADS, :] * IW_SCALE

    def scores_and_logits(c, carry, causal):
        raw_max, s_min, s_max = carry
        ik_low = ikb_ref[key_rows(c), 0:LANES]
        ik_high = ikb_ref[key_rows(c), LANES:2 * LANES]
        last = IDX_HEADS // 2 - 1
        for j in range(IDX_HEADS // 2):
            rel_even, rel_odd = pair_products(ik_low, ik_high, iq_ref[pair_rows(j), :])
            term = (jnp.maximum(rel_even, 0.0) * iw_t[2 * j:2 * j + 1, :]
                    + jnp.maximum(rel_odd, 0.0) * iw_t[2 * j + 1:2 * j + 2, :])
            total = term if j == 0 else score_ref[c] + term
            if j < last:
                score_ref[c] = total
            elif causal is None:
                score_ref[c] = total
                s_min = jnp.minimum(s_min, _fold_groups(total, jnp.min))
                s_max = jnp.maximum(s_max, _fold_groups(total, jnp.max))
            else:
                score_ref[c] = jnp.where(causal, total, NEG_INF)
                s_min = jnp.minimum(s_min, _fold_groups(jnp.where(causal, total, -NEG_INF), jnp.min))
                s_max = jnp.maximum(s_max, _fold_groups(jnp.where(causal, total, NEG_INF), jnp.max))
        out = list(raw_max)
        for j in range(DSA_HEADS // 2):
            keys = [dk_ref[key_rows(c), (2 * j + i) * HEAD_PAD:(2 * j + i + 1) * HEAD_PAD] for i in range(2)]
            for h, qk in zip((2 * j, 2 * j + 1), pair_products(keys[0], keys[1], dq_ref[pair_rows(j), :])):
                logit_ref[h, c] = qk
                out[h] = jnp.maximum(out[h], _fold_groups(qk, jnp.max))
        return tuple(out), s_min, s_max

    def paired_loop(start, stop, body, carry):
        n = stop - start
        carry = lax.fori_loop(
            0, lax.shift_right_logical(n, 1),
            lambda i, cr: body(start + 2 * i + 1, body(start + 2 * i, cr)), carry)
        return lax.cond(jnp.bitwise_and(n, 1) == 1, lambda cr: body(stop - 1, cr), lambda cr: cr, carry)

    low_init = jnp.full((SUBLANES, DSA_T), NEG_INF, F32)
    carry = (tuple(low_init for _ in range(DSA_HEADS)), -low_init, low_init)
    carry = paired_loop(0, qb, lambda c, cr: scores_and_logits(c, cr, None), carry)
    raw_max, s_min, s_max = scores_and_logits(qb, carry, key_in_chunk <= query_in_step)

    def count_where(pred):
        def body(c, acc):
            parts = [acc] + [jnp.zeros((SUBLANES, DSA_T), F32)] * (COUNT_ACCUMULATORS - 1)
            for g in range(groups):
                blk = score_ref[c, g * SUBLANES:(g + 1) * SUBLANES, :]
                a = g % COUNT_ACCUMULATORS
                parts[a] = parts[a] + jnp.where(pred(c, g * SUBLANES, blk), 1.0, 0.0)
            while len(parts) > 1:
                parts = [parts[i] + parts[i + 1] for i in range(0, len(parts), 2)]
            return parts[0]
        acc = paired_loop(0, n_chunks, body, jnp.zeros((SUBLANES, DSA_T), F32))
        return jnp.sum(acc, axis=0, keepdims=True)

    def count_ge(thr):
        thr_b = jnp.broadcast_to(thr, (SUBLANES, DSA_T))
        return count_where(lambda c, r0, blk: blk >= thr_b)

    q_min = jnp.min(s_min, axis=0, keepdims=True)
    q_max = jnp.max(s_max, axis=0, keepdims=True)
    t_q = q0 + lax.broadcasted_iota(jnp.int32, (1, DSA_T), 1)

    kf = float(TOPK)
    init = (q_min, q_max + jnp.maximum(jnp.abs(q_max), 1.0) * 1e-6,
            (t_q + 1).astype(F32), jnp.zeros((1, DSA_T), F32))

    def bisect_step(_i, carry):
        lo, hi, cnt_lo, cnt_hi = carry
        mid = 0.5 * (lo + hi)
        cnt = count_ge(mid)
        ge = cnt >= kf
        return (jnp.where(ge, mid, lo), jnp.where(ge, hi, mid),
                jnp.where(ge, cnt, cnt_lo), jnp.where(ge, cnt_hi, cnt))

    n_plain = jnp.where(qb == 0, 0, BISECT_PLAIN_STEPS)
    lo, hi, cnt_lo, cnt_hi = lax.fori_loop(0, n_plain, bisect_step, init)

    big = -NEG_INF

    def snap_body(c, carry):
        mn, mx = carry
        blk = score_ref[c]
        mn = jnp.minimum(mn, _fold_groups(jnp.where(blk >= lo, blk, big), jnp.min))
        mx = jnp.maximum(mx, _fold_groups(jnp.where(blk < hi, blk, -big), jnp.max))
        return mn, mx

    mn, mx = paired_loop(
        0, n_chunks, snap_body,
        (jnp.full((SUBLANES, DSA_T), big, F32), jnp.full((SUBLANES, DSA_T), -big, F32)))
    lo = jnp.min(mn, axis=0, keepdims=True)
    top = jnp.max(mx, axis=0, keepdims=True)

    def split_pair(carry):
        lo, top, cnt_lo, cnt_hi = carry
        pair = (cnt_lo > kf) & (lo < top) & (cnt_lo - cnt_hi == 2.0)
        return jnp.where(pair, top, lo), top, jnp.where(pair, cnt_hi + 1.0, cnt_lo), cnt_hi

    def pending_of(carry):
        lo, top, cnt_lo, _ = carry
        over = cnt_lo > kf
        return jnp.max(jnp.where(over & (lo < top), 2.0, jnp.where(over, 1.0, 0.0)))

    def value_step(state):
        (lo, top, cnt_lo, cnt_hi), _ = state
        unresolved = (cnt_lo > kf) & (lo < top)
        mid = lo + (top - lo) * 0.5
        mid = jnp.minimum(jnp.where(mid > lo, mid, top), top)

        def body(c, carry):
            cnt, mn, mx = carry
            blk = score_ref[c]
            is_ge = blk >= mid
            cnt = cnt + _fold_groups(jnp.where(is_ge, 1.0, 0.0), jnp.sum)
            mn = jnp.minimum(mn, _fold_groups(jnp.where(is_ge, blk, big), jnp.min))
            mx = jnp.maximum(mx, _fold_groups(jnp.where(is_ge, -big, blk), jnp.max))
            return cnt, mn, mx

        cnt, mn, mx = lax.fori_loop(
            0, n_chunks, body,
            (jnp.zeros((SUBLANES, DSA_T), F32), jnp.full((SUBLANES, DSA_T), big, F32),
             jnp.full((SUBLANES, DSA_T), -big, F32)))
        cnt = jnp.sum(cnt, axis=0, keepdims=True)
        ge = cnt >= kf
        raise_lo = unresolved & ge
        lower_top = unresolved & jnp.logical_not(ge)
        carry = split_pair((jnp.where(raise_lo, jnp.min(mn, axis=0, keepdims=True), lo),
                            jnp.where(lower_top, jnp.max(mx, axis=0, keepdims=True), top),
                            jnp.where(raise_lo, cnt, cnt_lo), jnp.where(lower_top, cnt, cnt_hi)))
        return carry, pending_of(carry)

    start = split_pair((lo, top, cnt_lo, cnt_hi))
    (lo, top, cnt_lo, cnt_hi), status = lax.while_loop(
        lambda state: state[1] > 1.5, value_step, (start, pending_of(start)))

    tied = cnt_lo > kf
    need = kf - cnt_hi
    lo_b = jnp.broadcast_to(lo, (SUBLANES, DSA_T))
    key_in_group = lax.broadcasted_iota(jnp.int32, (SUBLANES, DSA_T), 0)

    def tie_search(_):
        def body(_i, carry):
            lo_i, hi_i = carry
            mid_i = lax.shift_right_arithmetic(lo_i + hi_i, 1)
            mid_b = jnp.broadcast_to(mid_i, (SUBLANES, DSA_T))
            cnt = count_where(lambda c, r0, blk: (blk == lo_b) & (key_in_group + (c * DSA_T + r0) <= mid_b))
            enough = cnt >= need
            return jnp.where(enough, lo_i, mid_i), jnp.where(enough, mid_i, hi_i)
        lo_i = jnp.full((1, DSA_T), -1, jnp.int32)
        hi_i = jnp.full((1, DSA_T), SEQ - 1, jnp.int32)
        _, cut = lax.fori_loop(0, int(math.log2(SEQ)) + 1, body, (lo_i, hi_i))
        return jnp.where(tied, cut, SEQ)

    cut = lax.cond(status > 0.5, tie_search, lambda _: jnp.full((1, DSA_T), SEQ, jnp.int32), 0)

    def write_mask(c):
        blk = score_ref[c]
        keep = (blk > lo) | ((blk == lo) & (key_in_chunk + c * DSA_T <= cut))
        score_ref[c] = jnp.where(keep, 0.0, NEG_INF)

    n_far = jnp.maximum(qb - 1, 0)

    def masked_logits(c, h, far):
        lg = logit_ref[h, c] + score_ref[c]
        return lg if far else lg + bias_ref[c - qb + 1, h]

    def far_bias(h):
        return table_ref[T5_BUCKETS - 1, h] * LOG2E

    def den_row(h):
        return h * DVT_HEAD_ROWS + DSA_DH

    def exp_and_pv(shifts, scores_to_masks):
        acc_ref[...] = jnp.zeros_like(acc_ref)

        def body(c, carry, far):
            if scores_to_masks:
                write_mask(c)
            for h in range(DSA_HEADS):
                shift = shifts[h] - far_bias(h) if far else shifts[h]
                p = jnp.exp2(masked_logits(c, h, far) - shift).astype(BF16)
                rows = slice(h * DVT_HEAD_ROWS, (h + 1) * DVT_HEAD_ROWS)
                acc_ref[rows, :] += _dot(dvt_ref[c, rows, :], p)
            return carry

        paired_loop(0, n_far, lambda c, carry: body(c, carry, True), 0)
        paired_loop(n_far, n_chunks, lambda c, carry: body(c, carry, False), 0)

    upper = [jnp.max(raw_max[h], axis=0, keepdims=True) + bmax_ref[h:h + 1, 0:1] for h in range(DSA_HEADS)]
    exp_and_pv(upper, True)

    den_min = acc_ref[den_row(0):den_row(0) + 1, :]
    for h in range(1, DSA_HEADS):
        den_min = jnp.minimum(den_min, acc_ref[den_row(h):den_row(h) + 1, :])

    @pl.when(jnp.min(den_min) < DEN_FLOOR)
    def _():
        def max_body(c, ms, far):
            return tuple(jnp.maximum(ms[h], _fold_groups(masked_logits(c, h, far), jnp.max) + (far_bias(h) if far else 0.0))
                         for h in range(DSA_HEADS))
        ms = tuple(jnp.full((SUBLANES, DSA_T), NEG_INF, F32) for _ in range(DSA_HEADS))
        ms = lax.fori_loop(0, n_far, lambda c, ms: max_body(c, ms, True), ms)
        ms = lax.fori_loop(n_far, n_chunks, lambda c, ms: max_body(c, ms, False), ms)
        exp_and_pv([jnp.max(m, axis=0, keepdims=True) for m in ms], False)

    heads = []
    for h in range(DSA_HEADS):
        r0 = h * DVT_HEAD_ROWS
        heads.append(acc_ref[r0:r0 + DSA_DH, :] / acc_ref[den_row(h):den_row(h) + 1, :])
    o_ref[...] = jnp.concatenate(heads, axis=0).T.astype(BF16)


def _dsa(t5_table, bias, bias_max, iq, dq, ikw, ikb, dk, dvt):
    nqb = SEQ // DSA_T
    padded = DSA_HEADS * HEAD_PAD
    assert DSA_T <= TOPK
    qrows = lambda w: pl.BlockSpec((DSA_T, w), lambda b, i: (b * nqb + i, 0))
    qcols = lambda h: pl.BlockSpec((h, DSA_T), lambda b, i: (0, b * nqb + i))
    krows = lambda w: pl.BlockSpec((SEQ, w), lambda b, i: (b, 0))
    return pl.pallas_call(
        _dsa_kernel,
        out_shape=jax.ShapeDtypeStruct((N_TOK, DSA_W), BF16),
        grid=(BATCH, nqb),
        in_specs=[pl.BlockSpec(memory_space=pltpu.SMEM),
                  _const_spec((2, DSA_HEADS, DSA_T, DSA_T)), _const_spec((DSA_HEADS, LANES)),
                  qcols(IDX_Q_W), qcols(DSA_W), qrows(LANES),
                  krows(2 * LANES), krows(padded),
                  pl.BlockSpec((nqb, DVT_ROWS, DSA_T), lambda b, i: (b, 0, 0))],
        out_specs=qrows(DSA_W),
        scratch_shapes=[pltpu.VMEM((nqb, DSA_T, DSA_T), F32),
                        pltpu.VMEM((DSA_HEADS, nqb, DSA_T, DSA_T), F32),
                        pltpu.VMEM((DVT_ROWS, DSA_T), F32)],
        compiler_params=_cparams(("parallel", "arbitrary")),
        name="dsa_attention",
    )(t5_table, bias, bias_max, iq, dq, ikw, ikb, dk, dvt)


def _merge_kernel(h_ref, oret_ref, odsa_ref, omem_ref, wg_ref, wr_ref, wd_ref, wm_ref, wo_ref,
                  g_ref, b_ref, o_ref):
    h = h_ref[...]
    hb = h.astype(BF16)
    merged = None
    for i, (src, w) in enumerate(((oret_ref, wr_ref), (odsa_ref, wd_ref), (omem_ref, wm_ref))):
        gate = _sigmoid(_dot(hb, wg_ref[:, i * D_MODEL:(i + 1) * D_MODEL]))
        term = gate * _dot(src[...], w[...])
        merged = term if merged is None else merged + term
    mix = _dot(merged.astype(BF16), wo_ref[...])
    o_ref[...] = _layer_norm(ALPHA * h + mix, g_ref[...], b_ref[...])


def _merge(h, o_ret, o_dsa, o_mem, wg, wr, wd, wm, wo, g, b):
    rows = lambda w: pl.BlockSpec((MERGE_ROWS, w), lambda i: (i, 0))
    return pl.pallas_call(
        _merge_kernel,
        out_shape=jax.ShapeDtypeStruct((N_TOK, D_MODEL), F32),
        grid=(N_TOK // MERGE_ROWS,),
        in_specs=[rows(D_MODEL), rows(RET_V_W), rows(DSA_W), rows(MEM_W),
                  _const_spec((D_MODEL, 3 * D_MODEL)),
                  _const_spec((RET_V_W, D_MODEL)), _const_spec((DSA_W, D_MODEL)),
                  _const_spec((MEM_W, D_MODEL)), _const_spec((D_MODEL, D_MODEL)),
                  _const_spec((1, D_MODEL)), _const_spec((1, D_MODEL))],
        out_specs=rows(D_MODEL),
        compiler_params=_cparams(("parallel",)),
        name="merge_ln2",
    )(h, o_ret, o_dsa, o_mem, wg, wr, wd, wm, wo, g, b)


def _rope_tables():
    half = RET_DK // 2
    freqs = ROPE_BASE ** (-jnp.arange(half, dtype=F32) / half)
    ang = jnp.arange(SEQ).astype(F32)[:, None] * freqs[None, :]
    cos, sin = jnp.cos(ang), jnp.sin(ang)
    cos_t = jnp.tile(jnp.concatenate([cos, cos], axis=-1), (1, RET_HEADS))
    sin_t = jnp.tile(jnp.concatenate([-sin, sin], axis=-1), (1, RET_HEADS))
    return cos_t, sin_t


def _decay_tables():
    c = RET_CHUNK
    gamma = 1.0 - 2.0 ** (-5.0 - jnp.arange(RET_HEADS, dtype=F32))
    lg = jnp.log(gamma)
    i = jnp.arange(c)
    diff = i[:, None] - i[None, :]
    decay_in = jnp.where(diff[None] >= 0, jnp.exp(jnp.maximum(diff, 0)[None] * lg[:, None, None]), 0.0).astype(F32)
    k_dec = jnp.exp((c - 1 - i)[None, :] * lg[:, None]).astype(F32)
    q_dec = jnp.exp((i + 1)[None, :] * lg[:, None]).astype(F32)
    chunk_dec = jnp.exp(c * lg).astype(F32)
    kdec_t = jnp.repeat(k_dec.T, RET_DK, axis=1)
    qdec_t = jnp.repeat(q_dec.T, RET_DV, axis=1)
    cdec_t = jnp.broadcast_to(chunk_dec[:, None, None], (RET_HEADS, RET_DK, RET_DV))
    return decay_in, kdec_t, qdec_t, cdec_t


def _layer(x, mem2d, ffn1_w_in, ffn1_w_out, ln1_g, ln1_b, w_in, t5_table, ret_gn_g, ret_gn_b,
           w_mem_kv, w_br_ret, w_br_dsa, w_br_mem, w_out, ln2_g, ln2_b,
           ffn2_w_in, ffn2_w_out, ln3_g, ln3_b, tables):
    cos_t, sin_t, din, kdec, qdec, cdec = tables
    row = lambda v: v.reshape(1, -1)
    bf = lambda w: w.astype(BF16)

    w_in_bf = _to_bf16(w_in, "cast_w_in")
    w_mq = w_in_bf[:, W_MQ0:W_G0]
    w_dvt = w_in_bf[:, W_DV0:W_IQ0].T
    kdec_rows = jnp.tile(kdec, (PROJ_ROWS // RET_CHUNK, 1))

    h = _ffn_ln(x, _to_bf16(ffn1_w_in, "cast_ffn1_w_in"), _to_bf16(ffn1_w_out, "cast_ffn1_w_out"),
                row(ln1_g), row(ln1_b), "ffn1_ln1")
    rq, rk, rkdt, rv, rg, dq, dk, dvt, iq, mq, ikw, ikb = _proj(h, w_in_bf, w_mq, w_dvt, cos_t, sin_t, kdec_rows)
    o_ret = _retention(rq, rk, rkdt, rv, rg, din, qdec, cdec, row(ret_gn_g), row(ret_gn_b))
    o_mem = _mematt(mq, mem2d, bf(w_mem_kv))
    o_dsa = _dsa(t5_table, *_t5_bias(t5_table), iq, dq, ikw, ikb, dk, dvt)
    x2 = _merge(h, o_ret, o_dsa, o_mem, w_in_bf[:, W_G0:], bf(w_br_ret), bf(w_br_dsa), bf(w_br_mem),
                bf(w_out), row(ln2_g), row(ln2_b))
    return _ffn_ln(x2, _to_bf16(ffn2_w_in, "cast_ffn2_w_in"), _to_bf16(ffn2_w_out, "cast_ffn2_w_out"),
                   row(ln3_g), row(ln3_b), "ffn2_ln3")


def kernel(x, mem, ffn1_w_in, ffn1_w_out, ln1_g, ln1_b, w_in, t5_table, ret_gn_g, ret_gn_b,
           w_mem_kv, w_br_ret, w_br_dsa, w_br_mem, w_out, ln2_g, ln2_b,
           ffn2_w_in, ffn2_w_out, ln3_g, ln3_b):
    assert x.shape == (BATCH, SEQ, D_MODEL) and mem.shape == (BATCH, MEM_TOKENS, D_MODEL)
    tables = _rope_tables() + _decay_tables()
    y = x.reshape(N_TOK, D_MODEL)
    mem2d = mem.reshape(BATCH * MEM_TOKENS, D_MODEL)
    for l in range(DEPTH):
        y = _layer(y, mem2d, ffn1_w_in[l], ffn1_w_out[l], ln1_g[l], ln1_b[l], w_in[l], t5_table,
                   ret_gn_g[l], ret_gn_b[l], w_mem_kv[l], w_br_ret[l], w_br_dsa[l], w_br_mem[l],
                   w_out[l], ln2_g[l], ln2_b[l], ffn2_w_in[l], ffn2_w_out[l], ln3_g[l], ln3_b[l],
                   tables)
    return y.reshape(BATCH, SEQ, D_MODEL)
```

```python
import math

import jax
import jax.numpy as jnp
from jax import lax
from jax.experimental import pallas as pl
from jax.experimental.pallas import tpu as pltpu

F32 = jnp.float32
BF16 = jnp.bfloat16

D_MODEL = 1024
BATCH = 8
SEQ = 2048
MEM_TOKENS = 256
RET_HEADS, RET_DK, RET_DV, RET_CHUNK = 4, 64, 128, 128
DSA_HEADS, DSA_DH = 8, 64
IDX_HEADS, IDX_DIM = 8, 64
TOPK = min(256, SEQ // 4)
MEM_HEADS, MEM_DH = 4, 128
T5_BUCKETS, T5_MAX_DIST = 32, 128
D_FF = 2816
ROPE_BASE = 10000.0
LN_EPS = 1e-5
NEG_INF = -1e30
DEPTH = 1
ALPHA = (2.0 * DEPTH) ** 0.25

RET_QK_W = RET_HEADS * RET_DK
RET_V_W = RET_HEADS * RET_DV
DSA_W = DSA_HEADS * DSA_DH
IDX_Q_W = IDX_HEADS * IDX_DIM
MEM_W = MEM_HEADS * MEM_DH
N_TOK = BATCH * SEQ

V7X_VMEM_BYTES = 64 * 1024 * 1024
VMEM_LIMIT = V7X_VMEM_BYTES - 8 * 1024 * 1024
LANES = 128
SUBLANES = 8

FFN_ROWS = 1024
MXU_TILE = 256
FFN_CHUNK_EDGES = (0, 6 * MXU_TILE, D_FF)
PROJ_ROWS = 1024
MERGE_ROWS = 1024
RET_ROWS = 1024
MEMATT_ROWS = 2048
DSA_T = 256
HEAD_PAD = LANES
DVT_HEAD_ROWS = DSA_DH + 16
DVT_ROWS = DSA_HEADS * DVT_HEAD_ROWS
BISECT_PLAIN_STEPS = 16
COUNT_ACCUMULATORS = 4
LOG2E = math.log2(math.e)
DEN_FLOOR = 2.0 ** -100

W_RQ0 = 0
W_RK0 = W_RQ0 + RET_QK_W
W_RV0 = W_RK0 + RET_QK_W
W_RG0 = W_RV0 + RET_V_W
W_DQ0 = W_RG0 + RET_V_W
W_DK0 = W_DQ0 + DSA_W
W_DV0 = W_DK0 + DSA_W
W_IQ0 = W_DV0 + DSA_W
W_IK0 = W_IQ0 + IDX_Q_W
W_IW0 = W_IK0 + IDX_DIM
W_MQ0 = W_IW0 + IDX_HEADS
W_G0 = W_MQ0 + MEM_W
IW_SCALE = IDX_HEADS ** -0.5 * IDX_DIM ** -0.5


def _cparams(sem):
    return pltpu.CompilerParams(dimension_semantics=sem, vmem_limit_bytes=VMEM_LIMIT)


def _const_spec(shape):
    nd = len(shape)
    return pl.BlockSpec(shape, lambda *_: (0,) * nd, pipeline_mode=pl.Buffered(1))


def _layer_norm(y, g, b):
    mu = jnp.mean(y, axis=-1, keepdims=True)
    yc = y - mu
    var = jnp.mean(yc * yc, axis=-1, keepdims=True)
    return yc * lax.rsqrt(var + LN_EPS) * g + b


def _sigmoid(x):
    return 0.5 * jnp.tanh(0.5 * x) + 0.5


def _dot(a, b):
    return jnp.dot(a, b, preferred_element_type=F32)


def _dot_nt(a, b):
    return lax.dot_general(a, b, (((1,), (1,)), ((), ())), preferred_element_type=F32)


def _ffn_ln_kernel(x_ref, wi_ref, wo_ref, g_ref, b_ref, o_ref):
    x = x_ref[...]
    xb = x.astype(BF16)
    acc = None
    for lo, hi in zip(FFN_CHUNK_EDGES[:-1], FFN_CHUNK_EDGES[1:]):
        a = _dot(xb, wi_ref[:, lo:hi])
        u = _dot(xb, wi_ref[:, D_FF + lo:D_FF + hi])
        act = (a * _sigmoid(a) * u).astype(BF16)
        part = _dot(act, wo_ref[lo:hi, :])
        acc = part if acc is None else acc + part
    o_ref[...] = _layer_norm(ALPHA * x + 0.5 * acc, g_ref[...], b_ref[...])


def _ffn_ln(x, w_in_bf, w_out_bf, g, b, name):
    return pl.pallas_call(
        _ffn_ln_kernel,
        out_shape=jax.ShapeDtypeStruct((N_TOK, D_MODEL), F32),
        grid=(N_TOK // FFN_ROWS,),
        in_specs=[
            pl.BlockSpec((FFN_ROWS, D_MODEL), lambda i: (i, 0)),
            _const_spec((D_MODEL, 2 * D_FF)),
            _const_spec((D_FF, D_MODEL)),
            _const_spec((1, D_MODEL)),
            _const_spec((1, D_MODEL)),
        ],
        out_specs=pl.BlockSpec((FFN_ROWS, D_MODEL), lambda i: (i, 0)),
        compiler_params=_cparams(("parallel",)),
        name=name,
    )(x, w_in_bf, w_out_bf, g, b)


def _rope(x, cos, sin_signed):
    width = x.shape[-1]
    lane = lax.broadcasted_iota(jnp.int32, x.shape, 1)
    first_half = (lane % RET_DK) < (RET_DK // 2)
    swapped = jnp.where(first_half,
                        pltpu.roll(x, width - RET_DK // 2, 1),
                        pltpu.roll(x, RET_DK // 2, 1))
    return x * cos + swapped * sin_signed


def _store_split_heads(dst_ref, val):
    low = lax.broadcasted_iota(jnp.int32, (val.shape[0], LANES), 1) < DSA_DH
    for j in range(val.shape[1] // LANES):
        pair = val[:, j * LANES:(j + 1) * LANES]
        dst_ref[:, (2 * j) * LANES:(2 * j + 1) * LANES] = jnp.where(low, pair, 0.0).astype(dst_ref.dtype)
        dst_ref[:, (2 * j + 1) * LANES:(2 * j + 2) * LANES] = jnp.where(low, 0.0, pair).astype(dst_ref.dtype)


def _store_low_heads(dst_ref, val):
    low = lax.broadcasted_iota(jnp.int32, (val.shape[0], LANES), 1) < RET_DK
    for j in range(val.shape[1] // LANES):
        pair = val[:, j * LANES:(j + 1) * LANES]
        dst_ref[:, (2 * j) * LANES:(2 * j + 1) * LANES] = jnp.where(low, pair, 0.0).astype(dst_ref.dtype)
        dst_ref[:, (2 * j + 1) * LANES:(2 * j + 2) * LANES] = jnp.where(
            low, pltpu.roll(pair, RET_DK, 1), 0.0).astype(dst_ref.dtype)


def _proj_kernel(h_ref, w_ref, wmq_ref, wt_ref, cos_ref, sin_ref, kdec_ref,
                 rq_ref, rk_ref, rkdt_ref, rv_ref, rg_ref, dq_ref, dk_ref, dvt_ref, iq_ref, mq_ref,
                 ikwt_ref, ikb_ref):
    hb = h_ref[...].astype(BF16)

    def proj(lo, width):
        return _dot(hb, w_ref[:, lo:lo + width])

    cos = cos_ref[...]
    sin = sin_ref[...]
    _store_low_heads(rq_ref, _rope(proj(W_RQ0, RET_QK_W), cos, sin))
    rk = _rope(proj(W_RK0, RET_QK_W), cos, sin) * (RET_DK ** -0.5)
    _store_low_heads(rk_ref, rk)
    rkdt_ref[...] = (rk * kdec_ref[...]).T.astype(BF16)
    rv_ref[...] = proj(W_RV0, RET_V_W).astype(BF16)
    rg_ref[...] = proj(W_RG0, RET_V_W)
    dq_ref[...] = (proj(W_DQ0, DSA_W) * (DSA_DH ** -0.5 * LOG2E)).T.astype(BF16)
    _store_split_heads(dk_ref, proj(W_DK0, DSA_W))
    iq_ref[...] = proj(W_IQ0, IDX_Q_W).T.astype(BF16)
    mq_ref[...] = _dot(hb, wmq_ref[...]).astype(BF16)
    t = _dot_nt(wt_ref[...], hb)
    ikw_t = t[DSA_W:DSA_W + LANES]
    ikwt_ref[...] = ikw_t
    ikw = ikw_t.T
    low = lax.broadcasted_iota(jnp.int32, ikw.shape, 1) < IDX_DIM
    ik_low = jnp.where(low, ikw, 0.0)
    ikb_ref[:, 0:LANES] = ik_low.astype(BF16)
    ikb_ref[:, LANES:2 * LANES] = pltpu.roll(ik_low, IDX_DIM, 1).astype(BF16)
    vt = t[:DSA_W].astype(BF16)
    ones = jnp.ones((DVT_HEAD_ROWS - DSA_DH, DSA_T), BF16)
    for j in range(PROJ_ROWS // DSA_T):
        for h in range(DSA_HEADS):
            r0 = h * DVT_HEAD_ROWS
            dvt_ref[j, r0:r0 + DSA_DH, :] = vt[h * DSA_DH:(h + 1) * DSA_DH, j * DSA_T:(j + 1) * DSA_T]
            dvt_ref[j, r0 + DSA_DH:r0 + DVT_HEAD_ROWS, :] = ones


def _proj(h, w_in_bf, w_mq_bf, w_t_bf, cos_t, sin_t, kdec_rows):
    rows = lambda w: pl.BlockSpec((PROJ_ROWS, w), lambda i: (i, 0))
    seq_tiles = SEQ // PROJ_ROWS
    pos = lambda w: pl.BlockSpec((PROJ_ROWS, w), lambda i: (i % seq_tiles, 0))
    sd = jax.ShapeDtypeStruct
    ret_padded = RET_HEADS * HEAD_PAD
    dsa_padded = DSA_HEADS * HEAD_PAD
    slabs = PROJ_ROWS // DSA_T
    assert W_IK0 % LANES == 0
    return pl.pallas_call(
        _proj_kernel,
        out_shape=(
            sd((N_TOK, ret_padded), BF16), sd((N_TOK, ret_padded), BF16), sd((RET_QK_W, N_TOK), BF16),
            sd((N_TOK, RET_V_W), BF16), sd((N_TOK, RET_V_W), F32),
            sd((DSA_W, N_TOK), BF16), sd((N_TOK, dsa_padded), BF16),
            sd((N_TOK // DSA_T, DVT_ROWS, DSA_T), BF16),
            sd((IDX_Q_W, N_TOK), BF16), sd((N_TOK, MEM_W), BF16),
            sd((LANES, N_TOK), F32), sd((N_TOK, 2 * LANES), BF16),
        ),
        grid=(N_TOK // PROJ_ROWS,),
        in_specs=[rows(D_MODEL),
                  pl.BlockSpec((D_MODEL, W_IK0), lambda i: (0, 0), pipeline_mode=pl.Buffered(1)),
                  _const_spec((D_MODEL, MEM_W)), _const_spec((DSA_W + LANES, D_MODEL)),
                  pos(RET_QK_W), pos(RET_QK_W), _const_spec((PROJ_ROWS, RET_QK_W))],
        out_specs=(rows(ret_padded), rows(ret_padded),
                   pl.BlockSpec((RET_QK_W, PROJ_ROWS), lambda i: (0, i)),
                   rows(RET_V_W), rows(RET_V_W),
                   pl.BlockSpec((DSA_W, PROJ_ROWS), lambda i: (0, i)), rows(dsa_padded),
                   pl.BlockSpec((slabs, DVT_ROWS, DSA_T), lambda i: (i, 0, 0)),
                   pl.BlockSpec((IDX_Q_W, PROJ_ROWS), lambda i: (0, i)),
                   rows(MEM_W), pl.BlockSpec((LANES, PROJ_ROWS), lambda i: (0, i)), rows(2 * LANES)),
        compiler_params=_cparams(("parallel",)),
        name="mixer_proj",
    )(h, w_in_bf, w_mq_bf, w_t_bf, cos_t, sin_t, kdec_rows)


def _retention_kernel(q_ref, k_ref, kdt_ref, v_ref, g_ref, din_ref, qdec_ref, cdec_ref,
                      gng_ref, gnb_ref, o_ref, state_ref):
    @pl.when(pl.program_id(1) == 0)
    def _():
        state_ref[...] = jnp.zeros_like(state_ref)

    qdec = qdec_ref[...]
    gn_g = gng_ref[...]
    gn_b = gnb_ref[...]
    states = [state_ref[h] for h in range(RET_HEADS)]
    pad_rows = jnp.zeros((HEAD_PAD - RET_DK, RET_DV), BF16)
    for j in range(RET_ROWS // RET_CHUNK):
        rows = slice(j * RET_CHUNK, (j + 1) * RET_CHUNK)
        gate = g_ref[rows, :]
        for h in range(RET_HEADS):
            hs = slice(h * HEAD_PAD, (h + 1) * HEAD_PAD)
            vs = slice(h * RET_DV, (h + 1) * RET_DV)
            qh = q_ref[rows, hs]
            vh = v_ref[rows, vs]
            scores = _dot_nt(qh, k_ref[rows, hs]) * din_ref[h]
            intra = _dot(scores.astype(BF16), vh)
            state_padded = jnp.concatenate([states[h].astype(BF16), pad_rows], axis=0)
            cross = _dot(qh, state_padded) * qdec[:, vs]
            kv = _dot(kdt_ref[h * RET_DK:(h + 1) * RET_DK, rows], vh)
            states[h] = cdec_ref[h] * states[h] + kv
            o = intra + cross
            mu = jnp.mean(o, axis=-1, keepdims=True)
            oc = o - mu
            var = jnp.mean(oc * oc, axis=-1, keepdims=True)
            o = oc * lax.rsqrt(var + LN_EPS) * gn_g[:, vs] + gn_b[:, vs]
            o_ref[rows, vs] = (jax.nn.silu(gate[:, vs]) * o).astype(BF16)
    for h in range(RET_HEADS):
        state_ref[h] = states[h]


def _retention(rq, rk, rkdt, rv, rg, din, qdec, cdec, gn_g, gn_b):
    nc = SEQ // RET_ROWS
    rows = lambda w: pl.BlockSpec((RET_ROWS, w), lambda b, n: (b * nc + n, 0))
    padded = RET_HEADS * HEAD_PAD
    return pl.pallas_call(
        _retention_kernel,
        out_shape=jax.ShapeDtypeStruct((N_TOK, RET_V_W), BF16),
        grid=(BATCH, nc),
        in_specs=[rows(padded), rows(padded),
                  pl.BlockSpec((RET_QK_W, RET_ROWS), lambda b, n: (0, b * nc + n)),
                  rows(RET_V_W), rows(RET_V_W),
                  _const_spec((RET_HEADS, RET_CHUNK, RET_CHUNK)),
                  _const_spec((RET_CHUNK, RET_V_W)),
                  _const_spec((RET_HEADS, RET_DK, RET_DV)),
                  _const_spec((1, RET_V_W)), _const_spec((1, RET_V_W))],
        out_specs=rows(RET_V_W),
        scratch_shapes=[pltpu.VMEM((RET_HEADS, RET_DK, RET_DV), F32)],
        compiler_params=_cparams(("parallel", "arbitrary")),
        name="retention",
    )(rq, rk, rkdt, rv, rg, din, qdec, cdec, gn_g, gn_b)


def _mematt_kernel(q_ref, mem_ref, w_ref, o_ref, kv_ref):
    @pl.when(pl.program_id(1) == 0)
    def _():
        kv_ref[...] = _dot(mem_ref[...].astype(BF16), w_ref[...]).astype(BF16)

    q = q_ref[...]
    for h in range(MEM_HEADS):
        hs = slice(h * MEM_DH, (h + 1) * MEM_DH)
        logits = _dot_nt(q[:, hs], kv_ref[:, hs]) * (MEM_DH ** -0.5)
        m = jnp.max(logits, axis=-1, keepdims=True)
        p = jnp.exp(logits - m)
        denom = jnp.sum(p, axis=-1, keepdims=True)
        o_ref[:, hs] = (_dot(p.astype(BF16), kv_ref[:, MEM_W + h * MEM_DH:MEM_W + (h + 1) * MEM_DH])
                        / denom).astype(BF16)


def _mematt(mq, mem2d, w_kv_bf):
    tiles = SEQ // MEMATT_ROWS
    return pl.pallas_call(
        _mematt_kernel,
        out_shape=jax.ShapeDtypeStruct((N_TOK, MEM_W), BF16),
        grid=(BATCH, tiles),
        in_specs=[pl.BlockSpec((MEMATT_ROWS, MEM_W), lambda b, i: (b * tiles + i, 0)),
                  pl.BlockSpec((MEM_TOKENS, D_MODEL), lambda b, i: (b, 0)),
                  _const_spec((D_MODEL, 2 * MEM_W))],
        out_specs=pl.BlockSpec((MEMATT_ROWS, MEM_W), lambda b, i: (b * tiles + i, 0)),
        scratch_shapes=[pltpu.VMEM((MEM_TOKENS, 2 * MEM_W), BF16)],
        compiler_params=_cparams(("parallel", "arbitrary")),
        name="mem_attention",
    )(mq, mem2d, w_kv_bf)


def _t5_bias_kernel(table_ref, o_ref, bmax_ref):
    m = lax.broadcasted_iota(jnp.int32, (SUBLANES, 2 * DSA_T), 1)
    max_exact = T5_BUCKETS // 2
    head_max = [None] * DSA_HEADS
    for w in range(2):
        n = jnp.maximum(m - w * DSA_T, 0)
        nf = jnp.maximum(n, 1).astype(F32)
        large = max_exact + jnp.floor(jnp.log(nf / max_exact) / math.log(T5_MAX_DIST / max_exact)
                                      * (T5_BUCKETS - max_exact)).astype(jnp.int32)
        large = jnp.minimum(large, T5_BUCKETS - 1)
        bucket = jnp.where(n < max_exact, n, large)
        for h in range(DSA_HEADS):
            vec = jnp.zeros((SUBLANES, 2 * DSA_T), F32)
            for b in range(T5_BUCKETS):
                vec = jnp.where(bucket == b, table_ref[b, h] * LOG2E, vec)
            rows = jnp.broadcast_to(vec[0:1, :], (DSA_T, 2 * DSA_T))
            o_ref[w, h] = pltpu.roll(rows, 0, 1, stride=1, stride_axis=0)[:, DSA_T:]
            top = jnp.max(vec[0:1, :], axis=1, keepdims=True)
            head_max[h] = top if head_max[h] is None else jnp.maximum(head_max[h], top)
    for h in range(DSA_HEADS):
        bmax_ref[h:h + 1, :] = jnp.broadcast_to(head_max[h], (1, LANES))


def _t5_bias(t5_table):
    return pl.pallas_call(
        _t5_bias_kernel,
        out_shape=(jax.ShapeDtypeStruct((2, DSA_HEADS, DSA_T, DSA_T), F32),
                   jax.ShapeDtypeStruct((DSA_HEADS, LANES), F32)),
        in_specs=[pl.BlockSpec(memory_space=pltpu.SMEM)],
        out_specs=(pl.BlockSpec(memory_space=pltpu.VMEM), pl.BlockSpec(memory_space=pltpu.VMEM)),
        compiler_params=pltpu.CompilerParams(vmem_limit_bytes=VMEM_LIMIT),
        name="t5_bias",
    )(t5_table)


def _fold_groups(x, op):
    return op(x.reshape(DSA_T // SUBLANES, SUBLANES, DSA_T), axis=0)


def _dsa_kernel(table_ref, bias_ref, bmax_ref, iq_ref, dq_ref, ikw_q_ref, ikb_ref, dk_ref, dvt_ref,
                o_ref, score_ref, logit_ref, acc_ref):
    qb = pl.program_id(1)
    q0 = qb * DSA_T
    n_chunks = qb + 1
    groups = DSA_T // SUBLANES
    key_in_chunk = lax.broadcasted_iota(jnp.int32, (DSA_T, DSA_T), 0)
    query_in_step = lax.broadcasted_iota(jnp.int32, (DSA_T, DSA_T), 1)

    def key_rows(c):
        return pl.ds(pl.multiple_of(c * DSA_T, DSA_T), DSA_T)

    def pair_rows(j):
        return slice(j * LANES, (j + 1) * LANES)

    def pair_products(keys_even, keys_odd, queries_t):
        prod = _dot(jnp.concatenate([keys_even, keys_odd], axis=0), queries_t)
        return prod[:DSA_T], prod[DSA_T:]

    iw_t = ikw_q_ref[IDX_DIM:IDX_DIM + IDX_HEADS, :] * IW_SCALE

    def scores_and_logits(c, carry, causal):
        raw_max, s_min, s_max = carry
        ik_low = ikb_ref[key_rows(c), 0:LANES]
        ik_high = ikb_ref[key_rows(c), LANES:2 * LANES]
        last = IDX_HEADS // 2 - 1
        for j in range(IDX_HEADS // 2):
            rel_even, rel_odd = pair_products(ik_low, ik_high, iq_ref[pair_rows(j), :])
            term = (jnp.maximum(rel_even, 0.0) * iw_t[2 * j:2 * j + 1, :]
                    + jnp.maximum(rel_odd, 0.0) * iw_t[2 * j + 1:2 * j + 2, :])
            total = term if j == 0 else score_ref[c] + term
            if j < last:
                score_ref[c] = total
            elif causal is None:
                score_ref[c] = total
                s_min = jnp.minimum(s_min, _fold_groups(total, jnp.min))
                s_max = jnp.maximum(s_max, _fold_groups(total, jnp.max))
            else:
                score_ref[c] = jnp.where(causal, total, NEG_INF)
                s_min = jnp.minimum(s_min, _fold_groups(jnp.where(causal, total, -NEG_INF), jnp.min))
                s_max = jnp.maximum(s_max, _fold_groups(jnp.where(causal, total, NEG_INF), jnp.max))
        out = list(raw_max)
        for j in range(DSA_HEADS // 2):
            keys = [dk_ref[key_rows(c), (2 * j + i) * HEAD_PAD:(2 * j + i + 1) * HEAD_PAD] for i in range(2)]
            for h, qk in zip((2 * j, 2 * j + 1), pair_products(keys[0], keys[1], dq_ref[pair_rows(j), :])):
                logit_ref[h, c] = qk
                out[h] = jnp.maximum(out[h], _fold_groups(qk, jnp.max))
        return tuple(out), s_min, s_max

    def paired_loop(start, stop, body, carry):
        n = stop - start
        carry = lax.fori_loop(
            0, lax.shift_right_logical(n, 1),
            lambda i, cr: body(start + 2 * i + 1, body(start + 2 * i, cr)), carry)
        return lax.cond(jnp.bitwise_and(n, 1) == 1, lambda cr: body(stop - 1, cr), lambda cr: cr, carry)

    low_init = jnp.full((SUBLANES, DSA_T), NEG_INF, F32)
    carry = (tuple(low_init for _ in range(DSA_HEADS)), -low_init, low_init)
    carry = paired_loop(0, qb, lambda c, cr: scores_and_logits(c, cr, None), carry)
    raw_max, s_min, s_max = scores_and_logits(qb, carry, key_in_chunk <= query_in_step)

    def count_where(pred):
        def body(c, acc):
            parts = [acc] + [jnp.zeros((SUBLANES, DSA_T), F32)] * (COUNT_ACCUMULATORS - 1)
            for g in range(groups):
                blk = score_ref[c, g * SUBLANES:(g + 1) * SUBLANES, :]
                a = g % COUNT_ACCUMULATORS
                parts[a] = parts[a] + jnp.where(pred(c, g * SUBLANES, blk), 1.0, 0.0)
            while len(parts) > 1:
                parts = [parts[i] + parts[i + 1] for i in range(0, len(parts), 2)]
            return parts[0]
        acc = paired_loop(0, n_chunks, body, jnp.zeros((SUBLANES, DSA_T), F32))
        return jnp.sum(acc, axis=0, keepdims=True)

    def count_ge(thr):
        thr_b = jnp.broadcast_to(thr, (SUBLANES, DSA_T))
        return count_where(lambda c, r0, blk: blk >= thr_b)

    q_min = jnp.min(s_min, axis=0, keepdims=True)
    q_max = jnp.max(s_max, axis=0, keepdims=True)
    t_q = q0 + lax.broadcasted_iota(jnp.int32, (1, DSA_T), 1)

    kf = float(TOPK)
    init = (q_min, q_max + jnp.maximum(jnp.abs(q_max), 1.0) * 1e-6,
            (t_q + 1).astype(F32), jnp.zeros((1, DSA_T), F32))

    def bisect_step(_i, carry):
        lo, hi, cnt_lo, cnt_hi = carry
        mid = 0.5 * (lo + hi)
        cnt = count_ge(mid)
        ge = cnt >= kf
        return (jnp.where(ge, mid, lo), jnp.where(ge, hi, mid),
                jnp.where(ge, cnt, cnt_lo), jnp.where(ge, cnt_hi, cnt))

    n_plain = jnp.where(qb == 0, 0, BISECT_PLAIN_STEPS)
    lo, hi, cnt_lo, cnt_hi = lax.fori_loop(0, n_plain, bisect_step, init)

    big = -NEG_INF

    def snap_body(c, carry):
        mn, mx = carry
        blk = score_ref[c]
        mn = jnp.minimum(mn, _fold_groups(jnp.where(blk >= lo, blk, big), jnp.min))
        mx = jnp.maximum(mx, _fold_groups(jnp.where(blk < hi, blk, -big), jnp.max))
        return mn, mx

    mn, mx = paired_loop(
        0, n_chunks, snap_body,
        (jnp.full((SUBLANES, DSA_T), big, F32), jnp.full((SUBLANES, DSA_T), -big, F32)))
    lo = jnp.min(mn, axis=0, keepdims=True)
    top = jnp.max(mx, axis=0, keepdims=True)

    def split_pair(carry):
        lo, top, cnt_lo, cnt_hi = carry
        pair = (cnt_lo > kf) & (lo < top) & (cnt_lo - cnt_hi == 2.0)
        return jnp.where(pair, top, lo), top, jnp.where(pair, cnt_hi + 1.0, cnt_lo), cnt_hi

    def pending_of(carry):
        lo, top, cnt_lo, _ = carry
        over = cnt_lo > kf
        return jnp.max(jnp.where(over & (lo < top), 2.0, jnp.where(over, 1.0, 0.0)))

    def value_step(state):
        (lo, top, cnt_lo, cnt_hi), _ = state
        unresolved = (cnt_lo > kf) & (lo < top)
        mid = lo + (top - lo) * 0.5
        mid = jnp.minimum(jnp.where(mid > lo, mid, top), top)

        def body(c, carry):
            cnt, mn, mx = carry
            blk = score_ref[c]
            is_ge = blk >= mid
            cnt = cnt + _fold_groups(jnp.where(is_ge, 1.0, 0.0), jnp.sum)
            mn = jnp.minimum(mn, _fold_groups(jnp.where(is_ge, blk, big), jnp.min))
            mx = jnp.maximum(mx, _fold_groups(jnp.where(is_ge, -big, blk), jnp.max))
            return cnt, mn, mx

        cnt, mn, mx = lax.fori_loop(
            0, n_chunks, body,
            (jnp.zeros((SUBLANES, DSA_T), F32), jnp.full((SUBLANES, DSA_T), big, F32),
             jnp.full((SUBLANES, DSA_T), -big, F32)))
        cnt = jnp.sum(cnt, axis=0, keepdims=True)
        ge = cnt >= kf
        raise_lo = unresolved & ge
        lower_top = unresolved & jnp.logical_not(ge)
        carry = split_pair((jnp.where(raise_lo, jnp.min(mn, axis=0, keepdims=True), lo),
                            jnp.where(lower_top, jnp.max(mx, axis=0, keepdims=True), top),
                            jnp.where(raise_lo, cnt, cnt_lo), jnp.where(lower_top, cnt, cnt_hi)))
        return carry, pending_of(carry)

    start = split_pair((lo, top, cnt_lo, cnt_hi))
    (lo, top, cnt_lo, cnt_hi), status = lax.while_loop(
        lambda state: state[1] > 1.5, value_step, (start, pending_of(start)))

    tied = cnt_lo > kf
    need = kf - cnt_hi
    lo_b = jnp.broadcast_to(lo, (SUBLANES, DSA_T))
    key_in_group = lax.broadcasted_iota(jnp.int32, (SUBLANES, DSA_T), 0)

    def tie_search(_):
        def body(_i, carry):
            lo_i, hi_i = carry
            mid_i = lax.shift_right_arithmetic(lo_i + hi_i, 1)
            mid_b = jnp.broadcast_to(mid_i, (SUBLANES, DSA_T))
            cnt = count_where(lambda c, r0, blk: (blk == lo_b) & (key_in_group + (c * DSA_T + r0) <= mid_b))
            enough = cnt >= need
            return jnp.where(enough, lo_i, mid_i), jnp.where(enough, mid_i, hi_i)
        lo_i = jnp.full((1, DSA_T), -1, jnp.int32)
        hi_i = jnp.full((1, DSA_T), SEQ - 1, jnp.int32)
        _, cut = lax.fori_loop(0, int(math.log2(SEQ)) + 1, body, (lo_i, hi_i))
        return jnp.where(tied, cut, SEQ)

    cut = lax.cond(status > 0.5, tie_search, lambda _: jnp.full((1, DSA_T), SEQ, jnp.int32), 0)

    def write_mask(c):
        blk = score_ref[c]
        keep = (blk > lo) | ((blk == lo) & (key_in_chunk + c * DSA_T <= cut))
        score_ref[c] = jnp.where(keep, 0.0, NEG_INF)

    n_far = jnp.maximum(qb - 1, 0)

    def masked_logits(c, h, far):
        lg = logit_ref[h, c] + score_ref[c]
        return lg if far else lg + bias_ref[c - qb + 1, h]

    def far_bias(h):
        return table_ref[T5_BUCKETS - 1, h] * LOG2E

    def den_row(h):
        return h * DVT_HEAD_ROWS + DSA_DH

    def exp_and_pv(shifts, scores_to_masks):
        acc_ref[...] = jnp.zeros_like(acc_ref)

        def body(c, carry, far):
            if scores_to_masks:
                write_mask(c)
            for h in range(DSA_HEADS):
                shift = shifts[h] - far_bias(h) if far else shifts[h]
                p = jnp.exp2(masked_logits(c, h, far) - shift).astype(BF16)
                rows = slice(h * DVT_HEAD_ROWS, (h + 1) * DVT_HEAD_ROWS)
                acc_ref[rows, :] += _dot(dvt_ref[c, rows, :], p)
            return carry

        paired_loop(0, n_far, lambda c, carry: body(c, carry, True), 0)
        paired_loop(n_far, n_chunks, lambda c, carry: body(c, carry, False), 0)

    upper = [jnp.max(raw_max[h], axis=0, keepdims=True) + bmax_ref[h:h + 1, 0:1] for h in range(DSA_HEADS)]
    exp_and_pv(upper, True)

    den_min = acc_ref[den_row(0):den_row(0) + 1, :]
    for h in range(1, DSA_HEADS):
        den_min = jnp.minimum(den_min, acc_ref[den_row(h):den_row(h) + 1, :])

    @pl.when(jnp.min(den_min) < DEN_FLOOR)
    def _():
        def max_body(c, ms, far):
            return tuple(jnp.maximum(ms[h], _fold_groups(masked_logits(c, h, far), jnp.max) + (far_bias(h) if far else 0.0))
                         for h in range(DSA_HEADS))
        ms = tuple(jnp.full((SUBLANES, DSA_T), NEG_INF, F32) for _ in range(DSA_HEADS))
        ms = lax.fori_loop(0, n_far, lambda c, ms: max_body(c, ms, True), ms)
        ms = lax.fori_loop(n_far, n_chunks, lambda c, ms: max_body(c, ms, False), ms)
        exp_and_pv([jnp.max(m, axis=0, keepdims=True) for m in ms], False)

    heads = []
    for h in range(DSA_HEADS):
        r0 = h * DVT_HEAD_ROWS
        heads.append(acc_ref[r0:r0 + DSA_DH, :] / acc_ref[den_row(h):den_row(h) + 1, :])
    o_ref[...] = jnp.concatenate(heads, axis=0).T.astype(BF16)


def _dsa(t5_table, bias, bias_max, iq, dq, ikw, ikb, dk, dvt):
    nqb = SEQ // DSA_T
    padded = DSA_HEADS * HEAD_PAD
    assert DSA_T <= TOPK
    qrows = lambda w: pl.BlockSpec((DSA_T, w), lambda b, i: (b * nqb + i, 0))
    qcols = lambda h: pl.BlockSpec((h, DSA_T), lambda b, i: (0, b * nqb + i))
    krows = lambda w: pl.BlockSpec((SEQ, w), lambda b, i: (b, 0))
    return pl.pallas_call(
        _dsa_kernel,
        out_shape=jax.ShapeDtypeStruct((N_TOK, DSA_W), BF16),
        grid=(BATCH, nqb),
        in_specs=[pl.BlockSpec(memory_space=pltpu.SMEM),
                  _const_spec((2, DSA_HEADS, DSA_T, DSA_T)), _const_spec((DSA_HEADS, LANES)),
                  qcols(IDX_Q_W), qcols(DSA_W), qcols(LANES),
                  krows(2 * LANES), krows(padded),
                  pl.BlockSpec((nqb, DVT_ROWS, DSA_T), lambda b, i: (b, 0, 0))],
        out_specs=qrows(DSA_W),
        scratch_shapes=[pltpu.VMEM((nqb, DSA_T, DSA_T), F32),
                        pltpu.VMEM((DSA_HEADS, nqb, DSA_T, DSA_T), F32),
                        pltpu.VMEM((DVT_ROWS, DSA_T), F32)],
        compiler_params=_cparams(("parallel", "arbitrary")),
        name="dsa_attention",
    )(t5_table, bias, bias_max, iq, dq, ikw, ikb, dk, dvt)


def _merge_kernel(h_ref, oret_ref, odsa_ref, omem_ref, wg_ref, wr_ref, wd_ref, wm_ref, wo_ref,
                  g_ref, b_ref, o_ref):
    h = h_ref[...]
    hb = h.astype(BF16)
    merged = None
    for i, (src, w) in enumerate(((oret_ref, wr_ref), (odsa_ref, wd_ref), (omem_ref, wm_ref))):
        gate = _sigmoid(_dot(hb, wg_ref[:, i * D_MODEL:(i + 1) * D_MODEL]))
        term = gate * _dot(src[...], w[...])
        merged = term if merged is None else merged + term
    mix = _dot(merged.astype(BF16), wo_ref[...])
    o_ref[...] = _layer_norm(ALPHA * h + mix, g_ref[...], b_ref[...])


def _merge(h, o_ret, o_dsa, o_mem, wg, wr, wd, wm, wo, g, b):
    rows = lambda w: pl.BlockSpec((MERGE_ROWS, w), lambda i: (i, 0))
    return pl.pallas_call(
        _merge_kernel,
        out_shape=jax.ShapeDtypeStruct((N_TOK, D_MODEL), F32),
        grid=(N_TOK // MERGE_ROWS,),
        in_specs=[rows(D_MODEL), rows(RET_V_W), rows(DSA_W), rows(MEM_W),
                  _const_spec((D_MODEL, 3 * D_MODEL)),
                  _const_spec((RET_V_W, D_MODEL)), _const_spec((DSA_W, D_MODEL)),
                  _const_spec((MEM_W, D_MODEL)), _const_spec((D_MODEL, D_MODEL)),
                  _const_spec((1, D_MODEL)), _const_spec((1, D_MODEL))],
        out_specs=rows(D_MODEL),
        compiler_params=_cparams(("parallel",)),
        name="merge_ln2",
    )(h, o_ret, o_dsa, o_mem, wg, wr, wd, wm, wo, g, b)


def _rope_tables():
    half = RET_DK // 2
    freqs = ROPE_BASE ** (-jnp.arange(half, dtype=F32) / half)
    ang = jnp.arange(SEQ).astype(F32)[:, None] * freqs[None, :]
    cos, sin = jnp.cos(ang), jnp.sin(ang)
    cos_t = jnp.tile(jnp.concatenate([cos, cos], axis=-1), (1, RET_HEADS))
    sin_t = jnp.tile(jnp.concatenate([-sin, sin], axis=-1), (1, RET_HEADS))
    return cos_t, sin_t


def _decay_tables():
    c = RET_CHUNK
    gamma = 1.0 - 2.0 ** (-5.0 - jnp.arange(RET_HEADS, dtype=F32))
    lg = jnp.log(gamma)
    i = jnp.arange(c)
    diff = i[:, None] - i[None, :]
    decay_in = jnp.where(diff[None] >= 0, jnp.exp(jnp.maximum(diff, 0)[None] * lg[:, None, None]), 0.0).astype(F32)
    k_dec = jnp.exp((c - 1 - i)[None, :] * lg[:, None]).astype(F32)
    q_dec = jnp.exp((i + 1)[None, :] * lg[:, None]).astype(F32)
    chunk_dec = jnp.exp(c * lg).astype(F32)
    kdec_t = jnp.repeat(k_dec.T, RET_DK, axis=1)
    qdec_t = jnp.repeat(q_dec.T, RET_DV, axis=1)
    cdec_t = jnp.broadcast_to(chunk_dec[:, None, None], (RET_HEADS, RET_DK, RET_DV))
    return decay_in, kdec_t, qdec_t, cdec_t


def _layer(x, mem2d, ffn1_w_in, ffn1_w_out, ln1_g, ln1_b, w_in, t5_table, ret_gn_g, ret_gn_b,
           w_mem_kv, w_br_ret, w_br_dsa, w_br_mem, w_out, ln2_g, ln2_b,
           ffn2_w_in, ffn2_w_out, ln3_g, ln3_b, tables):
    cos_t, sin_t, din, kdec, qdec, cdec = tables
    row = lambda v: v.reshape(1, -1)
    bf = lambda w: w.astype(BF16)

    w_in_bf = bf(w_in)
    w_mq = w_in_bf[:, W_MQ0:W_G0]
    w_t = jnp.concatenate([w_in_bf[:, W_DV0:W_IQ0], w_in_bf[:, W_IK0:W_IK0 + LANES]], axis=1).T
    kdec_rows = jnp.tile(kdec, (PROJ_ROWS // RET_CHUNK, 1))

    h = _ffn_ln(x, bf(ffn1_w_in), bf(ffn1_w_out), row(ln1_g), row(ln1_b), "ffn1_ln1")
    rq, rk, rkdt, rv, rg, dq, dk, dvt, iq, mq, ikw, ikb = _proj(h, w_in_bf, w_mq, w_t, cos_t, sin_t, kdec_rows)
    o_ret = _retention(rq, rk, rkdt, rv, rg, din, qdec, cdec, row(ret_gn_g), row(ret_gn_b))
    o_mem = _mematt(mq, mem2d, bf(w_mem_kv))
    o_dsa = _dsa(t5_table, *_t5_bias(t5_table), iq, dq, ikw, ikb, dk, dvt)
    x2 = _merge(h, o_ret, o_dsa, o_mem, w_in_bf[:, W_G0:], bf(w_br_ret), bf(w_br_dsa), bf(w_br_mem),
                bf(w_out), row(ln2_g), row(ln2_b))
    return _ffn_ln(x2, bf(ffn2_w_in), bf(ffn2_w_out), row(ln3_g), row(ln3_b), "ffn2_ln3")


def kernel(x, mem, ffn1_w_in, ffn1_w_out, ln1_g, ln1_b, w_in, t5_table, ret_gn_g, ret_gn_b,
           w_mem_kv, w_br_ret, w_br_dsa, w_br_mem, w_out, ln2_g, ln2_b,
           ffn2_w_in, ffn2_w_out, ln3_g, ln3_b):
    assert x.shape == (BATCH, SEQ, D_MODEL) and mem.shape == (BATCH, MEM_TOKENS, D_MODEL)
    tables = _rope_tables() + _decay_tables()
    y = x.reshape(N_TOK, D_MODEL)
    mem2d = mem.reshape(BATCH * MEM_TOKENS, D_MODEL)
    for l in range(DEPTH):
        y = _layer(y, mem2d, ffn1_w_in[l], ffn1_w_out[l], ln1_g[l], ln1_b[l], w_in[l], t5_table,
                   ret_gn_g[l], ret_gn_b[l], w_mem_kv[l], w_br_ret[l], w_br_dsa[l], w_br_mem[l],
                   w_out[l], ln2_g[l], ln2_b[l], ffn2_w_in[l], ffn2_w_out[l], ln3_g[l], ln3_b[l],
                   tables)
    return y.reshape(BATCH, SEQ, D_MODEL)
```

```python
import math

import jax
import jax.numpy as jnp
from jax import lax
from jax.experimental import pallas as pl
from jax.experimental.pallas import tpu as pltpu

F32 = jnp.float32
BF16 = jnp.bfloat16

D_MODEL = 1024
BATCH = 8
SEQ = 2048
MEM_TOKENS = 256
RET_HEADS, RET_DK, RET_DV, RET_CHUNK = 4, 64, 128, 128
DSA_HEADS, DSA_DH = 8, 64
IDX_HEADS, IDX_DIM = 8, 64
TOPK = min(256, SEQ // 4)
MEM_HEADS, MEM_DH = 4, 128
T5_BUCKETS, T5_MAX_DIST = 32, 128
D_FF = 2816
ROPE_BASE = 10000.0
LN_EPS = 1e-5
NEG_INF = -1e30
DEPTH = 1
ALPHA = (2.0 * DEPTH) ** 0.25

RET_QK_W = RET_HEADS * RET_DK
RET_V_W = RET_HEADS * RET_DV
DSA_W = DSA_HEADS * DSA_DH
IDX_Q_W = IDX_HEADS * IDX_DIM
MEM_W = MEM_HEADS * MEM_DH
N_TOK = BATCH * SEQ

V7X_VMEM_BYTES = 64 * 1024 * 1024
VMEM_LIMIT = V7X_VMEM_BYTES - 8 * 1024 * 1024
LANES = 128
SUBLANES = 8

FFN_ROWS = 1024
MXU_TILE = 256
FFN_CHUNK_EDGES = (0, 6 * MXU_TILE, D_FF)
PROJ_ROWS = 1024
MERGE_ROWS = 1024
RET_ROWS = 1024
MEMATT_ROWS = 2048
DSA_T = 256
HEAD_PAD = LANES
DVT_HEAD_ROWS = DSA_DH + 16
DVT_ROWS = DSA_HEADS * DVT_HEAD_ROWS
BISECT_PLAIN_STEPS = 16
COUNT_ACCUMULATORS = 4
LOG2E = math.log2(math.e)
DEN_FLOOR = 2.0 ** -100

W_RQ0 = 0
W_RK0 = W_RQ0 + RET_QK_W
W_RV0 = W_RK0 + RET_QK_W
W_RG0 = W_RV0 + RET_V_W
W_DQ0 = W_RG0 + RET_V_W
W_DK0 = W_DQ0 + DSA_W
W_DV0 = W_DK0 + DSA_W
W_IQ0 = W_DV0 + DSA_W
W_IK0 = W_IQ0 + IDX_Q_W
W_IW0 = W_IK0 + IDX_DIM
W_MQ0 = W_IW0 + IDX_HEADS
W_G0 = W_MQ0 + MEM_W
IW_SCALE = IDX_HEADS ** -0.5 * IDX_DIM ** -0.5


def _cparams(sem):
    return pltpu.CompilerParams(dimension_semantics=sem, vmem_limit_bytes=VMEM_LIMIT)


def _const_spec(shape):
    nd = len(shape)
    return pl.BlockSpec(shape, lambda *_: (0,) * nd, pipeline_mode=pl.Buffered(1))


def _layer_norm(y, g, b):
    mu = jnp.mean(y, axis=-1, keepdims=True)
    yc = y - mu
    var = jnp.mean(yc * yc, axis=-1, keepdims=True)
    return yc * lax.rsqrt(var + LN_EPS) * g + b


def _sigmoid(x):
    return 0.5 * jnp.tanh(0.5 * x) + 0.5


def _dot(a, b):
    return jnp.dot(a, b, preferred_element_type=F32)


def _dot_nt(a, b):
    return lax.dot_general(a, b, (((1,), (1,)), ((), ())), preferred_element_type=F32)


def _ffn_ln_kernel(x_ref, wi_ref, wo_ref, g_ref, b_ref, o_ref):
    x = x_ref[...]
    xb = x.astype(BF16)
    acc = None
    for lo, hi in zip(FFN_CHUNK_EDGES[:-1], FFN_CHUNK_EDGES[1:]):
        a = _dot(xb, wi_ref[:, lo:hi])
        u = _dot(xb, wi_ref[:, D_FF + lo:D_FF + hi])
        act = (a * _sigmoid(a) * u).astype(BF16)
        part = _dot(act, wo_ref[lo:hi, :])
        acc = part if acc is None else acc + part
    o_ref[...] = _layer_norm(ALPHA * x + 0.5 * acc, g_ref[...], b_ref[...])


def _ffn_ln(x, w_in_bf, w_out_bf, g, b, name):
    return pl.pallas_call(
        _ffn_ln_kernel,
        out_shape=jax.ShapeDtypeStruct((N_TOK, D_MODEL), F32),
        grid=(N_TOK // FFN_ROWS,),
        in_specs=[
            pl.BlockSpec((FFN_ROWS, D_MODEL), lambda i: (i, 0)),
            _const_spec((D_MODEL, 2 * D_FF)),
            _const_spec((D_FF, D_MODEL)),
            _const_spec((1, D_MODEL)),
            _const_spec((1, D_MODEL)),
        ],
        out_specs=pl.BlockSpec((FFN_ROWS, D_MODEL), lambda i: (i, 0)),
        compiler_params=_cparams(("parallel",)),
        name=name,
    )(x, w_in_bf, w_out_bf, g, b)


def _rope(x, cos, sin_signed):
    width = x.shape[-1]
    lane = lax.broadcasted_iota(jnp.int32, x.shape, 1)
    first_half = (lane % RET_DK) < (RET_DK // 2)
    swapped = jnp.where(first_half,
                        pltpu.roll(x, width - RET_DK // 2, 1),
                        pltpu.roll(x, RET_DK // 2, 1))
    return x * cos + swapped * sin_signed


def _store_split_heads(dst_ref, val):
    low = lax.broadcasted_iota(jnp.int32, (val.shape[0], LANES), 1) < DSA_DH
    for j in range(val.shape[1] // LANES):
        pair = val[:, j * LANES:(j + 1) * LANES]
        dst_ref[:, (2 * j) * LANES:(2 * j + 1) * LANES] = jnp.where(low, pair, 0.0).astype(dst_ref.dtype)
        dst_ref[:, (2 * j + 1) * LANES:(2 * j + 2) * LANES] = jnp.where(low, 0.0, pair).astype(dst_ref.dtype)


def _store_low_heads(dst_ref, val):
    low = lax.broadcasted_iota(jnp.int32, (val.shape[0], LANES), 1) < RET_DK
    for j in range(val.shape[1] // LANES):
        pair = val[:, j * LANES:(j + 1) * LANES]
        dst_ref[:, (2 * j) * LANES:(2 * j + 1) * LANES] = jnp.where(low, pair, 0.0).astype(dst_ref.dtype)
        dst_ref[:, (2 * j + 1) * LANES:(2 * j + 2) * LANES] = jnp.where(
            low, pltpu.roll(pair, RET_DK, 1), 0.0).astype(dst_ref.dtype)


def _proj_kernel(h_ref, w_ref, wmq_ref, wt_ref, cos_ref, sin_ref, kdec_ref,
                 rq_ref, rk_ref, rkdt_ref, rv_ref, rg_ref, dq_ref, dk_ref, dvt_ref, iq_ref, mq_ref,
                 ikwt_ref, ikb_ref):
    hb = h_ref[...].astype(BF16)

    def proj(lo, width):
        return _dot(hb, w_ref[:, lo:lo + width])

    cos = cos_ref[...]
    sin = sin_ref[...]
    _store_low_heads(rq_ref, _rope(proj(W_RQ0, RET_QK_W), cos, sin))
    rk = _rope(proj(W_RK0, RET_QK_W), cos, sin) * (RET_DK ** -0.5)
    _store_low_heads(rk_ref, rk)
    rkdt_ref[...] = (rk * kdec_ref[...]).T.astype(BF16)
    rv_ref[...] = proj(W_RV0, RET_V_W).astype(BF16)
    rg_ref[...] = proj(W_RG0, RET_V_W)
    dq_ref[...] = (proj(W_DQ0, DSA_W) * (DSA_DH ** -0.5 * LOG2E)).T.astype(BF16)
    _store_split_heads(dk_ref, proj(W_DK0, DSA_W))
    iq_ref[...] = proj(W_IQ0, IDX_Q_W).T.astype(BF16)
    mq_ref[...] = _dot(hb, wmq_ref[...]).astype(BF16)
    t = _dot_nt(wt_ref[...], hb)
    ikw_t = t[DSA_W:DSA_W + LANES]
    ikwt_ref[...] = ikw_t
    ikw = ikw_t.T
    low = lax.broadcasted_iota(jnp.int32, ikw.shape, 1) < IDX_DIM
    ik_low = jnp.where(low, ikw, 0.0)
    ikb_ref[:, 0:LANES] = ik_low.astype(BF16)
    ikb_ref[:, LANES:2 * LANES] = pltpu.roll(ik_low, IDX_DIM, 1).astype(BF16)
    vt = t[:DSA_W].astype(BF16)
    ones = jnp.ones((DVT_HEAD_ROWS - DSA_DH, DSA_T), BF16)
    for j in range(PROJ_ROWS // DSA_T):
        for h in range(DSA_HEADS):
            r0 = h * DVT_HEAD_ROWS
            dvt_ref[j, r0:r0 + DSA_DH, :] = vt[h * DSA_DH:(h + 1) * DSA_DH, j * DSA_T:(j + 1) * DSA_T]
            dvt_ref[j, r0 + DSA_DH:r0 + DVT_HEAD_ROWS, :] = ones


def _proj(h, w_in_bf, w_mq_bf, w_t_bf, cos_t, sin_t, kdec_rows):
    rows = lambda w: pl.BlockSpec((PROJ_ROWS, w), lambda i: (i, 0))
    seq_tiles = SEQ // PROJ_ROWS
    pos = lambda w: pl.BlockSpec((PROJ_ROWS, w), lambda i: (i % seq_tiles, 0))
    sd = jax.ShapeDtypeStruct
    ret_padded = RET_HEADS * HEAD_PAD
    dsa_padded = DSA_HEADS * HEAD_PAD
    slabs = PROJ_ROWS // DSA_T
    assert W_IK0 % LANES == 0
    return pl.pallas_call(
        _proj_kernel,
        out_shape=(
            sd((N_TOK, ret_padded), BF16), sd((N_TOK, ret_padded), BF16), sd((RET_QK_W, N_TOK), BF16),
            sd((N_TOK, RET_V_W), BF16), sd((N_TOK, RET_V_W), F32),
            sd((DSA_W, N_TOK), BF16), sd((N_TOK, dsa_padded), BF16),
            sd((N_TOK // DSA_T, DVT_ROWS, DSA_T), BF16),
            sd((IDX_Q_W, N_TOK), BF16), sd((N_TOK, MEM_W), BF16),
            sd((LANES, N_TOK), F32), sd((N_TOK, 2 * LANES), BF16),
        ),
        grid=(N_TOK // PROJ_ROWS,),
        in_specs=[rows(D_MODEL),
                  pl.BlockSpec((D_MODEL, W_IK0), lambda i: (0, 0), pipeline_mode=pl.Buffered(1)),
                  _const_spec((D_MODEL, MEM_W)), _const_spec((DSA_W + LANES, D_MODEL)),
                  pos(RET_QK_W), pos(RET_QK_W), _const_spec((PROJ_ROWS, RET_QK_W))],
        out_specs=(rows(ret_padded), rows(ret_padded),
                   pl.BlockSpec((RET_QK_W, PROJ_ROWS), lambda i: (0, i)),
                   rows(RET_V_W), rows(RET_V_W),
                   pl.BlockSpec((DSA_W, PROJ_ROWS), lambda i: (0, i)), rows(dsa_padded),
                   pl.BlockSpec((slabs, DVT_ROWS, DSA_T), lambda i: (i, 0, 0)),
                   pl.BlockSpec((IDX_Q_W, PROJ_ROWS), lambda i: (0, i)),
                   rows(MEM_W), pl.BlockSpec((LANES, PROJ_ROWS), lambda i: (0, i)), rows(2 * LANES)),
        compiler_params=_cparams(("parallel",)),
        name="mixer_proj",
    )(h, w_in_bf, w_mq_bf, w_t_bf, cos_t, sin_t, kdec_rows)


def _retention_kernel(q_ref, k_ref, kdt_ref, v_ref, g_ref, din_ref, qdec_ref, cdec_ref,
                      gng_ref, gnb_ref, o_ref, state_ref):
    @pl.when(pl.program_id(1) == 0)
    def _():
        state_ref[...] = jnp.zeros_like(state_ref)

    qdec = qdec_ref[...]
    gn_g = gng_ref[...]
    gn_b = gnb_ref[...]
    states = [state_ref[h] for h in range(RET_HEADS)]
    pad_rows = jnp.zeros((HEAD_PAD - RET_DK, RET_DV), BF16)
    for j in range(RET_ROWS // RET_CHUNK):
        rows = slice(j * RET_CHUNK, (j + 1) * RET_CHUNK)
        gate = g_ref[rows, :]
        for h in range(RET_HEADS):
            hs = slice(h * HEAD_PAD, (h + 1) * HEAD_PAD)
            vs = slice(h * RET_DV, (h + 1) * RET_DV)
            qh = q_ref[rows, hs]
            vh = v_ref[rows, vs]
            scores = _dot_nt(qh, k_ref[rows, hs]) * din_ref[h]
            intra = _dot(scores.astype(BF16), vh)
            state_padded = jnp.concatenate([states[h].astype(BF16), pad_rows], axis=0)
            cross = _dot(qh, state_padded) * qdec[:, vs]
            kv = _dot(kdt_ref[h * RET_DK:(h + 1) * RET_DK, rows], vh)
            states[h] = cdec_ref[h] * states[h] + kv
            o = intra + cross
            mu = jnp.mean(o, axis=-1, keepdims=True)
            oc = o - mu
            var = jnp.mean(oc * oc, axis=-1, keepdims=True)
            o = oc * lax.rsqrt(var + LN_EPS) * gn_g[:, vs] + gn_b[:, vs]
            o_ref[rows, vs] = (jax.nn.silu(gate[:, vs]) * o).astype(BF16)
    for h in range(RET_HEADS):
        state_ref[h] = states[h]


def _retention(rq, rk, rkdt, rv, rg, din, qdec, cdec, gn_g, gn_b):
    nc = SEQ // RET_ROWS
    rows = lambda w: pl.BlockSpec((RET_ROWS, w), lambda b, n: (b * nc + n, 0))
    padded = RET_HEADS * HEAD_PAD
    return pl.pallas_call(
        _retention_kernel,
        out_shape=jax.ShapeDtypeStruct((N_TOK, RET_V_W), BF16),
        grid=(BATCH, nc),
        in_specs=[rows(padded), rows(padded),
                  pl.BlockSpec((RET_QK_W, RET_ROWS), lambda b, n: (0, b * nc + n)),
                  rows(RET_V_W), rows(RET_V_W),
                  _const_spec((RET_HEADS, RET_CHUNK, RET_CHUNK)),
                  _const_spec((RET_CHUNK, RET_V_W)),
                  _const_spec((RET_HEADS, RET_DK, RET_DV)),
                  _const_spec((1, RET_V_W)), _const_spec((1, RET_V_W))],
        out_specs=rows(RET_V_W),
        scratch_shapes=[pltpu.VMEM((RET_HEADS, RET_DK, RET_DV), F32)],
        compiler_params=_cparams(("parallel", "arbitrary")),
        name="retention",
    )(rq, rk, rkdt, rv, rg, din, qdec, cdec, gn_g, gn_b)


def _mematt_kernel(q_ref, mem_ref, w_ref, o_ref, kv_ref):
    @pl.when(pl.program_id(1) == 0)
    def _():
        kv_ref[...] = _dot(mem_ref[...].astype(BF16), w_ref[...]).astype(BF16)

    q = q_ref[...]
    for h in range(MEM_HEADS):
        hs = slice(h * MEM_DH, (h + 1) * MEM_DH)
        logits = _dot_nt(q[:, hs], kv_ref[:, hs]) * (MEM_DH ** -0.5)
        m = jnp.max(logits, axis=-1, keepdims=True)
        p = jnp.exp(logits - m)
        denom = jnp.sum(p, axis=-1, keepdims=True)
        o_ref[:, hs] = (_dot(p.astype(BF16), kv_ref[:, MEM_W + h * MEM_DH:MEM_W + (h + 1) * MEM_DH])
                        / denom).astype(BF16)


def _mematt(mq, mem2d, w_kv_bf):
    tiles = SEQ // MEMATT_ROWS
    return pl.pallas_call(
        _mematt_kernel,
        out_shape=jax.ShapeDtypeStruct((N_TOK, MEM_W), BF16),
        grid=(BATCH, tiles),
        in_specs=[pl.BlockSpec((MEMATT_ROWS, MEM_W), lambda b, i: (b * tiles + i, 0)),
                  pl.BlockSpec((MEM_TOKENS, D_MODEL), lambda b, i: (b, 0)),
                  _const_spec((D_MODEL, 2 * MEM_W))],
        out_specs=pl.BlockSpec((MEMATT_ROWS, MEM_W), lambda b, i: (b * tiles + i, 0)),
        scratch_shapes=[pltpu.VMEM((MEM_TOKENS, 2 * MEM_W), BF16)],
        compiler_params=_cparams(("parallel", "arbitrary")),
        name="mem_attention",
    )(mq, mem2d, w_kv_bf)


def _t5_bias_kernel(table_ref, o_ref, bmax_ref):
    m = lax.broadcasted_iota(jnp.int32, (SUBLANES, 2 * DSA_T), 1)
    max_exact = T5_BUCKETS // 2
    head_max = [None] * DSA_HEADS
    for w in range(2):
        n = jnp.maximum(m - w * DSA_T, 0)
        nf = jnp.maximum(n, 1).astype(F32)
        large = max_exact + jnp.floor(jnp.log(nf / max_exact) / math.log(T5_MAX_DIST / max_exact)
                                      * (T5_BUCKETS - max_exact)).astype(jnp.int32)
        large = jnp.minimum(large, T5_BUCKETS - 1)
        bucket = jnp.where(n < max_exact, n, large)
        for h in range(DSA_HEADS):
            vec = jnp.zeros((SUBLANES, 2 * DSA_T), F32)
            for b in range(T5_BUCKETS):
                vec = jnp.where(bucket == b, table_ref[b, h] * LOG2E, vec)
            rows = jnp.broadcast_to(vec[0:1, :], (DSA_T, 2 * DSA_T))
            o_ref[w, h] = pltpu.roll(rows, 0, 1, stride=1, stride_axis=0)[:, DSA_T:]
            top = jnp.max(vec[0:1, :], axis=1, keepdims=True)
            head_max[h] = top if head_max[h] is None else jnp.maximum(head_max[h], top)
    for h in range(DSA_HEADS):
        bmax_ref[h:h + 1, :] = jnp.broadcast_to(head_max[h], (1, LANES))


def _t5_bias(t5_table):
    return pl.pallas_call(
        _t5_bias_kernel,
        out_shape=(jax.ShapeDtypeStruct((2, DSA_HEADS, DSA_T, DSA_T), F32),
                   jax.ShapeDtypeStruct((DSA_HEADS, LANES), F32)),
        in_specs=[pl.BlockSpec(memory_space=pltpu.SMEM)],
        out_specs=(pl.BlockSpec(memory_space=pltpu.VMEM), pl.BlockSpec(memory_space=pltpu.VMEM)),
        compiler_params=pltpu.CompilerParams(vmem_limit_bytes=VMEM_LIMIT),
        name="t5_bias",
    )(t5_table)


def _fold_groups(x, op):
    return op(x.reshape(DSA_T // SUBLANES, SUBLANES, DSA_T), axis=0)


def _dsa_kernel(table_ref, bias_ref, bmax_ref, iq_ref, dq_ref, ikw_q_ref, ikb_ref, dk_ref, dvt_ref,
                o_ref, score_ref, logit_ref, acc_ref):
    qb = pl.program_id(1)
    q0 = qb * DSA_T
    n_chunks = qb + 1
    groups = DSA_T // SUBLANES
    key_in_chunk = lax.broadcasted_iota(jnp.int32, (DSA_T, DSA_T), 0)
    query_in_step = lax.broadcasted_iota(jnp.int32, (DSA_T, DSA_T), 1)

    def key_rows(c):
        return pl.ds(pl.multiple_of(c * DSA_T, DSA_T), DSA_T)

    def pair_rows(j):
        return slice(j * LANES, (j + 1) * LANES)

    def pair_products(keys_even, keys_odd, queries_t):
        prod = _dot(jnp.concatenate([keys_even, keys_odd], axis=0), queries_t)
        return prod[:DSA_T], prod[DSA_T:]

    iw_t = ikw_q_ref[IDX_DIM:IDX_DIM + IDX_HEADS, :] * IW_SCALE

    def scores_and_logits(c, carry, causal):
        raw_max, s_min, s_max = carry
        ik_low = ikb_ref[key_rows(c), 0:LANES]
        ik_high = ikb_ref[key_rows(c), LANES:2 * LANES]
        last = IDX_HEADS // 2 - 1
        for j in range(IDX_HEADS // 2):
            rel_even, rel_odd = pair_products(ik_low, ik_high, iq_ref[pair_rows(j), :])
            term = (jnp.maximum(rel_even, 0.0) * iw_t[2 * j:2 * j + 1, :]
                    + jnp.maximum(rel_odd, 0.0) * iw_t[2 * j + 1:2 * j + 2, :])
            total = term if j == 0 else score_ref[c] + term
            if j < last:
                score_ref[c] = total
            elif causal is None:
                score_ref[c] = total
                s_min = jnp.minimum(s_min, _fold_groups(total, jnp.min))
                s_max = jnp.maximum(s_max, _fold_groups(total, jnp.max))
            else:
                score_ref[c] = jnp.where(causal, total, NEG_INF)
                s_min = jnp.minimum(s_min, _fold_groups(jnp.where(causal, total, -NEG_INF), jnp.min))
                s_max = jnp.maximum(s_max, _fold_groups(jnp.where(causal, total, NEG_INF), jnp.max))
        out = list(raw_max)
        for j in range(DSA_HEADS // 2):
            keys = [dk_ref[key_rows(c), (2 * j + i) * HEAD_PAD:(2 * j + i + 1) * HEAD_PAD] for i in range(2)]
            for h, qk in zip((2 * j, 2 * j + 1), pair_products(keys[0], keys[1], dq_ref[pair_rows(j), :])):
                logit_ref[h, c] = qk
                out[h] = jnp.maximum(out[h], _fold_groups(qk, jnp.max))
        return tuple(out), s_min, s_max

    def paired_loop(start, stop, body, carry):
        n = stop - start
        carry = lax.fori_loop(
            0, lax.shift_right_logical(n, 1),
            lambda i, cr: body(start + 2 * i + 1, body(start + 2 * i, cr)), carry)
        return lax.cond(jnp.bitwise_and(n, 1) == 1, lambda cr: body(stop - 1, cr), lambda cr: cr, carry)

    low_init = jnp.full((SUBLANES, DSA_T), NEG_INF, F32)
    carry = (tuple(low_init for _ in range(DSA_HEADS)), -low_init, low_init)
    carry = paired_loop(0, qb, lambda c, cr: scores_and_logits(c, cr, None), carry)
    raw_max, s_min, s_max = scores_and_logits(qb, carry, key_in_chunk <= query_in_step)

    def count_where(pred):
        def body(c, acc):
            parts = [acc] + [jnp.zeros((SUBLANES, DSA_T), F32)] * (COUNT_ACCUMULATORS - 1)
            for g in range(groups):
                blk = score_ref[c, g * SUBLANES:(g + 1) * SUBLANES, :]
                a = g % COUNT_ACCUMULATORS
                parts[a] = parts[a] + jnp.where(pred(c, g * SUBLANES, blk), 1.0, 0.0)
            while len(parts) > 1:
                parts = [parts[i] + parts[i + 1] for i in range(0, len(parts), 2)]
            return parts[0]
        acc = paired_loop(0, n_chunks, body, jnp.zeros((SUBLANES, DSA_T), F32))
        return jnp.sum(acc, axis=0, keepdims=True)

    def count_ge(thr):
        thr_b = jnp.broadcast_to(thr, (SUBLANES, DSA_T))
        return count_where(lambda c, r0, blk: blk >= thr_b)

    q_min = jnp.min(s_min, axis=0, keepdims=True)
    q_max = jnp.max(s_max, axis=0, keepdims=True)
    t_q = q0 + lax.broadcasted_iota(jnp.int32, (1, DSA_T), 1)

    kf = float(TOPK)
    init = (q_min, q_max + jnp.maximum(jnp.abs(q_max), 1.0) * 1e-6,
            (t_q + 1).astype(F32), jnp.zeros((1, DSA_T), F32))

    def bisect_step(_i, carry):
        lo, hi, cnt_lo, cnt_hi = carry
        mid = 0.5 * (lo + hi)
        cnt = count_ge(mid)
        ge = cnt >= kf
        return (jnp.where(ge, mid, lo), jnp.where(ge, hi, mid),
                jnp.where(ge, cnt, cnt_lo), jnp.where(ge, cnt_hi, cnt))

    n_plain = jnp.where(qb == 0, 0, BISECT_PLAIN_STEPS)
    lo, hi, cnt_lo, cnt_hi = lax.fori_loop(0, n_plain, bisect_step, init)

    big = -NEG_INF

    def snap_body(c, carry):
        mn, mx = carry
        blk = score_ref[c]
        mn = jnp.minimum(mn, _fold_groups(jnp.where(blk >= lo, blk, big), jnp.min))
        mx = jnp.maximum(mx, _fold_groups(jnp.where(blk < hi, blk, -big), jnp.max))
        return mn, mx

    mn, mx = paired_loop(
        0, n_chunks, snap_body,
        (jnp.full((SUBLANES, DSA_T), big, F32), jnp.full((SUBLANES, DSA_T), -big, F32)))
    lo = jnp.min(mn, axis=0, keepdims=True)
    top = jnp.max(mx, axis=0, keepdims=True)

    def split_pair(carry):
        lo, top, cnt_lo, cnt_hi = carry
        pair = (cnt_lo > kf) & (lo < top) & (cnt_lo - cnt_hi == 2.0)
        return jnp.where(pair, top, lo), top, jnp.where(pair, cnt_hi + 1.0, cnt_lo), cnt_hi

    def pending_of(carry):
        lo, top, cnt_lo, _ = carry
        over = cnt_lo > kf
        return jnp.max(jnp.where(over & (lo < top), 2.0, jnp.where(over, 1.0, 0.0)))

    def value_step(state):
        (lo, top, cnt_lo, cnt_hi), _ = state
        unresolved = (cnt_lo > kf) & (lo < top)
        mid = lo + (top - lo) * 0.5
        mid = jnp.minimum(jnp.where(mid > lo, mid, top), top)

        def body(c, carry):
            cnt, mn, mx = carry
            blk = score_ref[c]
            is_ge = blk >= mid
            cnt = cnt + _fold_groups(jnp.where(is_ge, 1.0, 0.0), jnp.sum)
            mn = jnp.minimum(mn, _fold_groups(jnp.where(is_ge, blk, big), jnp.min))
            mx = jnp.maximum(mx, _fold_groups(jnp.where(is_ge, -big, blk), jnp.max))
            return cnt, mn, mx

        cnt, mn, mx = lax.fori_loop(
            0, n_chunks, body,
            (jnp.zeros((SUBLANES, DSA_T), F32), jnp.full((SUBLANES, DSA_T), big, F32),
             jnp.full((SUBLANES, DSA_T), -big, F32)))
        cnt = jnp.sum(cnt, axis=0, keepdims=True)
        ge = cnt >= kf
        raise_lo = unresolved & ge
        lower_top = unresolved & jnp.logical_not(ge)
        carry = split_pair((jnp.where(raise_lo, jnp.min(mn, axis=0, keepdims=True), lo),
                            jnp.where(lower_top, jnp.max(mx, axis=0, keepdims=True), top),
                            jnp.where(raise_lo, cnt, cnt_lo), jnp.where(lower_top, cnt, cnt_hi)))
        return carry, pending_of(carry)

    start = split_pair((lo, top, cnt_lo, cnt_hi))
    (lo, top, cnt_lo, cnt_hi), status = lax.while_loop(
        lambda state: state[1] > 1.5, value_step, (start, pending_of(start)))

    tied = cnt_lo > kf
    need = kf - cnt_hi
    lo_b = jnp.broadcast_to(lo, (SUBLANES, DSA_T))
    key_in_group = lax.broadcasted_iota(jnp.int32, (SUBLANES, DSA_T), 0)

    def tie_search(_):
        def body(_i, carry):
            lo_i, hi_i = carry
            mid_i = lax.shift_right_arithmetic(lo_i + hi_i, 1)
            mid_b = jnp.broadcast_to(mid_i, (SUBLANES, DSA_T))
            cnt = count_where(lambda c, r0, blk: (blk == lo_b) & (key_in_group + (c * DSA_T + r0) <= mid_b))
            enough = cnt >= need
            return jnp.where(enough, lo_i, mid_i), jnp.where(enough, mid_i, hi_i)
        lo_i = jnp.full((1, DSA_T), -1, jnp.int32)
        hi_i = jnp.full((1, DSA_T), SEQ - 1, jnp.int32)
        _, cut = lax.fori_loop(0, int(math.log2(SEQ)) + 1, body, (lo_i, hi_i))
        return jnp.where(tied, cut, SEQ)

    cut = lax.cond(status > 0.5, tie_search, lambda _: jnp.full((1, DSA_T), SEQ, jnp.int32), 0)

    def write_mask(c, ties):
        blk = score_ref[c]
        if ties:
            keep = (blk > lo) | ((blk == lo) & (key_in_chunk + c * DSA_T <= cut))
        else:
            keep = blk >= lo
        score_ref[c] = jnp.where(keep, 0.0, NEG_INF)

    n_far = jnp.maximum(qb - 1, 0)

    def masked_logits(c, h, far):
        lg = logit_ref[h, c] + score_ref[c]
        return lg if far else lg + bias_ref[c - qb + 1, h]

    def far_bias(h):
        return table_ref[T5_BUCKETS - 1, h] * LOG2E

    def den_row(h):
        return h * DVT_HEAD_ROWS + DSA_DH

    def exp_and_pv(shifts, scores_to_masks, ties=False):
        acc_ref[...] = jnp.zeros_like(acc_ref)

        def body(c, carry, far):
            if scores_to_masks:
                write_mask(c, ties)
            for h in range(DSA_HEADS):
                shift = shifts[h] - far_bias(h) if far else shifts[h]
                p = jnp.exp2(masked_logits(c, h, far) - shift).astype(BF16)
                rows = slice(h * DVT_HEAD_ROWS, (h + 1) * DVT_HEAD_ROWS)
                acc_ref[rows, :] += _dot(dvt_ref[c, rows, :], p)
            return carry

        paired_loop(0, n_far, lambda c, carry: body(c, carry, True), 0)
        paired_loop(n_far, n_chunks, lambda c, carry: body(c, carry, False), 0)

    upper = [jnp.max(raw_max[h], axis=0, keepdims=True) + bmax_ref[h:h + 1, 0:1] for h in range(DSA_HEADS)]
    @pl.when(status > 0.5)
    def _():
        exp_and_pv(upper, True, ties=True)

    @pl.when(status <= 0.5)
    def _():
        exp_and_pv(upper, True, ties=False)

    den_min = acc_ref[den_row(0):den_row(0) + 1, :]
    for h in range(1, DSA_HEADS):
        den_min = jnp.minimum(den_min, acc_ref[den_row(h):den_row(h) + 1, :])

    @pl.when(jnp.min(den_min) < DEN_FLOOR)
    def _():
        def max_body(c, ms, far):
            return tuple(jnp.maximum(ms[h], _fold_groups(masked_logits(c, h, far), jnp.max) + (far_bias(h) if far else 0.0))
                         for h in range(DSA_HEADS))
        ms = tuple(jnp.full((SUBLANES, DSA_T), NEG_INF, F32) for _ in range(DSA_HEADS))
        ms = lax.fori_loop(0, n_far, lambda c, ms: max_body(c, ms, True), ms)
        ms = lax.fori_loop(n_far, n_chunks, lambda c, ms: max_body(c, ms, False), ms)
        exp_and_pv([jnp.max(m, axis=0, keepdims=True) for m in ms], False)

    heads = []
    for h in range(DSA_HEADS):
        r0 = h * DVT_HEAD_ROWS
        heads.append(acc_ref[r0:r0 + DSA_DH, :] / acc_ref[den_row(h):den_row(h) + 1, :])
    o_ref[...] = jnp.concatenate(heads, axis=0).T.astype(BF16)


def _dsa(t5_table, bias, bias_max, iq, dq, ikw, ikb, dk, dvt):
    nqb = SEQ // DSA_T
    padded = DSA_HEADS * HEAD_PAD
    assert DSA_T <= TOPK
    qrows = lambda w: pl.BlockSpec((DSA_T, w), lambda b, i: (b * nqb + i, 0))
    qcols = lambda h: pl.BlockSpec((h, DSA_T), lambda b, i: (0, b * nqb + i))
    krows = lambda w: pl.BlockSpec((SEQ, w), lambda b, i: (b, 0))
    return pl.pallas_call(
        _dsa_kernel,
        out_shape=jax.ShapeDtypeStruct((N_TOK, DSA_W), BF16),
        grid=(BATCH, nqb),
        in_specs=[pl.BlockSpec(memory_space=pltpu.SMEM),
                  _const_spec((2, DSA_HEADS, DSA_T, DSA_T)), _const_spec((DSA_HEADS, LANES)),
                  qcols(IDX_Q_W), qcols(DSA_W), qcols(LANES),
                  krows(2 * LANES), krows(padded),
                  pl.BlockSpec((nqb, DVT_ROWS, DSA_T), lambda b, i: (b, 0, 0))],
        out_specs=qrows(DSA_W),
        scratch_shapes=[pltpu.VMEM((nqb, DSA_T, DSA_T), F32),
                        pltpu.VMEM((DSA_HEADS, nqb, DSA_T, DSA_T), F32),
                        pltpu.VMEM((DVT_ROWS, DSA_T), F32)],
        compiler_params=_cparams(("parallel", "arbitrary")),
        name="dsa_attention",
    )(t5_table, bias, bias_max, iq, dq, ikw, ikb, dk, dvt)


def _merge_kernel(h_ref, oret_ref, odsa_ref, omem_ref, wg_ref, wr_ref, wd_ref, wm_ref, wo_ref,
                  g_ref, b_ref, o_ref):
    h = h_ref[...]
    hb = h.astype(BF16)
    merged = None
    for i, (src, w) in enumerate(((oret_ref, wr_ref), (odsa_ref, wd_ref), (omem_ref, wm_ref))):
        gate = _sigmoid(_dot(hb, wg_ref[:, i * D_MODEL:(i + 1) * D_MODEL]))
        term = gate * _dot(src[...], w[...])
        merged = term if merged is None else merged + term
    mix = _dot(merged.astype(BF16), wo_ref[...])
    o_ref[...] = _layer_norm(ALPHA * h + mix, g_ref[...], b_ref[...])


def _merge(h, o_ret, o_dsa, o_mem, wg, wr, wd, wm, wo, g, b):
    rows = lambda w: pl.BlockSpec((MERGE_ROWS, w), lambda i: (i, 0))
    return pl.pallas_call(
        _merge_kernel,
        out_shape=jax.ShapeDtypeStruct((N_TOK, D_MODEL), F32),
        grid=(N_TOK // MERGE_ROWS,),
        in_specs=[rows(D_MODEL), rows(RET_V_W), rows(DSA_W), rows(MEM_W),
                  _const_spec((D_MODEL, 3 * D_MODEL)),
                  _const_spec((RET_V_W, D_MODEL)), _const_spec((DSA_W, D_MODEL)),
                  _const_spec((MEM_W, D_MODEL)), _const_spec((D_MODEL, D_MODEL)),
                  _const_spec((1, D_MODEL)), _const_spec((1, D_MODEL))],
        out_specs=rows(D_MODEL),
        compiler_params=_cparams(("parallel",)),
        name="merge_ln2",
    )(h, o_ret, o_dsa, o_mem, wg, wr, wd, wm, wo, g, b)


def _rope_tables():
    half = RET_DK // 2
    freqs = ROPE_BASE ** (-jnp.arange(half, dtype=F32) / half)
    ang = jnp.arange(SEQ).astype(F32)[:, None] * freqs[None, :]
    cos, sin = jnp.cos(ang), jnp.sin(ang)
    cos_t = jnp.tile(jnp.concatenate([cos, cos], axis=-1), (1, RET_HEADS))
    sin_t = jnp.tile(jnp.concatenate([-sin, sin], axis=-1), (1, RET_HEADS))
    return cos_t, sin_t


def _decay_tables():
    c = RET_CHUNK
    gamma = 1.0 - 2.0 ** (-5.0 - jnp.arange(RET_HEADS, dtype=F32))
    lg = jnp.log(gamma)
    i = jnp.arange(c)
    diff = i[:, None] - i[None, :]
    decay_in = jnp.where(diff[None] >= 0, jnp.exp(jnp.maximum(diff, 0)[None] * lg[:, None, None]), 0.0).astype(F32)
    k_dec = jnp.exp((c - 1 - i)[None, :] * lg[:, None]).astype(F32)
    q_dec = jnp.exp((i + 1)[None, :] * lg[:, None]).astype(F32)
    chunk_dec = jnp.exp(c * lg).astype(F32)
    kdec_t = jnp.repeat(k_dec.T, RET_DK, axis=1)
    qdec_t = jnp.repeat(q_dec.T, RET_DV, axis=1)
    cdec_t = jnp.broadcast_to(chunk_dec[:, None, None], (RET_HEADS, RET_DK, RET_DV))
    return decay_in, kdec_t, qdec_t, cdec_t


def _layer(x, mem2d, ffn1_w_in, ffn1_w_out, ln1_g, ln1_b, w_in, t5_table, ret_gn_g, ret_gn_b,
           w_mem_kv, w_br_ret, w_br_dsa, w_br_mem, w_out, ln2_g, ln2_b,
           ffn2_w_in, ffn2_w_out, ln3_g, ln3_b, tables):
    cos_t, sin_t, din, kdec, qdec, cdec = tables
    row = lambda v: v.reshape(1, -1)
    bf = lambda w: w.astype(BF16)

    w_in_bf = bf(w_in)
    w_mq = w_in_bf[:, W_MQ0:W_G0]
    w_t = jnp.concatenate([w_in_bf[:, W_DV0:W_IQ0], w_in_bf[:, W_IK0:W_IK0 + LANES]], axis=1).T
    kdec_rows = jnp.tile(kdec, (PROJ_ROWS // RET_CHUNK, 1))

    h = _ffn_ln(x, bf(ffn1_w_in), bf(ffn1_w_out), row(ln1_g), row(ln1_b), "ffn1_ln1")
    rq, rk, rkdt, rv, rg, dq, dk, dvt, iq, mq, ikw, ikb = _proj(h, w_in_bf, w_mq, w_t, cos_t, sin_t, kdec_rows)
    o_ret = _retention(rq, rk, rkdt, rv, rg, din, qdec, cdec, row(ret_gn_g), row(ret_gn_b))
    o_mem = _mematt(mq, mem2d, bf(w_mem_kv))
    o_dsa = _dsa(t5_table, *_t5_bias(t5_table), iq, dq, ikw, ikb, dk, dvt)
    x2 = _merge(h, o_ret, o_dsa, o_mem, w_in_bf[:, W_G0:], bf(w_br_ret), bf(w_br_dsa), bf(w_br_mem),
                bf(w_out), row(ln2_g), row(ln2_b))
    return _ffn_ln(x2, bf(ffn2_w_in), bf(ffn2_w_out), row(ln3_g), row(ln3_b), "ffn2_ln3")


def kernel(x, mem, ffn1_w_in, ffn1_w_out, ln1_g, ln1_b, w_in, t5_table, ret_gn_g, ret_gn_b,
           w_mem_kv, w_br_ret, w_br_dsa, w_br_mem, w_out, ln2_g, ln2_b,
           ffn2_w_in, ffn2_w_out, ln3_g, ln3_b):
    assert x.shape == (BATCH, SEQ, D_MODEL) and mem.shape == (BATCH, MEM_TOKENS, D_MODEL)
    tables = _rope_tables() + _decay_tables()
    y = x.reshape(N_TOK, D_MODEL)
    mem2d = mem.reshape(BATCH * MEM_TOKENS, D_MODEL)
    for l in range(DEPTH):
        y = _layer(y, mem2d, ffn1_w_in[l], ffn1_w_out[l], ln1_g[l], ln1_b[l], w_in[l], t5_table,
                   ret_gn_g[l], ret_gn_b[l], w_mem_kv[l], w_br_ret[l], w_br_dsa[l], w_br_mem[l],
                   w_out[l], ln2_g[l], ln2_b[l], ffn2_w_in[l], ffn2_w_out[l], ln3_g[l], ln3_b[l],
                   tables)
    return y.reshape(BATCH, SEQ, D_MODEL)
```

```python
import math

import jax
import jax.numpy as jnp
from jax import lax
from jax.experimental import pallas as pl
from jax.experimental.pallas import tpu as pltpu

F32 = jnp.float32
BF16 = jnp.bfloat16

D_MODEL = 1024
BATCH = 8
SEQ = 2048
MEM_TOKENS = 256
RET_HEADS, RET_DK, RET_DV, RET_CHUNK = 4, 64, 128, 128
DSA_HEADS, DSA_DH = 8, 64
IDX_HEADS, IDX_DIM = 8, 64
TOPK = min(256, SEQ // 4)
MEM_HEADS, MEM_DH = 4, 128
T5_BUCKETS, T5_MAX_DIST = 32, 128
D_FF = 2816
ROPE_BASE = 10000.0
LN_EPS = 1e-5
NEG_INF = -1e30
DEPTH = 1
ALPHA = (2.0 * DEPTH) ** 0.25

RET_QK_W = RET_HEADS * RET_DK
RET_V_W = RET_HEADS * RET_DV
DSA_W = DSA_HEADS * DSA_DH
IDX_Q_W = IDX_HEADS * IDX_DIM
MEM_W = MEM_HEADS * MEM_DH
N_TOK = BATCH * SEQ

V7X_VMEM_BYTES = 64 * 1024 * 1024
VMEM_LIMIT = V7X_VMEM_BYTES - 8 * 1024 * 1024
LANES = 128
SUBLANES = 8

FFN_ROWS = 1024
MXU_TILE = 256
FFN_CHUNK_EDGES = (0, 6 * MXU_TILE, D_FF)
PROJ_ROWS = 1024
MERGE_ROWS = 1024
RET_ROWS = 1024
MEMATT_ROWS = 2048
DSA_T = 256
HEAD_PAD = LANES
DVT_HEAD_ROWS = DSA_DH + 16
DVT_ROWS = DSA_HEADS * DVT_HEAD_ROWS
BISECT_PLAIN_STEPS = 16
COUNT_ACCUMULATORS = 4
LOG2E = math.log2(math.e)
DEN_FLOOR = 2.0 ** -100

W_RQ0 = 0
W_RK0 = W_RQ0 + RET_QK_W
W_RV0 = W_RK0 + RET_QK_W
W_RG0 = W_RV0 + RET_V_W
W_DQ0 = W_RG0 + RET_V_W
W_DK0 = W_DQ0 + DSA_W
W_DV0 = W_DK0 + DSA_W
W_IQ0 = W_DV0 + DSA_W
W_IK0 = W_IQ0 + IDX_Q_W
W_IW0 = W_IK0 + IDX_DIM
W_MQ0 = W_IW0 + IDX_HEADS
W_G0 = W_MQ0 + MEM_W
IW_SCALE = IDX_HEADS ** -0.5 * IDX_DIM ** -0.5


def _cparams(sem):
    return pltpu.CompilerParams(dimension_semantics=sem, vmem_limit_bytes=VMEM_LIMIT)


def _const_spec(shape):
    nd = len(shape)
    return pl.BlockSpec(shape, lambda *_: (0,) * nd, pipeline_mode=pl.Buffered(1))


def _layer_norm(y, g, b):
    mu = jnp.mean(y, axis=-1, keepdims=True)
    yc = y - mu
    var = jnp.mean(yc * yc, axis=-1, keepdims=True)
    return yc * lax.rsqrt(var + LN_EPS) * g + b


def _sigmoid(x):
    return 0.5 * jnp.tanh(0.5 * x) + 0.5


def _dot(a, b):
    return jnp.dot(a, b, preferred_element_type=F32)


def _dot_nt(a, b):
    return lax.dot_general(a, b, (((1,), (1,)), ((), ())), preferred_element_type=F32)


def _ffn_ln_kernel(x_ref, wi_ref, wo_ref, g_ref, b_ref, o_ref):
    x = x_ref[...]
    xb = x.astype(BF16)
    acc = None
    for lo, hi in zip(FFN_CHUNK_EDGES[:-1], FFN_CHUNK_EDGES[1:]):
        a = _dot(xb, wi_ref[:, lo:hi])
        u = _dot(xb, wi_ref[:, D_FF + lo:D_FF + hi])
        act = (a * _sigmoid(a) * u).astype(BF16)
        part = _dot(act, wo_ref[lo:hi, :])
        acc = part if acc is None else acc + part
    o_ref[...] = _layer_norm(ALPHA * x + 0.5 * acc, g_ref[...], b_ref[...])


def _ffn_ln(x, w_in_bf, w_out_bf, g, b, name):
    return pl.pallas_call(
        _ffn_ln_kernel,
        out_shape=jax.ShapeDtypeStruct((N_TOK, D_MODEL), F32),
        grid=(N_TOK // FFN_ROWS,),
        in_specs=[
            pl.BlockSpec((FFN_ROWS, D_MODEL), lambda i: (i, 0)),
            _const_spec((D_MODEL, 2 * D_FF)),
            _const_spec((D_FF, D_MODEL)),
            _const_spec((1, D_MODEL)),
            _const_spec((1, D_MODEL)),
        ],
        out_specs=pl.BlockSpec((FFN_ROWS, D_MODEL), lambda i: (i, 0)),
        compiler_params=_cparams(("parallel",)),
        name=name,
    )(x, w_in_bf, w_out_bf, g, b)


def _rope(x, cos, sin_signed):
    width = x.shape[-1]
    lane = lax.broadcasted_iota(jnp.int32, x.shape, 1)
    first_half = (lane % RET_DK) < (RET_DK // 2)
    swapped = jnp.where(first_half,
                        pltpu.roll(x, width - RET_DK // 2, 1),
                        pltpu.roll(x, RET_DK // 2, 1))
    return x * cos + swapped * sin_signed


def _store_split_heads(dst_ref, val):
    low = lax.broadcasted_iota(jnp.int32, (val.shape[0], LANES), 1) < DSA_DH
    for j in range(val.shape[1] // LANES):
        pair = val[:, j * LANES:(j + 1) * LANES]
        dst_ref[:, (2 * j) * LANES:(2 * j + 1) * LANES] = jnp.where(low, pair, 0.0).astype(dst_ref.dtype)
        dst_ref[:, (2 * j + 1) * LANES:(2 * j + 2) * LANES] = jnp.where(low, 0.0, pair).astype(dst_ref.dtype)


def _store_low_heads(dst_ref, val):
    low = lax.broadcasted_iota(jnp.int32, (val.shape[0], LANES), 1) < RET_DK
    for j in range(val.shape[1] // LANES):
        pair = val[:, j * LANES:(j + 1) * LANES]
        dst_ref[:, (2 * j) * LANES:(2 * j + 1) * LANES] = jnp.where(low, pair, 0.0).astype(dst_ref.dtype)
        dst_ref[:, (2 * j + 1) * LANES:(2 * j + 2) * LANES] = jnp.where(
            low, pltpu.roll(pair, RET_DK, 1), 0.0).astype(dst_ref.dtype)


def _proj_kernel(h_ref, w_ref, wmq_ref, wt_ref, cos_ref, sin_ref, kdec_ref,
                 rq_ref, rk_ref, rkdt_ref, rv_ref, rg_ref, dq_ref, dk_ref, dvt_ref, iq_ref, mq_ref,
                 ikwt_ref, ikb_ref):
    hb = h_ref[...].astype(BF16)

    def proj(lo, width):
        return _dot(hb, w_ref[:, lo:lo + width])

    cos = cos_ref[...]
    sin = sin_ref[...]
    _store_low_heads(rq_ref, _rope(proj(W_RQ0, RET_QK_W), cos, sin))
    rk = _rope(proj(W_RK0, RET_QK_W), cos, sin) * (RET_DK ** -0.5)
    _store_low_heads(rk_ref, rk)
    rkdt_ref[...] = (rk * kdec_ref[...]).T.astype(BF16)
    rv_ref[...] = proj(W_RV0, RET_V_W).astype(BF16)
    rg_ref[...] = proj(W_RG0, RET_V_W)
    dq_ref[...] = (proj(W_DQ0, DSA_W) * (DSA_DH ** -0.5 * LOG2E)).T.astype(BF16)
    _store_split_heads(dk_ref, proj(W_DK0, DSA_W))
    iq_ref[...] = proj(W_IQ0, IDX_Q_W).T.astype(BF16)
    mq_ref[...] = _dot(hb, wmq_ref[...]).astype(BF16)
    t = _dot_nt(wt_ref[...], hb)
    ikw_t = t[DSA_W:DSA_W + LANES]
    ikwt_ref[...] = ikw_t
    ikw = ikw_t.T
    low = lax.broadcasted_iota(jnp.int32, ikw.shape, 1) < IDX_DIM
    ik_low = jnp.where(low, ikw, 0.0)
    ikb_ref[:, 0:LANES] = ik_low.astype(BF16)
    ikb_ref[:, LANES:2 * LANES] = pltpu.roll(ik_low, IDX_DIM, 1).astype(BF16)
    vt = t[:DSA_W].astype(BF16)
    ones = jnp.ones((DVT_HEAD_ROWS - DSA_DH, DSA_T), BF16)
    for j in range(PROJ_ROWS // DSA_T):
        for h in range(DSA_HEADS):
            r0 = h * DVT_HEAD_ROWS
            dvt_ref[j, r0:r0 + DSA_DH, :] = vt[h * DSA_DH:(h + 1) * DSA_DH, j * DSA_T:(j + 1) * DSA_T]
            dvt_ref[j, r0 + DSA_DH:r0 + DVT_HEAD_ROWS, :] = ones


def _proj(h, w_in_bf, w_mq_bf, w_t_bf, cos_t, sin_t, kdec_rows):
    rows = lambda w: pl.BlockSpec((PROJ_ROWS, w), lambda i: (i, 0))
    seq_tiles = SEQ // PROJ_ROWS
    pos = lambda w: pl.BlockSpec((PROJ_ROWS, w), lambda i: (i % seq_tiles, 0))
    sd = jax.ShapeDtypeStruct
    ret_padded = RET_HEADS * HEAD_PAD
    dsa_padded = DSA_HEADS * HEAD_PAD
    slabs = PROJ_ROWS // DSA_T
    assert W_IK0 % LANES == 0
    return pl.pallas_call(
        _proj_kernel,
        out_shape=(
            sd((N_TOK, ret_padded), BF16), sd((N_TOK, ret_padded), BF16), sd((RET_QK_W, N_TOK), BF16),
            sd((N_TOK, RET_V_W), BF16), sd((N_TOK, RET_V_W), F32),
            sd((DSA_W, N_TOK), BF16), sd((N_TOK, dsa_padded), BF16),
            sd((N_TOK // DSA_T, DVT_ROWS, DSA_T), BF16),
            sd((IDX_Q_W, N_TOK), BF16), sd((N_TOK, MEM_W), BF16),
            sd((LANES, N_TOK), F32), sd((N_TOK, 2 * LANES), BF16),
        ),
        grid=(N_TOK // PROJ_ROWS,),
        in_specs=[rows(D_MODEL),
                  pl.BlockSpec((D_MODEL, W_IK0), lambda i: (0, 0), pipeline_mode=pl.Buffered(1)),
                  _const_spec((D_MODEL, MEM_W)), _const_spec((DSA_W + LANES, D_MODEL)),
                  pos(RET_QK_W), pos(RET_QK_W), _const_spec((PROJ_ROWS, RET_QK_W))],
        out_specs=(rows(ret_padded), rows(ret_padded),
                   pl.BlockSpec((RET_QK_W, PROJ_ROWS), lambda i: (0, i)),
                   rows(RET_V_W), rows(RET_V_W),
                   pl.BlockSpec((DSA_W, PROJ_ROWS), lambda i: (0, i)), rows(dsa_padded),
                   pl.BlockSpec((slabs, DVT_ROWS, DSA_T), lambda i: (i, 0, 0)),
                   pl.BlockSpec((IDX_Q_W, PROJ_ROWS), lambda i: (0, i)),
                   rows(MEM_W), pl.BlockSpec((LANES, PROJ_ROWS), lambda i: (0, i)), rows(2 * LANES)),
        compiler_params=_cparams(("parallel",)),
        name="mixer_proj",
    )(h, w_in_bf, w_mq_bf, w_t_bf, cos_t, sin_t, kdec_rows)


def _retention_kernel(q_ref, k_ref, kdt_ref, v_ref, g_ref, din_ref, qdec_ref, cdec_ref,
                      gng_ref, gnb_ref, o_ref, state_ref):
    @pl.when(pl.program_id(1) == 0)
    def _():
        state_ref[...] = jnp.zeros_like(state_ref)

    qdec = qdec_ref[...]
    gn_g = gng_ref[...]
    gn_b = gnb_ref[...]
    states = [state_ref[h] for h in range(RET_HEADS)]
    pad_rows = jnp.zeros((HEAD_PAD - RET_DK, RET_DV), BF16)
    for j in range(RET_ROWS // RET_CHUNK):
        rows = slice(j * RET_CHUNK, (j + 1) * RET_CHUNK)
        gate = g_ref[rows, :]
        kv_all = _dot(kdt_ref[:, rows], v_ref[rows, :])
        for h in range(RET_HEADS):
            hs = slice(h * HEAD_PAD, (h + 1) * HEAD_PAD)
            vs = slice(h * RET_DV, (h + 1) * RET_DV)
            qh = q_ref[rows, hs]
            vh = v_ref[rows, vs]
            scores = _dot_nt(qh, k_ref[rows, hs]) * din_ref[h]
            intra = _dot(scores.astype(BF16), vh)
            state_padded = jnp.concatenate([states[h].astype(BF16), pad_rows], axis=0)
            cross = _dot(qh, state_padded) * qdec[:, vs]
            states[h] = cdec_ref[h] * states[h] + kv_all[h * RET_DK:(h + 1) * RET_DK, vs]
            o = intra + cross
            mu = jnp.mean(o, axis=-1, keepdims=True)
            oc = o - mu
            var = jnp.mean(oc * oc, axis=-1, keepdims=True)
            o = oc * lax.rsqrt(var + LN_EPS) * gn_g[:, vs] + gn_b[:, vs]
            o_ref[rows, vs] = (jax.nn.silu(gate[:, vs]) * o).astype(BF16)
    for h in range(RET_HEADS):
        state_ref[h] = states[h]


def _retention(rq, rk, rkdt, rv, rg, din, qdec, cdec, gn_g, gn_b):
    nc = SEQ // RET_ROWS
    rows = lambda w: pl.BlockSpec((RET_ROWS, w), lambda b, n: (b * nc + n, 0))
    padded = RET_HEADS * HEAD_PAD
    return pl.pallas_call(
        _retention_kernel,
        out_shape=jax.ShapeDtypeStruct((N_TOK, RET_V_W), BF16),
        grid=(BATCH, nc),
        in_specs=[rows(padded), rows(padded),
                  pl.BlockSpec((RET_QK_W, RET_ROWS), lambda b, n: (0, b * nc + n)),
                  rows(RET_V_W), rows(RET_V_W),
                  _const_spec((RET_HEADS, RET_CHUNK, RET_CHUNK)),
                  _const_spec((RET_CHUNK, RET_V_W)),
                  _const_spec((RET_HEADS, RET_DK, RET_DV)),
                  _const_spec((1, RET_V_W)), _const_spec((1, RET_V_W))],
        out_specs=rows(RET_V_W),
        scratch_shapes=[pltpu.VMEM((RET_HEADS, RET_DK, RET_DV), F32)],
        compiler_params=_cparams(("parallel", "arbitrary")),
        name="retention",
    )(rq, rk, rkdt, rv, rg, din, qdec, cdec, gn_g, gn_b)


def _mematt_kernel(q_ref, mem_ref, w_ref, o_ref, kv_ref):
    @pl.when(pl.program_id(1) == 0)
    def _():
        kv_ref[...] = _dot(mem_ref[...].astype(BF16), w_ref[...]).astype(BF16)

    q = q_ref[...]
    for h in range(MEM_HEADS):
        hs = slice(h * MEM_DH, (h + 1) * MEM_DH)
        logits = _dot_nt(q[:, hs], kv_ref[:, hs]) * (MEM_DH ** -0.5)
        m = jnp.max(logits, axis=-1, keepdims=True)
        p = jnp.exp(logits - m)
        denom = jnp.sum(p, axis=-1, keepdims=True)
        o_ref[:, hs] = (_dot(p.astype(BF16), kv_ref[:, MEM_W + h * MEM_DH:MEM_W + (h + 1) * MEM_DH])
                        / denom).astype(BF16)


def _mematt(mq, mem2d, w_kv_bf):
    tiles = SEQ // MEMATT_ROWS
    return pl.pallas_call(
        _mematt_kernel,
        out_shape=jax.ShapeDtypeStruct((N_TOK, MEM_W), BF16),
        grid=(BATCH, tiles),
        in_specs=[pl.BlockSpec((MEMATT_ROWS, MEM_W), lambda b, i: (b * tiles + i, 0)),
                  pl.BlockSpec((MEM_TOKENS, D_MODEL), lambda b, i: (b, 0)),
                  _const_spec((D_MODEL, 2 * MEM_W))],
        out_specs=pl.BlockSpec((MEMATT_ROWS, MEM_W), lambda b, i: (b * tiles + i, 0)),
        scratch_shapes=[pltpu.VMEM((MEM_TOKENS, 2 * MEM_W), BF16)],
        compiler_params=_cparams(("parallel", "arbitrary")),
        name="mem_attention",
    )(mq, mem2d, w_kv_bf)


def _t5_bias_kernel(table_ref, o_ref, bmax_ref):
    m = lax.broadcasted_iota(jnp.int32, (SUBLANES, 2 * DSA_T), 1)
    max_exact = T5_BUCKETS // 2
    head_max = [None] * DSA_HEADS
    for w in range(2):
        n = jnp.maximum(m - w * DSA_T, 0)
        nf = jnp.maximum(n, 1).astype(F32)
        large = max_exact + jnp.floor(jnp.log(nf / max_exact) / math.log(T5_MAX_DIST / max_exact)
                                      * (T5_BUCKETS - max_exact)).astype(jnp.int32)
        large = jnp.minimum(large, T5_BUCKETS - 1)
        bucket = jnp.where(n < max_exact, n, large)
        for h in range(DSA_HEADS):
            vec = jnp.zeros((SUBLANES, 2 * DSA_T), F32)
            for b in range(T5_BUCKETS):
                vec = jnp.where(bucket == b, table_ref[b, h] * LOG2E, vec)
            rows = jnp.broadcast_to(vec[0:1, :], (DSA_T, 2 * DSA_T))
            o_ref[w, h] = pltpu.roll(rows, 0, 1, stride=1, stride_axis=0)[:, DSA_T:]
            top = jnp.max(vec[0:1, :], axis=1, keepdims=True)
            head_max[h] = top if head_max[h] is None else jnp.maximum(head_max[h], top)
    for h in range(DSA_HEADS):
        bmax_ref[h:h + 1, :] = jnp.broadcast_to(head_max[h], (1, LANES))


def _t5_bias(t5_table):
    return pl.pallas_call(
        _t5_bias_kernel,
        out_shape=(jax.ShapeDtypeStruct((2, DSA_HEADS, DSA_T, DSA_T), F32),
                   jax.ShapeDtypeStruct((DSA_HEADS, LANES), F32)),
        in_specs=[pl.BlockSpec(memory_space=pltpu.SMEM)],
        out_specs=(pl.BlockSpec(memory_space=pltpu.VMEM), pl.BlockSpec(memory_space=pltpu.VMEM)),
        compiler_params=pltpu.CompilerParams(vmem_limit_bytes=VMEM_LIMIT),
        name="t5_bias",
    )(t5_table)


def _fold_groups(x, op):
    return op(x.reshape(DSA_T // SUBLANES, SUBLANES, DSA_T), axis=0)


def _dsa_kernel(table_ref, bias_ref, bmax_ref, iq_ref, dq_ref, ikw_q_ref, ikb_ref, dk_ref, dvt_ref,
                o_ref, score_ref, logit_ref, acc_ref):
    qb = pl.program_id(1)
    q0 = qb * DSA_T
    n_chunks = qb + 1
    groups = DSA_T // SUBLANES
    key_in_chunk = lax.broadcasted_iota(jnp.int32, (DSA_T, DSA_T), 0)
    query_in_step = lax.broadcasted_iota(jnp.int32, (DSA_T, DSA_T), 1)

    def key_rows(c):
        return pl.ds(pl.multiple_of(c * DSA_T, DSA_T), DSA_T)

    def pair_rows(j):
        return slice(j * LANES, (j + 1) * LANES)

    def pair_products(keys_even, keys_odd, queries_t):
        prod = _dot(jnp.concatenate([keys_even, keys_odd], axis=0), queries_t)
        return prod[:DSA_T], prod[DSA_T:]

    iw_t = ikw_q_ref[IDX_DIM:IDX_DIM + IDX_HEADS, :] * IW_SCALE

    def scores_and_logits(c, carry, causal):
        raw_max, s_min, s_max = carry
        ik_low = ikb_ref[key_rows(c), 0:LANES]
        ik_high = ikb_ref[key_rows(c), LANES:2 * LANES]
        last = IDX_HEADS // 2 - 1
        for j in range(IDX_HEADS // 2):
            rel_even, rel_odd = pair_products(ik_low, ik_high, iq_ref[pair_rows(j), :])
            term = (jnp.maximum(rel_even, 0.0) * iw_t[2 * j:2 * j + 1, :]
                    + jnp.maximum(rel_odd, 0.0) * iw_t[2 * j + 1:2 * j + 2, :])
            total = term if j == 0 else score_ref[c] + term
            if j < last:
                score_ref[c] = total
            elif causal is None:
                score_ref[c] = total
                s_min = jnp.minimum(s_min, _fold_groups(total, jnp.min))
                s_max = jnp.maximum(s_max, _fold_groups(total, jnp.max))
            else:
                score_ref[c] = jnp.where(causal, total, NEG_INF)
                s_min = jnp.minimum(s_min, _fold_groups(jnp.where(causal, total, -NEG_INF), jnp.min))
                s_max = jnp.maximum(s_max, _fold_groups(jnp.where(causal, total, NEG_INF), jnp.max))
        out = list(raw_max)
        for j in range(DSA_HEADS // 2):
            keys = [dk_ref[key_rows(c), (2 * j + i) * HEAD_PAD:(2 * j + i + 1) * HEAD_PAD] for i in range(2)]
            for h, qk in zip((2 * j, 2 * j + 1), pair_products(keys[0], keys[1], dq_ref[pair_rows(j), :])):
                logit_ref[h, c] = qk
                out[h] = jnp.maximum(out[h], _fold_groups(qk, jnp.max))
        return tuple(out), s_min, s_max

    def paired_loop(start, stop, body, carry):
        n = stop - start
        carry = lax.fori_loop(
            0, lax.shift_right_logical(n, 1),
            lambda i, cr: body(start + 2 * i + 1, body(start + 2 * i, cr)), carry)
        return lax.cond(jnp.bitwise_and(n, 1) == 1, lambda cr: body(stop - 1, cr), lambda cr: cr, carry)

    low_init = jnp.full((SUBLANES, DSA_T), NEG_INF, F32)
    carry = (tuple(low_init for _ in range(DSA_HEADS)), -low_init, low_init)
    carry = paired_loop(0, qb, lambda c, cr: scores_and_logits(c, cr, None), carry)
    raw_max, s_min, s_max = scores_and_logits(qb, carry, key_in_chunk <= query_in_step)

    def count_where(pred):
        def body(c, acc):
            parts = [acc] + [jnp.zeros((SUBLANES, DSA_T), F32)] * (COUNT_ACCUMULATORS - 1)
            for g in range(groups):
                blk = score_ref[c, g * SUBLANES:(g + 1) * SUBLANES, :]
                a = g % COUNT_ACCUMULATORS
                parts[a] = parts[a] + jnp.where(pred(c, g * SUBLANES, blk), 1.0, 0.0)
            while len(parts) > 1:
                parts = [parts[i] + parts[i + 1] for i in range(0, len(parts), 2)]
            return parts[0]
        acc = paired_loop(0, n_chunks, body, jnp.zeros((SUBLANES, DSA_T), F32))
        return jnp.sum(acc, axis=0, keepdims=True)

    def count_ge(thr):
        thr_b = jnp.broadcast_to(thr, (SUBLANES, DSA_T))
        return count_where(lambda c, r0, blk: blk >= thr_b)

    q_min = jnp.min(s_min, axis=0, keepdims=True)
    q_max = jnp.max(s_max, axis=0, keepdims=True)
    t_q = q0 + lax.broadcasted_iota(jnp.int32, (1, DSA_T), 1)

    kf = float(TOPK)
    init = (q_min, q_max + jnp.maximum(jnp.abs(q_max), 1.0) * 1e-6,
            (t_q + 1).astype(F32), jnp.zeros((1, DSA_T), F32))

    def bisect_step(_i, carry):
        lo, hi, cnt_lo, cnt_hi = carry
        mid = 0.5 * (lo + hi)
        cnt = count_ge(mid)
        ge = cnt >= kf
        return (jnp.where(ge, mid, lo), jnp.where(ge, hi, mid),
                jnp.where(ge, cnt, cnt_lo), jnp.where(ge, cnt_hi, cnt))

    n_plain = jnp.where(qb == 0, 0, BISECT_PLAIN_STEPS)
    lo, hi, cnt_lo, cnt_hi = lax.fori_loop(0, n_plain, bisect_step, init)

    big = -NEG_INF

    def snap_body(c, carry):
        mn, mx = carry
        blk = score_ref[c]
        mn = jnp.minimum(mn, _fold_groups(jnp.where(blk >= lo, blk, big), jnp.min))
        mx = jnp.maximum(mx, _fold_groups(jnp.where(blk < hi, blk, -big), jnp.max))
        return mn, mx

    mn, mx = paired_loop(
        0, n_chunks, snap_body,
        (jnp.full((SUBLANES, DSA_T), big, F32), jnp.full((SUBLANES, DSA_T), -big, F32)))
    lo = jnp.min(mn, axis=0, keepdims=True)
    top = jnp.max(mx, axis=0, keepdims=True)

    def split_pair(carry):
        lo, top, cnt_lo, cnt_hi = carry
        pair = (cnt_lo > kf) & (lo < top) & (cnt_lo - cnt_hi == 2.0)
        return jnp.where(pair, top, lo), top, jnp.where(pair, cnt_hi + 1.0, cnt_lo), cnt_hi

    def pending_of(carry):
        lo, top, cnt_lo, _ = carry
        over = cnt_lo > kf
        return jnp.max(jnp.where(over & (lo < top), 2.0, jnp.where(over, 1.0, 0.0)))

    def value_step(state):
        (lo, top, cnt_lo, cnt_hi), _ = state
        unresolved = (cnt_lo > kf) & (lo < top)
        mid = lo + (top - lo) * 0.5
        mid = jnp.minimum(jnp.where(mid > lo, mid, top), top)

        def body(c, carry):
            cnt, mn, mx = carry
            blk = score_ref[c]
            is_ge = blk >= mid
            cnt = cnt + _fold_groups(jnp.where(is_ge, 1.0, 0.0), jnp.sum)
            mn = jnp.minimum(mn, _fold_groups(jnp.where(is_ge, blk, big), jnp.min))
            mx = jnp.maximum(mx, _fold_groups(jnp.where(is_ge, -big, blk), jnp.max))
            return cnt, mn, mx

        cnt, mn, mx = lax.fori_loop(
            0, n_chunks, body,
            (jnp.zeros((SUBLANES, DSA_T), F32), jnp.full((SUBLANES, DSA_T), big, F32),
             jnp.full((SUBLANES, DSA_T), -big, F32)))
        cnt = jnp.sum(cnt, axis=0, keepdims=True)
        ge = cnt >= kf
        raise_lo = unresolved & ge
        lower_top = unresolved & jnp.logical_not(ge)
        carry = split_pair((jnp.where(raise_lo, jnp.min(mn, axis=0, keepdims=True), lo),
                            jnp.where(lower_top, jnp.max(mx, axis=0, keepdims=True), top),
                            jnp.where(raise_lo, cnt, cnt_lo), jnp.where(lower_top, cnt, cnt_hi)))
        return carry, pending_of(carry)

    start = split_pair((lo, top, cnt_lo, cnt_hi))
    (lo, top, cnt_lo, cnt_hi), status = lax.while_loop(
        lambda state: state[1] > 1.5, value_step, (start, pending_of(start)))

    tied = cnt_lo > kf
    need = kf - cnt_hi
    lo_b = jnp.broadcast_to(lo, (SUBLANES, DSA_T))
    key_in_group = lax.broadcasted_iota(jnp.int32, (SUBLANES, DSA_T), 0)

    def tie_search(_):
        def body(_i, carry):
            lo_i, hi_i = carry
            mid_i = lax.shift_right_arithmetic(lo_i + hi_i, 1)
            mid_b = jnp.broadcast_to(mid_i, (SUBLANES, DSA_T))
            cnt = count_where(lambda c, r0, blk: (blk == lo_b) & (key_in_group + (c * DSA_T + r0) <= mid_b))
            enough = cnt >= need
            return jnp.where(enough, lo_i, mid_i), jnp.where(enough, mid_i, hi_i)
        lo_i = jnp.full((1, DSA_T), -1, jnp.int32)
        hi_i = jnp.full((1, DSA_T), SEQ - 1, jnp.int32)
        _, cut = lax.fori_loop(0, int(math.log2(SEQ)) + 1, body, (lo_i, hi_i))
        return jnp.where(tied, cut, SEQ)

    cut = lax.cond(status > 0.5, tie_search, lambda _: jnp.full((1, DSA_T), SEQ, jnp.int32), 0)

    def write_mask(c, ties):
        blk = score_ref[c]
        if ties:
            keep = (blk > lo) | ((blk == lo) & (key_in_chunk + c * DSA_T <= cut))
        else:
            keep = blk >= lo
        score_ref[c] = jnp.where(keep, 0.0, NEG_INF)

    n_far = jnp.maximum(qb - 1, 0)

    def masked_logits(c, h, far):
        lg = logit_ref[h, c] + score_ref[c]
        return lg if far else lg + bias_ref[c - qb + 1, h]

    def far_bias(h):
        return table_ref[T5_BUCKETS - 1, h] * LOG2E

    def den_row(h):
        return h * DVT_HEAD_ROWS + DSA_DH

    def exp_and_pv(shifts, scores_to_masks, ties=False):
        acc_ref[...] = jnp.zeros_like(acc_ref)

        def body(c, carry, far):
            if scores_to_masks:
                write_mask(c, ties)
            for h in range(DSA_HEADS):
                shift = shifts[h] - far_bias(h) if far else shifts[h]
                p = jnp.exp2(masked_logits(c, h, far) - shift).astype(BF16)
                rows = slice(h * DVT_HEAD_ROWS, (h + 1) * DVT_HEAD_ROWS)
                acc_ref[rows, :] += _dot(dvt_ref[c, rows, :], p)
            return carry

        paired_loop(0, n_far, lambda c, carry: body(c, carry, True), 0)
        paired_loop(n_far, n_chunks, lambda c, carry: body(c, carry, False), 0)

    upper = [jnp.max(raw_max[h], axis=0, keepdims=True) + bmax_ref[h:h + 1, 0:1] for h in range(DSA_HEADS)]
    @pl.when(status > 0.5)
    def _():
        exp_and_pv(upper, True, ties=True)

    @pl.when(status <= 0.5)
    def _():
        exp_and_pv(upper, True, ties=False)

    den_min = acc_ref[den_row(0):den_row(0) + 1, :]
    for h in range(1, DSA_HEADS):
        den_min = jnp.minimum(den_min, acc_ref[den_row(h):den_row(h) + 1, :])

    @pl.when(jnp.min(den_min) < DEN_FLOOR)
    def _():
        def max_body(c, ms, far):
            return tuple(jnp.maximum(ms[h], _fold_groups(masked_logits(c, h, far), jnp.max) + (far_bias(h) if far else 0.0))
                         for h in range(DSA_HEADS))
        ms = tuple(jnp.full((SUBLANES, DSA_T), NEG_INF, F32) for _ in range(DSA_HEADS))
        ms = lax.fori_loop(0, n_far, lambda c, ms: max_body(c, ms, True), ms)
        ms = lax.fori_loop(n_far, n_chunks, lambda c, ms: max_body(c, ms, False), ms)
        exp_and_pv([jnp.max(m, axis=0, keepdims=True) for m in ms], False)

    heads = []
    for h in range(DSA_HEADS):
        r0 = h * DVT_HEAD_ROWS
        heads.append(acc_ref[r0:r0 + DSA_DH, :] / acc_ref[den_row(h):den_row(h) + 1, :])
    o_ref[...] = jnp.concatenate(heads, axis=0).T.astype(BF16)


def _dsa(t5_table, bias, bias_max, iq, dq, ikw, ikb, dk, dvt):
    nqb = SEQ // DSA_T
    padded = DSA_HEADS * HEAD_PAD
    assert DSA_T <= TOPK
    qrows = lambda w: pl.BlockSpec((DSA_T, w), lambda b, i: (b * nqb + i, 0))
    qcols = lambda h: pl.BlockSpec((h, DSA_T), lambda b, i: (0, b * nqb + i))
    krows = lambda w: pl.BlockSpec((SEQ, w), lambda b, i: (b, 0))
    return pl.pallas_call(
        _dsa_kernel,
        out_shape=jax.ShapeDtypeStruct((N_TOK, DSA_W), BF16),
        grid=(BATCH, nqb),
        in_specs=[pl.BlockSpec(memory_space=pltpu.SMEM),
                  _const_spec((2, DSA_HEADS, DSA_T, DSA_T)), _const_spec((DSA_HEADS, LANES)),
                  qcols(IDX_Q_W), qcols(DSA_W), qcols(LANES),
                  krows(2 * LANES), krows(padded),
                  pl.BlockSpec((nqb, DVT_ROWS, DSA_T), lambda b, i: (b, 0, 0))],
        out_specs=qrows(DSA_W),
        scratch_shapes=[pltpu.VMEM((nqb, DSA_T, DSA_T), F32),
                        pltpu.VMEM((DSA_HEADS, nqb, DSA_T, DSA_T), F32),
                        pltpu.VMEM((DVT_ROWS, DSA_T), F32)],
        compiler_params=_cparams(("parallel", "arbitrary")),
        name="dsa_attention",
    )(t5_table, bias, bias_max, iq, dq, ikw, ikb, dk, dvt)


def _merge_kernel(h_ref, oret_ref, odsa_ref, omem_ref, wg_ref, wr_ref, wd_ref, wm_ref, wo_ref,
                  g_ref, b_ref, o_ref):
    h = h_ref[...]
    hb = h.astype(BF16)
    merged = None
    for i, (src, w) in enumerate(((oret_ref, wr_ref), (odsa_ref, wd_ref), (omem_ref, wm_ref))):
        gate = _sigmoid(_dot(hb, wg_ref[:, i * D_MODEL:(i + 1) * D_MODEL]))
        term = gate * _dot(src[...], w[...])
        merged = term if merged is None else merged + term
    mix = _dot(merged.astype(BF16), wo_ref[...])
    o_ref[...] = _layer_norm(ALPHA * h + mix, g_ref[...], b_ref[...])


def _merge(h, o_ret, o_dsa, o_mem, wg, wr, wd, wm, wo, g, b):
    rows = lambda w: pl.BlockSpec((MERGE_ROWS, w), lambda i: (i, 0))
    return pl.pallas_call(
        _merge_kernel,
        out_shape=jax.ShapeDtypeStruct((N_TOK, D_MODEL), F32),
        grid=(N_TOK // MERGE_ROWS,),
        in_specs=[rows(D_MODEL), rows(RET_V_W), rows(DSA_W), rows(MEM_W),
                  _const_spec((D_MODEL, 3 * D_MODEL)),
                  _const_spec((RET_V_W, D_MODEL)), _const_spec((DSA_W, D_MODEL)),
                  _const_spec((MEM_W, D_MODEL)), _const_spec((D_MODEL, D_MODEL)),
                  _const_spec((1, D_MODEL)), _const_spec((1, D_MODEL))],
        out_specs=rows(D_MODEL),
        compiler_params=_cparams(("parallel",)),
        name="merge_ln2",
    )(h, o_ret, o_dsa, o_mem, wg, wr, wd, wm, wo, g, b)


def _rope_tables():
    half = RET_DK // 2
    freqs = ROPE_BASE ** (-jnp.arange(half, dtype=F32) / half)
    ang = jnp.arange(SEQ).astype(F32)[:, None] * freqs[None, :]
    cos, sin = jnp.cos(ang), jnp.sin(ang)
    cos_t = jnp.tile(jnp.concatenate([cos, cos], axis=-1), (1, RET_HEADS))
    sin_t = jnp.tile(jnp.concatenate([-sin, sin], axis=-1), (1, RET_HEADS))
    return cos_t, sin_t


def _decay_tables():
    c = RET_CHUNK
    gamma = 1.0 - 2.0 ** (-5.0 - jnp.arange(RET_HEADS, dtype=F32))
    lg = jnp.log(gamma)
    i = jnp.arange(c)
    diff = i[:, None] - i[None, :]
    decay_in = jnp.where(diff[None] >= 0, jnp.exp(jnp.maximum(diff, 0)[None] * lg[:, None, None]), 0.0).astype(F32)
    k_dec = jnp.exp((c - 1 - i)[None, :] * lg[:, None]).astype(F32)
    q_dec = jnp.exp((i + 1)[None, :] * lg[:, None]).astype(F32)
    chunk_dec = jnp.exp(c * lg).astype(F32)
    kdec_t = jnp.repeat(k_dec.T, RET_DK, axis=1)
    qdec_t = jnp.repeat(q_dec.T, RET_DV, axis=1)
    cdec_t = jnp.broadcast_to(chunk_dec[:, None, None], (RET_HEADS, RET_DK, RET_DV))
    return decay_in, kdec_t, qdec_t, cdec_t


def _layer(x, mem2d, ffn1_w_in, ffn1_w_out, ln1_g, ln1_b, w_in, t5_table, ret_gn_g, ret_gn_b,
           w_mem_kv, w_br_ret, w_br_dsa, w_br_mem, w_out, ln2_g, ln2_b,
           ffn2_w_in, ffn2_w_out, ln3_g, ln3_b, tables):
    cos_t, sin_t, din, kdec, qdec, cdec = tables
    row = lambda v: v.reshape(1, -1)
    bf = lambda w: w.astype(BF16)

    w_in_bf = bf(w_in)
    w_mq = w_in_bf[:, W_MQ0:W_G0]
    w_t = jnp.concatenate([w_in_bf[:, W_DV0:W_IQ0], w_in_bf[:, W_IK0:W_IK0 + LANES]], axis=1).T
    kdec_rows = jnp.tile(kdec, (PROJ_ROWS // RET_CHUNK, 1))

    h = _ffn_ln(x, bf(ffn1_w_in), bf(ffn1_w_out), row(ln1_g), row(ln1_b), "ffn1_ln1")
    rq, rk, rkdt, rv, rg, dq, dk, dvt, iq, mq, ikw, ikb = _proj(h, w_in_bf, w_mq, w_t, cos_t, sin_t, kdec_rows)
    o_ret = _retention(rq, rk, rkdt, rv, rg, din, qdec, cdec, row(ret_gn_g), row(ret_gn_b))
    o_mem = _mematt(mq, mem2d, bf(w_mem_kv))
    o_dsa = _dsa(t5_table, *_t5_bias(t5_table), iq, dq, ikw, ikb, dk, dvt)
    x2 = _merge(h, o_ret, o_dsa, o_mem, w_in_bf[:, W_G0:], bf(w_br_ret), bf(w_br_dsa), bf(w_br_mem),
                bf(w_out), row(ln2_g), row(ln2_b))
    return _ffn_ln(x2, bf(ffn2_w_in), bf(ffn2_w_out), row(ln3_g), row(ln3_b), "ffn2_ln3")


def kernel(x, mem, ffn1_w_in, ffn1_w_out, ln1_g, ln1_b, w_in, t5_table, ret_gn_g, ret_gn_b,
           w_mem_kv, w_br_ret, w_br_dsa, w_br_mem, w_out, ln2_g, ln2_b,
           ffn2_w_in, ffn2_w_out, ln3_g, ln3_b):
    assert x.shape == (BATCH, SEQ, D_MODEL) and mem.shape == (BATCH, MEM_TOKENS, D_MODEL)
    tables = _rope_tables() + _decay_tables()
    y = x.reshape(N_TOK, D_MODEL)
    mem2d = mem.reshape(BATCH * MEM_TOKENS, D_MODEL)
    for l in range(DEPTH):
        y = _layer(y, mem2d, ffn1_w_in[l], ffn1_w_out[l], ln1_g[l], ln1_b[l], w_in[l], t5_table,
                   ret_gn_g[l], ret_gn_b[l], w_mem_kv[l], w_br_ret[l], w_br_dsa[l], w_br_mem[l],
                   w_out[l], ln2_g[l], ln2_b[l], ffn2_w_in[l], ffn2_w_out[l], ln3_g[l], ln3_b[l],
                   tables)
    return y.reshape(BATCH, SEQ, D_MODEL)
```

```python
import math

import jax
import jax.numpy as jnp
from jax import lax
from jax.experimental import pallas as pl
from jax.experimental.pallas import tpu as pltpu

F32 = jnp.float32
BF16 = jnp.bfloat16

D_MODEL = 1024
BATCH = 8
SEQ = 2048
MEM_TOKENS = 256
RET_HEADS, RET_DK, RET_DV, RET_CHUNK = 4, 64, 128, 128
DSA_HEADS, DSA_DH = 8, 64
IDX_HEADS, IDX_DIM = 8, 64
TOPK = min(256, SEQ // 4)
MEM_HEADS, MEM_DH = 4, 128
T5_BUCKETS, T5_MAX_DIST = 32, 128
D_FF = 2816
ROPE_BASE = 10000.0
LN_EPS = 1e-5
NEG_INF = -1e30
DEPTH = 1
ALPHA = (2.0 * DEPTH) ** 0.25

RET_QK_W = RET_HEADS * RET_DK
RET_V_W = RET_HEADS * RET_DV
DSA_W = DSA_HEADS * DSA_DH
IDX_Q_W = IDX_HEADS * IDX_DIM
MEM_W = MEM_HEADS * MEM_DH
N_TOK = BATCH * SEQ

V7X_VMEM_BYTES = 64 * 1024 * 1024
VMEM_LIMIT = V7X_VMEM_BYTES - 8 * 1024 * 1024
LANES = 128
SUBLANES = 8

FFN_ROWS = 1024
MXU_TILE = 256
FFN_SLICE = 2 * MXU_TILE
PROJ_ROWS = 1024
MERGE_ROWS = 1024
RET_ROWS = 1024
MEMATT_ROWS = 2048
DSA_T = 256
HEAD_PAD = LANES
DVT_HEAD_ROWS = DSA_DH + 16
DVT_ROWS = DSA_HEADS * DVT_HEAD_ROWS
BISECT_PLAIN_STEPS = 16
COUNT_ACCUMULATORS = 4
LOG2E = math.log2(math.e)
DEN_FLOOR = 2.0 ** -100

W_RQ0 = 0
W_RK0 = W_RQ0 + RET_QK_W
W_RV0 = W_RK0 + RET_QK_W
W_RG0 = W_RV0 + RET_V_W
W_DQ0 = W_RG0 + RET_V_W
W_DK0 = W_DQ0 + DSA_W
W_DV0 = W_DK0 + DSA_W
W_IQ0 = W_DV0 + DSA_W
W_IK0 = W_IQ0 + IDX_Q_W
W_IW0 = W_IK0 + IDX_DIM
W_MQ0 = W_IW0 + IDX_HEADS
W_G0 = W_MQ0 + MEM_W
IW_SCALE = IDX_HEADS ** -0.5 * IDX_DIM ** -0.5


def _cparams(sem):
    return pltpu.CompilerParams(dimension_semantics=sem, vmem_limit_bytes=VMEM_LIMIT)


def _const_spec(shape):
    nd = len(shape)
    return pl.BlockSpec(shape, lambda *_: (0,) * nd, pipeline_mode=pl.Buffered(1))


def _layer_norm(y, g, b):
    mu = jnp.mean(y, axis=-1, keepdims=True)
    yc = y - mu
    var = jnp.mean(yc * yc, axis=-1, keepdims=True)
    return yc * lax.rsqrt(var + LN_EPS) * g + b


def _sigmoid(x):
    return 0.5 * jnp.tanh(0.5 * x) + 0.5


def _dot(a, b):
    return jnp.dot(a, b, preferred_element_type=F32)


def _dot_nt(a, b):
    return lax.dot_general(a, b, (((1,), (1,)), ((), ())), preferred_element_type=F32)


def _ffn_ln_kernel(x_ref, wi_ref, wo_ref, g_ref, b_ref, o_ref, act_ref):
    x = x_ref[...]
    xb = x.astype(BF16)
    for lo in range(0, D_FF, FFN_SLICE):
        hi = min(lo + FFN_SLICE, D_FF)
        a = _dot(xb, wi_ref[:, lo:hi])
        u = _dot(xb, wi_ref[:, D_FF + lo:D_FF + hi])
        act_ref[:, lo:hi] = (a * _sigmoid(a) * u).astype(BF16)
    y = ALPHA * x + 0.5 * _dot(act_ref[...], wo_ref[...])
    o_ref[...] = _layer_norm(y, g_ref[...], b_ref[...])


def _ffn_ln(x, w_in_bf, w_out_bf, g, b, name):
    return pl.pallas_call(
        _ffn_ln_kernel,
        out_shape=jax.ShapeDtypeStruct((N_TOK, D_MODEL), F32),
        grid=(N_TOK // FFN_ROWS,),
        in_specs=[
            pl.BlockSpec((FFN_ROWS, D_MODEL), lambda i: (i, 0)),
            _const_spec((D_MODEL, 2 * D_FF)),
            _const_spec((D_FF, D_MODEL)),
            _const_spec((1, D_MODEL)),
            _const_spec((1, D_MODEL)),
        ],
        out_specs=pl.BlockSpec((FFN_ROWS, D_MODEL), lambda i: (i, 0)),
        scratch_shapes=[pltpu.VMEM((FFN_ROWS, D_FF), BF16)],
        compiler_params=_cparams(("parallel",)),
        name=name,
    )(x, w_in_bf, w_out_bf, g, b)


def _rope(x, cos, sin_signed):
    width = x.shape[-1]
    lane = lax.broadcasted_iota(jnp.int32, x.shape, 1)
    first_half = (lane % RET_DK) < (RET_DK // 2)
    swapped = jnp.where(first_half,
                        pltpu.roll(x, width - RET_DK // 2, 1),
                        pltpu.roll(x, RET_DK // 2, 1))
    return x * cos + swapped * sin_signed


def _store_split_heads(dst_ref, val):
    low = lax.broadcasted_iota(jnp.int32, (val.shape[0], LANES), 1) < DSA_DH
    for j in range(val.shape[1] // LANES):
        pair = val[:, j * LANES:(j + 1) * LANES]
        dst_ref[:, (2 * j) * LANES:(2 * j + 1) * LANES] = jnp.where(low, pair, 0.0).astype(dst_ref.dtype)
        dst_ref[:, (2 * j + 1) * LANES:(2 * j + 2) * LANES] = jnp.where(low, 0.0, pair).astype(dst_ref.dtype)


def _store_low_heads(dst_ref, val):
    low = lax.broadcasted_iota(jnp.int32, (val.shape[0], LANES), 1) < RET_DK
    for j in range(val.shape[1] // LANES):
        pair = val[:, j * LANES:(j + 1) * LANES]
        dst_ref[:, (2 * j) * LANES:(2 * j + 1) * LANES] = jnp.where(low, pair, 0.0).astype(dst_ref.dtype)
        dst_ref[:, (2 * j + 1) * LANES:(2 * j + 2) * LANES] = jnp.where(
            low, pltpu.roll(pair, RET_DK, 1), 0.0).astype(dst_ref.dtype)


def _proj_kernel(h_ref, w_ref, wmq_ref, wt_ref, cos_ref, sin_ref, kdec_ref,
                 rq_ref, rk_ref, rkdt_ref, rv_ref, rg_ref, dq_ref, dk_ref, dvt_ref, iq_ref, mq_ref,
                 ikwt_ref, ikb_ref):
    hb = h_ref[...].astype(BF16)

    def proj(lo, width):
        return _dot(hb, w_ref[:, lo:lo + width])

    cos = cos_ref[...]
    sin = sin_ref[...]
    _store_low_heads(rq_ref, _rope(proj(W_RQ0, RET_QK_W), cos, sin))
    rk = _rope(proj(W_RK0, RET_QK_W), cos, sin) * (RET_DK ** -0.5)
    _store_low_heads(rk_ref, rk)
    rkdt_ref[...] = (rk * kdec_ref[...]).T.astype(BF16)
    rv_ref[...] = proj(W_RV0, RET_V_W).astype(BF16)
    rg_ref[...] = proj(W_RG0, RET_V_W)
    dq_ref[...] = (proj(W_DQ0, DSA_W) * (DSA_DH ** -0.5 * LOG2E)).T.astype(BF16)
    _store_split_heads(dk_ref, proj(W_DK0, DSA_W))
    iq_ref[...] = proj(W_IQ0, IDX_Q_W).T.astype(BF16)
    mq_ref[...] = _dot(hb, wmq_ref[...]).astype(BF16)
    t = _dot_nt(wt_ref[...], hb)
    ikw_t = t[DSA_W:DSA_W + LANES]
    ikwt_ref[...] = ikw_t
    ikw = ikw_t.T
    low = lax.broadcasted_iota(jnp.int32, ikw.shape, 1) < IDX_DIM
    ik_low = jnp.where(low, ikw, 0.0)
    ikb_ref[:, 0:LANES] = ik_low.astype(BF16)
    ikb_ref[:, LANES:2 * LANES] = pltpu.roll(ik_low, IDX_DIM, 1).astype(BF16)
    vt = t[:DSA_W].astype(BF16)
    ones = jnp.ones((DVT_HEAD_ROWS - DSA_DH, DSA_T), BF16)
    for j in range(PROJ_ROWS // DSA_T):
        for h in range(DSA_HEADS):
            r0 = h * DVT_HEAD_ROWS
            dvt_ref[j, r0:r0 + DSA_DH, :] = vt[h * DSA_DH:(h + 1) * DSA_DH, j * DSA_T:(j + 1) * DSA_T]
            dvt_ref[j, r0 + DSA_DH:r0 + DVT_HEAD_ROWS, :] = ones


def _proj(h, w_in_bf, w_mq_bf, w_t_bf, cos_t, sin_t, kdec_rows):
    rows = lambda w: pl.BlockSpec((PROJ_ROWS, w), lambda i: (i, 0))
    seq_tiles = SEQ // PROJ_ROWS
    pos = lambda w: pl.BlockSpec((PROJ_ROWS, w), lambda i: (i % seq_tiles, 0))
    sd = jax.ShapeDtypeStruct
    ret_padded = RET_HEADS * HEAD_PAD
    dsa_padded = DSA_HEADS * HEAD_PAD
    slabs = PROJ_ROWS // DSA_T
    assert W_IK0 % LANES == 0
    return pl.pallas_call(
        _proj_kernel,
        out_shape=(
            sd((N_TOK, ret_padded), BF16), sd((N_TOK, ret_padded), BF16), sd((RET_QK_W, N_TOK), BF16),
            sd((N_TOK, RET_V_W), BF16), sd((N_TOK, RET_V_W), F32),
            sd((DSA_W, N_TOK), BF16), sd((N_TOK, dsa_padded), BF16),
            sd((N_TOK // DSA_T, DVT_ROWS, DSA_T), BF16),
            sd((IDX_Q_W, N_TOK), BF16), sd((N_TOK, MEM_W), BF16),
            sd((LANES, N_TOK), F32), sd((N_TOK, 2 * LANES), BF16),
        ),
        grid=(N_TOK // PROJ_ROWS,),
        in_specs=[rows(D_MODEL),
                  pl.BlockSpec((D_MODEL, W_IK0), lambda i: (0, 0), pipeline_mode=pl.Buffered(1)),
                  _const_spec((D_MODEL, MEM_W)), _const_spec((DSA_W + LANES, D_MODEL)),
                  pos(RET_QK_W), pos(RET_QK_W), _const_spec((PROJ_ROWS, RET_QK_W))],
        out_specs=(rows(ret_padded), rows(ret_padded),
                   pl.BlockSpec((RET_QK_W, PROJ_ROWS), lambda i: (0, i)),
                   rows(RET_V_W), rows(RET_V_W),
                   pl.BlockSpec((DSA_W, PROJ_ROWS), lambda i: (0, i)), rows(dsa_padded),
                   pl.BlockSpec((slabs, DVT_ROWS, DSA_T), lambda i: (i, 0, 0)),
                   pl.BlockSpec((IDX_Q_W, PROJ_ROWS), lambda i: (0, i)),
                   rows(MEM_W), pl.BlockSpec((LANES, PROJ_ROWS), lambda i: (0, i)), rows(2 * LANES)),
        compiler_params=_cparams(("parallel",)),
        name="mixer_proj",
    )(h, w_in_bf, w_mq_bf, w_t_bf, cos_t, sin_t, kdec_rows)


def _retention_kernel(q_ref, k_ref, kdt_ref, v_ref, g_ref, din_ref, qdec_ref, cdec_ref,
                      gng_ref, gnb_ref, o_ref, state_ref):
    @pl.when(pl.program_id(1) == 0)
    def _():
        state_ref[...] = jnp.zeros_like(state_ref)

    qdec = qdec_ref[...]
    gn_g = gng_ref[...]
    gn_b = gnb_ref[...]
    states = [state_ref[h] for h in range(RET_HEADS)]
    pad_rows = jnp.zeros((HEAD_PAD - RET_DK, RET_DV), BF16)
    for j in range(RET_ROWS // RET_CHUNK):
        rows = slice(j * RET_CHUNK, (j + 1) * RET_CHUNK)
        gate = g_ref[rows, :]
        kv_all = _dot(kdt_ref[:, rows], v_ref[rows, :])
        for h in range(RET_HEADS):
            hs = slice(h * HEAD_PAD, (h + 1) * HEAD_PAD)
            vs = slice(h * RET_DV, (h + 1) * RET_DV)
            qh = q_ref[rows, hs]
            vh = v_ref[rows, vs]
            scores = _dot_nt(qh, k_ref[rows, hs]) * din_ref[h]
            intra = _dot(scores.astype(BF16), vh)
            state_padded = jnp.concatenate([states[h].astype(BF16), pad_rows], axis=0)
            cross = _dot(qh, state_padded) * qdec[:, vs]
            states[h] = cdec_ref[h] * states[h] + kv_all[h * RET_DK:(h + 1) * RET_DK, vs]
            o = intra + cross
            mu = jnp.mean(o, axis=-1, keepdims=True)
            oc = o - mu
            var = jnp.mean(oc * oc, axis=-1, keepdims=True)
            o = oc * lax.rsqrt(var + LN_EPS) * gn_g[:, vs] + gn_b[:, vs]
            o_ref[rows, vs] = (jax.nn.silu(gate[:, vs]) * o).astype(BF16)
    for h in range(RET_HEADS):
        state_ref[h] = states[h]


def _retention(rq, rk, rkdt, rv, rg, din, qdec, cdec, gn_g, gn_b):
    nc = SEQ // RET_ROWS
    rows = lambda w: pl.BlockSpec((RET_ROWS, w), lambda b, n: (b * nc + n, 0))
    padded = RET_HEADS * HEAD_PAD
    return pl.pallas_call(
        _retention_kernel,
        out_shape=jax.ShapeDtypeStruct((N_TOK, RET_V_W), BF16),
        grid=(BATCH, nc),
        in_specs=[rows(padded), rows(padded),
                  pl.BlockSpec((RET_QK_W, RET_ROWS), lambda b, n: (0, b * nc + n)),
                  rows(RET_V_W), rows(RET_V_W),
                  _const_spec((RET_HEADS, RET_CHUNK, RET_CHUNK)),
                  _const_spec((RET_CHUNK, RET_V_W)),
                  _const_spec((RET_HEADS, RET_DK, RET_DV)),
                  _const_spec((1, RET_V_W)), _const_spec((1, RET_V_W))],
        out_specs=rows(RET_V_W),
        scratch_shapes=[pltpu.VMEM((RET_HEADS, RET_DK, RET_DV), F32)],
        compiler_params=_cparams(("parallel", "arbitrary")),
        name="retention",
    )(rq, rk, rkdt, rv, rg, din, qdec, cdec, gn_g, gn_b)


def _mematt_kernel(q_ref, mem_ref, w_ref, o_ref, kv_ref):
    @pl.when(pl.program_id(1) == 0)
    def _():
        kv_ref[...] = _dot(mem_ref[...].astype(BF16), w_ref[...]).astype(BF16)

    q = q_ref[...]
    for h in range(MEM_HEADS):
        hs = slice(h * MEM_DH, (h + 1) * MEM_DH)
        logits = _dot_nt(q[:, hs], kv_ref[:, hs]) * (MEM_DH ** -0.5)
        m = jnp.max(logits, axis=-1, keepdims=True)
        p = jnp.exp(logits - m)
        denom = jnp.sum(p, axis=-1, keepdims=True)
        o_ref[:, hs] = (_dot(p.astype(BF16), kv_ref[:, MEM_W + h * MEM_DH:MEM_W + (h + 1) * MEM_DH])
                        / denom).astype(BF16)


def _mematt(mq, mem2d, w_kv_bf):
    tiles = SEQ // MEMATT_ROWS
    return pl.pallas_call(
        _mematt_kernel,
        out_shape=jax.ShapeDtypeStruct((N_TOK, MEM_W), BF16),
        grid=(BATCH, tiles),
        in_specs=[pl.BlockSpec((MEMATT_ROWS, MEM_W), lambda b, i: (b * tiles + i, 0)),
                  pl.BlockSpec((MEM_TOKENS, D_MODEL), lambda b, i: (b, 0)),
                  _const_spec((D_MODEL, 2 * MEM_W))],
        out_specs=pl.BlockSpec((MEMATT_ROWS, MEM_W), lambda b, i: (b * tiles + i, 0)),
        scratch_shapes=[pltpu.VMEM((MEM_TOKENS, 2 * MEM_W), BF16)],
        compiler_params=_cparams(("parallel", "arbitrary")),
        name="mem_attention",
    )(mq, mem2d, w_kv_bf)


def _t5_bias_kernel(table_ref, o_ref, bmax_ref):
    m = lax.broadcasted_iota(jnp.int32, (SUBLANES, 2 * DSA_T), 1)
    max_exact = T5_BUCKETS // 2
    head_max = [None] * DSA_HEADS
    for w in range(2):
        n = jnp.maximum(m - w * DSA_T, 0)
        nf = jnp.maximum(n, 1).astype(F32)
        large = max_exact + jnp.floor(jnp.log(nf / max_exact) / math.log(T5_MAX_DIST / max_exact)
                                      * (T5_BUCKETS - max_exact)).astype(jnp.int32)
        large = jnp.minimum(large, T5_BUCKETS - 1)
        bucket = jnp.where(n < max_exact, n, large)
        for h in range(DSA_HEADS):
            vec = jnp.zeros((SUBLANES, 2 * DSA_T), F32)
            for b in range(T5_BUCKETS):
                vec = jnp.where(bucket == b, table_ref[b, h] * LOG2E, vec)
            rows = jnp.broadcast_to(vec[0:1, :], (DSA_T, 2 * DSA_T))
            o_ref[w, h] = pltpu.roll(rows, 0, 1, stride=1, stride_axis=0)[:, DSA_T:]
            top = jnp.max(vec[0:1, :], axis=1, keepdims=True)
            head_max[h] = top if head_max[h] is None else jnp.maximum(head_max[h], top)
    for h in range(DSA_HEADS):
        bmax_ref[h:h + 1, :] = jnp.broadcast_to(head_max[h], (1, LANES))


def _t5_bias(t5_table):
    return pl.pallas_call(
        _t5_bias_kernel,
        out_shape=(jax.ShapeDtypeStruct((2, DSA_HEADS, DSA_T, DSA_T), F32),
                   jax.ShapeDtypeStruct((DSA_HEADS, LANES), F32)),
        in_specs=[pl.BlockSpec(memory_space=pltpu.SMEM)],
        out_specs=(pl.BlockSpec(memory_space=pltpu.VMEM), pl.BlockSpec(memory_space=pltpu.VMEM)),
        compiler_params=pltpu.CompilerParams(vmem_limit_bytes=VMEM_LIMIT),
        name="t5_bias",
    )(t5_table)


def _fold_groups(x, op):
    return op(x.reshape(DSA_T // SUBLANES, SUBLANES, DSA_T), axis=0)


def _dsa_kernel(table_ref, bias_ref, bmax_ref, iq_ref, dq_ref, ikw_q_ref, ikb_ref, dk_ref, dvt_ref,
                o_ref, score_ref, logit_ref, acc_ref):
    qb = pl.program_id(1)
    q0 = qb * DSA_T
    n_chunks = qb + 1
    groups = DSA_T // SUBLANES
    key_in_chunk = lax.broadcasted_iota(jnp.int32, (DSA_T, DSA_T), 0)
    query_in_step = lax.broadcasted_iota(jnp.int32, (DSA_T, DSA_T), 1)

    def key_rows(c):
        return pl.ds(pl.multiple_of(c * DSA_T, DSA_T), DSA_T)

    def pair_rows(j):
        return slice(j * LANES, (j + 1) * LANES)

    def pair_products(keys_even, keys_odd, queries_t):
        prod = _dot(jnp.concatenate([keys_even, keys_odd], axis=0), queries_t)
        return prod[:DSA_T], prod[DSA_T:]

    iw_t = ikw_q_ref[IDX_DIM:IDX_DIM + IDX_HEADS, :] * IW_SCALE

    def scores_and_logits(c, carry, causal):
        raw_max, s_min, s_max = carry
        ik_low = ikb_ref[key_rows(c), 0:LANES]
        ik_high = ikb_ref[key_rows(c), LANES:2 * LANES]
        last = IDX_HEADS // 2 - 1
        for j in range(IDX_HEADS // 2):
            rel_even, rel_odd = pair_products(ik_low, ik_high, iq_ref[pair_rows(j), :])
            term = (jnp.maximum(rel_even, 0.0) * iw_t[2 * j:2 * j + 1, :]
                    + jnp.maximum(rel_odd, 0.0) * iw_t[2 * j + 1:2 * j + 2, :])
            total = term if j == 0 else score_ref[c] + term
            if j < last:
                score_ref[c] = total
            elif causal is None:
                score_ref[c] = total
                s_min = jnp.minimum(s_min, _fold_groups(total, jnp.min))
                s_max = jnp.maximum(s_max, _fold_groups(total, jnp.max))
            else:
                score_ref[c] = jnp.where(causal, total, NEG_INF)
                s_min = jnp.minimum(s_min, _fold_groups(jnp.where(causal, total, -NEG_INF), jnp.min))
                s_max = jnp.maximum(s_max, _fold_groups(jnp.where(causal, total, NEG_INF), jnp.max))
        out = list(raw_max)
        for j in range(DSA_HEADS // 2):
            keys = [dk_ref[key_rows(c), (2 * j + i) * HEAD_PAD:(2 * j + i + 1) * HEAD_PAD] for i in range(2)]
            for h, qk in zip((2 * j, 2 * j + 1), pair_products(keys[0], keys[1], dq_ref[pair_rows(j), :])):
                logit_ref[h, c] = qk
                out[h] = jnp.maximum(out[h], _fold_groups(qk, jnp.max))
        return tuple(out), s_min, s_max

    def paired_loop(start, stop, body, carry):
        n = stop - start
        carry = lax.fori_loop(
            0, lax.shift_right_logical(n, 1),
            lambda i, cr: body(start + 2 * i + 1, body(start + 2 * i, cr)), carry)
        return lax.cond(jnp.bitwise_and(n, 1) == 1, lambda cr: body(stop - 1, cr), lambda cr: cr, carry)

    low_init = jnp.full((SUBLANES, DSA_T), NEG_INF, F32)
    carry = (tuple(low_init for _ in range(DSA_HEADS)), -low_init, low_init)
    carry = paired_loop(0, qb, lambda c, cr: scores_and_logits(c, cr, None), carry)
    raw_max, s_min, s_max = scores_and_logits(qb, carry, key_in_chunk <= query_in_step)

    def count_where(pred):
        def body(c, acc):
            parts = [acc] + [jnp.zeros((SUBLANES, DSA_T), F32)] * (COUNT_ACCUMULATORS - 1)
            for g in range(groups):
                blk = score_ref[c, g * SUBLANES:(g + 1) * SUBLANES, :]
                a = g % COUNT_ACCUMULATORS
                parts[a] = parts[a] + jnp.where(pred(c, g * SUBLANES, blk), 1.0, 0.0)
            while len(parts) > 1:
                parts = [parts[i] + parts[i + 1] for i in range(0, len(parts), 2)]
            return parts[0]
        acc = paired_loop(0, n_chunks, body, jnp.zeros((SUBLANES, DSA_T), F32))
        return jnp.sum(acc, axis=0, keepdims=True)

    def count_ge(thr):
        thr_b = jnp.broadcast_to(thr, (SUBLANES, DSA_T))
        return count_where(lambda c, r0, blk: blk >= thr_b)

    q_min = jnp.min(s_min, axis=0, keepdims=True)
    q_max = jnp.max(s_max, axis=0, keepdims=True)
    t_q = q0 + lax.broadcasted_iota(jnp.int32, (1, DSA_T), 1)

    kf = float(TOPK)
    init = (q_min, q_max + jnp.maximum(jnp.abs(q_max), 1.0) * 1e-6,
            (t_q + 1).astype(F32), jnp.zeros((1, DSA_T), F32))

    def bisect_step(_i, carry):
        lo, hi, cnt_lo, cnt_hi = carry
        mid = 0.5 * (lo + hi)
        cnt = count_ge(mid)
        ge = cnt >= kf
        return (jnp.where(ge, mid, lo), jnp.where(ge, hi, mid),
                jnp.where(ge, cnt, cnt_lo), jnp.where(ge, cnt_hi, cnt))

    n_plain = jnp.where(qb == 0, 0, BISECT_PLAIN_STEPS)
    lo, hi, cnt_lo, cnt_hi = lax.fori_loop(0, n_plain, bisect_step, init)

    big = -NEG_INF

    def snap_body(c, carry):
        mn, mx = carry
        blk = score_ref[c]
        mn = jnp.minimum(mn, _fold_groups(jnp.where(blk >= lo, blk, big), jnp.min))
        mx = jnp.maximum(mx, _fold_groups(jnp.where(blk < hi, blk, -big), jnp.max))
        return mn, mx

    mn, mx = paired_loop(
        0, n_chunks, snap_body,
        (jnp.full((SUBLANES, DSA_T), big, F32), jnp.full((SUBLANES, DSA_T), -big, F32)))
    lo = jnp.min(mn, axis=0, keepdims=True)
    top = jnp.max(mx, axis=0, keepdims=True)

    def split_pair(carry):
        lo, top, cnt_lo, cnt_hi = carry
        pair = (cnt_lo > kf) & (lo < top) & (cnt_lo - cnt_hi == 2.0)
        return jnp.where(pair, top, lo), top, jnp.where(pair, cnt_hi + 1.0, cnt_lo), cnt_hi

    def pending_of(carry):
        lo, top, cnt_lo, _ = carry
        over = cnt_lo > kf
        return jnp.max(jnp.where(over & (lo < top), 2.0, jnp.where(over, 1.0, 0.0)))

    def value_step(state):
        (lo, top, cnt_lo, cnt_hi), _ = state
        unresolved = (cnt_lo > kf) & (lo < top)
        mid = lo + (top - lo) * 0.5
        mid = jnp.minimum(jnp.where(mid > lo, mid, top), top)

        def body(c, carry):
            cnt, mn, mx = carry
            blk = score_ref[c]
            is_ge = blk >= mid
            cnt = cnt + _fold_groups(jnp.where(is_ge, 1.0, 0.0), jnp.sum)
            mn = jnp.minimum(mn, _fold_groups(jnp.where(is_ge, blk, big), jnp.min))
            mx = jnp.maximum(mx, _fold_groups(jnp.where(is_ge, -big, blk), jnp.max))
            return cnt, mn, mx

        cnt, mn, mx = lax.fori_loop(
            0, n_chunks, body,
            (jnp.zeros((SUBLANES, DSA_T), F32), jnp.full((SUBLANES, DSA_T), big, F32),
             jnp.full((SUBLANES, DSA_T), -big, F32)))
        cnt = jnp.sum(cnt, axis=0, keepdims=True)
        ge = cnt >= kf
        raise_lo = unresolved & ge
        lower_top = unresolved & jnp.logical_not(ge)
        carry = split_pair((jnp.where(raise_lo, jnp.min(mn, axis=0, keepdims=True), lo),
                            jnp.where(lower_top, jnp.max(mx, axis=0, keepdims=True), top),
                            jnp.where(raise_lo, cnt, cnt_lo), jnp.where(lower_top, cnt, cnt_hi)))
        return carry, pending_of(carry)

    start = split_pair((lo, top, cnt_lo, cnt_hi))
    (lo, top, cnt_lo, cnt_hi), status = lax.while_loop(
        lambda state: state[1] > 1.5, value_step, (start, pending_of(start)))

    tied = cnt_lo > kf
    need = kf - cnt_hi
    lo_b = jnp.broadcast_to(lo, (SUBLANES, DSA_T))
    key_in_group = lax.broadcasted_iota(jnp.int32, (SUBLANES, DSA_T), 0)

    def tie_search(_):
        def body(_i, carry):
            lo_i, hi_i = carry
            mid_i = lax.shift_right_arithmetic(lo_i + hi_i, 1)
            mid_b = jnp.broadcast_to(mid_i, (SUBLANES, DSA_T))
            cnt = count_where(lambda c, r0, blk: (blk == lo_b) & (key_in_group + (c * DSA_T + r0) <= mid_b))
            enough = cnt >= need
            return jnp.where(enough, lo_i, mid_i), jnp.where(enough, mid_i, hi_i)
        lo_i = jnp.full((1, DSA_T), -1, jnp.int32)
        hi_i = jnp.full((1, DSA_T), SEQ - 1, jnp.int32)
        _, cut = lax.fori_loop(0, int(math.log2(SEQ)) + 1, body, (lo_i, hi_i))
        return jnp.where(tied, cut, SEQ)

    cut = lax.cond(status > 0.5, tie_search, lambda _: jnp.full((1, DSA_T), SEQ, jnp.int32), 0)

    def write_mask(c, ties):
        blk = score_ref[c]
        if ties:
            keep = (blk > lo) | ((blk == lo) & (key_in_chunk + c * DSA_T <= cut))
        else:
            keep = blk >= lo
        score_ref[c] = jnp.where(keep, 0.0, NEG_INF)

    n_far = jnp.maximum(qb - 1, 0)

    def masked_logits(c, h, far):
        lg = logit_ref[h, c] + score_ref[c]
        return lg if far else lg + bias_ref[c - qb + 1, h]

    def far_bias(h):
        return table_ref[T5_BUCKETS - 1, h] * LOG2E

    def den_row(h):
        return h * DVT_HEAD_ROWS + DSA_DH

    def exp_and_pv(shifts, scores_to_masks, ties=False):
        acc_ref[...] = jnp.zeros_like(acc_ref)

        def body(c, carry, far):
            if scores_to_masks:
                write_mask(c, ties)
            for h in range(DSA_HEADS):
                shift = shifts[h] - far_bias(h) if far else shifts[h]
                p = jnp.exp2(masked_logits(c, h, far) - shift).astype(BF16)
                rows = slice(h * DVT_HEAD_ROWS, (h + 1) * DVT_HEAD_ROWS)
                acc_ref[rows, :] += _dot(dvt_ref[c, rows, :], p)
            return carry

        paired_loop(0, n_far, lambda c, carry: body(c, carry, True), 0)
        paired_loop(n_far, n_chunks, lambda c, carry: body(c, carry, False), 0)

    upper = [jnp.max(raw_max[h], axis=0, keepdims=True) + bmax_ref[h:h + 1, 0:1] for h in range(DSA_HEADS)]
    @pl.when(status > 0.5)
    def _():
        exp_and_pv(upper, True, ties=True)

    @pl.when(status <= 0.5)
    def _():
        exp_and_pv(upper, True, ties=False)

    den_min = acc_ref[den_row(0):den_row(0) + 1, :]
    for h in range(1, DSA_HEADS):
        den_min = jnp.minimum(den_min, acc_ref[den_row(h):den_row(h) + 1, :])

    @pl.when(jnp.min(den_min) < DEN_FLOOR)
    def _():
        def max_body(c, ms, far):
            return tuple(jnp.maximum(ms[h], _fold_groups(masked_logits(c, h, far), jnp.max) + (far_bias(h) if far else 0.0))
                         for h in range(DSA_HEADS))
        ms = tuple(jnp.full((SUBLANES, DSA_T), NEG_INF, F32) for _ in range(DSA_HEADS))
        ms = lax.fori_loop(0, n_far, lambda c, ms: max_body(c, ms, True), ms)
        ms = lax.fori_loop(n_far, n_chunks, lambda c, ms: max_body(c, ms, False), ms)
        exp_and_pv([jnp.max(m, axis=0, keepdims=True) for m in ms], False)

    heads = []
    for h in range(DSA_HEADS):
        r0 = h * DVT_HEAD_ROWS
        heads.append(acc_ref[r0:r0 + DSA_DH, :] / acc_ref[den_row(h):den_row(h) + 1, :])
    o_ref[...] = jnp.concatenate(heads, axis=0).T.astype(BF16)


def _dsa(t5_table, bias, bias_max, iq, dq, ikw, ikb, dk, dvt):
    nqb = SEQ // DSA_T
    padded = DSA_HEADS * HEAD_PAD
    assert DSA_T <= TOPK
    qrows = lambda w: pl.BlockSpec((DSA_T, w), lambda b, i: (b * nqb + i, 0))
    qcols = lambda h: pl.BlockSpec((h, DSA_T), lambda b, i: (0, b * nqb + i))
    krows = lambda w: pl.BlockSpec((SEQ, w), lambda b, i: (b, 0))
    return pl.pallas_call(
        _dsa_kernel,
        out_shape=jax.ShapeDtypeStruct((N_TOK, DSA_W), BF16),
        grid=(BATCH, nqb),
        in_specs=[pl.BlockSpec(memory_space=pltpu.SMEM),
                  _const_spec((2, DSA_HEADS, DSA_T, DSA_T)), _const_spec((DSA_HEADS, LANES)),
                  qcols(IDX_Q_W), qcols(DSA_W), qcols(LANES),
                  krows(2 * LANES), krows(padded),
                  pl.BlockSpec((nqb, DVT_ROWS, DSA_T), lambda b, i: (b, 0, 0))],
        out_specs=qrows(DSA_W),
        scratch_shapes=[pltpu.VMEM((nqb, DSA_T, DSA_T), F32),
                        pltpu.VMEM((DSA_HEADS, nqb, DSA_T, DSA_T), F32),
                        pltpu.VMEM((DVT_ROWS, DSA_T), F32)],
        compiler_params=_cparams(("parallel", "arbitrary")),
        name="dsa_attention",
    )(t5_table, bias, bias_max, iq, dq, ikw, ikb, dk, dvt)


def _merge_kernel(h_ref, oret_ref, odsa_ref, omem_ref, wg_ref, wr_ref, wd_ref, wm_ref, wo_ref,
                  g_ref, b_ref, o_ref):
    h = h_ref[...]
    hb = h.astype(BF16)
    merged = None
    for i, (src, w) in enumerate(((oret_ref, wr_ref), (odsa_ref, wd_ref), (omem_ref, wm_ref))):
        gate = _sigmoid(_dot(hb, wg_ref[:, i * D_MODEL:(i + 1) * D_MODEL]))
        term = gate * _dot(src[...], w[...])
        merged = term if merged is None else merged + term
    mix = _dot(merged.astype(BF16), wo_ref[...])
    o_ref[...] = _layer_norm(ALPHA * h + mix, g_ref[...], b_ref[...])


def _merge(h, o_ret, o_dsa, o_mem, wg, wr, wd, wm, wo, g, b):
    rows = lambda w: pl.BlockSpec((MERGE_ROWS, w), lambda i: (i, 0))
    return pl.pallas_call(
        _merge_kernel,
        out_shape=jax.ShapeDtypeStruct((N_TOK, D_MODEL), F32),
        grid=(N_TOK // MERGE_ROWS,),
        in_specs=[rows(D_MODEL), rows(RET_V_W), rows(DSA_W), rows(MEM_W),
                  _const_spec((D_MODEL, 3 * D_MODEL)),
                  _const_spec((RET_V_W, D_MODEL)), _const_spec((DSA_W, D_MODEL)),
                  _const_spec((MEM_W, D_MODEL)), _const_spec((D_MODEL, D_MODEL)),
                  _const_spec((1, D_MODEL)), _const_spec((1, D_MODEL))],
        out_specs=rows(D_MODEL),
        compiler_params=_cparams(("parallel",)),
        name="merge_ln2",
    )(h, o_ret, o_dsa, o_mem, wg, wr, wd, wm, wo, g, b)


def _rope_tables():
    half = RET_DK // 2
    freqs = ROPE_BASE ** (-jnp.arange(half, dtype=F32) / half)
    ang = jnp.arange(SEQ).astype(F32)[:, None] * freqs[None, :]
    cos, sin = jnp.cos(ang), jnp.sin(ang)
    cos_t = jnp.tile(jnp.concatenate([cos, cos], axis=-1), (1, RET_HEADS))
    sin_t = jnp.tile(jnp.concatenate([-sin, sin], axis=-1), (1, RET_HEADS))
    return cos_t, sin_t


def _decay_tables():
    c = RET_CHUNK
    gamma = 1.0 - 2.0 ** (-5.0 - jnp.arange(RET_HEADS, dtype=F32))
    lg = jnp.log(gamma)
    i = jnp.arange(c)
    diff = i[:, None] - i[None, :]
    decay_in = jnp.where(diff[None] >= 0, jnp.exp(jnp.maximum(diff, 0)[None] * lg[:, None, None]), 0.0).astype(F32)
    k_dec = jnp.exp((c - 1 - i)[None, :] * lg[:, None]).astype(F32)
    q_dec = jnp.exp((i + 1)[None, :] * lg[:, None]).astype(F32)
    chunk_dec = jnp.exp(c * lg).astype(F32)
    kdec_t = jnp.repeat(k_dec.T, RET_DK, axis=1)
    qdec_t = jnp.repeat(q_dec.T, RET_DV, axis=1)
    cdec_t = jnp.broadcast_to(chunk_dec[:, None, None], (RET_HEADS, RET_DK, RET_DV))
    return decay_in, kdec_t, qdec_t, cdec_t


def _layer(x, mem2d, ffn1_w_in, ffn1_w_out, ln1_g, ln1_b, w_in, t5_table, ret_gn_g, ret_gn_b,
           w_mem_kv, w_br_ret, w_br_dsa, w_br_mem, w_out, ln2_g, ln2_b,
           ffn2_w_in, ffn2_w_out, ln3_g, ln3_b, tables):
    cos_t, sin_t, din, kdec, qdec, cdec = tables
    row = lambda v: v.reshape(1, -1)
    bf = lambda w: w.astype(BF16)

    w_in_bf = bf(w_in)
    w_mq = w_in_bf[:, W_MQ0:W_G0]
    w_t = jnp.concatenate([w_in_bf[:, W_DV0:W_IQ0], w_in_bf[:, W_IK0:W_IK0 + LANES]], axis=1).T
    kdec_rows = jnp.tile(kdec, (PROJ_ROWS // RET_CHUNK, 1))

    h = _ffn_ln(x, bf(ffn1_w_in), bf(ffn1_w_out), row(ln1_g), row(ln1_b), "ffn1_ln1")
    rq, rk, rkdt, rv, rg, dq, dk, dvt, iq, mq, ikw, ikb = _proj(h, w_in_bf, w_mq, w_t, cos_t, sin_t, kdec_rows)
    o_ret = _retention(rq, rk, rkdt, rv, rg, din, qdec, cdec, row(ret_gn_g), row(ret_gn_b))
    o_mem = _mematt(mq, mem2d, bf(w_mem_kv))
    o_dsa = _dsa(t5_table, *_t5_bias(t5_table), iq, dq, ikw, ikb, dk, dvt)
    x2 = _merge(h, o_ret, o_dsa, o_mem, w_in_bf[:, W_G0:], bf(w_br_ret), bf(w_br_dsa), bf(w_br_mem),
                bf(w_out), row(ln2_g), row(ln2_b))
    return _ffn_ln(x2, bf(ffn2_w_in), bf(ffn2_w_out), row(ln3_g), row(ln3_b), "ffn2_ln3")


def kernel(x, mem, ffn1_w_in, ffn1_w_out, ln1_g, ln1_b, w_in, t5_table, ret_gn_g, ret_gn_b,
           w_mem_kv, w_br_ret, w_br_dsa, w_br_mem, w_out, ln2_g, ln2_b,
           ffn2_w_in, ffn2_w_out, ln3_g, ln3_b):
    assert x.shape == (BATCH, SEQ, D_MODEL) and mem.shape == (BATCH, MEM_TOKENS, D_MODEL)
    tables = _rope_tables() + _decay_tables()
    y = x.reshape(N_TOK, D_MODEL)
    mem2d = mem.reshape(BATCH * MEM_TOKENS, D_MODEL)
    for l in range(DEPTH):
        y = _layer(y, mem2d, ffn1_w_in[l], ffn1_w_out[l], ln1_g[l], ln1_b[l], w_in[l], t5_table,
                   ret_gn_g[l], ret_gn_b[l], w_mem_kv[l], w_br_ret[l], w_br_dsa[l], w_br_mem[l],
                   w_out[l], ln2_g[l], ln2_b[l], ffn2_w_in[l], ffn2_w_out[l], ln3_g[l], ln3_b[l],
                   tables)
    return y.reshape(BATCH, SEQ, D_MODEL)
```
